```python
import jax, jax.numpy as jnp
from jax import lax
import numpy as np

D_MODEL = 1024
BATCH = 16
SEQ = 256
DEPTH = 2
DEC_BATCH = 4
DEC_SEQ = 2048
PAST_LEN = 256

GRID_W = 64
N_HEADS = 8
N_KV_HEADS = 2
HEAD_DIM = D_MODEL // 16
ATTN_W = N_HEADS * HEAD_DIM
KV_W = N_KV_HEADS * HEAD_DIM
CONV_W = D_MODEL // 4
POOL_W = D_MODEL // 4
POOL_WINDOWS = (2, 4, 8, 16)
POOL_GROUP = POOL_W // len(POOL_WINDOWS)
MIX_W = ATTN_W + CONV_W + POOL_W
SPLIT_SIZES = (ATTN_W, KV_W, KV_W, ATTN_W, CONV_W, CONV_W, CONV_W, CONV_W, POOL_W, POOL_W)
IN_W = sum(SPLIT_SIZES)
SPLIT_IDX = [int(i) for i in np.cumsum(SPLIT_SIZES)[:-1]]
CONV_K = 3
ROPE_THETA = 10000.0
Q_BLOCK = 128
EPS = 1e-6

kernel_name = "hybrid_dit_parallel_heads_step"


def _rmsnorm(x, g):
    xf = x.astype(jnp.float32)
    y = xf * lax.rsqrt(jnp.mean(xf * xf, axis=-1, keepdims=True) + EPS)
    return (y * g.astype(jnp.float32)).astype(x.dtype)


def _rope_angles(n):
    rows = n // GRID_W
    row = jnp.repeat(jnp.arange(rows), GRID_W).astype(jnp.float32)
    col = jnp.tile(jnp.arange(GRID_W), rows).astype(jnp.float32)
    half = HEAD_DIM // 2
    inv = 1.0 / (ROPE_THETA ** (jnp.arange(0, half, 2, dtype=jnp.float32) / half))
    return row[:, None] * inv, col[:, None] * inv


def _rot(x, ang):
    xf = x.astype(jnp.float32)
    x1, x2 = jnp.split(xf, 2, axis=-1)
    cs = jnp.cos(ang)[None, :, None, :]
    sn = jnp.sin(ang)[None, :, None, :]
    return jnp.concatenate([x1 * cs - x2 * sn, x2 * cs + x1 * sn], axis=-1).astype(x.dtype)


def _apply_rope(x, angs):
    half = HEAD_DIM // 2
    return jnp.concatenate([_rot(x[..., :half], angs[0]), _rot(x[..., half:], angs[1])], axis=-1)


def _attend(q, k, v):
    B, S, H, D = q.shape
    KV = k.shape[2]
    G = H // KV
    nb = S // Q_BLOCK
    qb = q.reshape(B, nb, Q_BLOCK, KV, G, D).transpose(1, 0, 2, 3, 4, 5)
    scale = HEAD_DIM ** -0.5

    def one(qblk):
        s = jnp.einsum('bqkgd,btkd->bkgqt', qblk, k).astype(jnp.float32) * scale
        p = jax.nn.softmax(s, axis=-1)
        return jnp.einsum('bkgqt,btkd->bqkgd', p.astype(v.dtype), v)

    o = lax.map(one, qb)
    return o.transpose(1, 0, 2, 3, 4, 5).reshape(B, S, H * D)


def _short_conv(x, w, b):
    n = x.shape[1]
    xp = jnp.pad(x, ((0, 0), (1, 1), (0, 0)))
    return xp[:, 0:n] * w[0] + xp[:, 1:n + 1] * w[1] + xp[:, 2:n + 2] * w[2] + b


def _pool_mix(u, w_pool, scale):
    n = u.shape[1]
    uf = u.astype(jnp.float32)
    cs = jnp.concatenate([jnp.zeros_like(uf[:, :1]), jnp.cumsum(uf, axis=1)], axis=1)
    t = jnp.arange(n)
    outs = []
    for gi, win in enumerate(POOL_WINDOWS):
        sl = slice(gi * POOL_GROUP, (gi + 1) * POOL_GROUP)
        lo = jnp.clip(t - win // 2, 0, n - 1)
        hi = jnp.clip(t - win // 2 + win - 1, 0, n - 1)
        cnt = (hi - lo + 1).astype(jnp.float32)
        d = (cs[:, hi + 1, sl] - cs[:, lo, sl]) / cnt[None, :, None] - uf[:, :, sl]
        outs.append(jnp.einsum('bnc,cd->bnd', d.astype(u.dtype), w_pool[gi]))
    return jnp.concatenate(outs, axis=-1) * scale


def _layer(x, mod, norm_g, w_in, q_g, k_g, conv_w, conv_b, pool_w, pool_scale, w_out,
           angs, ctx_k, ctx_v):
    B, n, _ = x.shape
    shift, scl, gate = jnp.split(mod, 3, axis=-1)
    xn = _rmsnorm(x, norm_g) * (1 + scl) + shift
    proj = xn @ w_in
    q, k, v, z_a, h_c, b_c, c_c, z_c, u_p, z_p = jnp.split(proj, SPLIT_IDX, axis=-1)
    q = _rmsnorm(q.reshape(B, n, N_HEADS, HEAD_DIM), q_g)
    k = _rmsnorm(k.reshape(B, n, N_KV_HEADS, HEAD_DIM), k_g)
    v = v.reshape(B, n, N_KV_HEADS, HEAD_DIM)
    if angs is None:
        k_all, v_all = k, v
    else:
        q = _apply_rope(q, angs)
        k_all = jnp.concatenate([_apply_rope(k, angs), ctx_k], axis=1)
        v_all = jnp.concatenate([v, ctx_v], axis=1)
    attn = _attend(q, k_all, v_all) * jax.nn.silu(z_a)
    conv = b_c * _short_conv(c_c * h_c, conv_w, conv_b) * jax.nn.silu(z_c)
    pool = _pool_mix(u_p, pool_w, pool_scale) * jax.nn.silu(z_p)
    out = jnp.concatenate([attn, conv, pool], axis=-1) @ w_out
    return x + gate * out, k, v


def setup_inputs(seed: int = 0) -> dict:
    key = jax.random.key(seed)
    ks = jax.random.split(key, 20)
    f32 = jnp.float32
    nrm = lambda k, shape, s: jax.random.normal(k, shape, f32) * s
    return {
        "x_prompt": nrm(ks[0], (BATCH, SEQ, D_MODEL), 1.0),
        "x_sample": nrm(ks[1], (DEC_BATCH, DEC_SEQ, D_MODEL), 1.0),
        "cache_k": nrm(ks[2], (DEC_BATCH, DEPTH, PAST_LEN, N_KV_HEADS, HEAD_DIM), 1.0),
        "cache_v": nrm(ks[3], (DEC_BATCH, DEPTH, PAST_LEN, N_KV_HEADS, HEAD_DIM), 1.0),
        "c": nrm(ks[4], (DEC_BATCH, D_MODEL), 1.0),
        "c_ctx": nrm(ks[5], (D_MODEL,), 1.0),
        "norm_g": 1.0 + nrm(ks[6], (DEPTH, D_MODEL), 0.05),
        "w_ada": nrm(ks[7], (DEPTH, D_MODEL, 3 * D_MODEL), 0.5 * D_MODEL ** -0.5),
        "b_ada": nrm(ks[8], (DEPTH, 3 * D_MODEL), 0.02),
        "w_in": nrm(ks[9], (DEPTH, D_MODEL, IN_W), D_MODEL ** -0.5),
        "q_norm_g": 1.0 + nrm(ks[10], (DEPTH, HEAD_DIM), 0.05),
        "k_norm_g": 1.0 + nrm(ks[11], (DEPTH, HEAD_DIM), 0.05),
        "conv_w": nrm(ks[12], (DEPTH, CONV_K, CONV_W), CONV_K ** -0.5),
        "conv_b": nrm(ks[13], (DEPTH, CONV_W), 0.02),
        "pool_w": nrm(ks[14], (DEPTH, len(POOL_WINDOWS), POOL_GROUP, POOL_GROUP), POOL_GROUP ** -0.5),
        "pool_scale": 1.0 + nrm(ks[15], (DEPTH, POOL_W), 0.1),
        "w_out": nrm(ks[16], (DEPTH, MIX_W, D_MODEL), MIX_W ** -0.5),
        "final_g": 1.0 + nrm(ks[17], (D_MODEL,), 0.05),
    }


def reference(x_prompt, x_sample, cache_k, cache_v, c, c_ctx, norm_g, w_ada, b_ada, w_in,
              q_norm_g, k_norm_g, conv_w, conv_b, pool_w, pool_scale, w_out, final_g):
    h = x_prompt
    new_ks, new_vs = [], []
    for l in range(DEPTH):
        mod = (jax.nn.silu(c_ctx) @ w_ada[l] + b_ada[l])[None, None, :]
        h, k_l, v_l = _layer(h, mod, norm_g[l], w_in[l], q_norm_g[l], k_norm_g[l], conv_w[l],
                             conv_b[l], pool_w[l], pool_scale[l], w_out[l], None, None, None)
        new_ks.append(k_l)
        new_vs.append(v_l)
    y_prompt = _rmsnorm(h, final_g)
    new_k = jnp.stack(new_ks, axis=1)
    new_v = jnp.stack(new_vs, axis=1)

    angs = _rope_angles(x_sample.shape[1])
    z = x_sample
    for l in range(DEPTH):
        mod = (jax.nn.silu(c) @ w_ada[l] + b_ada[l])[:, None, :]
        z, _, _ = _layer(z, mod, norm_g[l], w_in[l], q_norm_g[l], k_norm_g[l], conv_w[l],
                         conv_b[l], pool_w[l], pool_scale[l], w_out[l], angs,
                         cache_k[:, l], cache_v[:, l])
    y_sample = _rmsnorm(z, final_g)
    return (y_prompt, y_sample, new_k, new_v)
```

```python
import functools

import jax
import jax.numpy as jnp
from jax import lax
from jax.experimental import pallas as pl
from jax.experimental.pallas import tpu as pltpu

D_MODEL = 1024
GRID_W = 64
N_HEADS = 8
N_KV_HEADS = 2
HEAD_DIM = 64
ATTN_W = N_HEADS * HEAD_DIM
KV_W = N_KV_HEADS * HEAD_DIM
QK_W = ATTN_W + KV_W
CONV_W = 256
POOL_W = 256
POOL_GROUP = 64
IN_W = 2816
ROPE_THETA = 10000.0
EPS = 1e-6
LANES = 128
HALO = 8
ADA_ROWS = 8

F32 = jnp.float32
BF16 = jnp.bfloat16


def _silu(z):
    return z * (1.0 / (1.0 + jnp.exp(-z)))


def _dot(a, b):
    return jnp.dot(a, b, preferred_element_type=F32)


def _dot_nt(a, b):
    return lax.dot_general(a, b, (((1,), (1,)), ((), ())), preferred_element_type=F32)


def _ada_kernel(c_ref, w_ref, b_ref, o_ref):
    s = _silu(c_ref[...]).astype(BF16)
    o_ref[0] = _dot(s, w_ref[0].astype(BF16)) + b_ref[0]


def _ada(cvec, w_ada, b_ada):
    depth, _, n3 = w_ada.shape
    tn = 1024
    return pl.pallas_call(
        _ada_kernel,
        grid=(depth, n3 // tn),
        in_specs=[
            pl.BlockSpec((ADA_ROWS, D_MODEL), lambda l, j: (0, 0)),
            pl.BlockSpec((1, D_MODEL, tn), lambda l, j: (l, 0, j)),
            pl.BlockSpec((1, 1, tn), lambda l, j: (l, 0, j)),
        ],
        out_specs=pl.BlockSpec((1, ADA_ROWS, tn), lambda l, j: (l, 0, j)),
        out_shape=jax.ShapeDtypeStruct((depth, ADA_ROWS, n3), F32),
        name="ada_mod",
    )(cvec, w_ada, b_ada.reshape(depth, 1, n3))


def _inproj_kernel(*refs, rope):
    if rope:
        (x_ref, mod_ref, g_ref, w_ref, gains_ref, bd_ref, cos_ref, sa_ref, sb_ref,
         q_ref, k_ref, v_ref, za_ref, cv_ref, pl_ref) = refs
    else:
        (x_ref, mod_ref, g_ref, w_ref, gains_ref, bd_ref,
         q_ref, k_ref, v_ref, za_ref, cv_ref, pl_ref) = refs
    x = x_ref[...]
    ms = jnp.mean(x * x, axis=-1, keepdims=True)
    y = x * lax.rsqrt(ms + EPS) * g_ref[...]
    xn = (y * (1.0 + mod_ref[0, 1:2, :]) + mod_ref[0, 0:1, :]).astype(BF16)

    bd = bd_ref[...]
    for j in range(QK_W // LANES):
        sl = slice(j * LANES, (j + 1) * LANES)
        t = _dot(xn, w_ref[:, sl])
        sq = t * t
        hi = sq.astype(BF16)
        lo = (sq - hi.astype(F32)).astype(BF16)
        msq = _dot(hi, bd) + _dot(lo, bd)
        t = t * lax.rsqrt(msq + EPS) * gains_ref[:, sl]
        if rope:
            t = (t * cos_ref[...] + pltpu.roll(t, LANES - 16, 1) * sa_ref[...]
                 + pltpu.roll(t, 16, 1) * sb_ref[...])
        if j < ATTN_W // LANES:
            q_ref[:, sl] = (t * (HEAD_DIM ** -0.5)).astype(BF16)
        else:
            k_ref[...] = t
    v_ref[...] = _dot(xn, w_ref[:, QK_W:QK_W + KV_W])
    o = QK_W + KV_W
    za_ref[...] = _dot(xn, w_ref[:, o:o + ATTN_W])
    o += ATTN_W
    cv_ref[...] = _dot(xn, w_ref[:, o:o + 4 * CONV_W])
    o += 4 * CONV_W
    pl_ref[...] = _dot(xn, w_ref[:, o:o + 2 * POOL_W])


def _inproj(x2, mod, norm_g, w_in_bf, gains, bd, tables, *, seq, tm):
    tokens = x2.shape[0]
    tps = seq // tm
    shared = mod.shape[0] == 1
    rope = tables is not None
    row = lambda i: (i, 0)
    const = lambda i: (0, 0)
    in_specs = [
        pl.BlockSpec((tm, D_MODEL), row),
        pl.BlockSpec((1, 3, D_MODEL), (lambda i: (0, 0, 0)) if shared else (lambda i: (i // tps, 0, 0))),
        pl.BlockSpec((1, D_MODEL), const),
        pl.BlockSpec((D_MODEL, IN_W), const),
        pl.BlockSpec((1, QK_W), const),
        pl.BlockSpec((LANES, LANES), const),
    ]
    args = [x2, mod, norm_g.reshape(1, D_MODEL), w_in_bf, gains, bd]
    if rope:
        in_specs += [pl.BlockSpec((tm, LANES), lambda i: (i % tps, 0))] * 3
        args += list(tables)
    out_shapes = [
        jax.ShapeDtypeStruct((tokens, ATTN_W), BF16),
        jax.ShapeDtypeStruct((tokens, KV_W), F32),
        jax.ShapeDtypeStruct((tokens, KV_W), F32),
        jax.ShapeDtypeStruct((tokens, ATTN_W), F32),
        jax.ShapeDtypeStruct((tokens, 4 * CONV_W), F32),
        jax.ShapeDtypeStruct((tokens, 2 * POOL_W), F32),
    ]
    out_specs = [pl.BlockSpec((tm, s.shape[1]), row) for s in out_shapes]
    return pl.pallas_call(
        functools.partial(_inproj_kernel, rope=rope),
        grid=(tokens // tm,),
        in_specs=in_specs,
        out_specs=out_specs,
        out_shape=out_shapes,
        compiler_params=pltpu.CompilerParams(
            dimension_semantics=("arbitrary",), vmem_limit_bytes=48 * 1024 * 1024),
        name="inproj_rope" if rope else "inproj",
    )(*args)


def _attn_kernel(*refs, tq, n_lat, n_ctx):
    if n_ctx:
        q_ref, k_ref, v_ref, ck_ref, cv_ref, za_ref, o_ref, kab, vab = refs
    else:
        q_ref, k_ref, v_ref, za_ref, o_ref, kab, vab = refs
    kv = pl.program_id(1)
    iq = pl.program_id(2)

    @pl.when(iq == 0)
    def _build():
        def fill(k_src, v_src, r0, n):
            lane = lax.broadcasted_iota(jnp.int32, (n, LANES), 1)
            own = (lane // HEAD_DIM) == kv
            low = lane < HEAD_DIM
            k_own = jnp.where(own, k_src, 0.0)
            k_oth = pltpu.roll(k_own, HEAD_DIM, 1)
            v_own = jnp.where(own, v_src, 0.0)
            v_oth = pltpu.roll(v_own, HEAD_DIM, 1)
            is0 = kv == 0
            kab[0, r0:r0 + n, :] = jnp.where(is0, k_own, k_oth).astype(BF16)
            kab[1, r0:r0 + n, :] = jnp.where(is0, k_oth, k_own).astype(BF16)
            vab[0, r0:r0 + n, 0:LANES] = jnp.where(is0, v_own, v_oth).astype(BF16)
            vab[1, r0:r0 + n, 0:LANES] = jnp.where(is0, v_oth, v_own).astype(BF16)
            vab[0, r0:r0 + n, LANES:] = jnp.where(low, 1.0, 0.0).astype(BF16)
            vab[1, r0:r0 + n, LANES:] = jnp.where(low, 0.0, 1.0).astype(BF16)

        fill(k_ref[0], v_ref[0], 0, n_lat)
        if n_ctx:
            fill(ck_ref[0, 0], cv_ref[0, 0], n_lat, n_ctx)

    q2 = jnp.concatenate([q_ref[0, :, 0:LANES], q_ref[0, :, LANES:]], axis=0)
    acc = None
    for par in range(2):
        s = _dot_nt(q2, kab[par])
        m = jnp.max(s, axis=-1, keepdims=True)
        p = jnp.exp(s - m).astype(BF16)
        pv = _dot(p, vab[par])
        acc = pv if acc is None else acc + pv
    o = acc[:, 0:LANES] / acc[:, LANES:]
    za = za_ref[0]
    o_ref[0, :, 0:LANES] = (o[0:tq] * _silu(za[:, 0:LANES])).astype(BF16)
    o_ref[0, :, LANES:] = (o[tq:] * _silu(za[:, LANES:])).astype(BF16)


def _attention(q, k, v, za, cache_k, cache_v, layer, *, tq):
    bsz, n, _ = q.shape
    n_ctx = 0 if cache_k is None else cache_k.shape[2]
    t_all = n + n_ctx
    gw = 2 * LANES
    qmap = lambda b, g, i: (b, i, g)
    kmap = lambda b, g, i: (b, 0, 0)
    in_specs = [
        pl.BlockSpec((1, tq, gw), qmap),
        pl.BlockSpec((1, n, KV_W), kmap),
        pl.BlockSpec((1, n, KV_W), kmap),
    ]
    args = [q, k, v]
    if n_ctx:
        cmap = lambda b, g, i: (b, layer, 0, 0)
        in_specs += [pl.BlockSpec((1, 1, n_ctx, KV_W), cmap)] * 2
        args += [cache_k, cache_v]
    in_specs.append(pl.BlockSpec((1, tq, gw), qmap))
    args.append(za)
    return pl.pallas_call(
        functools.partial(_attn_kernel, tq=tq, n_lat=n, n_ctx=n_ctx),
        grid=(bsz, N_KV_HEADS, n // tq),
        in_specs=in_specs,
        out_specs=pl.BlockSpec((1, tq, gw), qmap),
        out_shape=jax.ShapeDtypeStruct((bsz, n, ATTN_W), BF16),
        scratch_shapes=[
            pltpu.VMEM((2, t_all, LANES), BF16),
            pltpu.VMEM((2, t_all, 2 * LANES), BF16),
        ],
        compiler_params=pltpu.CompilerParams(
            dimension_semantics=("arbitrary", "arbitrary", "arbitrary"),
            vmem_limit_bytes=48 * 1024 * 1024),
        name="attn_lat" if n_ctx else "attn_ctx",
    )(*args)


def _mix_kernel(attn_ref, cv_ref, hp_ref, cp_ref, hn_ref, cn_ref, pl_ref, up_ref, un_ref,
                x_ref, mod_ref, wout_ref, convw_ref, convb_ref, wpool_ref, pscale_ref, fg_ref,
                o_ref, yext, uext, *, tm, tps, seq, final):
    j = pl.program_id(0) % tps
    pm = jnp.where(j == 0, 0.0, 1.0)
    nm = jnp.where(j == tps - 1, 0.0, 1.0)
    c = CONV_W
    h_c = cv_ref[:, 0:c]
    b_c = cv_ref[:, c:2 * c]
    c_c = cv_ref[:, 2 * c:3 * c]
    z_c = cv_ref[:, 3 * c:4 * c]

    yext[0:HALO, :] = hp_ref[...] * cp_ref[...] * pm
    yext[HALO:HALO + tm, :] = c_c * h_c
    yext[HALO + tm:, :] = hn_ref[...] * cn_ref[...] * nm
    conv = (yext[HALO - 1:HALO - 1 + tm, :] * convw_ref[0:1, :]
            + yext[HALO:HALO + tm, :] * convw_ref[1:2, :]
            + yext[HALO + 1:HALO + 1 + tm, :] * convw_ref[2:3, :] + convb_ref[...])
    conv_o = (b_c * conv * _silu(z_c)).astype(BF16)

    u = pl_ref[:, 0:POOL_W]
    z_p = pl_ref[:, POOL_W:]
    uext[0:HALO, :] = up_ref[...] * pm
    uext[HALO:HALO + tm, :] = u
    uext[HALO + tm:, :] = un_ref[...] * nm

    def win(k0, k1, sl):
        tot = None
        for k in range(k0, k1):
            t = uext[HALO + k:HALO + k + tm, sl]
            tot = t if tot is None else tot + t
        return tot

    lane = lax.broadcasted_iota(jnp.int32, (tm, LANES), 1)
    pos = j * tm + lax.broadcasted_iota(jnp.int32, (tm, LANES), 0)
    low = lane < POOL_GROUP
    ds = []
    for ti, (wa, wb) in enumerate(((2, 4), (8, 16))):
        sl = slice(ti * LANES, (ti + 1) * LANES)
        s_a = win(-(wa // 2), wa // 2, sl)
        s_b = s_a + win(-(wb // 2), -(wa // 2), sl) + win(wa // 2, wb // 2, sl)
        half = jnp.where(low, wa // 2, wb // 2)
        cnt = jnp.minimum(pos + half - 1, seq - 1) - jnp.maximum(pos - half, 0) + 1
        tot = jnp.where(low, s_a, s_b)
        ds.append(tot / cnt.astype(F32) - u[:, sl])
    d = jnp.concatenate(ds, axis=-1).astype(BF16)
    pool_o = (_dot(d, wpool_ref[...]) * pscale_ref[...] * _silu(z_p)).astype(BF16)

    a = ATTN_W
    out = (_dot(attn_ref[...], wout_ref[0:a, :]) + _dot(conv_o, wout_ref[a:a + c, :])
           + _dot(pool_o, wout_ref[a + c:, :]))
    res = x_ref[...] + mod_ref[0, 2:3, :] * out
    if final:
        ms = jnp.mean(res * res, axis=-1, keepdims=True)
        res = res * lax.rsqrt(ms + EPS) * fg_ref[...]
    o_ref[...] = res


def _mix(attn2, cvp, plp, x2, mod, wout_bf, conv_w, conv_b, wpool_bd, pool_scale, final_g,
         *, seq, tm, final):
    tokens = x2.shape[0]
    tps = seq // tm
    shared = mod.shape[0] == 1
    hb = tm // HALO
    nhb = tokens // HALO
    row = lambda i: (i, 0)
    const = lambda i: (0, 0)
    prev = lambda col: (lambda i: (jnp.maximum(i * hb - 1, 0), col))
    nxt = lambda col: (lambda i: (jnp.minimum((i + 1) * hb, nhb - 1), col))
    in_specs = [
        pl.BlockSpec((tm, ATTN_W), row),
        pl.BlockSpec((tm, 4 * CONV_W), row),
        pl.BlockSpec((HALO, CONV_W), prev(0)),
        pl.BlockSpec((HALO, CONV_W), prev(2)),
        pl.BlockSpec((HALO, CONV_W), nxt(0)),
        pl.BlockSpec((HALO, CONV_W), nxt(2)),
        pl.BlockSpec((tm, 2 * POOL_W), row),
        pl.BlockSpec((HALO, POOL_W), prev(0)),
        pl.BlockSpec((HALO, POOL_W), nxt(0)),
        pl.BlockSpec((tm, D_MODEL), row),
        pl.BlockSpec((1, 3, D_MODEL), (lambda i: (0, 0, 0)) if shared else (lambda i: (i // tps, 0, 0))),
        pl.BlockSpec((D_MODEL, D_MODEL), const),
        pl.BlockSpec((3, CONV_W), const),
        pl.BlockSpec((1, CONV_W), const),
        pl.BlockSpec((POOL_W, POOL_W), const),
        pl.BlockSpec((1, POOL_W), const),
        pl.BlockSpec((1, D_MODEL), const),
    ]
    return pl.pallas_call(
        functools.partial(_mix_kernel, tm=tm, tps=tps, seq=seq, final=final),
        grid=(tokens // tm,),
        in_specs=in_specs,
        out_specs=pl.BlockSpec((tm, D_MODEL), row),
        out_shape=jax.ShapeDtypeStruct((tokens, D_MODEL), F32),
        scratch_shapes=[
            pltpu.VMEM((tm + 2 * HALO, CONV_W), F32),
            pltpu.VMEM((tm + 2 * HALO, POOL_W), F32),
        ],
        compiler_params=pltpu.CompilerParams(
            dimension_semantics=("arbitrary",), vmem_limit_bytes=48 * 1024 * 1024),
        name="mix_final" if final else "mix",
    )(attn2, cvp, cvp, cvp, cvp, cvp, plp, plp, plp, x2, mod, wout_bf,
      conv_w, conv_b.reshape(1, CONV_W), wpool_bd, pool_scale.reshape(1, POOL_W),
      final_g.reshape(1, D_MODEL))


def _rope_tables(n):
    rows = n // GRID_W
    row = jnp.repeat(jnp.arange(rows), GRID_W).astype(F32)
    col = jnp.tile(jnp.arange(GRID_W), rows).astype(F32)
    half = HEAD_DIM // 2
    inv = 1.0 / (ROPE_THETA ** (jnp.arange(0, half, 2, dtype=F32) / half))
    ar = row[:, None] * inv
    ac = col[:, None] * inv
    ang = jnp.tile(jnp.concatenate([ar, ar, ac, ac], axis=-1), (1, LANES // HEAD_DIM))
    cos = jnp.cos(ang)
    sin = jnp.sin(ang)
    second = (jnp.arange(LANES) % half) >= (half // 2)
    return cos, jnp.where(second, 0.0, -sin), jnp.where(second, sin, 0.0)


def _block_diag(blocks):
    n = len(blocks)
    rows = []
    for i, b in enumerate(blocks):
        rows.append(jnp.concatenate(
            [b if j == i else jnp.zeros((b.shape[0], blocks[j].shape[1]), b.dtype) for j in range(n)],
            axis=1))
    return jnp.concatenate(rows, axis=0)


def kernel(x_prompt, x_sample, cache_k, cache_v, c, c_ctx, norm_g, w_ada, b_ada, w_in,
           q_norm_g, k_norm_g, conv_w, conv_b, pool_w, pool_scale, w_out, final_g):
    depth = w_in.shape[0]
    bp, sp, _ = x_prompt.shape
    bs, ss, _ = x_sample.shape
    n_ctx = cache_k.shape[2]

    cvec = jnp.concatenate(
        [c, c_ctx[None, :], jnp.zeros((ADA_ROWS - bs - 1, D_MODEL), F32)], axis=0)
    mod = _ada(cvec, w_ada, b_ada)
    mod = mod.reshape(depth, ADA_ROWS, 3, D_MODEL)

    w_in_bf = w_in.astype(BF16)
    w_out_bf = w_out.astype(BF16)
    tables = _rope_tables(ss)
    ones = jnp.full((HEAD_DIM, HEAD_DIM), 1.0 / HEAD_DIM, BF16)
    bd = _block_diag([ones, ones])
    ck = cache_k.reshape(bs, depth, n_ctx, KV_W)
    cv = cache_v.reshape(bs, depth, n_ctx, KV_W)

    h = x_prompt.reshape(bp * sp, D_MODEL)
    z = x_sample.reshape(bs * ss, D_MODEL)
    new_ks, new_vs = [], []
    for l in range(depth):
        gains = jnp.concatenate(
            [jnp.tile(q_norm_g[l], N_HEADS), jnp.tile(k_norm_g[l], N_KV_HEADS)]).reshape(1, QK_W)
        wpool_bd = _block_diag([pool_w[l, g] for g in range(pool_w.shape[1])]).astype(BF16)
        final = l == depth - 1
        mod_ctx = mod[l, bs:bs + 1]
        mod_lat = mod[l, 0:bs]

        q, k, v, za, cvp, plp = _inproj(h, mod_ctx, norm_g[l], w_in_bf[l], gains, bd, None,
                                        seq=sp, tm=256)
        new_ks.append(k.reshape(bp, sp, N_KV_HEADS, HEAD_DIM))
        new_vs.append(v.reshape(bp, sp, N_KV_HEADS, HEAD_DIM))
        attn = _attention(q.reshape(bp, sp, ATTN_W), k.reshape(bp, sp, KV_W), v.reshape(bp, sp, KV_W),
                          za.reshape(bp, sp, ATTN_W), None, None, l, tq=256)
        h = _mix(attn.reshape(bp * sp, ATTN_W), cvp, plp, h, mod_ctx, w_out_bf[l], conv_w[l], conv_b[l],
                 wpool_bd, pool_scale[l], final_g, seq=sp, tm=256, final=final)

        q, k, v, za, cvp, plp = _inproj(z, mod_lat, norm_g[l], w_in_bf[l], gains, bd, tables,
                                        seq=ss, tm=256)
        attn = _attention(q.reshape(bs, ss, ATTN_W), k.reshape(bs, ss, KV_W), v.reshape(bs, ss, KV_W),
                          za.reshape(bs, ss, ATTN_W), ck, cv, l, tq=128)
        z = _mix(attn.reshape(bs * ss, ATTN_W), cvp, plp, z, mod_lat, w_out_bf[l], conv_w[l], conv_b[l],
                 wpool_bd, pool_scale[l], final_g, seq=ss, tm=256, final=final)

    y_prompt = h.reshape(bp, sp, D_MODEL)
    y_sample = z.reshape(bs, ss, D_MODEL)
    return (y_prompt, y_sample, jnp.stack(new_ks, axis=1), jnp.stack(new_vs, axis=1))
```

```python
import functools

import jax
import jax.numpy as jnp
from jax import lax
from jax.experimental import pallas as pl
from jax.experimental.pallas import tpu as pltpu

D_MODEL = 1024
GRID_W = 64
N_HEADS = 8
N_KV_HEADS = 2
HEAD_DIM = 64
ATTN_W = N_HEADS * HEAD_DIM
KV_W = N_KV_HEADS * HEAD_DIM
QK_W = ATTN_W + KV_W
CONV_W = 256
POOL_W = 256
POOL_GROUP = 64
IN_W = 2816
ROPE_THETA = 10000.0
EPS = 1e-6
LANES = 128
HALO = 8
ADA_ROWS = 8

F32 = jnp.float32
BF16 = jnp.bfloat16


def _silu(z):
    return z * (1.0 / (1.0 + jnp.exp(-z)))


def _dot(a, b):
    return jnp.dot(a, b, preferred_element_type=F32)


def _dot_nt(a, b):
    return lax.dot_general(a, b, (((1,), (1,)), ((), ())), preferred_element_type=F32)


def _ada_kernel(c_ref, w_ref, b_ref, o_ref):
    s = _silu(c_ref[...]).astype(BF16)
    o_ref[0] = _dot(s, w_ref[0].astype(BF16)) + b_ref[0]


def _ada(cvec, w_ada, b_ada):
    depth, _, n3 = w_ada.shape
    tn = 1024
    return pl.pallas_call(
        _ada_kernel,
        grid=(depth, n3 // tn),
        in_specs=[
            pl.BlockSpec((ADA_ROWS, D_MODEL), lambda l, j: (0, 0)),
            pl.BlockSpec((1, D_MODEL, tn), lambda l, j: (l, 0, j)),
            pl.BlockSpec((1, 1, tn), lambda l, j: (l, 0, j)),
        ],
        out_specs=pl.BlockSpec((1, ADA_ROWS, tn), lambda l, j: (l, 0, j)),
        out_shape=jax.ShapeDtypeStruct((depth, ADA_ROWS, n3), F32),
        name="ada_mod",
    )(cvec, w_ada, b_ada.reshape(depth, 1, n3))


def _inproj_kernel(*refs, rope):
    if rope:
        (x_ref, mod_ref, g_ref, w_ref, gains_ref, bd_ref, cos_ref, sa_ref, sb_ref,
         q_ref, k_ref, v_ref, za_ref, cv_ref, pl_ref) = refs
    else:
        (x_ref, mod_ref, g_ref, w_ref, gains_ref, bd_ref,
         q_ref, k_ref, v_ref, za_ref, cv_ref, pl_ref) = refs
    x = x_ref[...]
    ms = jnp.mean(x * x, axis=-1, keepdims=True)
    y = x * lax.rsqrt(ms + EPS) * g_ref[...]
    xn = (y * (1.0 + mod_ref[0, 1:2, :]) + mod_ref[0, 0:1, :]).astype(BF16)

    bd = bd_ref[...]
    for j in range(QK_W // LANES):
        sl = slice(j * LANES, (j + 1) * LANES)
        t = _dot(xn, w_ref[:, sl])
        sq = t * t
        hi = sq.astype(BF16)
        lo = (sq - hi.astype(F32)).astype(BF16)
        msq = _dot(hi, bd) + _dot(lo, bd)
        t = t * lax.rsqrt(msq + EPS) * gains_ref[:, sl]
        if rope:
            t = (t * cos_ref[...] + pltpu.roll(t, LANES - 16, 1) * sa_ref[...]
                 + pltpu.roll(t, 16, 1) * sb_ref[...])
        if j < ATTN_W // LANES:
            q_ref[:, sl] = (t * (HEAD_DIM ** -0.5)).astype(BF16)
        else:
            k_ref[...] = t
    v_ref[...] = _dot(xn, w_ref[:, QK_W:QK_W + KV_W])
    o = QK_W + KV_W
    za_ref[...] = _dot(xn, w_ref[:, o:o + ATTN_W])
    o += ATTN_W
    cv_ref[...] = _dot(xn, w_ref[:, o:o + 4 * CONV_W])
    o += 4 * CONV_W
    pl_ref[...] = _dot(xn, w_ref[:, o:o + 2 * POOL_W])


def _inproj(x2, mod, norm_g, w_in_bf, gains, bd, tables, *, seq, tm):
    tokens = x2.shape[0]
    tps = seq // tm
    shared = mod.shape[0] == 1
    rope = tables is not None
    row = lambda i: (i, 0)
    const = lambda i: (0, 0)
    in_specs = [
        pl.BlockSpec((tm, D_MODEL), row),
        pl.BlockSpec((1, 3, D_MODEL), (lambda i: (0, 0, 0)) if shared else (lambda i: (i // tps, 0, 0))),
        pl.BlockSpec((1, D_MODEL), const),
        pl.BlockSpec((D_MODEL, IN_W), const),
        pl.BlockSpec((1, QK_W), const),
        pl.BlockSpec((LANES, LANES), const),
    ]
    args = [x2, mod, norm_g.reshape(1, D_MODEL), w_in_bf, gains, bd]
    if rope:
        in_specs += [pl.BlockSpec((tm, LANES), lambda i: (i % tps, 0))] * 3
        args += list(tables)
    out_shapes = [
        jax.ShapeDtypeStruct((tokens, ATTN_W), BF16),
        jax.ShapeDtypeStruct((tokens, KV_W), F32),
        jax.ShapeDtypeStruct((tokens, KV_W), F32),
        jax.ShapeDtypeStruct((tokens, ATTN_W), F32),
        jax.ShapeDtypeStruct((tokens, 4 * CONV_W), F32),
        jax.ShapeDtypeStruct((tokens, 2 * POOL_W), F32),
    ]
    out_specs = [pl.BlockSpec((tm, s.shape[1]), row) for s in out_shapes]
    return pl.pallas_call(
        functools.partial(_inproj_kernel, rope=rope),
        grid=(tokens // tm,),
        in_specs=in_specs,
        out_specs=out_specs,
        out_shape=out_shapes,
        compiler_params=pltpu.CompilerParams(
            dimension_semantics=("arbitrary",), vmem_limit_bytes=48 * 1024 * 1024),
        name="inproj_rope" if rope else "inproj",
    )(*args)


def _attn_kernel(*refs, tq, n_lat, n_ctx):
    if n_ctx:
        q_ref, k_ref, v_ref, ck_ref, cv_ref, za_ref, o_ref, kab, vab = refs
    else:
        q_ref, k_ref, v_ref, za_ref, o_ref, kab, vab = refs
    kv = pl.program_id(1)
    iq = pl.program_id(2)

    @pl.when(iq == 0)
    def _build():
        def fill(k_src, v_src, r0, n):
            lane = lax.broadcasted_iota(jnp.int32, (n, LANES), 1)
            own = (lane // HEAD_DIM) == kv
            low = lane < HEAD_DIM
            k_own = jnp.where(own, k_src, 0.0)
            k_oth = pltpu.roll(k_own, HEAD_DIM, 1)
            v_own = jnp.where(own, v_src, 0.0)
            v_oth = pltpu.roll(v_own, HEAD_DIM, 1)
            is0 = kv == 0
            kab[0, r0:r0 + n, :] = jnp.where(is0, k_own, k_oth).astype(BF16)
            kab[1, r0:r0 + n, :] = jnp.where(is0, k_oth, k_own).astype(BF16)
            vab[0, r0:r0 + n, 0:LANES] = jnp.where(is0, v_own, v_oth).astype(BF16)
            vab[1, r0:r0 + n, 0:LANES] = jnp.where(is0, v_oth, v_own).astype(BF16)
            vab[0, r0:r0 + n, LANES:] = jnp.where(low, 1.0, 0.0).astype(BF16)
            vab[1, r0:r0 + n, LANES:] = jnp.where(low, 0.0, 1.0).astype(BF16)

        fill(k_ref[0], v_ref[0], 0, n_lat)
        if n_ctx:
            fill(ck_ref[0, 0], cv_ref[0, 0], n_lat, n_ctx)

    q2 = jnp.concatenate([q_ref[0, :, 0:LANES], q_ref[0, :, LANES:]], axis=0)
    acc = None
    for par in range(2):
        s = _dot_nt(q2, kab[par])
        m = jnp.max(s, axis=-1, keepdims=True)
        p = jnp.exp(s - m).astype(BF16)
        pv = _dot(p, vab[par])
        acc = pv if acc is None else acc + pv
    o = acc[:, 0:LANES] / acc[:, LANES:]
    za = za_ref[0]
    o_ref[0, :, 0:LANES] = (o[0:tq] * _silu(za[:, 0:LANES])).astype(BF16)
    o_ref[0, :, LANES:] = (o[tq:] * _silu(za[:, LANES:])).astype(BF16)


def _attention(q, k, v, za, cache_k, cache_v, layer, *, tq):
    bsz, n, _ = q.shape
    n_ctx = 0 if cache_k is None else cache_k.shape[2]
    t_all = n + n_ctx
    gw = 2 * LANES
    qmap = lambda b, g, i: (b, i, g)
    kmap = lambda b, g, i: (b, 0, 0)
    in_specs = [
        pl.BlockSpec((1, tq, gw), qmap),
        pl.BlockSpec((1, n, KV_W), kmap),
        pl.BlockSpec((1, n, KV_W), kmap),
    ]
    args = [q, k, v]
    if n_ctx:
        cmap = lambda b, g, i: (b, layer, 0, 0)
        in_specs += [pl.BlockSpec((1, 1, n_ctx, KV_W), cmap)] * 2
        args += [cache_k, cache_v]
    in_specs.append(pl.BlockSpec((1, tq, gw), qmap))
    args.append(za)
    return pl.pallas_call(
        functools.partial(_attn_kernel, tq=tq, n_lat=n, n_ctx=n_ctx),
        grid=(bsz, N_KV_HEADS, n // tq),
        in_specs=in_specs,
        out_specs=pl.BlockSpec((1, tq, gw), qmap),
        out_shape=jax.ShapeDtypeStruct((bsz, n, ATTN_W), BF16),
        scratch_shapes=[
            pltpu.VMEM((2, t_all, LANES), BF16),
            pltpu.VMEM((2, t_all, 2 * LANES), BF16),
        ],
        compiler_params=pltpu.CompilerParams(
            dimension_semantics=("arbitrary", "arbitrary", "arbitrary"),
            vmem_limit_bytes=48 * 1024 * 1024),
        name="attn_lat" if n_ctx else "attn_ctx",
    )(*args)


def _attn_lat_kernel(q_ref, k_ref, v_ref, ck_ref, cv_ref, za_ref, o_ref,
                     kt, vab, q2, acc, s0, s1, p0, p1, *, n, n_ctx, rb):
    kv = pl.program_id(1)
    nr = 2 * n // rb
    t_all = n + n_ctx
    sbuf = (s0, s1)
    pbuf = (p0, p1)

    def fill(k_src, v_src, r0, m):
        lane = lax.broadcasted_iota(jnp.int32, (m, LANES), 1)
        own = (lane // HEAD_DIM) == kv
        low = lane < HEAD_DIM
        k_own = jnp.where(own, k_src, 0.0)
        k_oth = pltpu.roll(k_own, HEAD_DIM, 1)
        v_own = jnp.where(own, v_src, 0.0)
        v_oth = pltpu.roll(v_own, HEAD_DIM, 1)
        is0 = kv == 0
        kt[0, :, r0:r0 + m] = jnp.where(is0, k_own, k_oth).T.astype(BF16)
        kt[1, :, r0:r0 + m] = jnp.where(is0, k_oth, k_own).T.astype(BF16)
        vab[0, r0:r0 + m, 0:LANES] = jnp.where(is0, v_own, v_oth).astype(BF16)
        vab[1, r0:r0 + m, 0:LANES] = jnp.where(is0, v_oth, v_own).astype(BF16)
        vab[0, r0:r0 + m, LANES:] = jnp.where(low, 1.0, 0.0).astype(BF16)
        vab[1, r0:r0 + m, LANES:] = jnp.where(low, 0.0, 1.0).astype(BF16)

    piece = 256
    for r0 in range(0, n, piece):
        fill(k_ref[0, r0:r0 + piece, :], v_ref[0, r0:r0 + piece, :], r0, piece)
    fill(ck_ref[0, 0], cv_ref[0, 0], n, n_ctx)
    q2[0:n, :] = q_ref[0, :, 0:LANES]
    q2[n:, :] = q_ref[0, :, LANES:]

    def scores(r, slot, dst):
        dst[...] = _dot(q2[pl.ds(pl.multiple_of(r * rb, rb), rb), :], kt[slot])

    def numer(src, dst):
        for i in range(0, rb, 16):
            s = src[i:i + 16, :]
            m = jnp.max(s, axis=-1, keepdims=True)
            dst[i:i + 16, :] = jnp.exp(s - m).astype(BF16)

    def weigh(src, slot, r, first):
        pv = _dot(src[...], vab[slot])
        rows = pl.ds(pl.multiple_of(r * rb, rb), rb)
        if first:
            acc[rows, :] = pv
        else:
            acc[rows, :] += pv

    p1[...] = jnp.zeros_like(p1)
    acc[nr * rb:, :] = jnp.zeros((rb, 2 * LANES), F32)
    scores(0, 0, s0)

    def body(r, carry):
        scores(r, 1, s1)
        numer(s0, p0)
        weigh(p1, 1, jnp.where(r == 0, nr, r - 1), False)
        scores(jnp.minimum(r + 1, nr - 1), 0, s0)
        numer(s1, p1)
        weigh(p0, 0, r, True)
        return carry

    lax.fori_loop(0, nr, body, 0)
    weigh(p1, 1, nr - 1, False)

    ch = 256
    for t in range(2):
        for r0 in range(0, n, ch):
            a = acc[t * n + r0:t * n + r0 + ch, :]
            o = a[:, 0:LANES] / a[:, LANES:]
            za = za_ref[0, r0:r0 + ch, t * LANES:(t + 1) * LANES]
            o_ref[0, r0:r0 + ch, t * LANES:(t + 1) * LANES] = (o * _silu(za)).astype(BF16)


def _attention_lat(q, k, v, za, cache_k, cache_v, layer, *, rb):
    bsz, n, _ = q.shape
    n_ctx = cache_k.shape[2]
    t_all = n + n_ctx
    gw = 2 * LANES
    qmap = lambda b, g: (b, 0, g)
    kmap = lambda b, g: (b, 0, 0)
    cmap = lambda b, g: (b, layer, 0, 0)
    return pl.pallas_call(
        functools.partial(_attn_lat_kernel, n=n, n_ctx=n_ctx, rb=rb),
        grid=(bsz, N_KV_HEADS),
        in_specs=[
            pl.BlockSpec((1, n, gw), qmap),
            pl.BlockSpec((1, n, KV_W), kmap),
            pl.BlockSpec((1, n, KV_W), kmap),
            pl.BlockSpec((1, 1, n_ctx, KV_W), cmap),
            pl.BlockSpec((1, 1, n_ctx, KV_W), cmap),
            pl.BlockSpec((1, n, gw), qmap),
        ],
        out_specs=pl.BlockSpec((1, n, gw), qmap),
        out_shape=jax.ShapeDtypeStruct((bsz, n, ATTN_W), BF16),
        scratch_shapes=[
            pltpu.VMEM((2, LANES, t_all), BF16),
            pltpu.VMEM((2, t_all, 2 * LANES), BF16),
            pltpu.VMEM((2 * n, LANES), BF16),
            pltpu.VMEM((2 * n + rb, 2 * LANES), F32),
            pltpu.VMEM((rb, t_all), F32),
            pltpu.VMEM((rb, t_all), F32),
            pltpu.VMEM((rb, t_all), BF16),
            pltpu.VMEM((rb, t_all), BF16),
        ],
        compiler_params=pltpu.CompilerParams(
            dimension_semantics=("arbitrary", "arbitrary"),
            vmem_limit_bytes=48 * 1024 * 1024),
        name="attn_lat",
    )(q, k, v, cache_k, cache_v, za)


def _mix_kernel(attn_ref, cv_ref, hp_ref, cp_ref, hn_ref, cn_ref, pl_ref, up_ref, un_ref,
                x_ref, mod_ref, wout_ref, convw_ref, convb_ref, wpool_ref, pscale_ref, fg_ref,
                o_ref, yext, uext, *, tm, tps, seq, final):
    j = pl.program_id(0) % tps
    pm = jnp.where(j == 0, 0.0, 1.0)
    nm = jnp.where(j == tps - 1, 0.0, 1.0)
    c = CONV_W
    h_c = cv_ref[:, 0:c]
    b_c = cv_ref[:, c:2 * c]
    c_c = cv_ref[:, 2 * c:3 * c]
    z_c = cv_ref[:, 3 * c:4 * c]

    yext[0:HALO, :] = hp_ref[...] * cp_ref[...] * pm
    yext[HALO:HALO + tm, :] = c_c * h_c
    yext[HALO + tm:, :] = hn_ref[...] * cn_ref[...] * nm
    conv = (yext[HALO - 1:HALO - 1 + tm, :] * convw_ref[0:1, :]
            + yext[HALO:HALO + tm, :] * convw_ref[1:2, :]
            + yext[HALO + 1:HALO + 1 + tm, :] * convw_ref[2:3, :] + convb_ref[...])
    conv_o = (b_c * conv * _silu(z_c)).astype(BF16)

    u = pl_ref[:, 0:POOL_W]
    z_p = pl_ref[:, POOL_W:]
    uext[0:HALO, :] = up_ref[...] * pm
    uext[HALO:HALO + tm, :] = u
    uext[HALO + tm:, :] = un_ref[...] * nm

    def win(k0, k1, sl):
        tot = None
        for k in range(k0, k1):
            t = uext[HALO + k:HALO + k + tm, sl]
            tot = t if tot is None else tot + t
        return tot

    lane = lax.broadcasted_iota(jnp.int32, (tm, LANES), 1)
    pos = j * tm + lax.broadcasted_iota(jnp.int32, (tm, LANES), 0)
    low = lane < POOL_GROUP
    ds = []
    for ti, (wa, wb) in enumerate(((2, 4), (8, 16))):
        sl = slice(ti * LANES, (ti + 1) * LANES)
        s_a = win(-(wa // 2), wa // 2, sl)
        s_b = s_a + win(-(wb // 2), -(wa // 2), sl) + win(wa // 2, wb // 2, sl)
        half = jnp.where(low, wa // 2, wb // 2)
        cnt = jnp.minimum(pos + half - 1, seq - 1) - jnp.maximum(pos - half, 0) + 1
        tot = jnp.where(low, s_a, s_b)
        ds.append(tot / cnt.astype(F32) - u[:, sl])
    d = jnp.concatenate(ds, axis=-1).astype(BF16)
    pool_o = (_dot(d, wpool_ref[...]) * pscale_ref[...] * _silu(z_p)).astype(BF16)

    a = ATTN_W
    out = (_dot(attn_ref[...], wout_ref[0:a, :]) + _dot(conv_o, wout_ref[a:a + c, :])
           + _dot(pool_o, wout_ref[a + c:, :]))
    res = x_ref[...] + mod_ref[0, 2:3, :] * out
    if final:
        ms = jnp.mean(res * res, axis=-1, keepdims=True)
        res = res * lax.rsqrt(ms + EPS) * fg_ref[...]
    o_ref[...] = res


def _mix(attn2, cvp, plp, x2, mod, wout_bf, conv_w, conv_b, wpool_bd, pool_scale, final_g,
         *, seq, tm, final):
    tokens = x2.shape[0]
    tps = seq // tm
    shared = mod.shape[0] == 1
    hb = tm // HALO
    nhb = tokens // HALO
    row = lambda i: (i, 0)
    const = lambda i: (0, 0)
    prev = lambda col: (lambda i: (jnp.maximum(i * hb - 1, 0), col))
    nxt = lambda col: (lambda i: (jnp.minimum((i + 1) * hb, nhb - 1), col))
    in_specs = [
        pl.BlockSpec((tm, ATTN_W), row),
        pl.BlockSpec((tm, 4 * CONV_W), row),
        pl.BlockSpec((HALO, CONV_W), prev(0)),
        pl.BlockSpec((HALO, CONV_W), prev(2)),
        pl.BlockSpec((HALO, CONV_W), nxt(0)),
        pl.BlockSpec((HALO, CONV_W), nxt(2)),
        pl.BlockSpec((tm, 2 * POOL_W), row),
        pl.BlockSpec((HALO, POOL_W), prev(0)),
        pl.BlockSpec((HALO, POOL_W), nxt(0)),
        pl.BlockSpec((tm, D_MODEL), row),
        pl.BlockSpec((1, 3, D_MODEL), (lambda i: (0, 0, 0)) if shared else (lambda i: (i // tps, 0, 0))),
        pl.BlockSpec((D_MODEL, D_MODEL), const),
        pl.BlockSpec((3, CONV_W), const),
        pl.BlockSpec((1, CONV_W), const),
        pl.BlockSpec((POOL_W, POOL_W), const),
        pl.BlockSpec((1, POOL_W), const),
        pl.BlockSpec((1, D_MODEL), const),
    ]
    return pl.pallas_call(
        functools.partial(_mix_kernel, tm=tm, tps=tps, seq=seq, final=final),
        grid=(tokens // tm,),
        in_specs=in_specs,
        out_specs=pl.BlockSpec((tm, D_MODEL), row),
        out_shape=jax.ShapeDtypeStruct((tokens, D_MODEL), F32),
        scratch_shapes=[
            pltpu.VMEM((tm + 2 * HALO, CONV_W), F32),
            pltpu.VMEM((tm + 2 * HALO, POOL_W), F32),
        ],
        compiler_params=pltpu.CompilerParams(
            dimension_semantics=("arbitrary",), vmem_limit_bytes=48 * 1024 * 1024),
        name="mix_final" if final else "mix",
    )(attn2, cvp, cvp, cvp, cvp, cvp, plp, plp, plp, x2, mod, wout_bf,
      conv_w, conv_b.reshape(1, CONV_W), wpool_bd, pool_scale.reshape(1, POOL_W),
      final_g.reshape(1, D_MODEL))


def _rope_tables(n):
    rows = n // GRID_W
    row = jnp.repeat(jnp.arange(rows), GRID_W).astype(F32)
    col = jnp.tile(jnp.arange(GRID_W), rows).astype(F32)
    half = HEAD_DIM // 2
    inv = 1.0 / (ROPE_THETA ** (jnp.arange(0, half, 2, dtype=F32) / half))
    ar = row[:, None] * inv
    ac = col[:, None] * inv
    ang = jnp.tile(jnp.concatenate([ar, ar, ac, ac], axis=-1), (1, LANES // HEAD_DIM))
    cos = jnp.cos(ang)
    sin = jnp.sin(ang)
    second = (jnp.arange(LANES) % half) >= (half // 2)
    return cos, jnp.where(second, 0.0, -sin), jnp.where(second, sin, 0.0)


def _block_diag(blocks):
    n = len(blocks)
    rows = []
    for i, b in enumerate(blocks):
        rows.append(jnp.concatenate(
            [b if j == i else jnp.zeros((b.shape[0], blocks[j].shape[1]), b.dtype) for j in range(n)],
            axis=1))
    return jnp.concatenate(rows, axis=0)


def kernel(x_prompt, x_sample, cache_k, cache_v, c, c_ctx, norm_g, w_ada, b_ada, w_in,
           q_norm_g, k_norm_g, conv_w, conv_b, pool_w, pool_scale, w_out, final_g):
    depth = w_in.shape[0]
    bp, sp, _ = x_prompt.shape
    bs, ss, _ = x_sample.shape
    n_ctx = cache_k.shape[2]

    cvec = jnp.concatenate(
        [c, c_ctx[None, :], jnp.zeros((ADA_ROWS - bs - 1, D_MODEL), F32)], axis=0)
    mod = _ada(cvec, w_ada, b_ada)
    mod = mod.reshape(depth, ADA_ROWS, 3, D_MODEL)

    w_in_bf = w_in.astype(BF16)
    w_out_bf = w_out.astype(BF16)
    tables = _rope_tables(ss)
    ones = jnp.full((HEAD_DIM, HEAD_DIM), 1.0 / HEAD_DIM, BF16)
    bd = _block_diag([ones, ones])
    ck = cache_k.reshape(bs, depth, n_ctx, KV_W)
    cv = cache_v.reshape(bs, depth, n_ctx, KV_W)

    h = x_prompt.reshape(bp * sp, D_MODEL)
    z = x_sample.reshape(bs * ss, D_MODEL)
    new_ks, new_vs = [], []
    for l in range(depth):
        gains = jnp.concatenate(
            [jnp.tile(q_norm_g[l], N_HEADS), jnp.tile(k_norm_g[l], N_KV_HEADS)]).reshape(1, QK_W)
        wpool_bd = _block_diag([pool_w[l, g] for g in range(pool_w.shape[1])]).astype(BF16)
        final = l == depth - 1
        mod_ctx = mod[l, bs:bs + 1]
        mod_lat = mod[l, 0:bs]

        q, k, v, za, cvp, plp = _inproj(h, mod_ctx, norm_g[l], w_in_bf[l], gains, bd, None,
                                        seq=sp, tm=256)
        new_ks.append(k.reshape(bp, sp, N_KV_HEADS, HEAD_DIM))
        new_vs.append(v.reshape(bp, sp, N_KV_HEADS, HEAD_DIM))
        attn = _attention(q.reshape(bp, sp, ATTN_W), k.reshape(bp, sp, KV_W), v.reshape(bp, sp, KV_W),
                          za.reshape(bp, sp, ATTN_W), None, None, l, tq=256)
        h = _mix(attn.reshape(bp * sp, ATTN_W), cvp, plp, h, mod_ctx, w_out_bf[l], conv_w[l], conv_b[l],
                 wpool_bd, pool_scale[l], final_g, seq=sp, tm=256, final=final)

        q, k, v, za, cvp, plp = _inproj(z, mod_lat, norm_g[l], w_in_bf[l], gains, bd, tables,
                                        seq=ss, tm=256)
        attn = _attention_lat(q.reshape(bs, ss, ATTN_W), k.reshape(bs, ss, KV_W), v.reshape(bs, ss, KV_W),
                              za.reshape(bs, ss, ATTN_W), ck, cv, l, rb=256)
        z = _mix(attn.reshape(bs * ss, ATTN_W), cvp, plp, z, mod_lat, w_out_bf[l], conv_w[l], conv_b[l],
                 wpool_bd, pool_scale[l], final_g, seq=ss, tm=256, final=final)

    y_prompt = h.reshape(bp, sp, D_MODEL)
    y_sample = z.reshape(bs, ss, D_MODEL)
    return (y_prompt, y_sample, jnp.stack(new_ks, axis=1), jnp.stack(new_vs, axis=1))
```

```python
import functools

import jax
import jax.numpy as jnp
from jax import lax
from jax.experimental import pallas as pl
from jax.experimental.pallas import tpu as pltpu

D_MODEL = 1024
GRID_W = 64
N_HEADS = 8
N_KV_HEADS = 2
HEAD_DIM = 64
ATTN_W = N_HEADS * HEAD_DIM
KV_W = N_KV_HEADS * HEAD_DIM
QK_W = ATTN_W + KV_W
QKV_W = QK_W + KV_W
CONV_W = 256
POOL_W = 256
POOL_GROUP = 64
IN_W = 2816
HALO_W = 2 * CONV_W + POOL_W
MAIN_W = ATTN_W + 2 * CONV_W + POOL_W
ROPE_THETA = 10000.0
EPS = 1e-6
LANES = 128
HALO = 8
ADA_ROWS = 8
VMEM_LIMIT = 48 * 1024 * 1024

F32 = jnp.float32
BF16 = jnp.bfloat16


def _silu(z):
    return z * (1.0 / (1.0 + jnp.exp(-z)))


def _dot(a, b):
    return jnp.dot(a, b, preferred_element_type=F32)


def _mod_norm(x, g, mod_ref):
    ms = jnp.mean(x * x, axis=-1, keepdims=True)
    y = x * lax.rsqrt(ms + EPS) * g
    return y * (1.0 + mod_ref[0, 1:2, :]) + mod_ref[0, 0:1, :]


def _ada_kernel(c_ref, w_ref, b_ref, o_ref):
    s = _silu(c_ref[...]).astype(BF16)
    o_ref[0] = _dot(s, w_ref[0].astype(BF16)) + b_ref[0]


def _ada(cvec, w_ada, b_ada):
    depth, _, n3 = w_ada.shape
    tn = 1024
    return pl.pallas_call(
        _ada_kernel,
        grid=(depth, n3 // tn),
        in_specs=[
            pl.BlockSpec((ADA_ROWS, D_MODEL), lambda l, j: (0, 0)),
            pl.BlockSpec((1, D_MODEL, tn), lambda l, j: (l, 0, j)),
            pl.BlockSpec((1, 1, tn), lambda l, j: (l, 0, j)),
        ],
        out_specs=pl.BlockSpec((1, ADA_ROWS, tn), lambda l, j: (l, 0, j)),
        out_shape=jax.ShapeDtypeStruct((depth, ADA_ROWS, n3), F32),
        name="ada_mod",
    )(cvec, w_ada, b_ada.reshape(depth, 1, n3))


def _qkv_kernel(*refs, rope):
    if rope:
        (x_ref, mod_ref, g_ref, w_ref, gains_ref, bd_ref, cos_ref, sa_ref, sb_ref,
         q_ref, k_ref, v_ref) = refs
    else:
        x_ref, mod_ref, g_ref, w_ref, gains_ref, bd_ref, q_ref, k_ref, v_ref = refs
    xn = _mod_norm(x_ref[...], g_ref[...], mod_ref).astype(BF16)
    bd = bd_ref[...]
    for j in range(QK_W // LANES):
        sl = slice(j * LANES, (j + 1) * LANES)
        t = _dot(xn, w_ref[:, sl])
        sq = t * t
        hi = sq.astype(BF16)
        lo = (sq - hi.astype(F32)).astype(BF16)
        msq = _dot(hi, bd) + _dot(lo, bd)
        t = t * lax.rsqrt(msq + EPS) * gains_ref[:, sl]
        if rope:
            t = (t * cos_ref[...] + pltpu.roll(t, LANES - 16, 1) * sa_ref[...]
                 + pltpu.roll(t, 16, 1) * sb_ref[...])
        if j < ATTN_W // LANES:
            q_ref[:, sl] = (t * (HEAD_DIM ** -0.5)).astype(BF16)
        else:
            k_ref[...] = t
    v_ref[...] = _dot(xn, w_ref[:, QK_W:QKV_W])


def _qkv(x2, mod, norm_g, w_qkv, gains, bd, tables, *, seq, tm):
    tokens = x2.shape[0]
    tps = seq // tm
    shared = mod.shape[0] == 1
    rope = tables is not None
    row = lambda i: (i, 0)
    const = lambda i: (0, 0)
    in_specs = [
        pl.BlockSpec((tm, D_MODEL), row),
        pl.BlockSpec((1, 3, D_MODEL), (lambda i: (0, 0, 0)) if shared else (lambda i: (i // tps, 0, 0))),
        pl.BlockSpec((1, D_MODEL), const),
        pl.BlockSpec((D_MODEL, QKV_W), const),
        pl.BlockSpec((1, QK_W), const),
        pl.BlockSpec((LANES, LANES), const),
    ]
    args = [x2, mod, norm_g.reshape(1, D_MODEL), w_qkv, gains, bd]
    if rope:
        in_specs += [pl.BlockSpec((tm, LANES), lambda i: (i % tps, 0))] * 3
        args += list(tables)
    out_shapes = [
        jax.ShapeDtypeStruct((tokens, ATTN_W), BF16),
        jax.ShapeDtypeStruct((tokens, KV_W), F32),
        jax.ShapeDtypeStruct((tokens, KV_W), F32),
    ]
    out_specs = [pl.BlockSpec((tm, s.shape[1]), row) for s in out_shapes]
    return pl.pallas_call(
        functools.partial(_qkv_kernel, rope=rope),
        grid=(tokens // tm,),
        in_specs=in_specs,
        out_specs=out_specs,
        out_shape=out_shapes,
        compiler_params=pltpu.CompilerParams(
            dimension_semantics=("arbitrary",), vmem_limit_bytes=VMEM_LIMIT),
        name="qkv_rope" if rope else "qkv",
    )(*args)


def _attn_kernel(*refs, n, n_ctx, rb):
    if n_ctx:
        q_ref, k_ref, v_ref, ck_ref, cv_ref, o_ref, kt, vab, q2, acc, s0, s1, p0, p1 = refs
    else:
        q_ref, k_ref, v_ref, o_ref, kt, vab, q2, acc, s0, s1, p0, p1 = refs
    kv = pl.program_id(1)
    nr = 2 * n // rb

    def fill(k_src, v_src, r0, m):
        lane = lax.broadcasted_iota(jnp.int32, (m, LANES), 1)
        own = (lane // HEAD_DIM) == kv
        low = lane < HEAD_DIM
        k_own = jnp.where(own, k_src, 0.0)
        k_oth = pltpu.roll(k_own, HEAD_DIM, 1)
        v_own = jnp.where(own, v_src, 0.0)
        v_oth = pltpu.roll(v_own, HEAD_DIM, 1)
        is0 = kv == 0
        kt[0, :, r0:r0 + m] = jnp.where(is0, k_own, k_oth).T.astype(BF16)
        kt[1, :, r0:r0 + m] = jnp.where(is0, k_oth, k_own).T.astype(BF16)
        vab[0, r0:r0 + m, 0:LANES] = jnp.where(is0, v_own, v_oth).astype(BF16)
        vab[1, r0:r0 + m, 0:LANES] = jnp.where(is0, v_oth, v_own).astype(BF16)
        vab[0, r0:r0 + m, LANES:] = jnp.where(low, 1.0, 0.0).astype(BF16)
        vab[1, r0:r0 + m, LANES:] = jnp.where(low, 0.0, 1.0).astype(BF16)

    piece = 256
    for r0 in range(0, n, piece):
        fill(k_ref[0, r0:r0 + piece, :], v_ref[0, r0:r0 + piece, :], r0, piece)
    if n_ctx:
        fill(ck_ref[0, 0], cv_ref[0, 0], n, n_ctx)
    q2[0:n, :] = q_ref[0, :, 0:LANES]
    q2[n:, :] = q_ref[0, :, LANES:]

    def scores(r, slot, dst):
        dst[...] = _dot(q2[pl.ds(pl.multiple_of(r * rb, rb), rb), :], kt[slot])

    def numer(src, dst):
        for i in range(0, rb, 16):
            s = src[i:i + 16, :]
            m = jnp.max(s, axis=-1, keepdims=True)
            dst[i:i + 16, :] = jnp.exp(s - m).astype(BF16)

    def weigh(src, slot, r, first):
        pv = _dot(src[...], vab[slot])
        rows = pl.ds(pl.multiple_of(r * rb, rb), rb)
        if first:
            acc[rows, :] = pv
        else:
            acc[rows, :] += pv

    p1[...] = jnp.zeros_like(p1)
    acc[nr * rb:, :] = jnp.zeros((rb, 2 * LANES), F32)
    scores(0, 0, s0)

    def body(r, carry):
        scores(r, 1, s1)
        numer(s0, p0)
        weigh(p1, 1, jnp.where(r == 0, nr, r - 1), False)
        scores(jnp.minimum(r + 1, nr - 1), 0, s0)
        numer(s1, p1)
        weigh(p0, 0, r, True)
        return carry

    lax.fori_loop(0, nr, body, 0)
    weigh(p1, 1, nr - 1, False)

    ch = 256
    for t in range(2):
        for r0 in range(0, n, ch):
            a = acc[t * n + r0:t * n + r0 + ch, :]
            o_ref[0, r0:r0 + ch, t * LANES:(t + 1) * LANES] = a[:, 0:LANES] / a[:, LANES:]


def _attention(q, k, v, cache_k, cache_v, layer, *, rb):
    bsz, n, _ = q.shape
    n_ctx = 0 if cache_k is None else cache_k.shape[2]
    t_all = n + n_ctx
    gw = 2 * LANES
    qmap = lambda b, g: (b, 0, g)
    kmap = lambda b, g: (b, 0, 0)
    in_specs = [
        pl.BlockSpec((1, n, gw), qmap),
        pl.BlockSpec((1, n, KV_W), kmap),
        pl.BlockSpec((1, n, KV_W), kmap),
    ]
    args = [q, k, v]
    if n_ctx:
        cmap = lambda b, g: (b, layer, 0, 0)
        in_specs += [pl.BlockSpec((1, 1, n_ctx, KV_W), cmap)] * 2
        args += [cache_k, cache_v]
    return pl.pallas_call(
        functools.partial(_attn_kernel, n=n, n_ctx=n_ctx, rb=rb),
        grid=(bsz, N_KV_HEADS),
        in_specs=in_specs,
        out_specs=pl.BlockSpec((1, n, gw), qmap),
        out_shape=jax.ShapeDtypeStruct((bsz, n, ATTN_W), F32),
        scratch_shapes=[
            pltpu.VMEM((2, LANES, t_all), BF16),
            pltpu.VMEM((2, t_all, 2 * LANES), BF16),
            pltpu.VMEM((2 * n, LANES), BF16),
            pltpu.VMEM((2 * n + rb, 2 * LANES), F32),
            pltpu.VMEM((rb, t_all), F32),
            pltpu.VMEM((rb, t_all), F32),
            pltpu.VMEM((rb, t_all), BF16),
            pltpu.VMEM((rb, t_all), BF16),
        ],
        compiler_params=pltpu.CompilerParams(
            dimension_semantics=("arbitrary", "arbitrary"), vmem_limit_bytes=VMEM_LIMIT),
        name="attn_lat" if n_ctx else "attn_ctx",
    )(*args)


def _mix_kernel(attn_ref, x_ref, xp_ref, xn_ref, mod_ref, g_ref, wh_ref, wm_ref, wout_ref,
                convw_ref, convb_ref, wpool_ref, pscale_ref, fg_ref,
                o_ref, xe, yext, uext, *, tm, tps, seq, final):
    j = pl.program_id(0) % tps
    pm = jnp.where(j == 0, 0.0, 1.0)
    nm = jnp.where(j == tps - 1, 0.0, 1.0)
    c = CONV_W
    g = g_ref[...]
    x = x_ref[...]
    xe[0:tm, :] = _mod_norm(x, g, mod_ref).astype(BF16)
    halo = jnp.concatenate([xn_ref[...], xp_ref[...]], axis=0)
    xe[tm:, :] = _mod_norm(halo, g, mod_ref).astype(BF16)

    ph = _dot(xe[...], wh_ref[...])
    pmn = _dot(xe[0:tm, :], wm_ref[...])
    z_a = pmn[:, 0:ATTN_W]
    b_c = pmn[:, ATTN_W:ATTN_W + c]
    z_c = pmn[:, ATTN_W + c:ATTN_W + 2 * c]
    z_p = pmn[:, ATTN_W + 2 * c:]

    y = ph[:, 0:c] * ph[:, c:2 * c]
    yext[0:HALO, :] = y[tm + HALO:, :] * pm
    yext[HALO:HALO + tm, :] = y[0:tm, :]
    yext[HALO + tm:, :] = y[tm:tm + HALO, :] * nm
    conv = (yext[HALO - 1:HALO - 1 + tm, :] * convw_ref[0:1, :]
            + yext[HALO:HALO + tm, :] * convw_ref[1:2, :]
            + yext[HALO + 1:HALO + 1 + tm, :] * convw_ref[2:3, :] + convb_ref[...])
    conv_o = (b_c * conv * _silu(z_c)).astype(BF16)

    u_all = ph[:, 2 * c:]
    u = u_all[0:tm, :]
    uext[0:HALO, :] = u_all[tm + HALO:, :] * pm
    uext[HALO:HALO + tm, :] = u
    uext[HALO + tm:, :] = u_all[tm:tm + HALO, :] * nm

    def win(k0, k1, sl):
        tot = None
        for k in range(k0, k1):
            t = uext[HALO + k:HALO + k + tm, sl]
            tot = t if tot is None else tot + t
        return tot

    lane = lax.broadcasted_iota(jnp.int32, (tm, LANES), 1)
    pos = j * tm + lax.broadcasted_iota(jnp.int32, (tm, LANES), 0)
    low = lane < POOL_GROUP
    ds = []
    for ti, (wa, wb) in enumerate(((2, 4), (8, 16))):
        sl = slice(ti * LANES, (ti + 1) * LANES)
        s_a = win(-(wa // 2), wa // 2, sl)
        s_b = s_a + win(-(wb // 2), -(wa // 2), sl) + win(wa // 2, wb // 2, sl)
        half = jnp.where(low, wa // 2, wb // 2)
        cnt = jnp.minimum(pos + half - 1, seq - 1) - jnp.maximum(pos - half, 0) + 1
        tot = jnp.where(low, s_a, s_b)
        ds.append(tot / cnt.astype(F32) - u[:, sl])
    d = jnp.concatenate(ds, axis=-1).astype(BF16)
    pool_o = (_dot(d, wpool_ref[...]) * pscale_ref[...] * _silu(z_p)).astype(BF16)

    attn_o = (attn_ref[...] * _silu(z_a)).astype(BF16)
    a = ATTN_W
    out = (_dot(attn_o, wout_ref[0:a, :]) + _dot(conv_o, wout_ref[a:a + c, :])
           + _dot(pool_o, wout_ref[a + c:, :]))
    res = x + mod_ref[0, 2:3, :] * out
    if final:
        ms = jnp.mean(res * res, axis=-1, keepdims=True)
        res = res * lax.rsqrt(ms + EPS) * fg_ref[...]
    o_ref[...] = res


def _mix(attn2, x2, mod, norm_g, w_halo, w_main, wout_bf, conv_w, conv_b, wpool_bd, pool_scale, final_g,
         *, seq, tm, final):
    tokens = x2.shape[0]
    tps = seq // tm
    shared = mod.shape[0] == 1
    hb = tm // HALO
    nhb = tokens // HALO
    row = lambda i: (i, 0)
    const = lambda i: (0, 0)
    in_specs = [
        pl.BlockSpec((tm, ATTN_W), row),
        pl.BlockSpec((tm, D_MODEL), row),
        pl.BlockSpec((HALO, D_MODEL), lambda i: (jnp.maximum(i * hb - 1, 0), 0)),
        pl.BlockSpec((HALO, D_MODEL), lambda i: (jnp.minimum((i + 1) * hb, nhb - 1), 0)),
        pl.BlockSpec((1, 3, D_MODEL), (lambda i: (0, 0, 0)) if shared else (lambda i: (i // tps, 0, 0))),
        pl.BlockSpec((1, D_MODEL), const),
        pl.BlockSpec((D_MODEL, HALO_W), const),
        pl.BlockSpec((D_MODEL, MAIN_W), const),
        pl.BlockSpec((D_MODEL, D_MODEL), const),
        pl.BlockSpec((3, CONV_W), const),
        pl.BlockSpec((1, CONV_W), const),
        pl.BlockSpec((POOL_W, POOL_W), const),
        pl.BlockSpec((1, POOL_W), const),
        pl.BlockSpec((1, D_MODEL), const),
    ]
    return pl.pallas_call(
        functools.partial(_mix_kernel, tm=tm, tps=tps, seq=seq, final=final),
        grid=(tokens // tm,),
        in_specs=in_specs,
        out_specs=pl.BlockSpec((tm, D_MODEL), row),
        out_shape=jax.ShapeDtypeStruct((tokens, D_MODEL), F32),
        scratch_shapes=[
            pltpu.VMEM((tm + 2 * HALO, D_MODEL), BF16),
            pltpu.VMEM((tm + 2 * HALO, CONV_W), F32),
            pltpu.VMEM((tm + 2 * HALO, POOL_W), F32),
        ],
        compiler_params=pltpu.CompilerParams(
            dimension_semantics=("arbitrary",), vmem_limit_bytes=VMEM_LIMIT),
        name="mix_final" if final else "mix",
    )(attn2, x2, x2, x2, mod, norm_g.reshape(1, D_MODEL), w_halo, w_main, wout_bf,
      conv_w, conv_b.reshape(1, CONV_W), wpool_bd, pool_scale.reshape(1, POOL_W),
      final_g.reshape(1, D_MODEL))


def _rope_tables(n):
    rows = n // GRID_W
    row = jnp.repeat(jnp.arange(rows), GRID_W).astype(F32)
    col = jnp.tile(jnp.arange(GRID_W), rows).astype(F32)
    half = HEAD_DIM // 2
    inv = 1.0 / (ROPE_THETA ** (jnp.arange(0, half, 2, dtype=F32) / half))
    ar = row[:, None] * inv
    ac = col[:, None] * inv
    ang = jnp.tile(jnp.concatenate([ar, ar, ac, ac], axis=-1), (1, LANES // HEAD_DIM))
    cos = jnp.cos(ang)
    sin = jnp.sin(ang)
    second = (jnp.arange(LANES) % half) >= (half // 2)
    return cos, jnp.where(second, 0.0, -sin), jnp.where(second, sin, 0.0)


def _block_diag(blocks):
    n = len(blocks)
    rows = []
    for i, b in enumerate(blocks):
        rows.append(jnp.concatenate(
            [b if j == i else jnp.zeros((b.shape[0], blocks[j].shape[1]), b.dtype) for j in range(n)],
            axis=1))
    return jnp.concatenate(rows, axis=0)


def kernel(x_prompt, x_sample, cache_k, cache_v, c, c_ctx, norm_g, w_ada, b_ada, w_in,
           q_norm_g, k_norm_g, conv_w, conv_b, pool_w, pool_scale, w_out, final_g):
    depth = w_in.shape[0]
    bp, sp, _ = x_prompt.shape
    bs, ss, _ = x_sample.shape
    n_ctx = cache_k.shape[2]

    cvec = jnp.concatenate(
        [c, c_ctx[None, :], jnp.zeros((ADA_ROWS - bs - 1, D_MODEL), F32)], axis=0)
    mod = _ada(cvec, w_ada, b_ada)
    mod = mod.reshape(depth, ADA_ROWS, 3, D_MODEL)

    w_in_bf = w_in.astype(BF16)
    w_qkv = w_in_bf[:, :, 0:QKV_W]
    o = QKV_W
    col = lambda a, n: w_in_bf[:, :, o + a:o + a + n]
    w_halo = jnp.concatenate([col(512, 256), col(1024, 256), col(1536, 256)], axis=-1)
    w_main = jnp.concatenate([col(0, 512), col(768, 256), col(1280, 256), col(1792, 256)], axis=-1)
    w_out_bf = w_out.astype(BF16)
    tables = _rope_tables(ss)
    ones = jnp.full((HEAD_DIM, HEAD_DIM), 1.0 / HEAD_DIM, BF16)
    bd = _block_diag([ones, ones])
    ck = cache_k.reshape(bs, depth, n_ctx, KV_W)
    cv = cache_v.reshape(bs, depth, n_ctx, KV_W)

    h = x_prompt.reshape(bp * sp, D_MODEL)
    z = x_sample.reshape(bs * ss, D_MODEL)
    new_ks, new_vs = [], []
    for l in range(depth):
        gains = jnp.concatenate(
            [jnp.tile(q_norm_g[l], N_HEADS), jnp.tile(k_norm_g[l], N_KV_HEADS)]).reshape(1, QK_W)
        wpool_bd = _block_diag([pool_w[l, g] for g in range(pool_w.shape[1])]).astype(BF16)
        final = l == depth - 1
        mod_ctx = mod[l, bs:bs + 1]
        mod_lat = mod[l, 0:bs]
        mix = functools.partial(
            _mix, norm_g=norm_g[l], w_halo=w_halo[l], w_main=w_main[l], wout_bf=w_out_bf[l],
            conv_w=conv_w[l], conv_b=conv_b[l], wpool_bd=wpool_bd, pool_scale=pool_scale[l],
            final_g=final_g, tm=256, final=final)

        q, k, v = _qkv(h, mod_ctx, norm_g[l], w_qkv[l], gains, bd, None, seq=sp, tm=256)
        new_ks.append(k.reshape(bp, sp, N_KV_HEADS, HEAD_DIM))
        new_vs.append(v.reshape(bp, sp, N_KV_HEADS, HEAD_DIM))
        attn = _attention(q.reshape(bp, sp, ATTN_W), k.reshape(bp, sp, KV_W), v.reshape(bp, sp, KV_W),
                          None, None, l, rb=256)
        h = mix(attn.reshape(bp * sp, ATTN_W), h, mod_ctx, seq=sp)

        q, k, v = _qkv(z, mod_lat, norm_g[l], w_qkv[l], gains, bd, tables, seq=ss, tm=256)
        attn = _attention(q.reshape(bs, ss, ATTN_W), k.reshape(bs, ss, KV_W), v.reshape(bs, ss, KV_W),
                          ck, cv, l, rb=256)
        z = mix(attn.reshape(bs * ss, ATTN_W), z, mod_lat, seq=ss)

    y_prompt = h.reshape(bp, sp, D_MODEL)
    y_sample = z.reshape(bs, ss, D_MODEL)
    return (y_prompt, y_sample, jnp.stack(new_ks, axis=1), jnp.stack(new_vs, axis=1))
```

```python
import functools

import jax
import jax.numpy as jnp
from jax import lax
from jax.experimental import pallas as pl
from jax.experimental.pallas import tpu as pltpu

D_MODEL = 1024
GRID_W = 64
N_HEADS = 8
N_KV_HEADS = 2
HEAD_DIM = 64
ATTN_W = N_HEADS * HEAD_DIM
KV_W = N_KV_HEADS * HEAD_DIM
QK_W = ATTN_W + KV_W
QKV_W = QK_W + KV_W
CONV_W = 256
POOL_W = 256
POOL_GROUP = 64
IN_W = 2816
HALO_W = 2 * CONV_W + POOL_W
MAIN_W = ATTN_W + 2 * CONV_W + POOL_W
ROPE_THETA = 10000.0
EPS = 1e-6
LANES = 128
HALO = 8
ADA_ROWS = 8
VMEM_LIMIT = 48 * 1024 * 1024

F32 = jnp.float32
BF16 = jnp.bfloat16


def _silu(z):
    return z * (1.0 / (1.0 + jnp.exp(-z)))


def _dot(a, b):
    return jnp.dot(a, b, preferred_element_type=F32)


def _mod_norm(x, g, mod_ref):
    ms = jnp.mean(x * x, axis=-1, keepdims=True)
    y = x * lax.rsqrt(ms + EPS) * g
    return y * (1.0 + mod_ref[0, 1:2, :]) + mod_ref[0, 0:1, :]


def _ada_kernel(c_ref, w_ref, b_ref, o_ref):
    s = _silu(c_ref[...]).astype(BF16)
    o_ref[0] = _dot(s, w_ref[0].astype(BF16)) + b_ref[0]


def _ada(cvec, w_ada, b_ada):
    depth, _, n3 = w_ada.shape
    tn = 1024
    return pl.pallas_call(
        _ada_kernel,
        grid=(depth, n3 // tn),
        in_specs=[
            pl.BlockSpec((ADA_ROWS, D_MODEL), lambda l, j: (0, 0)),
            pl.BlockSpec((1, D_MODEL, tn), lambda l, j: (l, 0, j)),
            pl.BlockSpec((1, 1, tn), lambda l, j: (l, 0, j)),
        ],
        out_specs=pl.BlockSpec((1, ADA_ROWS, tn), lambda l, j: (l, 0, j)),
        out_shape=jax.ShapeDtypeStruct((depth, ADA_ROWS, n3), F32),
        name="ada_mod",
    )(cvec, w_ada, b_ada.reshape(depth, 1, n3))


def _qkv_kernel(*refs, rope):
    if rope:
        (x_ref, mod_ref, g_ref, w_ref, gains_ref, bd_ref, cos_ref, sa_ref, sb_ref,
         q_ref, k_ref, v_ref) = refs
    else:
        x_ref, mod_ref, g_ref, w_ref, gains_ref, bd_ref, q_ref, k_ref, v_ref = refs
    xn = _mod_norm(x_ref[...], g_ref[...], mod_ref).astype(BF16)
    bd = bd_ref[...]
    qkv = _dot(xn, w_ref[...])
    v_ref[...] = qkv[:, QK_W:]
    for j in range(QK_W // LANES):
        sl = slice(j * LANES, (j + 1) * LANES)
        t = qkv[:, sl]
        sq = t * t
        hi = sq.astype(BF16)
        lo = (sq - hi.astype(F32)).astype(BF16)
        msq = _dot(hi, bd) + _dot(lo, bd)
        t = t * lax.rsqrt(msq + EPS) * gains_ref[:, sl]
        if rope:
            t = (t * cos_ref[...] + pltpu.roll(t, LANES - 16, 1) * sa_ref[...]
                 + pltpu.roll(t, 16, 1) * sb_ref[...])
        if j < ATTN_W // LANES:
            q_ref[:, sl] = (t * (HEAD_DIM ** -0.5)).astype(BF16)
        else:
            k_ref[...] = t


def _qkv(x2, mod, norm_g, w_qkv, gains, bd, tables, *, seq, tm):
    tokens = x2.shape[0]
    tps = seq // tm
    shared = mod.shape[0] == 1
    rope = tables is not None
    row = lambda i: (i, 0)
    const = lambda i: (0, 0)
    in_specs = [
        pl.BlockSpec((tm, D_MODEL), row),
        pl.BlockSpec((1, 3, D_MODEL), (lambda i: (0, 0, 0)) if shared else (lambda i: (i // tps, 0, 0))),
        pl.BlockSpec((1, D_MODEL), const),
        pl.BlockSpec((D_MODEL, QKV_W), const),
        pl.BlockSpec((1, QK_W), const),
        pl.BlockSpec((LANES, LANES), const),
    ]
    args = [x2, mod, norm_g.reshape(1, D_MODEL), w_qkv, gains, bd]
    if rope:
        in_specs += [pl.BlockSpec((tm, LANES), lambda i: (i % tps, 0))] * 3
        args += list(tables)
    out_shapes = [
        jax.ShapeDtypeStruct((tokens, ATTN_W), BF16),
        jax.ShapeDtypeStruct((tokens, KV_W), F32),
        jax.ShapeDtypeStruct((tokens, KV_W), F32),
    ]
    out_specs = [pl.BlockSpec((tm, s.shape[1]), row) for s in out_shapes]
    return pl.pallas_call(
        functools.partial(_qkv_kernel, rope=rope),
        grid=(tokens // tm,),
        in_specs=in_specs,
        out_specs=out_specs,
        out_shape=out_shapes,
        compiler_params=pltpu.CompilerParams(
            dimension_semantics=("arbitrary",), vmem_limit_bytes=VMEM_LIMIT),
        name="qkv_rope" if rope else "qkv",
    )(*args)


def _attn_kernel(*refs, n, n_ctx, rb):
    if n_ctx:
        q_ref, k_ref, v_ref, ck_ref, cv_ref, o_ref, kt, vab, q2, acc, s0, s1, p0, p1 = refs
    else:
        q_ref, k_ref, v_ref, o_ref, kt, vab, q2, acc, s0, s1, p0, p1 = refs
    kv = pl.program_id(1)
    nr = 2 * n // rb

    def fill(k_src, v_src, r0, m):
        lane = lax.broadcasted_iota(jnp.int32, (m, LANES), 1)
        own = (lane // HEAD_DIM) == kv
        low = lane < HEAD_DIM
        k_own = jnp.where(own, k_src, 0.0)
        k_oth = pltpu.roll(k_own, HEAD_DIM, 1)
        v_own = jnp.where(own, v_src, 0.0)
        v_oth = pltpu.roll(v_own, HEAD_DIM, 1)
        is0 = kv == 0
        kt[0, :, r0:r0 + m] = jnp.where(is0, k_own, k_oth).T.astype(BF16)
        kt[1, :, r0:r0 + m] = jnp.where(is0, k_oth, k_own).T.astype(BF16)
        vab[0, r0:r0 + m, 0:LANES] = jnp.where(is0, v_own, v_oth).astype(BF16)
        vab[1, r0:r0 + m, 0:LANES] = jnp.where(is0, v_oth, v_own).astype(BF16)
        vab[0, r0:r0 + m, LANES:] = jnp.where(low, 1.0, 0.0).astype(BF16)
        vab[1, r0:r0 + m, LANES:] = jnp.where(low, 0.0, 1.0).astype(BF16)

    piece = 256
    for r0 in range(0, n, piece):
        fill(k_ref[0, r0:r0 + piece, :], v_ref[0, r0:r0 + piece, :], r0, piece)
    if n_ctx:
        fill(ck_ref[0, 0], cv_ref[0, 0], n, n_ctx)
    q2[0:n, :] = q_ref[0, :, 0:LANES]
    q2[n:, :] = q_ref[0, :, LANES:]

    def scores(r, slot, dst):
        dst[...] = _dot(q2[pl.ds(pl.multiple_of(r * rb, rb), rb), :], kt[slot])

    def numer(src, dst):
        for i in range(0, rb, 16):
            s = src[i:i + 16, :]
            m = jnp.max(s, axis=-1, keepdims=True)
            dst[i:i + 16, :] = jnp.exp(s - m).astype(BF16)

    def weigh(src, slot, r, first):
        pv = _dot(src[...], vab[slot])
        rows = pl.ds(pl.multiple_of(r * rb, rb), rb)
        if first:
            acc[rows, :] = pv
        else:
            acc[rows, :] += pv

    p1[...] = jnp.zeros_like(p1)
    acc[nr * rb:, :] = jnp.zeros((rb, 2 * LANES), F32)
    scores(0, 0, s0)

    def body(r, carry):
        scores(r, 1, s1)
        numer(s0, p0)
        weigh(p1, 1, jnp.where(r == 0, nr, r - 1), False)
        scores(jnp.minimum(r + 1, nr - 1), 0, s0)
        numer(s1, p1)
        weigh(p0, 0, r, True)
        return carry

    lax.fori_loop(0, nr, body, 0, unroll=4)
    weigh(p1, 1, nr - 1, False)

    ch = 256
    for t in range(2):
        for r0 in range(0, n, ch):
            a = acc[t * n + r0:t * n + r0 + ch, :]
            o_ref[0, r0:r0 + ch, t * LANES:(t + 1) * LANES] = a[:, 0:LANES] / a[:, LANES:]


def _attention(q, k, v, cache_k, cache_v, layer, *, rb):
    bsz, n, _ = q.shape
    n_ctx = 0 if cache_k is None else cache_k.shape[2]
    t_all = n + n_ctx
    gw = 2 * LANES
    qmap = lambda b, g: (b, 0, g)
    kmap = lambda b, g: (b, 0, 0)
    in_specs = [
        pl.BlockSpec((1, n, gw), qmap),
        pl.BlockSpec((1, n, KV_W), kmap),
        pl.BlockSpec((1, n, KV_W), kmap),
    ]
    args = [q, k, v]
    if n_ctx:
        cmap = lambda b, g: (b, layer, 0, 0)
        in_specs += [pl.BlockSpec((1, 1, n_ctx, KV_W), cmap)] * 2
        args += [cache_k, cache_v]
    return pl.pallas_call(
        functools.partial(_attn_kernel, n=n, n_ctx=n_ctx, rb=rb),
        grid=(bsz, N_KV_HEADS),
        in_specs=in_specs,
        out_specs=pl.BlockSpec((1, n, gw), qmap),
        out_shape=jax.ShapeDtypeStruct((bsz, n, ATTN_W), F32),
        scratch_shapes=[
            pltpu.VMEM((2, LANES, t_all), BF16),
            pltpu.VMEM((2, t_all, 2 * LANES), BF16),
            pltpu.VMEM((2 * n, LANES), BF16),
            pltpu.VMEM((2 * n + rb, 2 * LANES), F32),
            pltpu.VMEM((rb, t_all), F32),
            pltpu.VMEM((rb, t_all), F32),
            pltpu.VMEM((rb, t_all), BF16),
            pltpu.VMEM((rb, t_all), BF16),
        ],
        compiler_params=pltpu.CompilerParams(
            dimension_semantics=("arbitrary", "arbitrary"), vmem_limit_bytes=VMEM_LIMIT),
        name="attn_lat" if n_ctx else "attn_ctx",
    )(*args)


def _attn_ctx_kernel(q_ref, k_ref, v_ref, o_ref, *, n):
    kt_all = k_ref[0].T
    v = v_ref[0]
    vr = pltpu.roll(v, HEAD_DIM, 1)
    low = lax.broadcasted_iota(jnp.int32, (n, LANES), 1) < HEAD_DIM
    ones_lo = jnp.where(low, 1.0, 0.0)
    zk = jnp.zeros((HEAD_DIM, n), F32)
    gw = 2 * LANES
    for g in range(N_KV_HEADS):
        kg = kt_all[g * HEAD_DIM:(g + 1) * HEAD_DIM, :]
        kts = (jnp.concatenate([kg, zk], axis=0).astype(BF16),
               jnp.concatenate([zk, kg], axis=0).astype(BF16))
        v_lo, v_hi = (v, vr) if g == 0 else (vr, v)
        vs = (jnp.concatenate([jnp.where(low, v_lo, 0.0), ones_lo], axis=1).astype(BF16),
              jnp.concatenate([jnp.where(low, 0.0, v_hi), 1.0 - ones_lo], axis=1).astype(BF16))
        q2 = jnp.concatenate([q_ref[0, :, g * gw:g * gw + LANES],
                              q_ref[0, :, g * gw + LANES:(g + 1) * gw]], axis=0)
        acc = None
        for kt_s, v_s in zip(kts, vs):
            s = _dot(q2, kt_s)
            m = jnp.max(s, axis=-1, keepdims=True)
            pv = _dot(jnp.exp(s - m).astype(BF16), v_s)
            acc = pv if acc is None else acc + pv
        o = acc[:, 0:LANES] / acc[:, LANES:]
        o_ref[0, :, g * gw:g * gw + LANES] = o[0:n]
        o_ref[0, :, g * gw + LANES:(g + 1) * gw] = o[n:]


def _attention_ctx(q, k, v):
    bsz, n, _ = q.shape
    blk = lambda w: pl.BlockSpec((1, n, w), lambda b: (b, 0, 0))
    return pl.pallas_call(
        functools.partial(_attn_ctx_kernel, n=n),
        grid=(bsz,),
        in_specs=[blk(ATTN_W), blk(KV_W), blk(KV_W)],
        out_specs=blk(ATTN_W),
        out_shape=jax.ShapeDtypeStruct((bsz, n, ATTN_W), F32),
        compiler_params=pltpu.CompilerParams(
            dimension_semantics=("arbitrary",), vmem_limit_bytes=VMEM_LIMIT),
        name="attn_ctx",
    )(q, k, v)


def _mix_kernel(attn_ref, x_ref, xp_ref, xn_ref, mod_ref, g_ref, wh_ref, wm_ref, wout_ref,
                convw_ref, convb_ref, wpool_ref, pscale_ref, fg_ref,
                o_ref, xe, yext, uext, *, tm, tps, seq, final):
    j = pl.program_id(0) % tps
    pm = jnp.where(j == 0, 0.0, 1.0)
    nm = jnp.where(j == tps - 1, 0.0, 1.0)
    c = CONV_W
    g = g_ref[...]
    x = x_ref[...]
    xe[0:tm, :] = _mod_norm(x, g, mod_ref).astype(BF16)
    halo = jnp.concatenate([xn_ref[...], xp_ref[...]], axis=0)
    xe[tm:, :] = _mod_norm(halo, g, mod_ref).astype(BF16)

    ph = _dot(xe[...], wh_ref[...])
    pmn = _dot(xe[0:tm, :], wm_ref[...])
    z_a = pmn[:, 0:ATTN_W]
    b_c = pmn[:, ATTN_W:ATTN_W + c]
    z_c = pmn[:, ATTN_W + c:ATTN_W + 2 * c]
    z_p = pmn[:, ATTN_W + 2 * c:]

    y = ph[:, 0:c] * ph[:, c:2 * c]
    yext[0:HALO, :] = y[tm + HALO:, :] * pm
    yext[HALO:HALO + tm, :] = y[0:tm, :]
    yext[HALO + tm:, :] = y[tm:tm + HALO, :] * nm
    conv = (yext[HALO - 1:HALO - 1 + tm, :] * convw_ref[0:1, :]
            + yext[HALO:HALO + tm, :] * convw_ref[1:2, :]
            + yext[HALO + 1:HALO + 1 + tm, :] * convw_ref[2:3, :] + convb_ref[...])
    conv_o = (b_c * conv * _silu(z_c)).astype(BF16)

    u_all = ph[:, 2 * c:]
    u = u_all[0:tm, :]
    uext[0:HALO, :] = u_all[tm + HALO:, :] * pm
    uext[HALO:HALO + tm, :] = u
    uext[HALO + tm:, :] = u_all[tm:tm + HALO, :] * nm

    def win(k0, k1, sl):
        tot = None
        for k in range(k0, k1):
            t = uext[HALO + k:HALO + k + tm, sl]
            tot = t if tot is None else tot + t
        return tot

    lane = lax.broadcasted_iota(jnp.int32, (tm, LANES), 1)
    pos = j * tm + lax.broadcasted_iota(jnp.int32, (tm, LANES), 0)
    low = lane < POOL_GROUP
    ds = []
    for ti, (wa, wb) in enumerate(((2, 4), (8, 16))):
        sl = slice(ti * LANES, (ti + 1) * LANES)
        s_a = win(-(wa // 2), wa // 2, sl)
        s_b = s_a + win(-(wb // 2), -(wa // 2), sl) + win(wa // 2, wb // 2, sl)
        half = jnp.where(low, wa // 2, wb // 2)
        cnt = jnp.minimum(pos + half - 1, seq - 1) - jnp.maximum(pos - half, 0) + 1
        tot = jnp.where(low, s_a, s_b)
        ds.append(tot / cnt.astype(F32) - u[:, sl])
    d = jnp.concatenate(ds, axis=-1).astype(BF16)
    pool_o = (_dot(d, wpool_ref[...]) * pscale_ref[...] * _silu(z_p)).astype(BF16)

    attn_o = (attn_ref[...] * _silu(z_a)).astype(BF16)
    a = ATTN_W
    out = (_dot(attn_o, wout_ref[0:a, :]) + _dot(conv_o, wout_ref[a:a + c, :])
           + _dot(pool_o, wout_ref[a + c:, :]))
    res = x + mod_ref[0, 2:3, :] * out
    if final:
        ms = jnp.mean(res * res, axis=-1, keepdims=True)
        res = res * lax.rsqrt(ms + EPS) * fg_ref[...]
    o_ref[...] = res


def _mix(attn2, x2, mod, norm_g, w_halo, w_main, wout_bf, conv_w, conv_b, wpool_bd, pool_scale, final_g,
         *, seq, tm, final):
    tokens = x2.shape[0]
    tps = seq // tm
    shared = mod.shape[0] == 1
    hb = tm // HALO
    nhb = tokens // HALO
    row = lambda i: (i, 0)
    const = lambda i: (0, 0)
    in_specs = [
        pl.BlockSpec((tm, ATTN_W), row),
        pl.BlockSpec((tm, D_MODEL), row),
        pl.BlockSpec((HALO, D_MODEL), lambda i: (jnp.maximum(i * hb - 1, 0), 0)),
        pl.BlockSpec((HALO, D_MODEL), lambda i: (jnp.minimum((i + 1) * hb, nhb - 1), 0)),
        pl.BlockSpec((1, 3, D_MODEL), (lambda i: (0, 0, 0)) if shared else (lambda i: (i // tps, 0, 0))),
        pl.BlockSpec((1, D_MODEL), const),
        pl.BlockSpec((D_MODEL, HALO_W), const),
        pl.BlockSpec((D_MODEL, MAIN_W), const),
        pl.BlockSpec((D_MODEL, D_MODEL), const),
        pl.BlockSpec((3, CONV_W), const),
        pl.BlockSpec((1, CONV_W), const),
        pl.BlockSpec((POOL_W, POOL_W), const),
        pl.BlockSpec((1, POOL_W), const),
        pl.BlockSpec((1, D_MODEL), const),
    ]
    return pl.pallas_call(
        functools.partial(_mix_kernel, tm=tm, tps=tps, seq=seq, final=final),
        grid=(tokens // tm,),
        in_specs=in_specs,
        out_specs=pl.BlockSpec((tm, D_MODEL), row),
        out_shape=jax.ShapeDtypeStruct((tokens, D_MODEL), F32),
        scratch_shapes=[
            pltpu.VMEM((tm + 2 * HALO, D_MODEL), BF16),
            pltpu.VMEM((tm + 2 * HALO, CONV_W), F32),
            pltpu.VMEM((tm + 2 * HALO, POOL_W), F32),
        ],
        compiler_params=pltpu.CompilerParams(
            dimension_semantics=("arbitrary",), vmem_limit_bytes=VMEM_LIMIT),
        name="mix_final" if final else "mix",
    )(attn2, x2, x2, x2, mod, norm_g.reshape(1, D_MODEL), w_halo, w_main, wout_bf,
      conv_w, conv_b.reshape(1, CONV_W), wpool_bd, pool_scale.reshape(1, POOL_W),
      final_g.reshape(1, D_MODEL))


def _rope_tables(n):
    rows = n // GRID_W
    row = jnp.repeat(jnp.arange(rows), GRID_W).astype(F32)
    col = jnp.tile(jnp.arange(GRID_W), rows).astype(F32)
    half = HEAD_DIM // 2
    inv = 1.0 / (ROPE_THETA ** (jnp.arange(0, half, 2, dtype=F32) / half))
    ar = row[:, None] * inv
    ac = col[:, None] * inv
    ang = jnp.tile(jnp.concatenate([ar, ar, ac, ac], axis=-1), (1, LANES // HEAD_DIM))
    cos = jnp.cos(ang)
    sin = jnp.sin(ang)
    second = (jnp.arange(LANES) % half) >= (half // 2)
    return cos, jnp.where(second, 0.0, -sin), jnp.where(second, sin, 0.0)


def _block_diag(blocks):
    n = len(blocks)
    rows = []
    for i, b in enumerate(blocks):
        rows.append(jnp.concatenate(
            [b if j == i else jnp.zeros((b.shape[0], blocks[j].shape[1]), b.dtype) for j in range(n)],
            axis=1))
    return jnp.concatenate(rows, axis=0)


def kernel(x_prompt, x_sample, cache_k, cache_v, c, c_ctx, norm_g, w_ada, b_ada, w_in,
           q_norm_g, k_norm_g, conv_w, conv_b, pool_w, pool_scale, w_out, final_g):
    depth = w_in.shape[0]
    bp, sp, _ = x_prompt.shape
    bs, ss, _ = x_sample.shape
    n_ctx = cache_k.shape[2]

    cvec = jnp.concatenate(
        [c, c_ctx[None, :], jnp.zeros((ADA_ROWS - bs - 1, D_MODEL), F32)], axis=0)
    mod = _ada(cvec, w_ada, b_ada)
    mod = mod.reshape(depth, ADA_ROWS, 3, D_MODEL)

    w_in_bf = w_in.astype(BF16)
    w_qkv = w_in_bf[:, :, 0:QKV_W]
    o = QKV_W
    col = lambda a, n: w_in_bf[:, :, o + a:o + a + n]
    w_halo = jnp.concatenate([col(512, 256), col(1024, 256), col(1536, 256)], axis=-1)
    w_main = jnp.concatenate([col(0, 512), col(768, 256), col(1280, 256), col(1792, 256)], axis=-1)
    w_out_bf = w_out.astype(BF16)
    tables = _rope_tables(ss)
    ones = jnp.full((HEAD_DIM, HEAD_DIM), 1.0 / HEAD_DIM, BF16)
    bd = _block_diag([ones, ones])
    ck = cache_k.reshape(bs, depth, n_ctx, KV_W)
    cv = cache_v.reshape(bs, depth, n_ctx, KV_W)

    h = x_prompt.reshape(bp * sp, D_MODEL)
    z = x_sample.reshape(bs * ss, D_MODEL)
    new_ks, new_vs = [], []
    for l in range(depth):
        gains = jnp.concatenate(
            [jnp.tile(q_norm_g[l], N_HEADS), jnp.tile(k_norm_g[l], N_KV_HEADS)]).reshape(1, QK_W)
        wpool_bd = _block_diag([pool_w[l, g] for g in range(pool_w.shape[1])]).astype(BF16)
        final = l == depth - 1
        mod_ctx = mod[l, bs:bs + 1]
        mod_lat = mod[l, 0:bs]
        mix = functools.partial(
            _mix, norm_g=norm_g[l], w_halo=w_halo[l], w_main=w_main[l], wout_bf=w_out_bf[l],
            conv_w=conv_w[l], conv_b=conv_b[l], wpool_bd=wpool_bd, pool_scale=pool_scale[l],
            final_g=final_g, final=final)

        q, k, v = _qkv(h, mod_ctx, norm_g[l], w_qkv[l], gains, bd, None, seq=sp, tm=512)
        new_ks.append(k.reshape(bp, sp, N_KV_HEADS, HEAD_DIM))
        new_vs.append(v.reshape(bp, sp, N_KV_HEADS, HEAD_DIM))
        attn = _attention_ctx(q.reshape(bp, sp, ATTN_W), k.reshape(bp, sp, KV_W), v.reshape(bp, sp, KV_W))
        h = mix(attn.reshape(bp * sp, ATTN_W), h, mod_ctx, seq=sp, tm=256)

        q, k, v = _qkv(z, mod_lat, norm_g[l], w_qkv[l], gains, bd, tables, seq=ss, tm=512)
        attn = _attention(q.reshape(bs, ss, ATTN_W), k.reshape(bs, ss, KV_W), v.reshape(bs, ss, KV_W),
                          ck, cv, l, rb=256)
        z = mix(attn.reshape(bs * ss, ATTN_W), z, mod_lat, seq=ss, tm=512)

    y_prompt = h.reshape(bp, sp, D_MODEL)
    y_sample = z.reshape(bs, ss, D_MODEL)
    return (y_prompt, y_sample, jnp.stack(new_ks, axis=1), jnp.stack(new_vs, axis=1))
```

```python
import functools

import jax
import jax.numpy as jnp
from jax import lax
from jax.experimental import pallas as pl
from jax.experimental.pallas import tpu as pltpu

D_MODEL = 1024
GRID_W = 64
N_HEADS = 8
N_KV_HEADS = 2
HEAD_DIM = 64
ATTN_W = N_HEADS * HEAD_DIM
KV_W = N_KV_HEADS * HEAD_DIM
QK_W = ATTN_W + KV_W
QKV_W = QK_W + KV_W
CONV_W = 256
POOL_W = 256
POOL_GROUP = 64
IN_W = 2816
HALO_W = 2 * CONV_W + POOL_W
MAIN_W = ATTN_W + 2 * CONV_W + POOL_W
W_MAIN_OFF = 0
W_HALO_OFF = 2 * HALO_W
W_QKV_OFF = 3 * QKV_W
W_PERM_W = W_QKV_OFF + QKV_W
ROPE_THETA = 10000.0
EPS = 1e-6
LANES = 128
HALO = 8
ADA_ROWS = 8
VMEM_LIMIT = 48 * 1024 * 1024

F32 = jnp.float32
BF16 = jnp.bfloat16


def _silu(z):
    return z * (1.0 / (1.0 + jnp.exp(-z)))


def _dot(a, b):
    return jnp.dot(a, b, preferred_element_type=F32)


def _mod_norm(x, g, mod_ref):
    ms = jnp.mean(x * x, axis=-1, keepdims=True)
    y = x * lax.rsqrt(ms + EPS) * g
    return y * (1.0 + mod_ref[0, 0, 1:2, :]) + mod_ref[0, 0, 0:1, :]


def _ada_kernel(c_ref, w_ref, b_ref, o_ref):
    s = _silu(c_ref[...]).astype(BF16)
    o_ref[0] = _dot(s, w_ref[0].astype(BF16)) + b_ref[0]


def _ada(cvec, w_ada, b_ada):
    depth, _, n3 = w_ada.shape
    tn = 1024
    return pl.pallas_call(
        _ada_kernel,
        grid=(depth, n3 // tn),
        in_specs=[
            pl.BlockSpec((ADA_ROWS, D_MODEL), lambda l, j: (0, 0)),
            pl.BlockSpec((1, D_MODEL, tn), lambda l, j: (l, 0, j)),
            pl.BlockSpec((1, 1, tn), lambda l, j: (l, 0, j)),
        ],
        out_specs=pl.BlockSpec((1, ADA_ROWS, tn), lambda l, j: (l, 0, j)),
        out_shape=jax.ShapeDtypeStruct((depth, ADA_ROWS, n3), F32),
        name="ada_mod",
    )(cvec, w_ada, b_ada.reshape(depth, 1, n3))


def _qkv_kernel(*refs, rope):
    if rope:
        (x_ref, mod_ref, g_ref, w_ref, gains_ref, bd_ref, cos_ref, sa_ref, sb_ref,
         q_ref, k_ref, v_ref) = refs
    else:
        x_ref, mod_ref, g_ref, w_ref, gains_ref, bd_ref = refs[:6]
        q_ref, k_ref, v_ref = refs[-3:]
    xn = _mod_norm(x_ref[...], g_ref[...], mod_ref).astype(BF16)
    bd = bd_ref[...]
    qkv = _dot(xn, w_ref[0])
    v_ref[...] = qkv[:, QK_W:].reshape(v_ref.shape)
    for j in range(QK_W // LANES):
        sl = slice(j * LANES, (j + 1) * LANES)
        t = qkv[:, sl]
        sq = t * t
        hi = sq.astype(BF16)
        lo = (sq - hi.astype(F32)).astype(BF16)
        msq = _dot(hi, bd) + _dot(lo, bd)
        t = t * lax.rsqrt(msq + EPS) * gains_ref[:, sl]
        if rope:
            t = (t * cos_ref[...] + pltpu.roll(t, LANES - 16, 1) * sa_ref[...]
                 + pltpu.roll(t, 16, 1) * sb_ref[...])
        if j < ATTN_W // LANES:
            q_ref[:, sl] = (t * (HEAD_DIM ** -0.5)).astype(BF16)
        else:
            k_ref[...] = t.reshape(k_ref.shape)


def _mod_spec(layer, mod_row, tps):
    if mod_row is None:
        return pl.BlockSpec((1, 1, 3, D_MODEL), lambda i: (layer, i // tps, 0, 0))
    return pl.BlockSpec((1, 1, 3, D_MODEL), lambda i: (layer, mod_row, 0, 0))


def _qkv(x2, mod, mod_row, norm_g, w_in_bf, layer, gains, bd, tables, kv_prev, *, seq, tm):
    tokens = x2.shape[0]
    depth = w_in_bf.shape[0]
    tps = max(seq // tm, 1)
    rope = tables is not None
    row = lambda i: (i, 0)
    const = lambda i: (0, 0)
    in_specs = [
        pl.BlockSpec((tm, D_MODEL), row),
        _mod_spec(layer, mod_row, tps),
        pl.BlockSpec((1, D_MODEL), const),
        pl.BlockSpec((1, D_MODEL, QKV_W), lambda i: (layer, 0, W_QKV_OFF // QKV_W)),
        pl.BlockSpec((1, QK_W), const),
        pl.BlockSpec((LANES, LANES), const),
    ]
    args = [x2, mod, norm_g.reshape(1, D_MODEL), w_in_bf, gains, bd]
    out_shapes = [jax.ShapeDtypeStruct((tokens, ATTN_W), BF16)]
    out_specs = [pl.BlockSpec((tm, ATTN_W), row)]
    aliases = {}
    if rope:
        in_specs += [pl.BlockSpec((tm, LANES), lambda i: (i % tps, 0))] * 3
        args += list(tables)
        out_shapes += [jax.ShapeDtypeStruct((tokens, KV_W), F32)] * 2
        out_specs += [pl.BlockSpec((tm, KV_W), row)] * 2
    else:
        nseq = tm // seq
        out_shapes += [jax.ShapeDtypeStruct((tokens // seq, depth, seq, KV_W), F32)] * 2
        out_specs += [pl.BlockSpec((nseq, 1, seq, KV_W), lambda i: (i, layer, 0, 0))] * 2
        if kv_prev is not None:
            in_specs += [pl.BlockSpec(memory_space=pl.ANY)] * 2
            aliases = {len(args): 1, len(args) + 1: 2}
            args += list(kv_prev)
    return pl.pallas_call(
        functools.partial(_qkv_kernel, rope=rope),
        grid=(tokens // tm,),
        in_specs=in_specs,
        out_specs=out_specs,
        out_shape=out_shapes,
        input_output_aliases=aliases,
        compiler_params=pltpu.CompilerParams(
            dimension_semantics=("arbitrary",), vmem_limit_bytes=VMEM_LIMIT),
        name="qkv_rope" if rope else "qkv",
    )(*args)


def _attn_kernel(*refs, n, n_ctx, rb):
    if n_ctx:
        q_ref, k_ref, v_ref, ck_ref, cv_ref, o_ref, kt, vab, q2, acc, s0, s1, p0, p1 = refs
    else:
        q_ref, k_ref, v_ref, o_ref, kt, vab, q2, acc, s0, s1, p0, p1 = refs
    kv = pl.program_id(1)
    nr = 2 * n // rb

    def fill(k_src, v_src, r0, m):
        lane = lax.broadcasted_iota(jnp.int32, (m, LANES), 1)
        own = (lane // HEAD_DIM) == kv
        low = lane < HEAD_DIM
        k_own = jnp.where(own, k_src, 0.0)
        k_oth = pltpu.roll(k_own, HEAD_DIM, 1)
        v_own = jnp.where(own, v_src, 0.0)
        v_oth = pltpu.roll(v_own, HEAD_DIM, 1)
        is0 = kv == 0
        kt[0, :, r0:r0 + m] = jnp.where(is0, k_own, k_oth).T.astype(BF16)
        kt[1, :, r0:r0 + m] = jnp.where(is0, k_oth, k_own).T.astype(BF16)
        vab[0, r0:r0 + m, 0:LANES] = jnp.where(is0, v_own, v_oth).astype(BF16)
        vab[1, r0:r0 + m, 0:LANES] = jnp.where(is0, v_oth, v_own).astype(BF16)
        vab[0, r0:r0 + m, LANES:] = jnp.where(low, 1.0, 0.0).astype(BF16)
        vab[1, r0:r0 + m, LANES:] = jnp.where(low, 0.0, 1.0).astype(BF16)

    piece = 256
    for r0 in range(0, n, piece):
        fill(k_ref[0, r0:r0 + piece, :], v_ref[0, r0:r0 + piece, :], r0, piece)
    if n_ctx:
        fill(ck_ref[0, 0], cv_ref[0, 0], n, n_ctx)
    q2[0:n, :] = q_ref[0, :, 0:LANES]
    q2[n:, :] = q_ref[0, :, LANES:]

    def scores(r, slot, dst):
        dst[...] = _dot(q2[pl.ds(pl.multiple_of(r * rb, rb), rb), :], kt[slot])

    def numer(src, dst):
        for i in range(0, rb, 16):
            s = src[i:i + 16, :]
            m = jnp.max(s, axis=-1, keepdims=True)
            dst[i:i + 16, :] = jnp.exp(s - m).astype(BF16)

    def weigh(src, slot, r, first):
        pv = _dot(src[...], vab[slot])
        rows = pl.ds(pl.multiple_of(r * rb, rb), rb)
        if first:
            acc[rows, :] = pv
        else:
            acc[rows, :] += pv

    p1[...] = jnp.zeros_like(p1)
    acc[nr * rb:, :] = jnp.zeros((rb, 2 * LANES), F32)
    scores(0, 0, s0)

    def body(r, carry):
        scores(r, 1, s1)
        numer(s0, p0)
        weigh(p1, 1, jnp.where(r == 0, nr, r - 1), False)
        scores(jnp.minimum(r + 1, nr - 1), 0, s0)
        numer(s1, p1)
        weigh(p0, 0, r, True)
        return carry

    lax.fori_loop(0, nr, body, 0, unroll=4)
    weigh(p1, 1, nr - 1, False)

    ch = 256
    for t in range(2):
        for r0 in range(0, n, ch):
            a = acc[t * n + r0:t * n + r0 + ch, :]
            o_ref[0, r0:r0 + ch, t * LANES:(t + 1) * LANES] = a[:, 0:LANES] / a[:, LANES:]


def _attention(q, k, v, cache_k, cache_v, layer, *, rb):
    bsz, n, _ = q.shape
    n_ctx = 0 if cache_k is None else cache_k.shape[2]
    t_all = n + n_ctx
    gw = 2 * LANES
    qmap = lambda b, g: (b, 0, g)
    kmap = lambda b, g: (b, 0, 0)
    in_specs = [
        pl.BlockSpec((1, n, gw), qmap),
        pl.BlockSpec((1, n, KV_W), kmap),
        pl.BlockSpec((1, n, KV_W), kmap),
    ]
    args = [q, k, v]
    if n_ctx:
        cmap = lambda b, g: (b, layer, 0, 0)
        in_specs += [pl.BlockSpec((1, 1, n_ctx, KV_W), cmap)] * 2
        args += [cache_k, cache_v]
    return pl.pallas_call(
        functools.partial(_attn_kernel, n=n, n_ctx=n_ctx, rb=rb),
        grid=(bsz, N_KV_HEADS),
        in_specs=in_specs,
        out_specs=pl.BlockSpec((1, n, gw), qmap),
        out_shape=jax.ShapeDtypeStruct((bsz, n, ATTN_W), F32),
        scratch_shapes=[
            pltpu.VMEM((2, LANES, t_all), BF16),
            pltpu.VMEM((2, t_all, 2 * LANES), BF16),
            pltpu.VMEM((2 * n, LANES), BF16),
            pltpu.VMEM((2 * n + rb, 2 * LANES), F32),
            pltpu.VMEM((rb, t_all), F32),
            pltpu.VMEM((rb, t_all), F32),
            pltpu.VMEM((rb, t_all), BF16),
            pltpu.VMEM((rb, t_all), BF16),
        ],
        compiler_params=pltpu.CompilerParams(
            dimension_semantics=("arbitrary", "arbitrary"), vmem_limit_bytes=VMEM_LIMIT),
        name="attn_lat" if n_ctx else "attn_ctx",
    )(*args)


def _attn_ctx_kernel(q_ref, k_ref, v_ref, o_ref, *, n):
    kt_all = k_ref[0, 0].T
    v = v_ref[0, 0]
    vr = pltpu.roll(v, HEAD_DIM, 1)
    low = lax.broadcasted_iota(jnp.int32, (n, LANES), 1) < HEAD_DIM
    ones_lo = jnp.where(low, 1.0, 0.0)
    zk = jnp.zeros((HEAD_DIM, n), F32)
    gw = 2 * LANES
    for g in range(N_KV_HEADS):
        kg = kt_all[g * HEAD_DIM:(g + 1) * HEAD_DIM, :]
        kts = (jnp.concatenate([kg, zk], axis=0).astype(BF16),
               jnp.concatenate([zk, kg], axis=0).astype(BF16))
        v_lo, v_hi = (v, vr) if g == 0 else (vr, v)
        vs = (jnp.concatenate([jnp.where(low, v_lo, 0.0), ones_lo], axis=1).astype(BF16),
              jnp.concatenate([jnp.where(low, 0.0, v_hi), 1.0 - ones_lo], axis=1).astype(BF16))
        q2 = jnp.concatenate([q_ref[0, :, g * gw:g * gw + LANES],
                              q_ref[0, :, g * gw + LANES:(g + 1) * gw]], axis=0)
        acc = None
        for kt_s, v_s in zip(kts, vs):
            s = _dot(q2, kt_s)
            m = jnp.max(s, axis=-1, keepdims=True)
            pv = _dot(jnp.exp(s - m).astype(BF16), v_s)
            acc = pv if acc is None else acc + pv
        o = acc[:, 0:LANES] / acc[:, LANES:]
        o_ref[0, :, g * gw:g * gw + LANES] = o[0:n]
        o_ref[0, :, g * gw + LANES:(g + 1) * gw] = o[n:]


def _attention_ctx(q, k, v, layer):
    bsz, n, _ = q.shape
    blk = lambda w: pl.BlockSpec((1, n, w), lambda b: (b, 0, 0))
    kv_blk = pl.BlockSpec((1, 1, n, KV_W), lambda b: (b, layer, 0, 0))
    return pl.pallas_call(
        functools.partial(_attn_ctx_kernel, n=n),
        grid=(bsz,),
        in_specs=[blk(ATTN_W), kv_blk, kv_blk],
        out_specs=blk(ATTN_W),
        out_shape=jax.ShapeDtypeStruct((bsz, n, ATTN_W), F32),
        compiler_params=pltpu.CompilerParams(
            dimension_semantics=("arbitrary",), vmem_limit_bytes=VMEM_LIMIT),
        name="attn_ctx",
    )(q, k, v)


def _mix_kernel(attn_ref, x_ref, xp_ref, xn_ref, mod_ref, g_ref, wm_ref, wh_ref, wout_ref,
                convw_ref, convb_ref, wpool_ref, pscale_ref, fg_ref,
                o_ref, xe, yext, uext, *, tm, tps, seq, final):
    j = pl.program_id(0) % tps
    pm = jnp.where(j == 0, 0.0, 1.0)
    nm = jnp.where(j == tps - 1, 0.0, 1.0)
    c = CONV_W
    g = g_ref[...]
    x = x_ref[...]
    xe[0:tm, :] = _mod_norm(x, g, mod_ref).astype(BF16)
    halo = jnp.concatenate([xn_ref[...], xp_ref[...]], axis=0)
    xe[tm:, :] = _mod_norm(halo, g, mod_ref).astype(BF16)

    ph = _dot(xe[...], wh_ref[0])
    pmn = _dot(xe[0:tm, :], wm_ref[0])
    z_a = pmn[:, 0:ATTN_W]
    b_c = pmn[:, ATTN_W:ATTN_W + c]
    z_c = pmn[:, ATTN_W + c:ATTN_W + 2 * c]
    z_p = pmn[:, ATTN_W + 2 * c:]
    u_all = ph[:, 2 * c:]

    y = ph[:, 0:c] * ph[:, c:2 * c]
    yext[0:HALO, :] = y[tm + HALO:, :] * pm
    yext[HALO:HALO + tm, :] = y[0:tm, :]
    yext[HALO + tm:, :] = y[tm:tm + HALO, :] * nm
    conv = (yext[HALO - 1:HALO - 1 + tm, :] * convw_ref[0:1, :]
            + yext[HALO:HALO + tm, :] * convw_ref[1:2, :]
            + yext[HALO + 1:HALO + 1 + tm, :] * convw_ref[2:3, :] + convb_ref[...])
    conv_o = (b_c * conv * _silu(z_c)).astype(BF16)

    u = u_all[0:tm, :]
    uext[0:HALO, :] = u_all[tm + HALO:, :] * pm
    uext[HALO:HALO + tm, :] = u
    uext[HALO + tm:, :] = u_all[tm:tm + HALO, :] * nm

    def win(k0, k1, sl):
        tot = None
        for k in range(k0, k1):
            t = uext[HALO + k:HALO + k + tm, sl]
            tot = t if tot is None else tot + t
        return tot

    lane = lax.broadcasted_iota(jnp.int32, (tm, LANES), 1)
    pos = j * tm + lax.broadcasted_iota(jnp.int32, (tm, LANES), 0)
    low = lane < POOL_GROUP
    ds = []
    for ti, (wa, wb) in enumerate(((2, 4), (8, 16))):
        sl = slice(ti * LANES, (ti + 1) * LANES)
        s_a = win(-(wa // 2), wa // 2, sl)
        s_b = s_a + win(-(wb // 2), -(wa // 2), sl) + win(wa // 2, wb // 2, sl)
        half = jnp.where(low, wa // 2, wb // 2)
        cnt = jnp.minimum(pos + half - 1, seq - 1) - jnp.maximum(pos - half, 0) + 1
        tot = jnp.where(low, s_a, s_b)
        ds.append(tot / cnt.astype(F32) - u[:, sl])
    d = jnp.concatenate(ds, axis=-1).astype(BF16)
    pool_o = (_dot(d, wpool_ref[...]) * pscale_ref[...] * _silu(z_p)).astype(BF16)

    attn_o = (attn_ref[...] * _silu(z_a)).astype(BF16)
    a = ATTN_W
    out = (_dot(attn_o, wout_ref[0, 0:a, :]) + _dot(conv_o, wout_ref[0, a:a + c, :])
           + _dot(pool_o, wout_ref[0, a + c:, :]))
    res = x + mod_ref[0, 0, 2:3, :] * out
    if final:
        ms = jnp.mean(res * res, axis=-1, keepdims=True)
        res = res * lax.rsqrt(ms + EPS) * fg_ref[...]
    o_ref[...] = res


def _mix(attn2, x2, mod, mod_row, norm_g, w_in_bf, wout_bf, layer, conv_w, conv_b, wpool_bd, pool_scale,
         final_g, *, seq, tm, final):
    tokens = x2.shape[0]
    tps = seq // tm
    hb = tm // HALO
    nhb = tokens // HALO
    row = lambda i: (i, 0)
    const = lambda i: (0, 0)
    in_specs = [
        pl.BlockSpec((tm, ATTN_W), row),
        pl.BlockSpec((tm, D_MODEL), row),
        pl.BlockSpec((HALO, D_MODEL), lambda i: (jnp.maximum(i * hb - 1, 0), 0)),
        pl.BlockSpec((HALO, D_MODEL), lambda i: (jnp.minimum((i + 1) * hb, nhb - 1), 0)),
        _mod_spec(layer, mod_row, tps),
        pl.BlockSpec((1, D_MODEL), const),
        pl.BlockSpec((1, D_MODEL, MAIN_W), lambda i: (layer, 0, W_MAIN_OFF // MAIN_W)),
        pl.BlockSpec((1, D_MODEL, HALO_W), lambda i: (layer, 0, W_HALO_OFF // HALO_W)),
        pl.BlockSpec((1, D_MODEL, D_MODEL), lambda i: (layer, 0, 0)),
        pl.BlockSpec((3, CONV_W), const),
        pl.BlockSpec((1, CONV_W), const),
        pl.BlockSpec((POOL_W, POOL_W), const),
        pl.BlockSpec((1, POOL_W), const),
        pl.BlockSpec((1, D_MODEL), const),
    ]
    return pl.pallas_call(
        functools.partial(_mix_kernel, tm=tm, tps=tps, seq=seq, final=final),
        grid=(tokens // tm,),
        in_specs=in_specs,
        out_specs=pl.BlockSpec((tm, D_MODEL), row),
        out_shape=jax.ShapeDtypeStruct((tokens, D_MODEL), F32),
        scratch_shapes=[
            pltpu.VMEM((tm + 2 * HALO, D_MODEL), BF16),
            pltpu.VMEM((tm + 2 * HALO, CONV_W), F32),
            pltpu.VMEM((tm + 2 * HALO, POOL_W), F32),
        ],
        compiler_params=pltpu.CompilerParams(
            dimension_semantics=("arbitrary",), vmem_limit_bytes=VMEM_LIMIT),
        name="mix_final" if final else "mix",
    )(attn2, x2, x2, x2, mod, norm_g.reshape(1, D_MODEL), w_in_bf, w_in_bf, wout_bf,
      conv_w, conv_b.reshape(1, CONV_W), wpool_bd, pool_scale.reshape(1, POOL_W),
      final_g.reshape(1, D_MODEL))


def _rope_tables(n):
    rows = n // GRID_W
    row = jnp.repeat(jnp.arange(rows), GRID_W).astype(F32)
    col = jnp.tile(jnp.arange(GRID_W), rows).astype(F32)
    half = HEAD_DIM // 2
    inv = 1.0 / (ROPE_THETA ** (jnp.arange(0, half, 2, dtype=F32) / half))
    ar = row[:, None] * inv
    ac = col[:, None] * inv
    ang = jnp.tile(jnp.concatenate([ar, ar, ac, ac], axis=-1), (1, LANES // HEAD_DIM))
    cos = jnp.cos(ang)
    sin = jnp.sin(ang)
    second = (jnp.arange(LANES) % half) >= (half // 2)
    return cos, jnp.where(second, 0.0, -sin), jnp.where(second, sin, 0.0)


def _block_diag(blocks):
    n = len(blocks)
    rows = []
    for i, b in enumerate(blocks):
        rows.append(jnp.concatenate(
            [b if j == i else jnp.zeros((b.shape[0], blocks[j].shape[1]), b.dtype) for j in range(n)],
            axis=1))
    return jnp.concatenate(rows, axis=0)


def kernel(x_prompt, x_sample, cache_k, cache_v, c, c_ctx, norm_g, w_ada, b_ada, w_in,
           q_norm_g, k_norm_g, conv_w, conv_b, pool_w, pool_scale, w_out, final_g):
    depth = w_in.shape[0]
    bp, sp, _ = x_prompt.shape
    bs, ss, _ = x_sample.shape
    n_ctx = cache_k.shape[2]

    cvec = jnp.concatenate(
        [c, c_ctx[None, :], jnp.zeros((ADA_ROWS - bs - 1, D_MODEL), F32)], axis=0)
    mod = _ada(cvec, w_ada, b_ada)
    mod = mod.reshape(depth, ADA_ROWS, 3, D_MODEL)

    o = QKV_W
    col = lambda a, n: w_in[:, :, o + a:o + a + n]
    w_in_bf = jnp.concatenate(
        [col(0, 512), col(768, 256), col(1280, 256), col(1792, 256),
         jnp.zeros((depth, D_MODEL, W_HALO_OFF - MAIN_W), F32),
         col(512, 256), col(1024, 256), col(1536, 256), w_in[:, :, 0:o]], axis=-1).astype(BF16)
    w_out_bf = w_out.astype(BF16)
    tables = _rope_tables(ss)
    ones = jnp.full((HEAD_DIM, HEAD_DIM), 1.0 / HEAD_DIM, BF16)
    bd = _block_diag([ones, ones])
    ck = cache_k.reshape(bs, depth, n_ctx, KV_W)
    cv = cache_v.reshape(bs, depth, n_ctx, KV_W)

    h = x_prompt.reshape(bp * sp, D_MODEL)
    z = x_sample.reshape(bs * ss, D_MODEL)
    kv_ctx = None
    for l in range(depth):
        gains = jnp.concatenate(
            [jnp.tile(q_norm_g[l], N_HEADS), jnp.tile(k_norm_g[l], N_KV_HEADS)]).reshape(1, QK_W)
        wpool_bd = _block_diag([pool_w[l, g] for g in range(pool_w.shape[1])]).astype(BF16)
        final = l == depth - 1
        mix = functools.partial(
            _mix, norm_g=norm_g[l], w_in_bf=w_in_bf, wout_bf=w_out_bf, layer=l,
            conv_w=conv_w[l], conv_b=conv_b[l], wpool_bd=wpool_bd, pool_scale=pool_scale[l],
            final_g=final_g, final=final)

        q, k, v = _qkv(h, mod, bs, norm_g[l], w_in_bf, l, gains, bd, None, kv_ctx, seq=sp, tm=512)
        kv_ctx = (k, v)
        attn = _attention_ctx(q.reshape(bp, sp, ATTN_W), k, v, l)
        h = mix(attn.reshape(bp * sp, ATTN_W), h, mod, bs, seq=sp, tm=256)

        q, k, v = _qkv(z, mod, None, norm_g[l], w_in_bf, l, gains, bd, tables, None, seq=ss, tm=512)
        attn = _attention(q.reshape(bs, ss, ATTN_W), k.reshape(bs, ss, KV_W), v.reshape(bs, ss, KV_W),
                          ck, cv, l, rb=256)
        z = mix(attn.reshape(bs * ss, ATTN_W), z, mod, None, seq=ss, tm=512)

    y_prompt = h.reshape(bp, sp, D_MODEL)
    y_sample = z.reshape(bs, ss, D_MODEL)
    new_k, new_v = (a.reshape(bp, depth, sp, N_KV_HEADS, HEAD_DIM) for a in kv_ctx)
    return (y_prompt, y_sample, new_k, new_v)
```

```python
import functools

import jax
import jax.numpy as jnp
from jax import lax
from jax.experimental import pallas as pl
from jax.experimental.pallas import tpu as pltpu

D_MODEL = 1024
GRID_W = 64
N_HEADS = 8
N_KV_HEADS = 2
HEAD_DIM = 64
ATTN_W = N_HEADS * HEAD_DIM
KV_W = N_KV_HEADS * HEAD_DIM
QK_W = ATTN_W + KV_W
QKV_W = QK_W + KV_W
CONV_W = 256
POOL_W = 256
POOL_GROUP = 64
IN_W = 2816
COL_ATTN_GATE = QKV_W
COL_CONV_H = COL_ATTN_GATE + ATTN_W
COL_CONV_B = COL_CONV_H + CONV_W
COL_CONV_C = COL_CONV_B + CONV_W
COL_CONV_GATE = COL_CONV_C + CONV_W
COL_POOL_U = COL_CONV_GATE + CONV_W
COL_POOL_GATE = COL_POOL_U + POOL_W
ROPE_THETA = 10000.0
EPS = 1e-6
LANES = 128
HALO = 8
ADA_ROWS = 8
VMEM_LIMIT = 48 * 1024 * 1024

F32 = jnp.float32
BF16 = jnp.bfloat16


def _silu(z):
    return z * (1.0 / (1.0 + jnp.exp(-z)))


def _dot(a, b):
    return jnp.dot(a, b, preferred_element_type=F32)


def _mod_norm(x, g, mod_ref):
    ms = jnp.mean(x * x, axis=-1, keepdims=True)
    gs = g * (1.0 + mod_ref[0, 0, 1:2, :])
    return x * lax.rsqrt(ms + EPS) * gs + mod_ref[0, 0, 0:1, :]


def _ada_kernel(c_ref, w_ref, b_ref, o_ref):
    s = _silu(c_ref[...]).astype(BF16)
    o_ref[0] = _dot(s, w_ref[0].astype(BF16)) + b_ref[0]


def _ada(cvec, w_ada, b_ada):
    depth, _, n3 = w_ada.shape
    tn = 1024
    return pl.pallas_call(
        _ada_kernel,
        grid=(depth, n3 // tn),
        in_specs=[
            pl.BlockSpec((ADA_ROWS, D_MODEL), lambda l, j: (0, 0)),
            pl.BlockSpec((1, D_MODEL, tn), lambda l, j: (l, 0, j)),
            pl.BlockSpec((1, 1, tn), lambda l, j: (l, 0, j)),
        ],
        out_specs=pl.BlockSpec((1, ADA_ROWS, tn), lambda l, j: (l, 0, j)),
        out_shape=jax.ShapeDtypeStruct((depth, ADA_ROWS, n3), F32),
        name="ada_mod",
    )(cvec, w_ada, b_ada.reshape(depth, 1, n3))


def _qkv_kernel(*refs, rope, nsub):
    if rope:
        (x_ref, mod_ref, g_ref, w_ref, gains_ref, bd_ref, cos_ref, sin_ref,
         q_ref, k_ref, v_ref) = refs
    else:
        x_ref, mod_ref, g_ref, w_ref, gains_ref, bd_ref = refs[:6]
        q_ref, k_ref, v_ref = refs[-3:]
    tm = x_ref.shape[0]
    sub = tm // nsub
    bd = bd_ref[...]
    g = g_ref[...]
    if rope:
        lane = lax.broadcasted_iota(jnp.int32, (sub, LANES), 1)
        second = (lane % (HEAD_DIM // 2)) >= (HEAD_DIM // 4)
    qkvs = []
    for s in range(nsub):
        xn = _mod_norm(x_ref[s * sub:(s + 1) * sub, :], g, mod_ref).astype(BF16)
        qkvs.append(_dot(xn, w_ref[0]))
    k_parts, v_parts = [], []
    for s, qkv in enumerate(qkvs):
        rows = slice(s * sub, (s + 1) * sub)
        v_parts.append(qkv[:, QK_W:])
        for j in range(QK_W // LANES):
            sl = slice(j * LANES, (j + 1) * LANES)
            t = qkv[:, sl]
            sq = t * t
            hi = sq.astype(BF16)
            lo = (sq - hi.astype(F32)).astype(BF16)
            msq = _dot(hi, bd) + _dot(lo, bd)
            t = t * lax.rsqrt(msq + EPS) * gains_ref[0, :, sl]
            if rope:
                partner = jnp.where(second, pltpu.roll(t, HEAD_DIM // 4, 1),
                                    pltpu.roll(t, LANES - HEAD_DIM // 4, 1))
                t = t * cos_ref[rows, :] + partner * sin_ref[rows, :]
            if j < ATTN_W // LANES:
                q_ref[rows, sl] = t.astype(BF16)
            else:
                k_parts.append(t)
    k_ref[...] = jnp.concatenate(k_parts, axis=0).reshape(k_ref.shape)
    v_ref[...] = jnp.concatenate(v_parts, axis=0).reshape(v_ref.shape)


def _mod_spec(layer, mod_row, tps):
    if mod_row is None:
        return pl.BlockSpec((1, 1, 3, D_MODEL), lambda i: (layer, i // tps, 0, 0))
    return pl.BlockSpec((1, 1, 3, D_MODEL), lambda i: (layer, mod_row, 0, 0))


def _qkv(x2, mod, mod_row, norm_g, w_in_bf, layer, gains, bd, tables, kv_prev, *, seq, tm):
    tokens = x2.shape[0]
    depth = w_in_bf.shape[0]
    tps = max(seq // tm, 1)
    rope = tables is not None
    row = lambda i: (i, 0)
    const = lambda i: (0, 0)
    in_specs = [
        pl.BlockSpec((tm, D_MODEL), row),
        _mod_spec(layer, mod_row, tps),
        pl.BlockSpec((1, D_MODEL), const),
        pl.BlockSpec((1, D_MODEL, QKV_W), lambda i: (layer, 0, 0)),
        pl.BlockSpec((1, 1, QK_W), lambda i: (layer, 0, 0)),
        pl.BlockSpec((LANES, LANES), const),
    ]
    args = [x2, mod, norm_g.reshape(1, D_MODEL), w_in_bf, gains, bd]
    out_shapes = [jax.ShapeDtypeStruct((tokens, ATTN_W), BF16)]
    out_specs = [pl.BlockSpec((tm, ATTN_W), row)]
    aliases = {}
    if rope:
        in_specs += [pl.BlockSpec((tm, LANES), lambda i: (i % tps, 0))] * 2
        args += list(tables)
        out_shapes += [jax.ShapeDtypeStruct((tokens, KV_W), F32)] * 2
        out_specs += [pl.BlockSpec((tm, KV_W), row)] * 2
    else:
        nseq = tm // seq
        out_shapes += [jax.ShapeDtypeStruct((tokens // seq, depth, seq, KV_W), F32)] * 2
        out_specs += [pl.BlockSpec((nseq, 1, seq, KV_W), lambda i: (i, layer, 0, 0))] * 2
        if kv_prev is not None:
            in_specs += [pl.BlockSpec(memory_space=pl.ANY)] * 2
            aliases = {len(args): 1, len(args) + 1: 2}
            args += list(kv_prev)
    return pl.pallas_call(
        functools.partial(_qkv_kernel, rope=rope, nsub=1 if rope else 2),
        grid=(tokens // tm,),
        in_specs=in_specs,
        out_specs=out_specs,
        out_shape=out_shapes,
        input_output_aliases=aliases,
        compiler_params=pltpu.CompilerParams(
            dimension_semantics=("arbitrary",), vmem_limit_bytes=VMEM_LIMIT),
        name="qkv_rope" if rope else "qkv",
    )(*args)


def _attn_kernel(*refs, n, n_ctx, rb):
    if n_ctx:
        q_ref, k_ref, v_ref, ck_ref, cv_ref, o_ref, kt, vab, q2, acc, s0, s1, p0, p1 = refs
    else:
        q_ref, k_ref, v_ref, o_ref, kt, vab, q2, acc, s0, s1, p0, p1 = refs
    kv = pl.program_id(1)
    nr = 2 * n // rb

    def fill(k_src, v_src, r0, m):
        lane = lax.broadcasted_iota(jnp.int32, (m, LANES), 1)
        own = (lane // HEAD_DIM) == kv
        low = lane < HEAD_DIM
        k_own = jnp.where(own, k_src, 0.0)
        k_oth = pltpu.roll(k_own, HEAD_DIM, 1)
        v_own = jnp.where(own, v_src, 0.0)
        v_oth = pltpu.roll(v_own, HEAD_DIM, 1)
        is0 = kv == 0
        kt[0, :, r0:r0 + m] = jnp.where(is0, k_own, k_oth).T.astype(BF16)
        kt[1, :, r0:r0 + m] = jnp.where(is0, k_oth, k_own).T.astype(BF16)
        vab[0, r0:r0 + m, 0:LANES] = jnp.where(is0, v_own, v_oth).astype(BF16)
        vab[1, r0:r0 + m, 0:LANES] = jnp.where(is0, v_oth, v_own).astype(BF16)
        vab[0, r0:r0 + m, LANES:] = jnp.where(low, 1.0, 0.0).astype(BF16)
        vab[1, r0:r0 + m, LANES:] = jnp.where(low, 0.0, 1.0).astype(BF16)

    piece = 256
    for r0 in range(0, n, piece):
        fill(k_ref[0, r0:r0 + piece, :], v_ref[0, r0:r0 + piece, :], r0, piece)
    if n_ctx:
        fill(ck_ref[0, 0], cv_ref[0, 0], n, n_ctx)
    q2[0:n, :] = q_ref[0, :, 0:LANES]
    q2[n:, :] = q_ref[0, :, LANES:]

    def scores(r, slot, dst):
        dst[...] = _dot(q2[pl.ds(pl.multiple_of(r * rb, rb), rb), :], kt[slot])

    def numer(src, dst):
        for i in range(0, rb, 16):
            s = src[i:i + 16, :]
            m = jnp.max(s, axis=-1, keepdims=True)
            dst[i:i + 16, :] = jnp.exp(s - m).astype(BF16)

    def weigh(src, slot, r, first):
        pv = _dot(src[...], vab[slot])
        rows = pl.ds(pl.multiple_of(r * rb, rb), rb)
        if first:
            acc[rows, :] = pv
        else:
            acc[rows, :] += pv

    p1[...] = jnp.zeros_like(p1)
    acc[nr * rb:, :] = jnp.zeros((rb, 2 * LANES), F32)
    scores(0, 0, s0)

    def body(r, carry):
        scores(r, 1, s1)
        numer(s0, p0)
        weigh(p1, 1, jnp.where(r == 0, nr, r - 1), False)
        scores(jnp.minimum(r + 1, nr - 1), 0, s0)
        numer(s1, p1)
        weigh(p0, 0, r, True)
        return carry

    lax.fori_loop(0, nr, body, 0, unroll=4)
    weigh(p1, 1, nr - 1, False)

    ch = 256
    for t in range(2):
        for r0 in range(0, n, ch):
            a = acc[t * n + r0:t * n + r0 + ch, :]
            o_ref[0, r0:r0 + ch, t * LANES:(t + 1) * LANES] = a[:, 0:LANES] / a[:, LANES:]


def _attention(q, k, v, cache_k, cache_v, layer, *, rb):
    bsz, n, _ = q.shape
    n_ctx = 0 if cache_k is None else cache_k.shape[2]
    t_all = n + n_ctx
    gw = 2 * LANES
    qmap = lambda b, g: (b, 0, g)
    kmap = lambda b, g: (b, 0, 0)
    in_specs = [
        pl.BlockSpec((1, n, gw), qmap),
        pl.BlockSpec((1, n, KV_W), kmap),
        pl.BlockSpec((1, n, KV_W), kmap),
    ]
    args = [q, k, v]
    if n_ctx:
        cmap = lambda b, g: (b, layer, 0, 0)
        in_specs += [pl.BlockSpec((1, 1, n_ctx, KV_W), cmap)] * 2
        args += [cache_k, cache_v]
    return pl.pallas_call(
        functools.partial(_attn_kernel, n=n, n_ctx=n_ctx, rb=rb),
        grid=(bsz, N_KV_HEADS),
        in_specs=in_specs,
        out_specs=pl.BlockSpec((1, n, gw), qmap),
        out_shape=jax.ShapeDtypeStruct((bsz, n, ATTN_W), F32),
        scratch_shapes=[
            pltpu.VMEM((2, LANES, t_all), BF16),
            pltpu.VMEM((2, t_all, 2 * LANES), BF16),
            pltpu.VMEM((2 * n, LANES), BF16),
            pltpu.VMEM((2 * n + rb, 2 * LANES), F32),
            pltpu.VMEM((rb, t_all), F32),
            pltpu.VMEM((rb, t_all), F32),
            pltpu.VMEM((rb, t_all), BF16),
            pltpu.VMEM((rb, t_all), BF16),
        ],
        compiler_params=pltpu.CompilerParams(
            dimension_semantics=("arbitrary", "arbitrary"), vmem_limit_bytes=VMEM_LIMIT),
        name="attn_lat" if n_ctx else "attn_ctx",
    )(*args)


def _attn_ctx_kernel(q_ref, k_ref, v_ref, o_ref, *, n):
    kt_all = k_ref[0, 0].T
    v = v_ref[0, 0]
    vr = pltpu.roll(v, HEAD_DIM, 1)
    low = lax.broadcasted_iota(jnp.int32, (n, LANES), 1) < HEAD_DIM
    ones_lo = jnp.where(low, 1.0, 0.0)
    zk = jnp.zeros((HEAD_DIM, n), F32)
    gw = 2 * LANES
    for g in range(N_KV_HEADS):
        kg = kt_all[g * HEAD_DIM:(g + 1) * HEAD_DIM, :]
        kts = (jnp.concatenate([kg, zk], axis=0).astype(BF16),
               jnp.concatenate([zk, kg], axis=0).astype(BF16))
        v_lo, v_hi = (v, vr) if g == 0 else (vr, v)
        vs = (jnp.concatenate([jnp.where(low, v_lo, 0.0), ones_lo], axis=1).astype(BF16),
              jnp.concatenate([jnp.where(low, 0.0, v_hi), 1.0 - ones_lo], axis=1).astype(BF16))
        q2 = jnp.concatenate([q_ref[0, :, g * gw:g * gw + LANES],
                              q_ref[0, :, g * gw + LANES:(g + 1) * gw]], axis=0)
        acc = None
        for kt_s, v_s in zip(kts, vs):
            s = _dot(q2, kt_s)
            m = jnp.max(s, axis=-1, keepdims=True)
            pv = _dot(jnp.exp(s - m).astype(BF16), v_s)
            acc = pv if acc is None else acc + pv
        o = acc[:, 0:LANES] / acc[:, LANES:]
        o_ref[0, :, g * gw:g * gw + LANES] = o[0:n]
        o_ref[0, :, g * gw + LANES:(g + 1) * gw] = o[n:]


def _attention_ctx(q, k, v, layer):
    bsz, n, _ = q.shape
    blk = lambda w: pl.BlockSpec((1, n, w), lambda b: (b, 0, 0))
    kv_blk = pl.BlockSpec((1, 1, n, KV_W), lambda b: (b, layer, 0, 0))
    return pl.pallas_call(
        functools.partial(_attn_ctx_kernel, n=n),
        grid=(bsz,),
        in_specs=[blk(ATTN_W), kv_blk, kv_blk],
        out_specs=blk(ATTN_W),
        out_shape=jax.ShapeDtypeStruct((bsz, n, ATTN_W), F32),
        compiler_params=pltpu.CompilerParams(
            dimension_semantics=("arbitrary",), vmem_limit_bytes=VMEM_LIMIT),
        name="attn_ctx",
    )(q, k, v)


def _mix_kernel(attn_ref, x_ref, xp_ref, xn_ref, mod_ref, g_ref, w_ref, wout_ref,
                convw_ref, convb_ref, wpool_ref, pscale_ref, fg_ref,
                o_ref, xe, yext, uext, *, tm, tps, seq, final):
    j = pl.program_id(0) % tps
    pm = jnp.where(j == 0, 0.0, 1.0)
    nm = jnp.where(j == tps - 1, 0.0, 1.0)
    c = CONV_W
    g = g_ref[...]
    x = x_ref[...]
    xe[0:tm, :] = _mod_norm(x, g, mod_ref).astype(BF16)
    halo = jnp.concatenate([xn_ref[...], xp_ref[...]], axis=0)
    xe[tm:, :] = _mod_norm(halo, g, mod_ref).astype(BF16)

    def proj(rows, off, width):
        return _dot(xe[0:rows, :], w_ref[0, :, off:off + width])

    h_all = proj(tm + 2 * HALO, COL_CONV_H, c)
    c_all = proj(tm + 2 * HALO, COL_CONV_C, c)
    u_all = proj(tm + 2 * HALO, COL_POOL_U, POOL_W)
    z_a = proj(tm, COL_ATTN_GATE, ATTN_W)
    b_c = proj(tm, COL_CONV_B, c)
    z_c = proj(tm, COL_CONV_GATE, c)
    z_p = proj(tm, COL_POOL_GATE, POOL_W)

    y = h_all * c_all
    yext[0:HALO, :] = y[tm + HALO:, :] * pm
    yext[HALO:HALO + tm, :] = y[0:tm, :]
    yext[HALO + tm:, :] = y[tm:tm + HALO, :] * nm
    conv = (yext[HALO - 1:HALO - 1 + tm, :] * convw_ref[0:1, :]
            + yext[HALO:HALO + tm, :] * convw_ref[1:2, :]
            + yext[HALO + 1:HALO + 1 + tm, :] * convw_ref[2:3, :] + convb_ref[...])
    conv_o = (b_c * conv * _silu(z_c)).astype(BF16)

    u = u_all[0:tm, :]
    uext[0:HALO, :] = u_all[tm + HALO:, :] * pm
    uext[HALO:HALO + tm, :] = u
    uext[HALO + tm:, :] = u_all[tm:tm + HALO, :] * nm

    def win(k0, k1, sl):
        tot = None
        for k in range(k0, k1):
            t = uext[HALO + k:HALO + k + tm, sl]
            tot = t if tot is None else tot + t
        return tot

    lane = lax.broadcasted_iota(jnp.int32, (tm, LANES), 1)
    pos = j * tm + lax.broadcasted_iota(jnp.int32, (tm, LANES), 0)
    low = lane < POOL_GROUP
    ds = []
    for ti, (wa, wb) in enumerate(((2, 4), (8, 16))):
        sl = slice(ti * LANES, (ti + 1) * LANES)
        s_a = win(-(wa // 2), wa // 2, sl)
        s_b = s_a + win(-(wb // 2), -(wa // 2), sl) + win(wa // 2, wb // 2, sl)
        half = jnp.where(low, wa // 2, wb // 2)
        cnt = jnp.minimum(pos + half - 1, seq - 1) - jnp.maximum(pos - half, 0) + 1
        tot = jnp.where(low, s_a, s_b)
        ds.append(tot / cnt.astype(F32) - u[:, sl])
    d = jnp.concatenate(ds, axis=-1).astype(BF16)
    pool_o = (_dot(d, wpool_ref[...]) * pscale_ref[...] * _silu(z_p)).astype(BF16)

    attn_o = (attn_ref[...] * _silu(z_a)).astype(BF16)
    a = ATTN_W
    out = (_dot(attn_o, wout_ref[0, 0:a, :]) + _dot(conv_o, wout_ref[0, a:a + c, :])
           + _dot(pool_o, wout_ref[0, a + c:, :]))
    res = x + mod_ref[0, 0, 2:3, :] * out
    if final:
        ms = jnp.mean(res * res, axis=-1, keepdims=True)
        res = res * lax.rsqrt(ms + EPS) * fg_ref[...]
    o_ref[...] = res


def _mix(attn2, x2, mod, mod_row, norm_g, w_in_bf, wout_bf, layer, conv_w, conv_b, wpool_bd, pool_scale,
         final_g, *, seq, tm, final):
    tokens = x2.shape[0]
    tps = seq // tm
    hb = tm // HALO
    nhb = tokens // HALO
    row = lambda i: (i, 0)
    const = lambda i: (0, 0)
    in_specs = [
        pl.BlockSpec((tm, ATTN_W), row),
        pl.BlockSpec((tm, D_MODEL), row),
        pl.BlockSpec((HALO, D_MODEL), lambda i: (jnp.maximum(i * hb - 1, 0), 0)),
        pl.BlockSpec((HALO, D_MODEL), lambda i: (jnp.minimum((i + 1) * hb, nhb - 1), 0)),
        _mod_spec(layer, mod_row, tps),
        pl.BlockSpec((1, D_MODEL), const),
        pl.BlockSpec((1, D_MODEL, IN_W), lambda i: (layer, 0, 0)),
        pl.BlockSpec((1, D_MODEL, D_MODEL), lambda i: (layer, 0, 0)),
        pl.BlockSpec((3, CONV_W), const),
        pl.BlockSpec((1, CONV_W), const),
        pl.BlockSpec((POOL_W, POOL_W), const),
        pl.BlockSpec((1, POOL_W), const),
        pl.BlockSpec((1, D_MODEL), const),
    ]
    return pl.pallas_call(
        functools.partial(_mix_kernel, tm=tm, tps=tps, seq=seq, final=final),
        grid=(tokens // tm,),
        in_specs=in_specs,
        out_specs=pl.BlockSpec((tm, D_MODEL), row),
        out_shape=jax.ShapeDtypeStruct((tokens, D_MODEL), F32),
        scratch_shapes=[
            pltpu.VMEM((tm + 2 * HALO, D_MODEL), BF16),
            pltpu.VMEM((tm + 2 * HALO, CONV_W), F32),
            pltpu.VMEM((tm + 2 * HALO, POOL_W), F32),
        ],
        compiler_params=pltpu.CompilerParams(
            dimension_semantics=("arbitrary",), vmem_limit_bytes=VMEM_LIMIT),
        name="mix_final" if final else "mix",
    )(attn2, x2, x2, x2, mod, norm_g.reshape(1, D_MODEL), w_in_bf, wout_bf,
      conv_w, conv_b.reshape(1, CONV_W), wpool_bd, pool_scale.reshape(1, POOL_W),
      final_g.reshape(1, D_MODEL))


def _rope_tables(n):
    rows = n // GRID_W
    row = jnp.repeat(jnp.arange(rows), GRID_W).astype(F32)
    col = jnp.tile(jnp.arange(GRID_W), rows).astype(F32)
    half = HEAD_DIM // 2
    inv = 1.0 / (ROPE_THETA ** (jnp.arange(0, half, 2, dtype=F32) / half))
    ar = row[:, None] * inv
    ac = col[:, None] * inv
    ang = jnp.tile(jnp.concatenate([ar, ar, ac, ac], axis=-1), (1, LANES // HEAD_DIM))
    cos = jnp.cos(ang)
    sin = jnp.sin(ang)
    second = (jnp.arange(LANES) % half) >= (half // 2)
    return cos, jnp.where(second, sin, -sin)


def _block_diag(blocks):
    n = len(blocks)
    rows = []
    for i, b in enumerate(blocks):
        rows.append(jnp.concatenate(
            [b if j == i else jnp.zeros((b.shape[0], blocks[j].shape[1]), b.dtype) for j in range(n)],
            axis=1))
    return jnp.concatenate(rows, axis=0)


def kernel(x_prompt, x_sample, cache_k, cache_v, c, c_ctx, norm_g, w_ada, b_ada, w_in,
           q_norm_g, k_norm_g, conv_w, conv_b, pool_w, pool_scale, w_out, final_g):
    depth = w_in.shape[0]
    bp, sp, _ = x_prompt.shape
    bs, ss, _ = x_sample.shape
    n_ctx = cache_k.shape[2]

    cvec = jnp.concatenate(
        [c, c_ctx[None, :], jnp.zeros((ADA_ROWS - bs - 1, D_MODEL), F32)], axis=0)
    mod = _ada(cvec, w_ada, b_ada)
    mod = mod.reshape(depth, ADA_ROWS, 3, D_MODEL)

    w_in_bf = w_in.astype(BF16)
    w_out_bf = w_out.astype(BF16)
    tables = _rope_tables(ss)
    ones = jnp.full((HEAD_DIM, HEAD_DIM), 1.0 / HEAD_DIM, BF16)
    bd = _block_diag([ones, ones])
    ck = cache_k.reshape(bs, depth, n_ctx, KV_W)
    cv = cache_v.reshape(bs, depth, n_ctx, KV_W)

    h = x_prompt.reshape(bp * sp, D_MODEL)
    z = x_sample.reshape(bs * ss, D_MODEL)
    kv_ctx = None
    gains = jnp.concatenate(
        [jnp.tile(q_norm_g * (HEAD_DIM ** -0.5), (1, N_HEADS)), jnp.tile(k_norm_g, (1, N_KV_HEADS))],
        axis=-1).reshape(depth, 1, QK_W)
    for l in range(depth):
        wpool_bd = _block_diag([pool_w[l, g] for g in range(pool_w.shape[1])]).astype(BF16)
        final = l == depth - 1
        mix = functools.partial(
            _mix, norm_g=norm_g[l], w_in_bf=w_in_bf, wout_bf=w_out_bf, layer=l,
            conv_w=conv_w[l], conv_b=conv_b[l], wpool_bd=wpool_bd, pool_scale=pool_scale[l],
            final_g=final_g, final=final)

        q, k, v = _qkv(h, mod, bs, norm_g[l], w_in_bf, l, gains, bd, None, kv_ctx, seq=sp, tm=512)
        kv_ctx = (k, v)
        attn = _attention_ctx(q.reshape(bp, sp, ATTN_W), k, v, l)
        h = mix(attn.reshape(bp * sp, ATTN_W), h, mod, bs, seq=sp, tm=256)

        q, k, v = _qkv(z, mod, None, norm_g[l], w_in_bf, l, gains, bd, tables, None, seq=ss, tm=512)
        attn = _attention(q.reshape(bs, ss, ATTN_W), k.reshape(bs, ss, KV_W), v.reshape(bs, ss, KV_W),
                          ck, cv, l, rb=256)
        z = mix(attn.reshape(bs * ss, ATTN_W), z, mod, None, seq=ss, tm=512)

    y_prompt = h.reshape(bp, sp, D_MODEL)
    y_sample = z.reshape(bs, ss, D_MODEL)
    new_k, new_v = (a.reshape(bp, depth, sp, N_KV_HEADS, HEAD_DIM) for a in kv_ctx)
    return (y_prompt, y_sample, new_k, new_v)
```

```python
import functools

import jax
import jax.numpy as jnp
from jax import lax
from jax.experimental import pallas as pl
from jax.experimental.pallas import tpu as pltpu

D_MODEL = 1024
GRID_W = 64
N_HEADS = 8
N_KV_HEADS = 2
HEAD_DIM = 64
ATTN_W = N_HEADS * HEAD_DIM
KV_W = N_KV_HEADS * HEAD_DIM
QK_W = ATTN_W + KV_W
QKV_W = QK_W + KV_W
CONV_W = 256
POOL_W = 256
POOL_GROUP = 64
IN_W = 2816
COL_ATTN_GATE = QKV_W
COL_CONV_H = COL_ATTN_GATE + ATTN_W
COL_CONV_B = COL_CONV_H + CONV_W
COL_CONV_C = COL_CONV_B + CONV_W
COL_CONV_GATE = COL_CONV_C + CONV_W
COL_POOL_U = COL_CONV_GATE + CONV_W
COL_POOL_GATE = COL_POOL_U + POOL_W
ROPE_THETA = 10000.0
EPS = 1e-6
LANES = 128
HALO = 8
ADA_ROWS = 8
VMEM_LIMIT = 48 * 1024 * 1024

F32 = jnp.float32
BF16 = jnp.bfloat16


def _silu(z):
    return z * (1.0 / (1.0 + jnp.exp(-z)))


def _dot(a, b):
    return jnp.dot(a, b, preferred_element_type=F32)


def _mod_norm(x, g, mod_ref):
    ms = jnp.mean(x * x, axis=-1, keepdims=True)
    gs = g * (1.0 + mod_ref[0, 0, 1:2, :])
    return x * lax.rsqrt(ms + EPS) * gs + mod_ref[0, 0, 0:1, :]


def _ada_kernel(c_ref, w_ref, b_ref, o_ref):
    s = _silu(c_ref[...]).astype(BF16)
    o_ref[0] = _dot(s, w_ref[0].astype(BF16)) + b_ref[0]


def _ada(cvec, w_ada, b_ada):
    depth, _, n3 = w_ada.shape
    tn = 1024
    return pl.pallas_call(
        _ada_kernel,
        grid=(depth, n3 // tn),
        in_specs=[
            pl.BlockSpec((ADA_ROWS, D_MODEL), lambda l, j: (0, 0)),
            pl.BlockSpec((1, D_MODEL, tn), lambda l, j: (l, 0, j)),
            pl.BlockSpec((1, 1, tn), lambda l, j: (l, 0, j)),
        ],
        out_specs=pl.BlockSpec((1, ADA_ROWS, tn), lambda l, j: (l, 0, j)),
        out_shape=jax.ShapeDtypeStruct((depth, ADA_ROWS, n3), F32),
        name="ada_mod",
    )(cvec, w_ada, b_ada.reshape(depth, 1, n3))


def _qkv_kernel(*refs, rope, nsub):
    if rope:
        (x_ref, mod_ref, g_ref, w_ref, gains_ref, bd_ref, cos_ref, sin_ref,
         q_ref, k_ref, v_ref) = refs
    else:
        x_ref, mod_ref, g_ref, w_ref, gains_ref, bd_ref = refs[:6]
        q_ref, k_ref, v_ref = refs[-3:]
    tm = x_ref.shape[0]
    sub = tm // nsub
    bd = bd_ref[...]
    g = g_ref[...]
    if rope:
        lane = lax.broadcasted_iota(jnp.int32, (sub, LANES), 1)
        second = (lane % (HEAD_DIM // 2)) >= (HEAD_DIM // 4)
    qkvs = []
    for s in range(nsub):
        xn = _mod_norm(x_ref[s * sub:(s + 1) * sub, :], g, mod_ref).astype(BF16)
        qkvs.append(_dot(xn, w_ref[0]))
    k_parts, v_parts = [], []
    for s, qkv in enumerate(qkvs):
        rows = slice(s * sub, (s + 1) * sub)
        v_parts.append(qkv[:, QK_W:])
        for j in range(QK_W // LANES):
            sl = slice(j * LANES, (j + 1) * LANES)
            t = qkv[:, sl]
            sq = t * t
            hi = sq.astype(BF16)
            lo = (sq - hi.astype(F32)).astype(BF16)
            msq = _dot(hi, bd) + _dot(lo, bd)
            t = t * lax.rsqrt(msq + EPS) * gains_ref[0, :, sl]
            if rope:
                partner = jnp.where(second, pltpu.roll(t, HEAD_DIM // 4, 1),
                                    pltpu.roll(t, LANES - HEAD_DIM // 4, 1))
                t = t * cos_ref[rows, :] + partner * sin_ref[rows, :]
            if j < ATTN_W // LANES:
                q_ref[rows, sl] = t.astype(BF16)
            else:
                k_parts.append(t)
    k_ref[...] = jnp.concatenate(k_parts, axis=0).reshape(k_ref.shape)
    v_ref[...] = jnp.concatenate(v_parts, axis=0).reshape(v_ref.shape)


def _mod_spec(layer, mod_row, tps):
    if mod_row is None:
        return pl.BlockSpec((1, 1, 3, D_MODEL), lambda i: (layer, i // tps, 0, 0))
    return pl.BlockSpec((1, 1, 3, D_MODEL), lambda i: (layer, mod_row, 0, 0))


def _qkv(x2, mod, mod_row, norm_g, w_in_bf, layer, gains, bd, tables, kv_prev, *, seq, tm):
    tokens = x2.shape[0]
    depth = w_in_bf.shape[0]
    tps = max(seq // tm, 1)
    rope = tables is not None
    row = lambda i: (i, 0)
    const = lambda i: (0, 0)
    in_specs = [
        pl.BlockSpec((tm, D_MODEL), row),
        _mod_spec(layer, mod_row, tps),
        pl.BlockSpec((1, D_MODEL), const),
        pl.BlockSpec((1, D_MODEL, QKV_W), lambda i: (layer, 0, 0)),
        pl.BlockSpec((1, 1, QK_W), lambda i: (layer, 0, 0)),
        pl.BlockSpec((LANES, LANES), const),
    ]
    args = [x2, mod, norm_g.reshape(1, D_MODEL), w_in_bf, gains, bd]
    out_shapes = [jax.ShapeDtypeStruct((tokens, ATTN_W), BF16)]
    out_specs = [pl.BlockSpec((tm, ATTN_W), row)]
    aliases = {}
    if rope:
        in_specs += [pl.BlockSpec((tm, LANES), lambda i: (i % tps, 0))] * 2
        args += list(tables)
        out_shapes += [jax.ShapeDtypeStruct((tokens, KV_W), F32)] * 2
        out_specs += [pl.BlockSpec((tm, KV_W), row)] * 2
    else:
        nseq = tm // seq
        out_shapes += [jax.ShapeDtypeStruct((tokens // seq, depth, seq, KV_W), F32)] * 2
        out_specs += [pl.BlockSpec((nseq, 1, seq, KV_W), lambda i: (i, layer, 0, 0))] * 2
        if kv_prev is not None:
            in_specs += [pl.BlockSpec(memory_space=pl.ANY)] * 2
            aliases = {len(args): 1, len(args) + 1: 2}
            args += list(kv_prev)
    return pl.pallas_call(
        functools.partial(_qkv_kernel, rope=rope, nsub=1 if rope else 2),
        grid=(tokens // tm,),
        in_specs=in_specs,
        out_specs=out_specs,
        out_shape=out_shapes,
        input_output_aliases=aliases,
        compiler_params=pltpu.CompilerParams(
            dimension_semantics=("arbitrary",), vmem_limit_bytes=VMEM_LIMIT),
        name="qkv_rope" if rope else "qkv",
    )(*args)


VT_ROWS = HEAD_DIM + 16
ATTN_REGION = 4
ATTN_BUFS = 2 * ATTN_REGION
ATTN_KEY_CHUNK = 256


def _attn_kernel(q_ref, k_ref, v_ref, ck_ref, cv_ref, o_ref, kab, vt, q2, acct, *bufs, n, n_ctx, rb):
    kv = pl.program_id(1)
    nr = 2 * n // rb
    t_all = n + n_ctx
    sbuf, pbuf = bufs[:ATTN_BUFS], bufs[ATTN_BUFS:]

    def fill(k_src, v_src, r0, m):
        lane = lax.broadcasted_iota(jnp.int32, (m, LANES), 1)
        k_own = jnp.where((lane // HEAD_DIM) == kv, k_src, 0.0)
        k_oth = pltpu.roll(k_own, HEAD_DIM, 1)
        is0 = kv == 0
        kab[0, r0:r0 + m, :] = jnp.where(is0, k_own, k_oth).astype(BF16)
        kab[1, r0:r0 + m, :] = jnp.where(is0, k_oth, k_own).astype(BF16)
        v_t = v_src.T
        vt[0:HEAD_DIM, r0:r0 + m] = jnp.where(is0, v_t[0:HEAD_DIM, :], v_t[HEAD_DIM:, :]).astype(BF16)

    piece = 256
    for r0 in range(0, n, piece):
        fill(k_ref[0, r0:r0 + piece, :], v_ref[0, r0:r0 + piece, :], r0, piece)
    fill(ck_ref[0, 0], cv_ref[0, 0], n, n_ctx)
    vt[HEAD_DIM:, :] = jnp.ones((VT_ROWS - HEAD_DIM, t_all), BF16)
    q2[0:n, :] = q_ref[0, :, 0:LANES]
    q2[n:, :] = q_ref[0, :, LANES:]

    def scores(r, slot, dst):
        qr = q2[pl.ds(pl.multiple_of(r * rb, rb), rb), :]
        dst[...] = lax.dot_general(kab[slot], qr, (((1,), (1,)), ((), ())), preferred_element_type=F32)

    def numer(src, dst):
        step = 64
        acc = src[0:step, :]
        for k0 in range(step, t_all, step):
            acc = jnp.maximum(acc, src[k0:k0 + step, :])
        m = jnp.max(acc, axis=0, keepdims=True)
        for k0 in range(0, t_all, piece):
            dst[k0:k0 + piece, :] = jnp.exp(src[k0:k0 + piece, :] - m).astype(BF16)

    def weigh(src, slot, r):
        res = _dot(vt[...], src[...])
        acct[r, slot * HEAD_DIM:(slot + 1) * HEAD_DIM, :] = (
            res[0:HEAD_DIM, :] * (1.0 / res[HEAD_DIM:HEAD_DIM + 1, :]))

    span = ATTN_REGION
    n_regions = 2 * nr // span

    def region(g, parity, do_scores, do_weigh):
        cur, oth = parity * span, (1 - parity) * span
        nt = (((1,), (1,)), ((), ()))
        for j in range(span):
            slot = j % 2
            src, dst = sbuf[cur + j], pbuf[cur + j]
            step = 64
            acc = src[0:step, :]
            for k0 in range(step, t_all, step):
                acc = jnp.maximum(acc, src[k0:k0 + step, :])
            m = jnp.max(acc, axis=0, keepdims=True)
            if do_scores:
                r_next = (g + 1) * (span // 2) + j // 2
                qr = q2[pl.ds(pl.multiple_of(r_next * rb, rb), rb), :]
            res = None
            for k0 in range(0, t_all, ATTN_KEY_CHUNK):
                ks = slice(k0, k0 + ATTN_KEY_CHUNK)
                if do_scores:
                    sbuf[oth + j][ks, :] = lax.dot_general(kab[slot, ks, :], qr, nt, preferred_element_type=F32)
                dst[ks, :] = jnp.exp(src[ks, :] - m).astype(BF16)
                if do_weigh:
                    part = _dot(vt[:, ks], pbuf[oth + j][ks, :])
                    res = part if res is None else res + part
            if do_weigh:
                acct[(g - 1) * (span // 2) + j // 2, slot * HEAD_DIM:(slot + 1) * HEAD_DIM, :] = (
                    res[0:HEAD_DIM, :] * (1.0 / res[HEAD_DIM:HEAD_DIM + 1, :]))

    for j in range(span):
        scores(j // 2, j % 2, sbuf[j])
    region(0, 0, True, False)

    def body(g, carry):
        @pl.when(g % 2 == 1)
        def _odd():
            region(g, 1, True, True)

        @pl.when(g % 2 == 0)
        def _even():
            region(g, 0, True, True)

        return carry

    lax.fori_loop(1, n_regions - 1, body, 0)
    end_parity = (n_regions - 1) % 2
    region(n_regions - 1, end_parity, False, True)
    for j in range(span):
        weigh(pbuf[end_parity * span + j], j % 2, (n_regions - 1) * (span // 2) + j // 2)

    per_tile = n // rb
    for r in range(nr):
        t, r0 = r // per_tile, (r % per_tile) * rb
        o_ref[0, r0:r0 + rb, t * LANES:(t + 1) * LANES] = acct[r].T


def _attention(q, k, v, cache_k, cache_v, layer, *, rb):
    bsz, n, _ = q.shape
    n_ctx = cache_k.shape[2]
    t_all = n + n_ctx
    gw = 2 * LANES
    qmap = lambda b, g: (b, 0, g)
    kmap = lambda b, g: (b, 0, 0)
    cmap = lambda b, g: (b, layer, 0, 0)
    return pl.pallas_call(
        functools.partial(_attn_kernel, n=n, n_ctx=n_ctx, rb=rb),
        grid=(bsz, N_KV_HEADS),
        in_specs=[
            pl.BlockSpec((1, n, gw), qmap),
            pl.BlockSpec((1, n, KV_W), kmap),
            pl.BlockSpec((1, n, KV_W), kmap),
            pl.BlockSpec((1, 1, n_ctx, KV_W), cmap),
            pl.BlockSpec((1, 1, n_ctx, KV_W), cmap),
        ],
        out_specs=pl.BlockSpec((1, n, gw), qmap),
        out_shape=jax.ShapeDtypeStruct((bsz, n, ATTN_W), F32),
        scratch_shapes=[
            pltpu.VMEM((2, t_all, LANES), BF16),
            pltpu.VMEM((VT_ROWS, t_all), BF16),
            pltpu.VMEM((2 * n, LANES), BF16),
            pltpu.VMEM((2 * n // rb, LANES, rb), F32),
        ] + [pltpu.VMEM((t_all, rb), F32)] * ATTN_BUFS + [pltpu.VMEM((t_all, rb), BF16)] * ATTN_BUFS,
        compiler_params=pltpu.CompilerParams(
            dimension_semantics=("arbitrary", "arbitrary"), vmem_limit_bytes=VMEM_LIMIT),
        name="attn_lat",
    )(q, k, v, cache_k, cache_v)


def _attn_ctx_kernel(q_ref, k_ref, v_ref, o_ref, *, n):
    kt_all = k_ref[0, 0].T
    v = v_ref[0, 0]
    vr = pltpu.roll(v, HEAD_DIM, 1)
    low = lax.broadcasted_iota(jnp.int32, (n, LANES), 1) < HEAD_DIM
    ones_lo = jnp.where(low, 1.0, 0.0)
    zk = jnp.zeros((HEAD_DIM, n), F32)
    gw = 2 * LANES
    for g in range(N_KV_HEADS):
        kg = kt_all[g * HEAD_DIM:(g + 1) * HEAD_DIM, :]
        kts = (jnp.concatenate([kg, zk], axis=0).astype(BF16),
               jnp.concatenate([zk, kg], axis=0).astype(BF16))
        v_lo, v_hi = (v, vr) if g == 0 else (vr, v)
        vs = (jnp.concatenate([jnp.where(low, v_lo, 0.0), ones_lo], axis=1).astype(BF16),
              jnp.concatenate([jnp.where(low, 0.0, v_hi), 1.0 - ones_lo], axis=1).astype(BF16))
        q2 = jnp.concatenate([q_ref[0, :, g * gw:g * gw + LANES],
                              q_ref[0, :, g * gw + LANES:(g + 1) * gw]], axis=0)
        acc = None
        for kt_s, v_s in zip(kts, vs):
            s = _dot(q2, kt_s)
            m = jnp.max(s, axis=-1, keepdims=True)
            pv = _dot(jnp.exp(s - m).astype(BF16), v_s)
            acc = pv if acc is None else acc + pv
        o = acc[:, 0:LANES] / acc[:, LANES:]
        o_ref[0, :, g * gw:g * gw + LANES] = o[0:n]
        o_ref[0, :, g * gw + LANES:(g + 1) * gw] = o[n:]


def _attention_ctx(q, k, v, layer):
    bsz, n, _ = q.shape
    blk = lambda w: pl.BlockSpec((1, n, w), lambda b: (b, 0, 0))
    kv_blk = pl.BlockSpec((1, 1, n, KV_W), lambda b: (b, layer, 0, 0))
    return pl.pallas_call(
        functools.partial(_attn_ctx_kernel, n=n),
        grid=(bsz,),
        in_specs=[blk(ATTN_W), kv_blk, kv_blk],
        out_specs=blk(ATTN_W),
        out_shape=jax.ShapeDtypeStruct((bsz, n, ATTN_W), F32),
        compiler_params=pltpu.CompilerParams(
            dimension_semantics=("arbitrary",), vmem_limit_bytes=VMEM_LIMIT),
        name="attn_ctx",
    )(q, k, v)


def _mix_kernel(attn_ref, x_ref, xp_ref, xn_ref, mod_ref, g_ref, w_ref, wout_ref,
                convw_ref, convb_ref, wpool_ref, pscale_ref, fg_ref,
                o_ref, xe, yext, uext, *, tm, tps, seq, final):
    j = pl.program_id(0) % tps
    pm = jnp.where(j == 0, 0.0, 1.0)
    nm = jnp.where(j == tps - 1, 0.0, 1.0)
    c = CONV_W
    g = g_ref[...]
    x = x_ref[...]
    xe[0:tm, :] = _mod_norm(x, g, mod_ref).astype(BF16)
    halo = jnp.concatenate([xn_ref[...], xp_ref[...]], axis=0)
    xe[tm:, :] = _mod_norm(halo, g, mod_ref).astype(BF16)

    def proj(rows, off, width):
        return _dot(xe[0:rows, :], w_ref[0, :, off:off + width])

    h_all = proj(tm + 2 * HALO, COL_CONV_H, c)
    c_all = proj(tm + 2 * HALO, COL_CONV_C, c)
    u_all = proj(tm + 2 * HALO, COL_POOL_U, POOL_W)
    z_a = proj(tm, COL_ATTN_GATE, ATTN_W)
    b_c = proj(tm, COL_CONV_B, c)
    z_c = proj(tm, COL_CONV_GATE, c)
    z_p = proj(tm, COL_POOL_GATE, POOL_W)

    y = h_all * c_all
    yext[0:HALO, :] = y[tm + HALO:, :] * pm
    yext[HALO:HALO + tm, :] = y[0:tm, :]
    yext[HALO + tm:, :] = y[tm:tm + HALO, :] * nm
    conv = (yext[HALO - 1:HALO - 1 + tm, :] * convw_ref[0:1, :]
            + yext[HALO:HALO + tm, :] * convw_ref[1:2, :]
            + yext[HALO + 1:HALO + 1 + tm, :] * convw_ref[2:3, :] + convb_ref[...])
    conv_o = (b_c * conv * _silu(z_c)).astype(BF16)

    u = u_all[0:tm, :]
    uext[0:HALO, :] = u_all[tm + HALO:, :] * pm
    uext[HALO:HALO + tm, :] = u
    uext[HALO + tm:, :] = u_all[tm:tm + HALO, :] * nm

    def win(k0, k1, sl):
        tot = None
        for k in range(k0, k1):
            t = uext[HALO + k:HALO + k + tm, sl]
            tot = t if tot is None else tot + t
        return tot

    lane = lax.broadcasted_iota(jnp.int32, (tm, LANES), 1)
    pos = j * tm + lax.broadcasted_iota(jnp.int32, (tm, LANES), 0)
    low = lane < POOL_GROUP
    ds = []
    for ti, (wa, wb) in enumerate(((2, 4), (8, 16))):
        sl = slice(ti * LANES, (ti + 1) * LANES)
        s_a = win(-(wa // 2), wa // 2, sl)
        s_b = s_a + win(-(wb // 2), -(wa // 2), sl) + win(wa // 2, wb // 2, sl)
        half = jnp.where(low, wa // 2, wb // 2)
        cnt = jnp.minimum(pos + half - 1, seq - 1) - jnp.maximum(pos - half, 0) + 1
        tot = jnp.where(low, s_a, s_b)
        ds.append(tot / cnt.astype(F32) - u[:, sl])
    d = jnp.concatenate(ds, axis=-1).astype(BF16)
    pool_o = (_dot(d, wpool_ref[...]) * pscale_ref[...] * _silu(z_p)).astype(BF16)

    attn_o = (attn_ref[...] * _silu(z_a)).astype(BF16)
    a = ATTN_W
    out = (_dot(attn_o, wout_ref[0, 0:a, :]) + _dot(conv_o, wout_ref[0, a:a + c, :])
           + _dot(pool_o, wout_ref[0, a + c:, :]))
    res = x + mod_ref[0, 0, 2:3, :] * out
    if final:
        ms = jnp.mean(res * res, axis=-1, keepdims=True)
        res = res * lax.rsqrt(ms + EPS) * fg_ref[...]
    o_ref[...] = res


def _mix(attn2, x2, mod, mod_row, norm_g, w_in_bf, wout_bf, layer, conv_w, conv_b, wpool_bd, pool_scale,
         final_g, *, seq, tm, final):
    tokens = x2.shape[0]
    tps = seq // tm
    hb = tm // HALO
    nhb = tokens // HALO
    row = lambda i: (i, 0)
    const = lambda i: (0, 0)
    in_specs = [
        pl.BlockSpec((tm, ATTN_W), row),
        pl.BlockSpec((tm, D_MODEL), row),
        pl.BlockSpec((HALO, D_MODEL), lambda i: (jnp.maximum(i * hb - 1, 0), 0)),
        pl.BlockSpec((HALO, D_MODEL), lambda i: (jnp.minimum((i + 1) * hb, nhb - 1), 0)),
        _mod_spec(layer, mod_row, tps),
        pl.BlockSpec((1, D_MODEL), const),
        pl.BlockSpec((1, D_MODEL, IN_W), lambda i: (layer, 0, 0)),
        pl.BlockSpec((1, D_MODEL, D_MODEL), lambda i: (layer, 0, 0)),
        pl.BlockSpec((3, CONV_W), const),
        pl.BlockSpec((1, CONV_W), const),
        pl.BlockSpec((POOL_W, POOL_W), const),
        pl.BlockSpec((1, POOL_W), const),
        pl.BlockSpec((1, D_MODEL), const),
    ]
    return pl.pallas_call(
        functools.partial(_mix_kernel, tm=tm, tps=tps, seq=seq, final=final),
        grid=(tokens // tm,),
        in_specs=in_specs,
        out_specs=pl.BlockSpec((tm, D_MODEL), row),
        out_shape=jax.ShapeDtypeStruct((tokens, D_MODEL), F32),
        scratch_shapes=[
            pltpu.VMEM((tm + 2 * HALO, D_MODEL), BF16),
            pltpu.VMEM((tm + 2 * HALO, CONV_W), F32),
            pltpu.VMEM((tm + 2 * HALO, POOL_W), F32),
        ],
        compiler_params=pltpu.CompilerParams(
            dimension_semantics=("arbitrary",), vmem_limit_bytes=VMEM_LIMIT),
        name="mix_final" if final else "mix",
    )(attn2, x2, x2, x2, mod, norm_g.reshape(1, D_MODEL), w_in_bf, wout_bf,
      conv_w, conv_b.reshape(1, CONV_W), wpool_bd, pool_scale.reshape(1, POOL_W),
      final_g.reshape(1, D_MODEL))


def _rope_tables(n):
    rows = n // GRID_W
    row = jnp.repeat(jnp.arange(rows), GRID_W).astype(F32)
    col = jnp.tile(jnp.arange(GRID_W), rows).astype(F32)
    half = HEAD_DIM // 2
    inv = 1.0 / (ROPE_THETA ** (jnp.arange(0, half, 2, dtype=F32) / half))
    ar = row[:, None] * inv
    ac = col[:, None] * inv
    ang = jnp.tile(jnp.concatenate([ar, ar, ac, ac], axis=-1), (1, LANES // HEAD_DIM))
    cos = jnp.cos(ang)
    sin = jnp.sin(ang)
    second = (jnp.arange(LANES) % half) >= (half // 2)
    return cos, jnp.where(second, sin, -sin)


def _block_diag(blocks):
    n = len(blocks)
    rows = []
    for i, b in enumerate(blocks):
        rows.append(jnp.concatenate(
            [b if j == i else jnp.zeros((b.shape[0], blocks[j].shape[1]), b.dtype) for j in range(n)],
            axis=1))
    return jnp.concatenate(rows, axis=0)


def kernel(x_prompt, x_sample, cache_k, cache_v, c, c_ctx, norm_g, w_ada, b_ada, w_in,
           q_norm_g, k_norm_g, conv_w, conv_b, pool_w, pool_scale, w_out, final_g):
    depth = w_in.shape[0]
    bp, sp, _ = x_prompt.shape
    bs, ss, _ = x_sample.shape
    n_ctx = cache_k.shape[2]

    cvec = jnp.concatenate(
        [c, c_ctx[None, :], jnp.zeros((ADA_ROWS - bs - 1, D_MODEL), F32)], axis=0)
    mod = _ada(cvec, w_ada, b_ada)
    mod = mod.reshape(depth, ADA_ROWS, 3, D_MODEL)

    w_in_bf = w_in.astype(BF16)
    w_out_bf = w_out.astype(BF16)
    tables = _rope_tables(ss)
    ones = jnp.full((HEAD_DIM, HEAD_DIM), 1.0 / HEAD_DIM, BF16)
    bd = _block_diag([ones, ones])
    ck = cache_k.reshape(bs, depth, n_ctx, KV_W)
    cv = cache_v.reshape(bs, depth, n_ctx, KV_W)

    h = x_prompt.reshape(bp * sp, D_MODEL)
    z = x_sample.reshape(bs * ss, D_MODEL)
    kv_ctx = None
    gains = jnp.concatenate(
        [jnp.tile(q_norm_g * (HEAD_DIM ** -0.5), (1, N_HEADS)), jnp.tile(k_norm_g, (1, N_KV_HEADS))],
        axis=-1).reshape(depth, 1, QK_W)
    for l in range(depth):
        wpool_bd = _block_diag([pool_w[l, g] for g in range(pool_w.shape[1])]).astype(BF16)
        final = l == depth - 1
        mix = functools.partial(
            _mix, norm_g=norm_g[l], w_in_bf=w_in_bf, wout_bf=w_out_bf, layer=l,
            conv_w=conv_w[l], conv_b=conv_b[l], wpool_bd=wpool_bd, pool_scale=pool_scale[l],
            final_g=final_g, final=final)

        q, k, v = _qkv(h, mod, bs, norm_g[l], w_in_bf, l, gains, bd, None, kv_ctx, seq=sp, tm=512)
        kv_ctx = (k, v)
        attn = _attention_ctx(q.reshape(bp, sp, ATTN_W), k, v, l)
        h = mix(attn.reshape(bp * sp, ATTN_W), h, mod, bs, seq=sp, tm=256)

        q, k, v = _qkv(z, mod, None, norm_g[l], w_in_bf, l, gains, bd, tables, None, seq=ss, tm=512)
        attn = _attention(q.reshape(bs, ss, ATTN_W), k.reshape(bs, ss, KV_W), v.reshape(bs, ss, KV_W),
                          ck, cv, l, rb=256)
        z = mix(attn.reshape(bs * ss, ATTN_W), z, mod, None, seq=ss, tm=512)

    y_prompt = h.reshape(bp, sp, D_MODEL)
    y_sample = z.reshape(bs, ss, D_MODEL)
    new_k, new_v = (a.reshape(bp, depth, sp, N_KV_HEADS, HEAD_DIM) for a in kv_ctx)
    return (y_prompt, y_sample, new_k, new_v)
```

```python
import functools

import jax
import jax.numpy as jnp
from jax import lax
from jax.experimental import pallas as pl
from jax.experimental.pallas import tpu as pltpu

D_MODEL = 1024
GRID_W = 64
N_HEADS = 8
N_KV_HEADS = 2
HEAD_DIM = 64
ATTN_W = N_HEADS * HEAD_DIM
KV_W = N_KV_HEADS * HEAD_DIM
QK_W = ATTN_W + KV_W
QKV_W = QK_W + KV_W
CONV_W = 256
POOL_W = 256
POOL_GROUP = 64
IN_W = 2816
COL_ATTN_GATE = QKV_W
COL_CONV_H = COL_ATTN_GATE + ATTN_W
COL_CONV_B = COL_CONV_H + CONV_W
COL_CONV_C = COL_CONV_B + CONV_W
COL_CONV_GATE = COL_CONV_C + CONV_W
COL_POOL_U = COL_CONV_GATE + CONV_W
COL_POOL_GATE = COL_POOL_U + POOL_W
ROPE_THETA = 10000.0
EPS = 1e-6
LANES = 128
HALO = 8
ADA_ROWS = 8
VMEM_LIMIT = 48 * 1024 * 1024

F32 = jnp.float32
BF16 = jnp.bfloat16


def _silu(z):
    return z * (1.0 / (1.0 + jnp.exp(-z)))


def _dot(a, b):
    return jnp.dot(a, b, preferred_element_type=F32)


def _mod_norm(x, g, mod_ref):
    ms = jnp.mean(x * x, axis=-1, keepdims=True)
    gs = g * (1.0 + mod_ref[0, 0, 1:2, :])
    return x * lax.rsqrt(ms + EPS) * gs + mod_ref[0, 0, 0:1, :]


def _ada_kernel(c_ref, w_ref, b_ref, o_ref):
    s = _silu(c_ref[...]).astype(BF16)
    o_ref[0] = _dot(s, w_ref[0].astype(BF16)) + b_ref[0]


def _ada(cvec, w_ada, b_ada):
    depth, _, n3 = w_ada.shape
    tn = 1024
    return pl.pallas_call(
        _ada_kernel,
        grid=(depth, n3 // tn),
        in_specs=[
            pl.BlockSpec((ADA_ROWS, D_MODEL), lambda l, j: (0, 0)),
            pl.BlockSpec((1, D_MODEL, tn), lambda l, j: (l, 0, j)),
            pl.BlockSpec((1, 1, tn), lambda l, j: (l, 0, j)),
        ],
        out_specs=pl.BlockSpec((1, ADA_ROWS, tn), lambda l, j: (l, 0, j)),
        out_shape=jax.ShapeDtypeStruct((depth, ADA_ROWS, n3), F32),
        name="ada_mod",
    )(cvec, w_ada, b_ada.reshape(depth, 1, n3))


def _qkv_kernel(*refs, rope, nsub):
    if rope:
        (x_ref, mod_ref, g_ref, w_ref, gains_ref, bd_ref, cos_ref, sin_ref,
         q_ref, k_ref, v_ref) = refs
    else:
        x_ref, mod_ref, g_ref, w_ref, gains_ref, bd_ref = refs[:6]
        q_ref, k_ref, v_ref = refs[-3:]
    tm = x_ref.shape[0]
    sub = tm // nsub
    bd = bd_ref[...]
    g = g_ref[...]
    if rope:
        lane = lax.broadcasted_iota(jnp.int32, (sub, LANES), 1)
        second = (lane % (HEAD_DIM // 2)) >= (HEAD_DIM // 4)
    qkvs = []
    for s in range(nsub):
        xn = _mod_norm(x_ref[s * sub:(s + 1) * sub, :], g, mod_ref).astype(BF16)
        qkvs.append(_dot(xn, w_ref[0]))
    k_parts, v_parts = [], []
    for s, qkv in enumerate(qkvs):
        rows = slice(s * sub, (s + 1) * sub)
        v_parts.append(qkv[:, QK_W:])
        for j in range(QK_W // LANES):
            sl = slice(j * LANES, (j + 1) * LANES)
            t = qkv[:, sl]
            sq = t * t
            hi = sq.astype(BF16)
            lo = (sq - hi.astype(F32)).astype(BF16)
            msq = _dot(hi, bd) + _dot(lo, bd)
            t = t * lax.rsqrt(msq + EPS) * gains_ref[0, :, sl]
            if rope:
                partner = jnp.where(second, pltpu.roll(t, HEAD_DIM // 4, 1),
                                    pltpu.roll(t, LANES - HEAD_DIM // 4, 1))
                t = t * cos_ref[rows, :] + partner * sin_ref[rows, :]
            if j < ATTN_W // LANES:
                q_ref[rows, sl] = t.astype(BF16)
            else:
                k_parts.append(t)
    k_ref[...] = jnp.concatenate(k_parts, axis=0).reshape(k_ref.shape)
    v_ref[...] = jnp.concatenate(v_parts, axis=0).reshape(v_ref.shape)


def _mod_spec(layer, mod_row, tps):
    if mod_row is None:
        return pl.BlockSpec((1, 1, 3, D_MODEL), lambda i: (layer, i // tps, 0, 0))
    return pl.BlockSpec((1, 1, 3, D_MODEL), lambda i: (layer, mod_row, 0, 0))


def _qkv(x2, mod, mod_row, norm_g, w_in_bf, layer, gains, bd, tables, kv_prev, *, seq, tm):
    tokens = x2.shape[0]
    depth = w_in_bf.shape[0]
    tps = max(seq // tm, 1)
    rope = tables is not None
    row = lambda i: (i, 0)
    const = lambda i: (0, 0)
    in_specs = [
        pl.BlockSpec((tm, D_MODEL), row),
        _mod_spec(layer, mod_row, tps),
        pl.BlockSpec((1, D_MODEL), const),
        pl.BlockSpec((1, D_MODEL, QKV_W), lambda i: (layer, 0, 0)),
        pl.BlockSpec((1, 1, QK_W), lambda i: (layer, 0, 0)),
        pl.BlockSpec((LANES, LANES), const),
    ]
    args = [x2, mod, norm_g.reshape(1, D_MODEL), w_in_bf, gains, bd]
    out_shapes = [jax.ShapeDtypeStruct((tokens, ATTN_W), BF16)]
    out_specs = [pl.BlockSpec((tm, ATTN_W), row)]
    aliases = {}
    if rope:
        in_specs += [pl.BlockSpec((tm, LANES), lambda i: (i % tps, 0))] * 2
        args += list(tables)
        out_shapes += [jax.ShapeDtypeStruct((tokens, KV_W), F32)] * 2
        out_specs += [pl.BlockSpec((tm, KV_W), row)] * 2
    else:
        nseq = tm // seq
        out_shapes += [jax.ShapeDtypeStruct((tokens // seq, depth, seq, KV_W), F32)] * 2
        out_specs += [pl.BlockSpec((nseq, 1, seq, KV_W), lambda i: (i, layer, 0, 0))] * 2
        if kv_prev is not None:
            in_specs += [pl.BlockSpec(memory_space=pl.ANY)] * 2
            aliases = {len(args): 1, len(args) + 1: 2}
            args += list(kv_prev)
    return pl.pallas_call(
        functools.partial(_qkv_kernel, rope=rope, nsub=1 if rope else 2),
        grid=(tokens // tm,),
        in_specs=in_specs,
        out_specs=out_specs,
        out_shape=out_shapes,
        input_output_aliases=aliases,
        compiler_params=pltpu.CompilerParams(
            dimension_semantics=("arbitrary",), vmem_limit_bytes=VMEM_LIMIT),
        name="qkv_rope" if rope else "qkv",
    )(*args)


VT_ROWS = HEAD_DIM + 16
ATTN_REGION = 4
ATTN_BUFS = 2 * ATTN_REGION
ATTN_KEY_CHUNK = 256


def _attn_kernel(q_ref, k_ref, v_ref, ck_ref, cv_ref, o_ref, kab, vt, q2, acct, mbuf, *bufs, n, n_ctx, rb):
    kv = pl.program_id(1)
    nr = 2 * n // rb
    t_all = n + n_ctx
    sbuf, pbuf = bufs[:ATTN_BUFS], bufs[ATTN_BUFS:]

    def fill(k_src, v_src, r0, m):
        lane = lax.broadcasted_iota(jnp.int32, (m, LANES), 1)
        k_own = jnp.where((lane // HEAD_DIM) == kv, k_src, 0.0)
        k_oth = pltpu.roll(k_own, HEAD_DIM, 1)
        is0 = kv == 0
        kab[0, r0:r0 + m, :] = jnp.where(is0, k_own, k_oth).astype(BF16)
        kab[1, r0:r0 + m, :] = jnp.where(is0, k_oth, k_own).astype(BF16)
        v_t = v_src.T
        vt[0:HEAD_DIM, r0:r0 + m] = jnp.where(is0, v_t[0:HEAD_DIM, :], v_t[HEAD_DIM:, :]).astype(BF16)

    piece = 256
    for r0 in range(0, n, piece):
        fill(k_ref[0, r0:r0 + piece, :], v_ref[0, r0:r0 + piece, :], r0, piece)
    fill(ck_ref[0, 0], cv_ref[0, 0], n, n_ctx)
    vt[HEAD_DIM:, :] = jnp.ones((VT_ROWS - HEAD_DIM, t_all), BF16)
    q2[0:n, :] = q_ref[0, :, 0:LANES]
    q2[n:, :] = q_ref[0, :, LANES:]

    span = ATTN_REGION
    n_regions = 2 * nr // span
    chunks = [slice(k0, k0 + ATTN_KEY_CHUNK) for k0 in range(0, t_all, ATTN_KEY_CHUNK)]

    def score_chunk(slot, ks, qr, buf, run):
        sc = lax.dot_general(kab[slot, ks, :], qr, (((1,), (1,)), ((), ())), preferred_element_type=F32)
        sbuf[buf][ks, :] = sc
        while sc.shape[0] > 8:
            half = sc.shape[0] // 2
            sc = jnp.maximum(sc[0:half, :], sc[half:, :])
        return sc if run is None else jnp.maximum(run, sc)

    def q_rows(r):
        return q2[pl.ds(pl.multiple_of(r * rb, rb), rb), :]

    def region(g, parity, do_scores, do_weigh):
        cur, oth = parity * span, (1 - parity) * span
        for j in range(span):
            slot = j % 2
            m = mbuf[cur + j]
            if do_scores:
                qr = q_rows((g + 1) * (span // 2) + j // 2)
            run, res = None, None
            for ks in chunks:
                if do_scores:
                    run = score_chunk(slot, ks, qr, oth + j, run)
                pbuf[cur + j][ks, :] = jnp.exp(sbuf[cur + j][ks, :] - m).astype(BF16)
                if do_weigh:
                    part = _dot(vt[:, ks], pbuf[oth + j][ks, :])
                    res = part if res is None else res + part
            if do_scores:
                mbuf[oth + j] = jnp.max(run, axis=0, keepdims=True)
            if do_weigh:
                acct[(g - 1) * (span // 2) + j // 2, slot * HEAD_DIM:(slot + 1) * HEAD_DIM, :] = (
                    res[0:HEAD_DIM, :] * (1.0 / res[HEAD_DIM:HEAD_DIM + 1, :]))

    for j in range(span):
        run = None
        for ks in chunks:
            run = score_chunk(j % 2, ks, q_rows(j // 2), j, run)
        mbuf[j] = jnp.max(run, axis=0, keepdims=True)
    region(0, 0, True, False)

    def body(g, carry):
        @pl.when(g % 2 == 1)
        def _odd():
            region(g, 1, True, True)

        @pl.when(g % 2 == 0)
        def _even():
            region(g, 0, True, True)

        return carry

    lax.fori_loop(1, n_regions - 1, body, 0)
    end_parity = (n_regions - 1) % 2
    region(n_regions - 1, end_parity, False, True)
    for j in range(span):
        res = _dot(vt[...], pbuf[end_parity * span + j][...])
        acct[(n_regions - 1) * (span // 2) + j // 2, (j % 2) * HEAD_DIM:(j % 2 + 1) * HEAD_DIM, :] = (
            res[0:HEAD_DIM, :] * (1.0 / res[HEAD_DIM:HEAD_DIM + 1, :]))

    per_tile = n // rb
    for r in range(nr):
        t, r0 = r // per_tile, (r % per_tile) * rb
        o_ref[0, r0:r0 + rb, t * LANES:(t + 1) * LANES] = acct[r].T


def _attention(q, k, v, cache_k, cache_v, layer, *, rb):
    bsz, n, _ = q.shape
    n_ctx = cache_k.shape[2]
    t_all = n + n_ctx
    gw = 2 * LANES
    qmap = lambda b, g: (b, 0, g)
    kmap = lambda b, g: (b, 0, 0)
    cmap = lambda b, g: (b, layer, 0, 0)
    return pl.pallas_call(
        functools.partial(_attn_kernel, n=n, n_ctx=n_ctx, rb=rb),
        grid=(bsz, N_KV_HEADS),
        in_specs=[
            pl.BlockSpec((1, n, gw), qmap),
            pl.BlockSpec((1, n, KV_W), kmap),
            pl.BlockSpec((1, n, KV_W), kmap),
            pl.BlockSpec((1, 1, n_ctx, KV_W), cmap),
            pl.BlockSpec((1, 1, n_ctx, KV_W), cmap),
        ],
        out_specs=pl.BlockSpec((1, n, gw), qmap),
        out_shape=jax.ShapeDtypeStruct((bsz, n, ATTN_W), F32),
        scratch_shapes=[
            pltpu.VMEM((2, t_all, LANES), BF16),
            pltpu.VMEM((VT_ROWS, t_all), BF16),
            pltpu.VMEM((2 * n, LANES), BF16),
            pltpu.VMEM((2 * n // rb, LANES, rb), F32),
            pltpu.VMEM((ATTN_BUFS, 1, rb), F32),
        ] + [pltpu.VMEM((t_all, rb), F32)] * ATTN_BUFS + [pltpu.VMEM((t_all, rb), BF16)] * ATTN_BUFS,
        compiler_params=pltpu.CompilerParams(
            dimension_semantics=("arbitrary", "arbitrary"), vmem_limit_bytes=VMEM_LIMIT),
        name="attn_lat",
    )(q, k, v, cache_k, cache_v)


def _attn_ctx_kernel(q_ref, k_ref, v_ref, o_ref, *, n):
    kt_all = k_ref[0, 0].T
    v = v_ref[0, 0]
    vr = pltpu.roll(v, HEAD_DIM, 1)
    low = lax.broadcasted_iota(jnp.int32, (n, LANES), 1) < HEAD_DIM
    ones_lo = jnp.where(low, 1.0, 0.0)
    zk = jnp.zeros((HEAD_DIM, n), F32)
    gw = 2 * LANES
    for g in range(N_KV_HEADS):
        kg = kt_all[g * HEAD_DIM:(g + 1) * HEAD_DIM, :]
        kts = (jnp.concatenate([kg, zk], axis=0).astype(BF16),
               jnp.concatenate([zk, kg], axis=0).astype(BF16))
        v_lo, v_hi = (v, vr) if g == 0 else (vr, v)
        vs = (jnp.concatenate([jnp.where(low, v_lo, 0.0), ones_lo], axis=1).astype(BF16),
              jnp.concatenate([jnp.where(low, 0.0, v_hi), 1.0 - ones_lo], axis=1).astype(BF16))
        q2 = jnp.concatenate([q_ref[0, :, g * gw:g * gw + LANES],
                              q_ref[0, :, g * gw + LANES:(g + 1) * gw]], axis=0)
        acc = None
        for kt_s, v_s in zip(kts, vs):
            s = _dot(q2, kt_s)
            m = jnp.max(s, axis=-1, keepdims=True)
            pv = _dot(jnp.exp(s - m).astype(BF16), v_s)
            acc = pv if acc is None else acc + pv
        o = acc[:, 0:LANES] / acc[:, LANES:]
        o_ref[0, :, g * gw:g * gw + LANES] = o[0:n]
        o_ref[0, :, g * gw + LANES:(g + 1) * gw] = o[n:]


def _attention_ctx(q, k, v, layer):
    bsz, n, _ = q.shape
    blk = lambda w: pl.BlockSpec((1, n, w), lambda b: (b, 0, 0))
    kv_blk = pl.BlockSpec((1, 1, n, KV_W), lambda b: (b, layer, 0, 0))
    return pl.pallas_call(
        functools.partial(_attn_ctx_kernel, n=n),
        grid=(bsz,),
        in_specs=[blk(ATTN_W), kv_blk, kv_blk],
        out_specs=blk(ATTN_W),
        out_shape=jax.ShapeDtypeStruct((bsz, n, ATTN_W), F32),
        compiler_params=pltpu.CompilerParams(
            dimension_semantics=("arbitrary",), vmem_limit_bytes=VMEM_LIMIT),
        name="attn_ctx",
    )(q, k, v)


def _mix_kernel(attn_ref, x_ref, xp_ref, xn_ref, mod_ref, g_ref, w_ref, wout_ref,
                convw_ref, convb_ref, wpool_ref, pscale_ref, fg_ref,
                o_ref, xe, yext, uext, *, tm, tps, seq, final):
    j = pl.program_id(0) % tps
    pm = jnp.where(j == 0, 0.0, 1.0)
    nm = jnp.where(j == tps - 1, 0.0, 1.0)
    c = CONV_W
    g = g_ref[...]
    x = x_ref[...]
    xe[0:tm, :] = _mod_norm(x, g, mod_ref).astype(BF16)
    halo = jnp.concatenate([xn_ref[...], xp_ref[...]], axis=0)
    xe[tm:, :] = _mod_norm(halo, g, mod_ref).astype(BF16)

    def proj(rows, off, width):
        return _dot(xe[0:rows, :], w_ref[0, :, off:off + width])

    h_all = proj(tm + 2 * HALO, COL_CONV_H, c)
    c_all = proj(tm + 2 * HALO, COL_CONV_C, c)
    u_all = proj(tm + 2 * HALO, COL_POOL_U, POOL_W)
    z_a = proj(tm, COL_ATTN_GATE, ATTN_W)
    b_c = proj(tm, COL_CONV_B, c)
    z_c = proj(tm, COL_CONV_GATE, c)
    z_p = proj(tm, COL_POOL_GATE, POOL_W)

    y = h_all * c_all
    yext[0:HALO, :] = y[tm + HALO:, :] * pm
    yext[HALO:HALO + tm, :] = y[0:tm, :]
    yext[HALO + tm:, :] = y[tm:tm + HALO, :] * nm
    conv = (yext[HALO - 1:HALO - 1 + tm, :] * convw_ref[0:1, :]
            + yext[HALO:HALO + tm, :] * convw_ref[1:2, :]
            + yext[HALO + 1:HALO + 1 + tm, :] * convw_ref[2:3, :] + convb_ref[...])
    conv_o = (b_c * conv * _silu(z_c)).astype(BF16)

    u = u_all[0:tm, :]
    uext[0:HALO, :] = u_all[tm + HALO:, :] * pm
    uext[HALO:HALO + tm, :] = u
    uext[HALO + tm:, :] = u_all[tm:tm + HALO, :] * nm

    def win(k0, k1, sl):
        tot = None
        for k in range(k0, k1):
            t = uext[HALO + k:HALO + k + tm, sl]
            tot = t if tot is None else tot + t
        return tot

    lane = lax.broadcasted_iota(jnp.int32, (tm, LANES), 1)
    pos = j * tm + lax.broadcasted_iota(jnp.int32, (tm, LANES), 0)
    low = lane < POOL_GROUP
    ds = []
    for ti, (wa, wb) in enumerate(((2, 4), (8, 16))):
        sl = slice(ti * LANES, (ti + 1) * LANES)
        s_a = win(-(wa // 2), wa // 2, sl)
        s_b = s_a + win(-(wb // 2), -(wa // 2), sl) + win(wa // 2, wb // 2, sl)
        half = jnp.where(low, wa // 2, wb // 2)
        cnt = jnp.minimum(pos + half - 1, seq - 1) - jnp.maximum(pos - half, 0) + 1
        tot = jnp.where(low, s_a, s_b)
        ds.append(tot / cnt.astype(F32) - u[:, sl])
    d = jnp.concatenate(ds, axis=-1).astype(BF16)
    pool_o = (_dot(d, wpool_ref[...]) * pscale_ref[...] * _silu(z_p)).astype(BF16)

    attn_o = (attn_ref[...] * _silu(z_a)).astype(BF16)
    a = ATTN_W
    out = (_dot(attn_o, wout_ref[0, 0:a, :]) + _dot(conv_o, wout_ref[0, a:a + c, :])
           + _dot(pool_o, wout_ref[0, a + c:, :]))
    res = x + mod_ref[0, 0, 2:3, :] * out
    if final:
        ms = jnp.mean(res * res, axis=-1, keepdims=True)
        res = res * lax.rsqrt(ms + EPS) * fg_ref[...]
    o_ref[...] = res


def _mix(attn2, x2, mod, mod_row, norm_g, w_in_bf, wout_bf, layer, conv_w, conv_b, wpool_bd, pool_scale,
         final_g, *, seq, tm, final):
    tokens = x2.shape[0]
    tps = seq // tm
    hb = tm // HALO
    nhb = tokens // HALO
    row = lambda i: (i, 0)
    const = lambda i: (0, 0)
    in_specs = [
        pl.BlockSpec((tm, ATTN_W), row),
        pl.BlockSpec((tm, D_MODEL), row),
        pl.BlockSpec((HALO, D_MODEL), lambda i: (jnp.maximum(i * hb - 1, 0), 0)),
        pl.BlockSpec((HALO, D_MODEL), lambda i: (jnp.minimum((i + 1) * hb, nhb - 1), 0)),
        _mod_spec(layer, mod_row, tps),
        pl.BlockSpec((1, D_MODEL), const),
        pl.BlockSpec((1, D_MODEL, IN_W), lambda i: (layer, 0, 0)),
        pl.BlockSpec((1, D_MODEL, D_MODEL), lambda i: (layer, 0, 0)),
        pl.BlockSpec((3, CONV_W), const),
        pl.BlockSpec((1, CONV_W), const),
        pl.BlockSpec((POOL_W, POOL_W), const),
        pl.BlockSpec((1, POOL_W), const),
        pl.BlockSpec((1, D_MODEL), const),
    ]
    return pl.pallas_call(
        functools.partial(_mix_kernel, tm=tm, tps=tps, seq=seq, final=final),
        grid=(tokens // tm,),
        in_specs=in_specs,
        out_specs=pl.BlockSpec((tm, D_MODEL), row),
        out_shape=jax.ShapeDtypeStruct((tokens, D_MODEL), F32),
        scratch_shapes=[
            pltpu.VMEM((tm + 2 * HALO, D_MODEL), BF16),
            pltpu.VMEM((tm + 2 * HALO, CONV_W), F32),
            pltpu.VMEM((tm + 2 * HALO, POOL_W), F32),
        ],
        compiler_params=pltpu.CompilerParams(
            dimension_semantics=("arbitrary",), vmem_limit_bytes=VMEM_LIMIT),
        name="mix_final" if final else "mix",
    )(attn2, x2, x2, x2, mod, norm_g.reshape(1, D_MODEL), w_in_bf, wout_bf,
      conv_w, conv_b.reshape(1, CONV_W), wpool_bd, pool_scale.reshape(1, POOL_W),
      final_g.reshape(1, D_MODEL))


def _rope_tables(n):
    rows = n // GRID_W
    row = jnp.repeat(jnp.arange(rows), GRID_W).astype(F32)
    col = jnp.tile(jnp.arange(GRID_W), rows).astype(F32)
    half = HEAD_DIM // 2
    inv = 1.0 / (ROPE_THETA ** (jnp.arange(0, half, 2, dtype=F32) / half))
    ar = row[:, None] * inv
    ac = col[:, None] * inv
    ang = jnp.tile(jnp.concatenate([ar, ar, ac, ac], axis=-1), (1, LANES // HEAD_DIM))
    cos = jnp.cos(ang)
    sin = jnp.sin(ang)
    second = (jnp.arange(LANES) % half) >= (half // 2)
    return cos, jnp.where(second, sin, -sin)


def _block_diag(blocks):
    n = len(blocks)
    rows = []
    for i, b in enumerate(blocks):
        rows.append(jnp.concatenate(
            [b if j == i else jnp.zeros((b.shape[0], blocks[j].shape[1]), b.dtype) for j in range(n)],
            axis=1))
    return jnp.concatenate(rows, axis=0)


def kernel(x_prompt, x_sample, cache_k, cache_v, c, c_ctx, norm_g, w_ada, b_ada, w_in,
           q_norm_g, k_norm_g, conv_w, conv_b, pool_w, pool_scale, w_out, final_g):
    depth = w_in.shape[0]
    bp, sp, _ = x_prompt.shape
    bs, ss, _ = x_sample.shape
    n_ctx = cache_k.shape[2]

    cvec = jnp.concatenate(
        [c, c_ctx[None, :], jnp.zeros((ADA_ROWS - bs - 1, D_MODEL), F32)], axis=0)
    mod = _ada(cvec, w_ada, b_ada)
    mod = mod.reshape(depth, ADA_ROWS, 3, D_MODEL)

    w_in_bf = w_in.astype(BF16)
    w_out_bf = w_out.astype(BF16)
    tables = _rope_tables(ss)
    ones = jnp.full((HEAD_DIM, HEAD_DIM), 1.0 / HEAD_DIM, BF16)
    bd = _block_diag([ones, ones])
    ck = cache_k.reshape(bs, depth, n_ctx, KV_W)
    cv = cache_v.reshape(bs, depth, n_ctx, KV_W)

    h = x_prompt.reshape(bp * sp, D_MODEL)
    z = x_sample.reshape(bs * ss, D_MODEL)
    kv_ctx = (jnp.zeros((bp, depth, sp, KV_W), F32), jnp.zeros((bp, depth, sp, KV_W), F32))
    gains = jnp.concatenate(
        [jnp.tile(q_norm_g * (HEAD_DIM ** -0.5), (1, N_HEADS)), jnp.tile(k_norm_g, (1, N_KV_HEADS))],
        axis=-1).reshape(depth, 1, QK_W)
    for l in range(depth):
        wpool_bd = _block_diag([pool_w[l, g] for g in range(pool_w.shape[1])]).astype(BF16)
        final = l == depth - 1
        mix = functools.partial(
            _mix, norm_g=norm_g[l], w_in_bf=w_in_bf, wout_bf=w_out_bf, layer=l,
            conv_w=conv_w[l], conv_b=conv_b[l], wpool_bd=wpool_bd, pool_scale=pool_scale[l],
            final_g=final_g, final=final)

        q, k, v = _qkv(h, mod, bs, norm_g[l], w_in_bf, l, gains, bd, None, kv_ctx, seq=sp, tm=512)
        kv_ctx = (k, v)
        attn = _attention_ctx(q.reshape(bp, sp, ATTN_W), k, v, l)
        h = mix(attn.reshape(bp * sp, ATTN_W), h, mod, bs, seq=sp, tm=256)

        q, k, v = _qkv(z, mod, None, norm_g[l], w_in_bf, l, gains, bd, tables, None, seq=ss, tm=512)
        attn = _attention(q.reshape(bs, ss, ATTN_W), k.reshape(bs, ss, KV_W), v.reshape(bs, ss, KV_W),
                          ck, cv, l, rb=256)
        z = mix(attn.reshape(bs * ss, ATTN_W), z, mod, None, seq=ss, tm=512)

    y_prompt = h.reshape(bp, sp, D_MODEL)
    y_sample = z.reshape(bs, ss, D_MODEL)
    new_k, new_v = (a.reshape(bp, depth, sp, N_KV_HEADS, HEAD_DIM) for a in kv_ctx)
    return (y_prompt, y_sample, new_k, new_v)
```

```python
import functools

import jax
import jax.numpy as jnp
from jax import lax
from jax.experimental import pallas as pl
from jax.experimental.pallas import tpu as pltpu

D_MODEL = 1024
GRID_W = 64
N_HEADS = 8
N_KV_HEADS = 2
HEAD_DIM = 64
ATTN_W = N_HEADS * HEAD_DIM
KV_W = N_KV_HEADS * HEAD_DIM
QK_W = ATTN_W + KV_W
QKV_W = QK_W + KV_W
CONV_W = 256
POOL_W = 256
POOL_GROUP = 64
IN_W = 2816
COL_ATTN_GATE = QKV_W
COL_CONV_H = COL_ATTN_GATE + ATTN_W
COL_CONV_B = COL_CONV_H + CONV_W
COL_CONV_C = COL_CONV_B + CONV_W
COL_CONV_GATE = COL_CONV_C + CONV_W
COL_POOL_U = COL_CONV_GATE + CONV_W
COL_POOL_GATE = COL_POOL_U + POOL_W
ROPE_THETA = 10000.0
EPS = 1e-6
LANES = 128
HALO = 8
ADA_ROWS = 8
VMEM_LIMIT = 48 * 1024 * 1024

F32 = jnp.float32
BF16 = jnp.bfloat16


def _silu(z):
    return z * (1.0 / (1.0 + jnp.exp(-z)))


def _dot(a, b):
    return jnp.dot(a, b, preferred_element_type=F32)


def _mod_norm(x, g, mod_ref):
    ms = jnp.mean(x * x, axis=-1, keepdims=True)
    gs = g * (1.0 + mod_ref[0, 0, 1:2, :])
    return x * lax.rsqrt(ms + EPS) * gs + mod_ref[0, 0, 0:1, :]


def _ada_kernel(c_ref, w_ref, b_ref, o_ref):
    s = _silu(c_ref[...]).astype(BF16)
    o_ref[0] = _dot(s, w_ref[0].astype(BF16)) + b_ref[0]


def _ada(cvec, w_ada, b_ada):
    depth, _, n3 = w_ada.shape
    tn = 1024
    return pl.pallas_call(
        _ada_kernel,
        grid=(depth, n3 // tn),
        in_specs=[
            pl.BlockSpec((ADA_ROWS, D_MODEL), lambda l, j: (0, 0)),
            pl.BlockSpec((1, D_MODEL, tn), lambda l, j: (l, 0, j)),
            pl.BlockSpec((1, 1, tn), lambda l, j: (l, 0, j)),
        ],
        out_specs=pl.BlockSpec((1, ADA_ROWS, tn), lambda l, j: (l, 0, j)),
        out_shape=jax.ShapeDtypeStruct((depth, ADA_ROWS, n3), F32),
        name="ada_mod",
    )(cvec, w_ada, b_ada.reshape(depth, 1, n3))


QKV_SUB = 256


def _qkv_kernel(*refs, rope, nsub):
    if rope:
        (x_ref, mod_ref, g_ref, w_ref, gains_ref, bd_ref, cos_ref, sin_ref,
         q_ref, k_ref, v_ref) = refs
    else:
        x_ref, mod_ref, g_ref, w_ref, gains_ref, bd_ref = refs[:6]
        q_ref, k_ref, v_ref = refs[-3:]
    tm = x_ref.shape[0]
    sub = tm // nsub
    bd = bd_ref[...]
    g = g_ref[...]
    if rope:
        lane = lax.broadcasted_iota(jnp.int32, (sub, LANES), 1)
        second = (lane % (HEAD_DIM // 2)) >= (HEAD_DIM // 4)
    k_parts, v_parts = [None] * nsub, [None] * nsub

    def finish(s, c, part):
        rows = slice(s * sub, (s + 1) * sub)
        for h in range(2):
            j = 2 * c + h
            t = part[:, h * LANES:(h + 1) * LANES]
            if j * LANES >= QK_W:
                v_parts[s] = t
                continue
            sl = slice(j * LANES, (j + 1) * LANES)
            sq = t * t
            hi = sq.astype(BF16)
            lo = (sq - hi.astype(F32)).astype(BF16)
            msq = _dot(hi, bd) + _dot(lo, bd)
            t = t * lax.rsqrt(msq + EPS) * gains_ref[0, :, sl]
            if rope:
                partner = jnp.where(second, pltpu.roll(t, HEAD_DIM // 4, 1),
                                    pltpu.roll(t, LANES - HEAD_DIM // 4, 1))
                t = t * cos_ref[rows, :] + partner * sin_ref[rows, :]
            if j * LANES < ATTN_W:
                q_ref[rows, sl] = t.astype(BF16)
            else:
                k_parts[s] = t

    ncol = QKV_W // (2 * LANES)
    prev = None
    for s in range(nsub + 1):
        if s < nsub:
            xn = _mod_norm(x_ref[s * sub:(s + 1) * sub, :], g, mod_ref).astype(BF16)
        cur = []
        for c in range(ncol):
            if s < nsub:
                cur.append(_dot(xn, w_ref[0, :, c * 2 * LANES:(c + 1) * 2 * LANES]))
            if prev is not None:
                finish(s - 1, c, prev[c])
        prev = cur
    k_ref[...] = jnp.concatenate(k_parts, axis=0).reshape(k_ref.shape)
    v_ref[...] = jnp.concatenate(v_parts, axis=0).reshape(v_ref.shape)


def _mod_spec(layer, mod_row, tps):
    if mod_row is None:
        return pl.BlockSpec((1, 1, 3, D_MODEL), lambda i: (layer, i // tps, 0, 0))
    return pl.BlockSpec((1, 1, 3, D_MODEL), lambda i: (layer, mod_row, 0, 0))


def _qkv(x2, mod, mod_row, norm_g, w_in_bf, layer, gains, bd, tables, kv_prev, *, seq, tm):
    tokens = x2.shape[0]
    depth = w_in_bf.shape[0]
    tps = max(seq // tm, 1)
    rope = tables is not None
    row = lambda i: (i, 0)
    const = lambda i: (0, 0)
    in_specs = [
        pl.BlockSpec((tm, D_MODEL), row),
        _mod_spec(layer, mod_row, tps),
        pl.BlockSpec((1, D_MODEL), const),
        pl.BlockSpec((1, D_MODEL, QKV_W), lambda i: (layer, 0, 0)),
        pl.BlockSpec((1, 1, QK_W), lambda i: (layer, 0, 0)),
        pl.BlockSpec((LANES, LANES), const),
    ]
    args = [x2, mod, norm_g.reshape(1, D_MODEL), w_in_bf, gains, bd]
    out_shapes = [jax.ShapeDtypeStruct((tokens, ATTN_W), BF16)]
    out_specs = [pl.BlockSpec((tm, ATTN_W), row)]
    aliases = {}
    if rope:
        in_specs += [pl.BlockSpec((tm, LANES), lambda i: (i % tps, 0))] * 2
        args += list(tables)
        out_shapes += [jax.ShapeDtypeStruct((tokens, KV_W), F32)] * 2
        out_specs += [pl.BlockSpec((tm, KV_W), row)] * 2
    else:
        nseq = tm // seq
        out_shapes += [jax.ShapeDtypeStruct((tokens // seq, depth, seq, KV_W), F32)] * 2
        out_specs += [pl.BlockSpec((nseq, 1, seq, KV_W), lambda i: (i, layer, 0, 0))] * 2
        if kv_prev is not None:
            in_specs += [pl.BlockSpec(memory_space=pl.ANY)] * 2
            aliases = {len(args): 1, len(args) + 1: 2}
            args += list(kv_prev)
    return pl.pallas_call(
        functools.partial(_qkv_kernel, rope=rope, nsub=tm // QKV_SUB),
        grid=(tokens // tm,),
        in_specs=in_specs,
        out_specs=out_specs,
        out_shape=out_shapes,
        input_output_aliases=aliases,
        compiler_params=pltpu.CompilerParams(
            dimension_semantics=("arbitrary",), vmem_limit_bytes=VMEM_LIMIT),
        name="qkv_rope" if rope else "qkv",
    )(*args)


VT_ROWS = HEAD_DIM + 16
ATTN_REGION = 4
ATTN_BUFS = 2 * ATTN_REGION
ATTN_KEY_CHUNK = 256


def _attn_kernel(q_ref, k_ref, v_ref, ck_ref, cv_ref, o_ref, kab, vt, q2, acct, mbuf, *bufs, n, n_ctx, rb):
    kv = pl.program_id(1)
    nr = 2 * n // rb
    t_all = n + n_ctx
    sbuf, pbuf = bufs[:ATTN_BUFS], bufs[ATTN_BUFS:]

    def fill(k_src, v_src, r0, m):
        lane = lax.broadcasted_iota(jnp.int32, (m, LANES), 1)
        k_own = jnp.where((lane // HEAD_DIM) == kv, k_src, 0.0)
        k_oth = pltpu.roll(k_own, HEAD_DIM, 1)
        is0 = kv == 0
        kab[0, r0:r0 + m, :] = jnp.where(is0, k_own, k_oth).astype(BF16)
        kab[1, r0:r0 + m, :] = jnp.where(is0, k_oth, k_own).astype(BF16)
        v_t = v_src.T
        vt[0:HEAD_DIM, r0:r0 + m] = jnp.where(is0, v_t[0:HEAD_DIM, :], v_t[HEAD_DIM:, :]).astype(BF16)

    piece = 256
    for r0 in range(0, n, piece):
        fill(k_ref[0, r0:r0 + piece, :], v_ref[0, r0:r0 + piece, :], r0, piece)
    fill(ck_ref[0, 0], cv_ref[0, 0], n, n_ctx)
    vt[HEAD_DIM:, :] = jnp.ones((VT_ROWS - HEAD_DIM, t_all), BF16)
    q2[0:n, :] = q_ref[0, :, 0:LANES]
    q2[n:, :] = q_ref[0, :, LANES:]

    span = ATTN_REGION
    n_regions = 2 * nr // span
    chunks = [slice(k0, k0 + ATTN_KEY_CHUNK) for k0 in range(0, t_all, ATTN_KEY_CHUNK)]

    def score_chunk(slot, ks, qr, buf, run):
        sc = lax.dot_general(kab[slot, ks, :], qr, (((1,), (1,)), ((), ())), preferred_element_type=F32)
        sbuf[buf][ks, :] = sc
        while sc.shape[0] > 8:
            half = sc.shape[0] // 2
            sc = jnp.maximum(sc[0:half, :], sc[half:, :])
        return sc if run is None else jnp.maximum(run, sc)

    def q_rows(r):
        return q2[pl.ds(pl.multiple_of(r * rb, rb), rb), :]

    def region(g, parity, do_scores, do_weigh):
        cur, oth = parity * span, (1 - parity) * span
        for j in range(span):
            slot = j % 2
            m = mbuf[cur + j]
            if do_scores:
                qr = q_rows((g + 1) * (span // 2) + j // 2)
            run, res = None, None
            for ks in chunks:
                if do_scores:
                    run = score_chunk(slot, ks, qr, oth + j, run)
                pbuf[cur + j][ks, :] = jnp.exp(sbuf[cur + j][ks, :] - m).astype(BF16)
                if do_weigh:
                    part = _dot(vt[:, ks], pbuf[oth + j][ks, :])
                    res = part if res is None else res + part
            if do_scores:
                mbuf[oth + j] = jnp.max(run, axis=0, keepdims=True)
            if do_weigh:
                acct[(g - 1) * (span // 2) + j // 2, slot * HEAD_DIM:(slot + 1) * HEAD_DIM, :] = (
                    res[0:HEAD_DIM, :] * (1.0 / res[HEAD_DIM:HEAD_DIM + 1, :]))

    for j in range(span):
        run = None
        for ks in chunks:
            run = score_chunk(j % 2, ks, q_rows(j // 2), j, run)
        mbuf[j] = jnp.max(run, axis=0, keepdims=True)
    region(0, 0, True, False)

    def body(g, carry):
        @pl.when(g % 2 == 1)
        def _odd():
            region(g, 1, True, True)

        @pl.when(g % 2 == 0)
        def _even():
            region(g, 0, True, True)

        return carry

    lax.fori_loop(1, n_regions - 1, body, 0)
    end_parity = (n_regions - 1) % 2
    region(n_regions - 1, end_parity, False, True)
    for j in range(span):
        res = _dot(vt[...], pbuf[end_parity * span + j][...])
        acct[(n_regions - 1) * (span // 2) + j // 2, (j % 2) * HEAD_DIM:(j % 2 + 1) * HEAD_DIM, :] = (
            res[0:HEAD_DIM, :] * (1.0 / res[HEAD_DIM:HEAD_DIM + 1, :]))

    per_tile = n // rb
    for r in range(nr):
        t, r0 = r // per_tile, (r % per_tile) * rb
        o_ref[0, r0:r0 + rb, t * LANES:(t + 1) * LANES] = acct[r].T


def _attention(q, k, v, cache_k, cache_v, layer, *, rb):
    bsz, n, _ = q.shape
    n_ctx = cache_k.shape[2]
    t_all = n + n_ctx
    gw = 2 * LANES
    qmap = lambda b, g: (b, 0, g)
    kmap = lambda b, g: (b, 0, 0)
    cmap = lambda b, g: (b, layer, 0, 0)
    return pl.pallas_call(
        functools.partial(_attn_kernel, n=n, n_ctx=n_ctx, rb=rb),
        grid=(bsz, N_KV_HEADS),
        in_specs=[
            pl.BlockSpec((1, n, gw), qmap),
            pl.BlockSpec((1, n, KV_W), kmap),
            pl.BlockSpec((1, n, KV_W), kmap),
            pl.BlockSpec((1, 1, n_ctx, KV_W), cmap),
            pl.BlockSpec((1, 1, n_ctx, KV_W), cmap),
        ],
        out_specs=pl.BlockSpec((1, n, gw), qmap),
        out_shape=jax.ShapeDtypeStruct((bsz, n, ATTN_W), F32),
        scratch_shapes=[
            pltpu.VMEM((2, t_all, LANES), BF16),
            pltpu.VMEM((VT_ROWS, t_all), BF16),
            pltpu.VMEM((2 * n, LANES), BF16),
            pltpu.VMEM((2 * n // rb, LANES, rb), F32),
            pltpu.VMEM((ATTN_BUFS, 1, rb), F32),
        ] + [pltpu.VMEM((t_all, rb), F32)] * ATTN_BUFS + [pltpu.VMEM((t_all, rb), BF16)] * ATTN_BUFS,
        compiler_params=pltpu.CompilerParams(
            dimension_semantics=("arbitrary", "arbitrary"), vmem_limit_bytes=VMEM_LIMIT),
        name="attn_lat",
    )(q, k, v, cache_k, cache_v)


def _attn_ctx_kernel(q_ref, k_ref, v_ref, o_ref, *, n):
    kt_all = k_ref[0, 0].T
    v = v_ref[0, 0]
    vr = pltpu.roll(v, HEAD_DIM, 1)
    low = lax.broadcasted_iota(jnp.int32, (n, LANES), 1) < HEAD_DIM
    ones_lo = jnp.where(low, 1.0, 0.0)
    zk = jnp.zeros((HEAD_DIM, n), F32)
    gw = 2 * LANES
    for g in range(N_KV_HEADS):
        kg = kt_all[g * HEAD_DIM:(g + 1) * HEAD_DIM, :]
        kts = (jnp.concatenate([kg, zk], axis=0).astype(BF16),
               jnp.concatenate([zk, kg], axis=0).astype(BF16))
        v_lo, v_hi = (v, vr) if g == 0 else (vr, v)
        vs = (jnp.concatenate([jnp.where(low, v_lo, 0.0), ones_lo], axis=1).astype(BF16),
              jnp.concatenate([jnp.where(low, 0.0, v_hi), 1.0 - ones_lo], axis=1).astype(BF16))
        q2 = jnp.concatenate([q_ref[0, :, g * gw:g * gw + LANES],
                              q_ref[0, :, g * gw + LANES:(g + 1) * gw]], axis=0)
        acc = None
        for kt_s, v_s in zip(kts, vs):
            s = _dot(q2, kt_s)
            m = jnp.max(s, axis=-1, keepdims=True)
            pv = _dot(jnp.exp(s - m).astype(BF16), v_s)
            acc = pv if acc is None else acc + pv
        o = acc[:, 0:LANES] / acc[:, LANES:]
        o_ref[0, :, g * gw:g * gw + LANES] = o[0:n]
        o_ref[0, :, g * gw + LANES:(g + 1) * gw] = o[n:]


def _attention_ctx(q, k, v, layer):
    bsz, n, _ = q.shape
    blk = lambda w: pl.BlockSpec((1, n, w), lambda b: (b, 0, 0))
    kv_blk = pl.BlockSpec((1, 1, n, KV_W), lambda b: (b, layer, 0, 0))
    return pl.pallas_call(
        functools.partial(_attn_ctx_kernel, n=n),
        grid=(bsz,),
        in_specs=[blk(ATTN_W), kv_blk, kv_blk],
        out_specs=blk(ATTN_W),
        out_shape=jax.ShapeDtypeStruct((bsz, n, ATTN_W), F32),
        compiler_params=pltpu.CompilerParams(
            dimension_semantics=("arbitrary",), vmem_limit_bytes=VMEM_LIMIT),
        name="attn_ctx",
    )(q, k, v)


def _mix_kernel(attn_ref, x_ref, xp_ref, xn_ref, mod_ref, g_ref, w_ref, wout_ref,
                convw_ref, convb_ref, wpool_ref, pscale_ref, fg_ref,
                o_ref, xe, yext, uext, *, tm, tps, seq, final):
    j = pl.program_id(0) % tps
    pm = jnp.where(j == 0, 0.0, 1.0)
    nm = jnp.where(j == tps - 1, 0.0, 1.0)
    c = CONV_W
    g = g_ref[...]
    x = x_ref[...]
    xe[0:tm, :] = _mod_norm(x, g, mod_ref).astype(BF16)
    halo = jnp.concatenate([xn_ref[...], xp_ref[...]], axis=0)
    xe[tm:, :] = _mod_norm(halo, g, mod_ref).astype(BF16)

    def proj(rows, off, width):
        return _dot(xe[0:rows, :], w_ref[0, :, off:off + width])

    h_all = proj(tm + 2 * HALO, COL_CONV_H, c)
    c_all = proj(tm + 2 * HALO, COL_CONV_C, c)
    u_all = proj(tm + 2 * HALO, COL_POOL_U, POOL_W)
    z_a = proj(tm, COL_ATTN_GATE, ATTN_W)
    b_c = proj(tm, COL_CONV_B, c)
    z_c = proj(tm, COL_CONV_GATE, c)
    z_p = proj(tm, COL_POOL_GATE, POOL_W)

    y = h_all * c_all
    yext[0:HALO, :] = y[tm + HALO:, :] * pm
    yext[HALO:HALO + tm, :] = y[0:tm, :]
    yext[HALO + tm:, :] = y[tm:tm + HALO, :] * nm
    conv = (yext[HALO - 1:HALO - 1 + tm, :] * convw_ref[0:1, :]
            + yext[HALO:HALO + tm, :] * convw_ref[1:2, :]
            + yext[HALO + 1:HALO + 1 + tm, :] * convw_ref[2:3, :] + convb_ref[...])
    conv_o = (b_c * conv * _silu(z_c)).astype(BF16)

    u = u_all[0:tm, :]
    uext[0:HALO, :] = u_all[tm + HALO:, :] * pm
    uext[HALO:HALO + tm, :] = u
    uext[HALO + tm:, :] = u_all[tm:tm + HALO, :] * nm

    def win(k0, k1, sl):
        tot = None
        for k in range(k0, k1):
            t = uext[HALO + k:HALO + k + tm, sl]
            tot = t if tot is None else tot + t
        return tot

    lane = lax.broadcasted_iota(jnp.int32, (tm, LANES), 1)
    pos = j * tm + lax.broadcasted_iota(jnp.int32, (tm, LANES), 0)
    low = lane < POOL_GROUP
    ds = []
    for ti, (wa, wb) in enumerate(((2, 4), (8, 16))):
        sl = slice(ti * LANES, (ti + 1) * LANES)
        s_a = win(-(wa // 2), wa // 2, sl)
        s_b = s_a + win(-(wb // 2), -(wa // 2), sl) + win(wa // 2, wb // 2, sl)
        half = jnp.where(low, wa // 2, wb // 2)
        cnt = jnp.minimum(pos + half - 1, seq - 1) - jnp.maximum(pos - half, 0) + 1
        tot = jnp.where(low, s_a, s_b)
        ds.append(tot / cnt.astype(F32) - u[:, sl])
    d = jnp.concatenate(ds, axis=-1).astype(BF16)
    pool_o = (_dot(d, wpool_ref[...]) * pscale_ref[...] * _silu(z_p)).astype(BF16)

    attn_o = (attn_ref[...] * _silu(z_a)).astype(BF16)
    a = ATTN_W
    out = (_dot(attn_o, wout_ref[0, 0:a, :]) + _dot(conv_o, wout_ref[0, a:a + c, :])
           + _dot(pool_o, wout_ref[0, a + c:, :]))
    res = x + mod_ref[0, 0, 2:3, :] * out
    if final:
        ms = jnp.mean(res * res, axis=-1, keepdims=True)
        res = res * lax.rsqrt(ms + EPS) * fg_ref[...]
    o_ref[...] = res


def _mix(attn2, x2, mod, mod_row, norm_g, w_in_bf, wout_bf, layer, conv_w, conv_b, wpool_bd, pool_scale,
         final_g, *, seq, tm, final):
    tokens = x2.shape[0]
    tps = seq // tm
    hb = tm // HALO
    nhb = tokens // HALO
    row = lambda i: (i, 0)
    const = lambda i: (0, 0)
    in_specs = [
        pl.BlockSpec((tm, ATTN_W), row),
        pl.BlockSpec((tm, D_MODEL), row),
        pl.BlockSpec((HALO, D_MODEL), lambda i: (jnp.maximum(i * hb - 1, 0), 0)),
        pl.BlockSpec((HALO, D_MODEL), lambda i: (jnp.minimum((i + 1) * hb, nhb - 1), 0)),
        _mod_spec(layer, mod_row, tps),
        pl.BlockSpec((1, D_MODEL), const),
        pl.BlockSpec((1, D_MODEL, IN_W), lambda i: (layer, 0, 0)),
        pl.BlockSpec((1, D_MODEL, D_MODEL), lambda i: (layer, 0, 0)),
        pl.BlockSpec((3, CONV_W), const),
        pl.BlockSpec((1, CONV_W), const),
        pl.BlockSpec((POOL_W, POOL_W), const),
        pl.BlockSpec((1, POOL_W), const),
        pl.BlockSpec((1, D_MODEL), const),
    ]
    return pl.pallas_call(
        functools.partial(_mix_kernel, tm=tm, tps=tps, seq=seq, final=final),
        grid=(tokens // tm,),
        in_specs=in_specs,
        out_specs=pl.BlockSpec((tm, D_MODEL), row),
        out_shape=jax.ShapeDtypeStruct((tokens, D_MODEL), F32),
        scratch_shapes=[
            pltpu.VMEM((tm + 2 * HALO, D_MODEL), BF16),
            pltpu.VMEM((tm + 2 * HALO, CONV_W), F32),
            pltpu.VMEM((tm + 2 * HALO, POOL_W), F32),
        ],
        compiler_params=pltpu.CompilerParams(
            dimension_semantics=("arbitrary",), vmem_limit_bytes=VMEM_LIMIT),
        name="mix_final" if final else "mix",
    )(attn2, x2, x2, x2, mod, norm_g.reshape(1, D_MODEL), w_in_bf, wout_bf,
      conv_w, conv_b.reshape(1, CONV_W), wpool_bd, pool_scale.reshape(1, POOL_W),
      final_g.reshape(1, D_MODEL))


def _rope_tables(n):
    rows = n // GRID_W
    row = jnp.repeat(jnp.arange(rows), GRID_W).astype(F32)
    col = jnp.tile(jnp.arange(GRID_W), rows).astype(F32)
    half = HEAD_DIM // 2
    inv = 1.0 / (ROPE_THETA ** (jnp.arange(0, half, 2, dtype=F32) / half))
    ar = row[:, None] * inv
    ac = col[:, None] * inv
    ang = jnp.tile(jnp.concatenate([ar, ar, ac, ac], axis=-1), (1, LANES // HEAD_DIM))
    cos = jnp.cos(ang)
    sin = jnp.sin(ang)
    second = (jnp.arange(LANES) % half) >= (half // 2)
    return cos, jnp.where(second, sin, -sin)


def _block_diag(blocks):
    n = len(blocks)
    rows = []
    for i, b in enumerate(blocks):
        rows.append(jnp.concatenate(
            [b if j == i else jnp.zeros((b.shape[0], blocks[j].shape[1]), b.dtype) for j in range(n)],
            axis=1))
    return jnp.concatenate(rows, axis=0)


def kernel(x_prompt, x_sample, cache_k, cache_v, c, c_ctx, norm_g, w_ada, b_ada, w_in,
           q_norm_g, k_norm_g, conv_w, conv_b, pool_w, pool_scale, w_out, final_g):
    depth = w_in.shape[0]
    bp, sp, _ = x_prompt.shape
    bs, ss, _ = x_sample.shape
    n_ctx = cache_k.shape[2]

    cvec = jnp.concatenate(
        [c, c_ctx[None, :], jnp.zeros((ADA_ROWS - bs - 1, D_MODEL), F32)], axis=0)
    mod = _ada(cvec, w_ada, b_ada)
    mod = mod.reshape(depth, ADA_ROWS, 3, D_MODEL)

    w_in_bf = w_in.astype(BF16)
    w_out_bf = w_out.astype(BF16)
    tables = _rope_tables(ss)
    ones = jnp.full((HEAD_DIM, HEAD_DIM), 1.0 / HEAD_DIM, BF16)
    bd = _block_diag([ones, ones])
    ck = cache_k.reshape(bs, depth, n_ctx, KV_W)
    cv = cache_v.reshape(bs, depth, n_ctx, KV_W)

    h = x_prompt.reshape(bp * sp, D_MODEL)
    z = x_sample.reshape(bs * ss, D_MODEL)
    kv_ctx = (jnp.zeros((bp, depth, sp, KV_W), F32), jnp.zeros((bp, depth, sp, KV_W), F32))
    gains = jnp.concatenate(
        [jnp.tile(q_norm_g * (HEAD_DIM ** -0.5), (1, N_HEADS)), jnp.tile(k_norm_g, (1, N_KV_HEADS))],
        axis=-1).reshape(depth, 1, QK_W)
    for l in range(depth):
        wpool_bd = _block_diag([pool_w[l, g] for g in range(pool_w.shape[1])]).astype(BF16)
        final = l == depth - 1
        mix = functools.partial(
            _mix, norm_g=norm_g[l], w_in_bf=w_in_bf, wout_bf=w_out_bf, layer=l,
            conv_w=conv_w[l], conv_b=conv_b[l], wpool_bd=wpool_bd, pool_scale=pool_scale[l],
            final_g=final_g, final=final)

        q, k, v = _qkv(h, mod, bs, norm_g[l], w_in_bf, l, gains, bd, None, kv_ctx, seq=sp, tm=1024)
        kv_ctx = (k, v)
        attn = _attention_ctx(q.reshape(bp, sp, ATTN_W), k, v, l)
        h = mix(attn.reshape(bp * sp, ATTN_W), h, mod, bs, seq=sp, tm=256)

        q, k, v = _qkv(z, mod, None, norm_g[l], w_in_bf, l, gains, bd, tables, None, seq=ss, tm=1024)
        attn = _attention(q.reshape(bs, ss, ATTN_W), k.reshape(bs, ss, KV_W), v.reshape(bs, ss, KV_W),
                          ck, cv, l, rb=256)
        z = mix(attn.reshape(bs * ss, ATTN_W), z, mod, None, seq=ss, tm=512)

    y_prompt = h.reshape(bp, sp, D_MODEL)
    y_sample = z.reshape(bs, ss, D_MODEL)
    new_k, new_v = (a.reshape(bp, depth, sp, N_KV_HEADS, HEAD_DIM) for a in kv_ctx)
    return (y_prompt, y_sample, new_k, new_v)
```

```python
import functools

import jax
import jax.numpy as jnp
from jax import lax
from jax.experimental import pallas as pl
from jax.experimental.pallas import tpu as pltpu

D_MODEL = 1024
GRID_W = 64
N_HEADS = 8
N_KV_HEADS = 2
HEAD_DIM = 64
ATTN_W = N_HEADS * HEAD_DIM
KV_W = N_KV_HEADS * HEAD_DIM
QK_W = ATTN_W + KV_W
QKV_W = QK_W + KV_W
CONV_W = 256
POOL_W = 256
POOL_GROUP = 64
IN_W = 2816
COL_ATTN_GATE = QKV_W
COL_CONV_H = COL_ATTN_GATE + ATTN_W
COL_CONV_B = COL_CONV_H + CONV_W
COL_CONV_C = COL_CONV_B + CONV_W
COL_CONV_GATE = COL_CONV_C + CONV_W
COL_POOL_U = COL_CONV_GATE + CONV_W
COL_POOL_GATE = COL_POOL_U + POOL_W
ROPE_THETA = 10000.0
EPS = 1e-6
LANES = 128
HALO = 8
ADA_ROWS = 8
VMEM_LIMIT = 48 * 1024 * 1024

F32 = jnp.float32
BF16 = jnp.bfloat16


def _silu(z):
    return z * (1.0 / (1.0 + jnp.exp(-z)))


def _dot(a, b):
    return jnp.dot(a, b, preferred_element_type=F32)


def _mod_norm(x, g, mod_ref):
    ms = jnp.mean(x * x, axis=-1, keepdims=True)
    gs = g * (1.0 + mod_ref[0, 0, 1:2, :])
    return x * lax.rsqrt(ms + EPS) * gs + mod_ref[0, 0, 0:1, :]


def _ada_kernel(c_ref, w_ref, b_ref, o_ref):
    s = _silu(c_ref[...]).astype(BF16)
    o_ref[0] = _dot(s, w_ref[0].astype(BF16)) + b_ref[0]


def _ada(cvec, w_ada, b_ada):
    depth, _, n3 = w_ada.shape
    tn = 1024
    return pl.pallas_call(
        _ada_kernel,
        grid=(depth, n3 // tn),
        in_specs=[
            pl.BlockSpec((ADA_ROWS, D_MODEL), lambda l, j: (0, 0)),
            pl.BlockSpec((1, D_MODEL, tn), lambda l, j: (l, 0, j)),
            pl.BlockSpec((1, 1, tn), lambda l, j: (l, 0, j)),
        ],
        out_specs=pl.BlockSpec((1, ADA_ROWS, tn), lambda l, j: (l, 0, j)),
        out_shape=jax.ShapeDtypeStruct((depth, ADA_ROWS, n3), F32),
        name="ada_mod",
    )(cvec, w_ada, b_ada.reshape(depth, 1, n3))


QKV_SUB = 256


def _qkv_kernel(*refs, rope, nsub):
    if rope:
        (x_ref, mod_ref, g_ref, w_ref, gains_ref, bd_ref, cos_ref, sin_ref,
         q_ref, k_ref, v_ref) = refs
    else:
        x_ref, mod_ref, g_ref, w_ref, gains_ref, bd_ref = refs[:6]
        q_ref, k_ref, v_ref = refs[-3:]
    tm = x_ref.shape[0]
    sub = tm // nsub
    bd = bd_ref[...]
    g = g_ref[...]
    if rope:
        lane = lax.broadcasted_iota(jnp.int32, (sub, LANES), 1)
        second = (lane % (HEAD_DIM // 2)) >= (HEAD_DIM // 4)
    k_parts, v_parts = [None] * nsub, [None] * nsub

    def finish(s, c, part):
        rows = slice(s * sub, (s + 1) * sub)
        for h in range(2):
            j = 2 * c + h
            t = part[:, h * LANES:(h + 1) * LANES]
            if j * LANES >= QK_W:
                v_parts[s] = t
                continue
            sl = slice(j * LANES, (j + 1) * LANES)
            sq = t * t
            hi = sq.astype(BF16)
            lo = (sq - hi.astype(F32)).astype(BF16)
            msq = _dot(hi, bd) + _dot(lo, bd)
            t = t * lax.rsqrt(msq + EPS) * gains_ref[0, :, sl]
            if rope:
                partner = jnp.where(second, pltpu.roll(t, HEAD_DIM // 4, 1),
                                    pltpu.roll(t, LANES - HEAD_DIM // 4, 1))
                t = t * cos_ref[rows, :] + partner * sin_ref[rows, :]
            if j * LANES < ATTN_W:
                q_ref[rows, sl] = t.astype(BF16)
            else:
                k_parts[s] = t

    ncol = QKV_W // (2 * LANES)
    prev = None
    for s in range(nsub + 1):
        if s < nsub:
            xn = _mod_norm(x_ref[s * sub:(s + 1) * sub, :], g, mod_ref).astype(BF16)
        cur = []
        for c in range(ncol):
            if s < nsub:
                cur.append(_dot(xn, w_ref[0, :, c * 2 * LANES:(c + 1) * 2 * LANES]))
            if prev is not None:
                finish(s - 1, c, prev[c])
        prev = cur
    k_ref[...] = jnp.concatenate(k_parts, axis=0).reshape(k_ref.shape)
    v_ref[...] = jnp.concatenate(v_parts, axis=0).reshape(v_ref.shape)


def _mod_spec(layer, mod_row, tps):
    if mod_row is None:
        return pl.BlockSpec((1, 1, 3, D_MODEL), lambda i: (layer, i // tps, 0, 0))
    return pl.BlockSpec((1, 1, 3, D_MODEL), lambda i: (layer, mod_row, 0, 0))


def _qkv(x2, mod, mod_row, norm_g, w_in_bf, layer, gains, bd, tables, kv_prev, *, seq, tm):
    tokens = x2.shape[0]
    depth = w_in_bf.shape[0]
    tps = max(seq // tm, 1)
    rope = tables is not None
    row = lambda i: (i, 0)
    const = lambda i: (0, 0)
    in_specs = [
        pl.BlockSpec((tm, D_MODEL), row),
        _mod_spec(layer, mod_row, tps),
        pl.BlockSpec((1, D_MODEL), const),
        pl.BlockSpec((1, D_MODEL, QKV_W), lambda i: (layer, 0, 0)),
        pl.BlockSpec((1, 1, QK_W), lambda i: (layer, 0, 0)),
        pl.BlockSpec((LANES, LANES), const),
    ]
    args = [x2, mod, norm_g.reshape(1, D_MODEL), w_in_bf, gains, bd]
    out_shapes = [jax.ShapeDtypeStruct((tokens, ATTN_W), BF16)]
    out_specs = [pl.BlockSpec((tm, ATTN_W), row)]
    aliases = {}
    if rope:
        in_specs += [pl.BlockSpec((tm, LANES), lambda i: (i % tps, 0))] * 2
        args += list(tables)
        out_shapes += [jax.ShapeDtypeStruct((tokens, KV_W), F32)] * 2
        out_specs += [pl.BlockSpec((tm, KV_W), row)] * 2
    else:
        nseq = tm // seq
        out_shapes += [jax.ShapeDtypeStruct((tokens // seq, depth, seq, KV_W), F32)] * 2
        out_specs += [pl.BlockSpec((nseq, 1, seq, KV_W), lambda i: (i, layer, 0, 0))] * 2
        if kv_prev is not None:
            in_specs += [pl.BlockSpec(memory_space=pl.ANY)] * 2
            aliases = {len(args): 1, len(args) + 1: 2}
            args += list(kv_prev)
    return pl.pallas_call(
        functools.partial(_qkv_kernel, rope=rope, nsub=tm // QKV_SUB),
        grid=(tokens // tm,),
        in_specs=in_specs,
        out_specs=out_specs,
        out_shape=out_shapes,
        input_output_aliases=aliases,
        compiler_params=pltpu.CompilerParams(
            dimension_semantics=("arbitrary",), vmem_limit_bytes=VMEM_LIMIT),
        name="qkv_rope" if rope else "qkv",
    )(*args)


VT_ROWS = HEAD_DIM + 16
ATTN_REGION = 4
ATTN_BUFS = 2 * ATTN_REGION
ATTN_KEY_CHUNK = 256


def _attn_kernel(q_ref, k_ref, v_ref, ck_ref, cv_ref, o_ref, kab, vt, q2, acct, mbuf, *bufs, n, n_ctx, rb):
    kv = pl.program_id(1)
    nr = 2 * n // rb
    t_all = n + n_ctx
    sbuf, pbuf = bufs[:ATTN_BUFS], bufs[ATTN_BUFS:]

    def fill(k_src, v_src, r0, m):
        lane = lax.broadcasted_iota(jnp.int32, (m, LANES), 1)
        k_own = jnp.where((lane // HEAD_DIM) == kv, k_src, 0.0)
        k_oth = pltpu.roll(k_own, HEAD_DIM, 1)
        is0 = kv == 0
        kab[0, r0:r0 + m, :] = jnp.where(is0, k_own, k_oth).astype(BF16)
        kab[1, r0:r0 + m, :] = jnp.where(is0, k_oth, k_own).astype(BF16)
        v_t = v_src.T
        vt[0:HEAD_DIM, r0:r0 + m] = jnp.where(is0, v_t[0:HEAD_DIM, :], v_t[HEAD_DIM:, :]).astype(BF16)

    piece = 256
    for r0 in range(0, n, piece):
        fill(k_ref[0, r0:r0 + piece, :], v_ref[0, r0:r0 + piece, :], r0, piece)
    fill(ck_ref[0, 0], cv_ref[0, 0], n, n_ctx)
    vt[HEAD_DIM:, :] = jnp.ones((VT_ROWS - HEAD_DIM, t_all), BF16)
    for c0 in range(0, n, piece):
        q2[:, c0:c0 + piece] = q_ref[0, c0:c0 + piece, 0:LANES].astype(F32).T.astype(BF16)
        q2[:, n + c0:n + c0 + piece] = q_ref[0, c0:c0 + piece, LANES:].astype(F32).T.astype(BF16)

    span = ATTN_REGION
    n_regions = 2 * nr // span
    chunks = [slice(k0, k0 + ATTN_KEY_CHUNK) for k0 in range(0, t_all, ATTN_KEY_CHUNK)]

    def score_chunk(slot, ks, qr, buf, run):
        sc = _dot(kab[slot, ks, :], qr)
        sbuf[buf][ks, :] = sc
        while sc.shape[0] > 8:
            half = sc.shape[0] // 2
            sc = jnp.maximum(sc[0:half, :], sc[half:, :])
        return sc if run is None else jnp.maximum(run, sc)

    def q_rows(r):
        return q2[:, pl.ds(pl.multiple_of(r * rb, rb), rb)]

    def region(g, parity, do_scores, do_weigh):
        cur, oth = parity * span, (1 - parity) * span
        for j in range(span):
            slot = j % 2
            m = mbuf[cur + j]
            if do_scores:
                qr = q_rows((g + 1) * (span // 2) + j // 2)
            run, res = None, None
            for ks in chunks:
                if do_scores:
                    run = score_chunk(slot, ks, qr, oth + j, run)
                pbuf[cur + j][ks, :] = jnp.exp(sbuf[cur + j][ks, :] - m).astype(BF16)
                if do_weigh:
                    part = _dot(vt[:, ks], pbuf[oth + j][ks, :])
                    res = part if res is None else res + part
            if do_scores:
                mbuf[oth + j] = jnp.max(run, axis=0, keepdims=True)
            if do_weigh:
                acct[(g - 1) * (span // 2) + j // 2, slot * HEAD_DIM:(slot + 1) * HEAD_DIM, :] = (
                    res[0:HEAD_DIM, :] * (1.0 / res[HEAD_DIM:HEAD_DIM + 1, :]))

    for j in range(span):
        run = None
        for ks in chunks:
            run = score_chunk(j % 2, ks, q_rows(j // 2), j, run)
        mbuf[j] = jnp.max(run, axis=0, keepdims=True)
    region(0, 0, True, False)

    def body(g, carry):
        @pl.when(g % 2 == 1)
        def _odd():
            region(g, 1, True, True)

        @pl.when(g % 2 == 0)
        def _even():
            region(g, 0, True, True)

        return carry

    lax.fori_loop(1, n_regions - 1, body, 0)
    end_parity = (n_regions - 1) % 2
    region(n_regions - 1, end_parity, False, True)
    for j in range(span):
        res = _dot(vt[...], pbuf[end_parity * span + j][...])
        acct[(n_regions - 1) * (span // 2) + j // 2, (j % 2) * HEAD_DIM:(j % 2 + 1) * HEAD_DIM, :] = (
            res[0:HEAD_DIM, :] * (1.0 / res[HEAD_DIM:HEAD_DIM + 1, :]))

    per_tile = n // rb
    for r in range(nr):
        t, r0 = r // per_tile, (r % per_tile) * rb
        o_ref[0, r0:r0 + rb, t * LANES:(t + 1) * LANES] = acct[r].T


def _attention(q, k, v, cache_k, cache_v, layer, *, rb):
    bsz, n, _ = q.shape
    n_ctx = cache_k.shape[2]
    t_all = n + n_ctx
    gw = 2 * LANES
    qmap = lambda b, g: (b, 0, g)
    kmap = lambda b, g: (b, 0, 0)
    cmap = lambda b, g: (b, layer, 0, 0)
    return pl.pallas_call(
        functools.partial(_attn_kernel, n=n, n_ctx=n_ctx, rb=rb),
        grid=(bsz, N_KV_HEADS),
        in_specs=[
            pl.BlockSpec((1, n, gw), qmap),
            pl.BlockSpec((1, n, KV_W), kmap),
            pl.BlockSpec((1, n, KV_W), kmap),
            pl.BlockSpec((1, 1, n_ctx, KV_W), cmap),
            pl.BlockSpec((1, 1, n_ctx, KV_W), cmap),
        ],
        out_specs=pl.BlockSpec((1, n, gw), qmap),
        out_shape=jax.ShapeDtypeStruct((bsz, n, ATTN_W), F32),
        scratch_shapes=[
            pltpu.VMEM((2, t_all, LANES), BF16),
            pltpu.VMEM((VT_ROWS, t_all), BF16),
            pltpu.VMEM((LANES, 2 * n), BF16),
            pltpu.VMEM((2 * n // rb, LANES, rb), F32),
            pltpu.VMEM((ATTN_BUFS, 1, rb), F32),
        ] + [pltpu.VMEM((t_all, rb), F32)] * ATTN_BUFS + [pltpu.VMEM((t_all, rb), BF16)] * ATTN_BUFS,
        compiler_params=pltpu.CompilerParams(
            dimension_semantics=("arbitrary", "arbitrary"), vmem_limit_bytes=VMEM_LIMIT),
        name="attn_lat",
    )(q, k, v, cache_k, cache_v)


def _attn_ctx_kernel(q_ref, k_ref, v_ref, o_ref, *, n):
    kt_all = k_ref[0, 0].T
    v = v_ref[0, 0]
    vr = pltpu.roll(v, HEAD_DIM, 1)
    low = lax.broadcasted_iota(jnp.int32, (n, LANES), 1) < HEAD_DIM
    ones_lo = jnp.where(low, 1.0, 0.0)
    zk = jnp.zeros((HEAD_DIM, n), F32)
    gw = 2 * LANES
    for g in range(N_KV_HEADS):
        kg = kt_all[g * HEAD_DIM:(g + 1) * HEAD_DIM, :]
        kts = (jnp.concatenate([kg, zk], axis=0).astype(BF16),
               jnp.concatenate([zk, kg], axis=0).astype(BF16))
        v_lo, v_hi = (v, vr) if g == 0 else (vr, v)
        vs = (jnp.concatenate([jnp.where(low, v_lo, 0.0), ones_lo], axis=1).astype(BF16),
              jnp.concatenate([jnp.where(low, 0.0, v_hi), 1.0 - ones_lo], axis=1).astype(BF16))
        q2 = jnp.concatenate([q_ref[0, :, g * gw:g * gw + LANES],
                              q_ref[0, :, g * gw + LANES:(g + 1) * gw]], axis=0)
        acc = None
        for kt_s, v_s in zip(kts, vs):
            s = _dot(q2, kt_s)
            m = jnp.max(s, axis=-1, keepdims=True)
            pv = _dot(jnp.exp(s - m).astype(BF16), v_s)
            acc = pv if acc is None else acc + pv
        o = acc[:, 0:LANES] / acc[:, LANES:]
        o_ref[0, :, g * gw:g * gw + LANES] = o[0:n]
        o_ref[0, :, g * gw + LANES:(g + 1) * gw] = o[n:]


def _attention_ctx(q, k, v, layer):
    bsz, n, _ = q.shape
    blk = lambda w: pl.BlockSpec((1, n, w), lambda b: (b, 0, 0))
    kv_blk = pl.BlockSpec((1, 1, n, KV_W), lambda b: (b, layer, 0, 0))
    return pl.pallas_call(
        functools.partial(_attn_ctx_kernel, n=n),
        grid=(bsz,),
        in_specs=[blk(ATTN_W), kv_blk, kv_blk],
        out_specs=blk(ATTN_W),
        out_shape=jax.ShapeDtypeStruct((bsz, n, ATTN_W), F32),
        compiler_params=pltpu.CompilerParams(
            dimension_semantics=("arbitrary",), vmem_limit_bytes=VMEM_LIMIT),
        name="attn_ctx",
    )(q, k, v)


def _mix_kernel(attn_ref, x_ref, xp_ref, xn_ref, mod_ref, g_ref, w_ref, wout_ref,
                convw_ref, convb_ref, wpool_ref, pscale_ref, fg_ref,
                o_ref, xe, yext, uext, *, tm, tps, seq, final):
    j = pl.program_id(0) % tps
    pm = jnp.where(j == 0, 0.0, 1.0)
    nm = jnp.where(j == tps - 1, 0.0, 1.0)
    c = CONV_W
    g = g_ref[...]
    x = x_ref[...]
    xe[0:tm, :] = _mod_norm(x, g, mod_ref).astype(BF16)
    halo = jnp.concatenate([xn_ref[...], xp_ref[...]], axis=0)
    xe[tm:, :] = _mod_norm(halo, g, mod_ref).astype(BF16)

    def proj(rows, off, width):
        return _dot(xe[0:rows, :], w_ref[0, :, off:off + width])

    h_all = proj(tm + 2 * HALO, COL_CONV_H, c)
    c_all = proj(tm + 2 * HALO, COL_CONV_C, c)
    u_all = proj(tm + 2 * HALO, COL_POOL_U, POOL_W)
    z_a = proj(tm, COL_ATTN_GATE, ATTN_W)
    a = ATTN_W
    attn_o = (attn_ref[...] * _silu(z_a)).astype(BF16)
    out_a = _dot(attn_o, wout_ref[0, 0:a, :])
    b_c = proj(tm, COL_CONV_B, c)
    z_c = proj(tm, COL_CONV_GATE, c)
    z_p = proj(tm, COL_POOL_GATE, POOL_W)

    y = h_all * c_all
    yext[0:HALO, :] = y[tm + HALO:, :] * pm
    yext[HALO:HALO + tm, :] = y[0:tm, :]
    yext[HALO + tm:, :] = y[tm:tm + HALO, :] * nm
    conv = (yext[HALO - 1:HALO - 1 + tm, :] * convw_ref[0:1, :]
            + yext[HALO:HALO + tm, :] * convw_ref[1:2, :]
            + yext[HALO + 1:HALO + 1 + tm, :] * convw_ref[2:3, :] + convb_ref[...])
    conv_o = (b_c * conv * _silu(z_c)).astype(BF16)

    u = u_all[0:tm, :]
    uext[0:HALO, :] = u_all[tm + HALO:, :] * pm
    uext[HALO:HALO + tm, :] = u
    uext[HALO + tm:, :] = u_all[tm:tm + HALO, :] * nm

    def win(k0, k1, sl):
        tot = None
        for k in range(k0, k1):
            t = uext[HALO + k:HALO + k + tm, sl]
            tot = t if tot is None else tot + t
        return tot

    lane = lax.broadcasted_iota(jnp.int32, (tm, LANES), 1)
    pos = j * tm + lax.broadcasted_iota(jnp.int32, (tm, LANES), 0)
    low = lane < POOL_GROUP
    ds = []
    for ti, (wa, wb) in enumerate(((2, 4), (8, 16))):
        sl = slice(ti * LANES, (ti + 1) * LANES)
        s_a = win(-(wa // 2), wa // 2, sl)
        s_b = s_a + win(-(wb // 2), -(wa // 2), sl) + win(wa // 2, wb // 2, sl)
        half = jnp.where(low, wa // 2, wb // 2)
        cnt = jnp.minimum(pos + half - 1, seq - 1) - jnp.maximum(pos - half, 0) + 1
        tot = jnp.where(low, s_a, s_b)
        ds.append(tot / cnt.astype(F32) - u[:, sl])
    d = jnp.concatenate(ds, axis=-1).astype(BF16)
    pool_o = (_dot(d, wpool_ref[...]) * pscale_ref[...] * _silu(z_p)).astype(BF16)

    out = out_a + _dot(conv_o, wout_ref[0, a:a + c, :]) + _dot(pool_o, wout_ref[0, a + c:, :])
    res = x + mod_ref[0, 0, 2:3, :] * out
    if final:
        ms = jnp.mean(res * res, axis=-1, keepdims=True)
        res = res * lax.rsqrt(ms + EPS) * fg_ref[...]
    o_ref[...] = res


def _mix(attn2, x2, mod, mod_row, norm_g, w_in_bf, wout_bf, layer, conv_w, conv_b, wpool_bd, pool_scale,
         final_g, *, seq, tm, final):
    tokens = x2.shape[0]
    tps = seq // tm
    hb = tm // HALO
    nhb = tokens // HALO
    row = lambda i: (i, 0)
    const = lambda i: (0, 0)
    in_specs = [
        pl.BlockSpec((tm, ATTN_W), row),
        pl.BlockSpec((tm, D_MODEL), row),
        pl.BlockSpec((HALO, D_MODEL), lambda i: (jnp.maximum(i * hb - 1, 0), 0)),
        pl.BlockSpec((HALO, D_MODEL), lambda i: (jnp.minimum((i + 1) * hb, nhb - 1), 0)),
        _mod_spec(layer, mod_row, tps),
        pl.BlockSpec((1, D_MODEL), const),
        pl.BlockSpec((1, D_MODEL, IN_W), lambda i: (layer, 0, 0)),
        pl.BlockSpec((1, D_MODEL, D_MODEL), lambda i: (layer, 0, 0)),
        pl.BlockSpec((3, CONV_W), const),
        pl.BlockSpec((1, CONV_W), const),
        pl.BlockSpec((POOL_W, POOL_W), const),
        pl.BlockSpec((1, POOL_W), const),
        pl.BlockSpec((1, D_MODEL), const),
    ]
    return pl.pallas_call(
        functools.partial(_mix_kernel, tm=tm, tps=tps, seq=seq, final=final),
        grid=(tokens // tm,),
        in_specs=in_specs,
        out_specs=pl.BlockSpec((tm, D_MODEL), row),
        out_shape=jax.ShapeDtypeStruct((tokens, D_MODEL), F32),
        scratch_shapes=[
            pltpu.VMEM((tm + 2 * HALO, D_MODEL), BF16),
            pltpu.VMEM((tm + 2 * HALO, CONV_W), F32),
            pltpu.VMEM((tm + 2 * HALO, POOL_W), F32),
        ],
        compiler_params=pltpu.CompilerParams(
            dimension_semantics=("arbitrary",), vmem_limit_bytes=VMEM_LIMIT),
        name="mix_final" if final else "mix",
    )(attn2, x2, x2, x2, mod, norm_g.reshape(1, D_MODEL), w_in_bf, wout_bf,
      conv_w, conv_b.reshape(1, CONV_W), wpool_bd, pool_scale.reshape(1, POOL_W),
      final_g.reshape(1, D_MODEL))


def _rope_tables(n):
    rows = n // GRID_W
    row = jnp.repeat(jnp.arange(rows), GRID_W).astype(F32)
    col = jnp.tile(jnp.arange(GRID_W), rows).astype(F32)
    half = HEAD_DIM // 2
    inv = 1.0 / (ROPE_THETA ** (jnp.arange(0, half, 2, dtype=F32) / half))
    ar = row[:, None] * inv
    ac = col[:, None] * inv
    ang = jnp.tile(jnp.concatenate([ar, ar, ac, ac], axis=-1), (1, LANES // HEAD_DIM))
    cos = jnp.cos(ang)
    sin = jnp.sin(ang)
    second = (jnp.arange(LANES) % half) >= (half // 2)
    return cos, jnp.where(second, sin, -sin)


def _block_diag(blocks):
    n = len(blocks)
    rows = []
    for i, b in enumerate(blocks):
        rows.append(jnp.concatenate(
            [b if j == i else jnp.zeros((b.shape[0], blocks[j].shape[1]), b.dtype) for j in range(n)],
            axis=1))
    return jnp.concatenate(rows, axis=0)


def kernel(x_prompt, x_sample, cache_k, cache_v, c, c_ctx, norm_g, w_ada, b_ada, w_in,
           q_norm_g, k_norm_g, conv_w, conv_b, pool_w, pool_scale, w_out, final_g):
    depth = w_in.shape[0]
    bp, sp, _ = x_prompt.shape
    bs, ss, _ = x_sample.shape
    n_ctx = cache_k.shape[2]

    cvec = jnp.concatenate(
        [c, c_ctx[None, :], jnp.zeros((ADA_ROWS - bs - 1, D_MODEL), F32)], axis=0)
    mod = _ada(cvec, w_ada, b_ada)
    mod = mod.reshape(depth, ADA_ROWS, 3, D_MODEL)

    w_in_bf = w_in.astype(BF16)
    w_out_bf = w_out.astype(BF16)
    tables = _rope_tables(ss)
    ones = jnp.full((HEAD_DIM, HEAD_DIM), 1.0 / HEAD_DIM, BF16)
    bd = _block_diag([ones, ones])
    ck = cache_k.reshape(bs, depth, n_ctx, KV_W)
    cv = cache_v.reshape(bs, depth, n_ctx, KV_W)

    h = x_prompt.reshape(bp * sp, D_MODEL)
    z = x_sample.reshape(bs * ss, D_MODEL)
    kv_ctx = (jnp.zeros((bp, depth, sp, KV_W), F32), jnp.zeros((bp, depth, sp, KV_W), F32))
    gains = jnp.concatenate(
        [jnp.tile(q_norm_g * (HEAD_DIM ** -0.5), (1, N_HEADS)), jnp.tile(k_norm_g, (1, N_KV_HEADS))],
        axis=-1).reshape(depth, 1, QK_W)
    for l in range(depth):
        wpool_bd = _block_diag([pool_w[l, g] for g in range(pool_w.shape[1])]).astype(BF16)
        final = l == depth - 1
        mix = functools.partial(
            _mix, norm_g=norm_g[l], w_in_bf=w_in_bf, wout_bf=w_out_bf, layer=l,
            conv_w=conv_w[l], conv_b=conv_b[l], wpool_bd=wpool_bd, pool_scale=pool_scale[l],
            final_g=final_g, final=final)

        q, k, v = _qkv(h, mod, bs, norm_g[l], w_in_bf, l, gains, bd, None, kv_ctx, seq=sp, tm=1024)
        kv_ctx = (k, v)
        attn = _attention_ctx(q.reshape(bp, sp, ATTN_W), k, v, l)
        h = mix(attn.reshape(bp * sp, ATTN_W), h, mod, bs, seq=sp, tm=256)

        q, k, v = _qkv(z, mod, None, norm_g[l], w_in_bf, l, gains, bd, tables, None, seq=ss, tm=1024)
        attn = _attention(q.reshape(bs, ss, ATTN_W), k.reshape(bs, ss, KV_W), v.reshape(bs, ss, KV_W),
                          ck, cv, l, rb=256)
        z = mix(attn.reshape(bs * ss, ATTN_W), z, mod, None, seq=ss, tm=512)

    y_prompt = h.reshape(bp, sp, D_MODEL)
    y_sample = z.reshape(bs, ss, D_MODEL)
    new_k, new_v = (a.reshape(bp, depth, sp, N_KV_HEADS, HEAD_DIM) for a in kv_ctx)
    return (y_prompt, y_sample, new_k, new_v)
```

```python
import functools

import jax
import jax.numpy as jnp
from jax import lax
from jax.experimental import pallas as pl
from jax.experimental.pallas import tpu as pltpu

D_MODEL = 1024
GRID_W = 64
N_HEADS = 8
N_KV_HEADS = 2
HEAD_DIM = 64
ATTN_W = N_HEADS * HEAD_DIM
KV_W = N_KV_HEADS * HEAD_DIM
QK_W = ATTN_W + KV_W
QKV_W = QK_W + KV_W
CONV_W = 256
POOL_W = 256
POOL_GROUP = 64
IN_W = 2816
COL_ATTN_GATE = QKV_W
COL_CONV_H = COL_ATTN_GATE + ATTN_W
COL_CONV_B = COL_CONV_H + CONV_W
COL_CONV_C = COL_CONV_B + CONV_W
COL_CONV_GATE = COL_CONV_C + CONV_W
COL_POOL_U = COL_CONV_GATE + CONV_W
COL_POOL_GATE = COL_POOL_U + POOL_W
ROPE_THETA = 10000.0
EPS = 1e-6
LANES = 128
HALO = 8
ADA_ROWS = 8
VMEM_LIMIT = 48 * 1024 * 1024

F32 = jnp.float32
BF16 = jnp.bfloat16


def _silu(z):
    return z * (1.0 / (1.0 + jnp.exp(-z)))


def _dot(a, b):
    return jnp.dot(a, b, preferred_element_type=F32)


def _mod_norm(x, g, mod_ref):
    ms = jnp.mean(x * x, axis=-1, keepdims=True)
    gs = g * (1.0 + mod_ref[0, 0, 1:2, :])
    return x * lax.rsqrt(ms + EPS) * gs + mod_ref[0, 0, 0:1, :]


def _ada_kernel(c_ref, w_ref, b_ref, o_ref):
    s = _silu(c_ref[...]).astype(BF16)
    o_ref[0] = _dot(s, w_ref[0].astype(BF16)) + b_ref[0]


def _ada(cvec, w_ada, b_ada):
    depth, _, n3 = w_ada.shape
    tn = 1024
    return pl.pallas_call(
        _ada_kernel,
        grid=(depth, n3 // tn),
        in_specs=[
            pl.BlockSpec((ADA_ROWS, D_MODEL), lambda l, j: (0, 0)),
            pl.BlockSpec((1, D_MODEL, tn), lambda l, j: (l, 0, j)),
            pl.BlockSpec((1, 1, tn), lambda l, j: (l, 0, j)),
        ],
        out_specs=pl.BlockSpec((1, ADA_ROWS, tn), lambda l, j: (l, 0, j)),
        out_shape=jax.ShapeDtypeStruct((depth, ADA_ROWS, n3), F32),
        name="ada_mod",
    )(cvec, w_ada, b_ada.reshape(depth, 1, n3))


QKV_SUB = 256


def _qkv_kernel(*refs, rope, nsub):
    if rope:
        (x_ref, mod_ref, g_ref, w_ref, gains_ref, bd_ref, cos_ref, sin_ref,
         q_ref, k_ref, v_ref) = refs
    else:
        x_ref, mod_ref, g_ref, w_ref, gains_ref, bd_ref = refs[:6]
        q_ref, k_ref, v_ref = refs[-3:]
    tm = x_ref.shape[0]
    sub = tm // nsub
    bd = bd_ref[...]
    g = g_ref[...]
    if rope:
        lane = lax.broadcasted_iota(jnp.int32, (sub, LANES), 1)
        second = (lane % (HEAD_DIM // 2)) >= (HEAD_DIM // 4)
    k_parts, v_parts = [None] * nsub, [None] * nsub

    def finish(s, c, part):
        rows = slice(s * sub, (s + 1) * sub)
        for h in range(2):
            j = 2 * c + h
            t = part[:, h * LANES:(h + 1) * LANES]
            if j * LANES >= QK_W:
                v_parts[s] = t
                continue
            sl = slice(j * LANES, (j + 1) * LANES)
            sq = t * t
            hi = sq.astype(BF16)
            lo = (sq - hi.astype(F32)).astype(BF16)
            msq = _dot(hi, bd) + _dot(lo, bd)
            t = t * lax.rsqrt(msq + EPS) * gains_ref[0, :, sl]
            if rope:
                partner = jnp.where(second, pltpu.roll(t, HEAD_DIM // 4, 1),
                                    pltpu.roll(t, LANES - HEAD_DIM // 4, 1))
                t = t * cos_ref[rows, :] + partner * sin_ref[rows, :]
            if j * LANES < ATTN_W:
                q_ref[rows, sl] = t.astype(BF16)
            else:
                k_parts[s] = t

    ncol = QKV_W // (2 * LANES)
    prev = None
    for s in range(nsub + 1):
        if s < nsub:
            xn = _mod_norm(x_ref[s * sub:(s + 1) * sub, :], g, mod_ref).astype(BF16)
        cur = []
        for c in range(ncol):
            if s < nsub:
                cur.append(_dot(xn, w_ref[0, :, c * 2 * LANES:(c + 1) * 2 * LANES]))
            if prev is not None:
                finish(s - 1, c, prev[c])
        prev = cur
    k_ref[...] = jnp.concatenate(k_parts, axis=0).reshape(k_ref.shape)
    v_ref[...] = jnp.concatenate(v_parts, axis=0).reshape(v_ref.shape)


def _mod_spec(layer, mod_row, tps):
    if mod_row is None:
        return pl.BlockSpec((1, 1, 3, D_MODEL), lambda i: (layer, i // tps, 0, 0))
    return pl.BlockSpec((1, 1, 3, D_MODEL), lambda i: (layer, mod_row, 0, 0))


def _qkv(x2, mod, mod_row, norm_g, w_in_bf, layer, gains, bd, tables, kv_prev, *, seq, tm):
    tokens = x2.shape[0]
    depth = w_in_bf.shape[0]
    tps = max(seq // tm, 1)
    rope = tables is not None
    row = lambda i: (i, 0)
    const = lambda i: (0, 0)
    in_specs = [
        pl.BlockSpec((tm, D_MODEL), row),
        _mod_spec(layer, mod_row, tps),
        pl.BlockSpec((1, D_MODEL), const),
        pl.BlockSpec((1, D_MODEL, QKV_W), lambda i: (layer, 0, 0)),
        pl.BlockSpec((1, 1, QK_W), lambda i: (layer, 0, 0)),
        pl.BlockSpec((LANES, LANES), const),
    ]
    args = [x2, mod, norm_g.reshape(1, D_MODEL), w_in_bf, gains, bd]
    out_shapes = [jax.ShapeDtypeStruct((tokens, ATTN_W), BF16)]
    out_specs = [pl.BlockSpec((tm, ATTN_W), row)]
    aliases = {}
    if rope:
        in_specs += [pl.BlockSpec((tm, LANES), lambda i: (i % tps, 0))] * 2
        args += list(tables)
        out_shapes += [jax.ShapeDtypeStruct((tokens, KV_W), F32)] * 2
        out_specs += [pl.BlockSpec((tm, KV_W), row)] * 2
    else:
        nseq = tm // seq
        out_shapes += [jax.ShapeDtypeStruct((tokens // seq, depth, seq, KV_W), F32)] * 2
        out_specs += [pl.BlockSpec((nseq, 1, seq, KV_W), lambda i: (i, layer, 0, 0))] * 2
        if kv_prev is not None:
            in_specs += [pl.BlockSpec(memory_space=pl.ANY)] * 2
            aliases = {len(args): 1, len(args) + 1: 2}
            args += list(kv_prev)
    return pl.pallas_call(
        functools.partial(_qkv_kernel, rope=rope, nsub=tm // QKV_SUB),
        grid=(tokens // tm,),
        in_specs=in_specs,
        out_specs=out_specs,
        out_shape=out_shapes,
        input_output_aliases=aliases,
        compiler_params=pltpu.CompilerParams(
            dimension_semantics=("arbitrary",), vmem_limit_bytes=VMEM_LIMIT),
        name="qkv_rope" if rope else "qkv",
    )(*args)


VT_ROWS = HEAD_DIM + 16
ATTN_REGION = 4
ATTN_BUFS = 2 * ATTN_REGION
ATTN_KEY_CHUNK = 256


def _attn_kernel(q_ref, k_ref, v_ref, ck_ref, cv_ref, o_ref, kab, vt, q2, acct, mbuf, *bufs, n, n_ctx, rb):
    kv = pl.program_id(1)
    nr = 2 * n // rb
    t_all = n + n_ctx
    sbuf, pbuf = bufs[:ATTN_BUFS], bufs[ATTN_BUFS:]

    def fill(k_src, v_src, r0, m):
        lane = lax.broadcasted_iota(jnp.int32, (m, LANES), 1)
        k_own = jnp.where((lane // HEAD_DIM) == kv, k_src, 0.0)
        k_oth = pltpu.roll(k_own, HEAD_DIM, 1)
        is0 = kv == 0
        kab[0, r0:r0 + m, :] = jnp.where(is0, k_own, k_oth).astype(BF16)
        kab[1, r0:r0 + m, :] = jnp.where(is0, k_oth, k_own).astype(BF16)
        v_t = v_src.T
        vt[0:HEAD_DIM, r0:r0 + m] = jnp.where(is0, v_t[0:HEAD_DIM, :], v_t[HEAD_DIM:, :]).astype(BF16)

    piece = 256
    for r0 in range(0, n, piece):
        fill(k_ref[0, r0:r0 + piece, :], v_ref[0, r0:r0 + piece, :], r0, piece)
    fill(ck_ref[0, 0], cv_ref[0, 0], n, n_ctx)
    vt[HEAD_DIM:, :] = jnp.ones((VT_ROWS - HEAD_DIM, t_all), BF16)
    for c0 in range(0, n, piece):
        q2[:, c0:c0 + piece] = q_ref[0, c0:c0 + piece, 0:LANES].astype(F32).T.astype(BF16)
        q2[:, n + c0:n + c0 + piece] = q_ref[0, c0:c0 + piece, LANES:].astype(F32).T.astype(BF16)

    span = ATTN_REGION
    n_regions = 2 * nr // span
    chunks = [slice(k0, k0 + ATTN_KEY_CHUNK) for k0 in range(0, t_all, ATTN_KEY_CHUNK)]

    def score_chunk(slot, ks, qr, buf, run):
        sc = _dot(kab[slot, ks, :], qr)
        sbuf[buf][ks, :] = sc
        while sc.shape[0] > 8:
            half = sc.shape[0] // 2
            sc = jnp.maximum(sc[0:half, :], sc[half:, :])
        return sc if run is None else jnp.maximum(run, sc)

    def q_rows(r):
        return q2[:, pl.ds(pl.multiple_of(r * rb, rb), rb)]

    def region(g, parity, do_scores, do_weigh):
        cur, oth = parity * span, (1 - parity) * span
        for j in range(span):
            slot = j % 2
            m = mbuf[cur + j]
            if do_scores:
                qr = q_rows((g + 1) * (span // 2) + j // 2)
            run, res = None, None
            for ks in chunks:
                if do_scores:
                    run = score_chunk(slot, ks, qr, oth + j, run)
                pbuf[cur + j][ks, :] = jnp.exp(sbuf[cur + j][ks, :] - m).astype(BF16)
                if do_weigh:
                    part = _dot(vt[:, ks], pbuf[oth + j][ks, :])
                    res = part if res is None else res + part
            if do_scores:
                mbuf[oth + j] = jnp.max(run, axis=0, keepdims=True)
            if do_weigh:
                acct[(g - 1) * (span // 2) + j // 2, slot * HEAD_DIM:(slot + 1) * HEAD_DIM, :] = (
                    res[0:HEAD_DIM, :] * (1.0 / res[HEAD_DIM:HEAD_DIM + 1, :]))

    for j in range(span):
        run = None
        for ks in chunks:
            run = score_chunk(j % 2, ks, q_rows(j // 2), j, run)
        mbuf[j] = jnp.max(run, axis=0, keepdims=True)
    region(0, 0, True, False)

    def body(g, carry):
        @pl.when(g % 2 == 1)
        def _odd():
            region(g, 1, True, True)

        @pl.when(g % 2 == 0)
        def _even():
            region(g, 0, True, True)

        return carry

    lax.fori_loop(1, n_regions - 1, body, 0)
    end_parity = (n_regions - 1) % 2
    region(n_regions - 1, end_parity, False, True)
    for j in range(span):
        res = _dot(vt[...], pbuf[end_parity * span + j][...])
        acct[(n_regions - 1) * (span // 2) + j // 2, (j % 2) * HEAD_DIM:(j % 2 + 1) * HEAD_DIM, :] = (
            res[0:HEAD_DIM, :] * (1.0 / res[HEAD_DIM:HEAD_DIM + 1, :]))

    per_tile = n // rb
    for r in range(nr):
        t, r0 = r // per_tile, (r % per_tile) * rb
        o_ref[0, r0:r0 + rb, t * LANES:(t + 1) * LANES] = acct[r].T


def _attention(q, k, v, cache_k, cache_v, layer, *, rb):
    bsz, n, _ = q.shape
    n_ctx = cache_k.shape[2]
    t_all = n + n_ctx
    gw = 2 * LANES
    qmap = lambda b, g: (b, 0, g)
    kmap = lambda b, g: (b, 0, 0)
    cmap = lambda b, g: (b, layer, 0, 0)
    return pl.pallas_call(
        functools.partial(_attn_kernel, n=n, n_ctx=n_ctx, rb=rb),
        grid=(bsz, N_KV_HEADS),
        in_specs=[
            pl.BlockSpec((1, n, gw), qmap),
            pl.BlockSpec((1, n, KV_W), kmap),
            pl.BlockSpec((1, n, KV_W), kmap),
            pl.BlockSpec((1, 1, n_ctx, KV_W), cmap),
            pl.BlockSpec((1, 1, n_ctx, KV_W), cmap),
        ],
        out_specs=pl.BlockSpec((1, n, gw), qmap),
        out_shape=jax.ShapeDtypeStruct((bsz, n, ATTN_W), F32),
        scratch_shapes=[
            pltpu.VMEM((2, t_all, LANES), BF16),
            pltpu.VMEM((VT_ROWS, t_all), BF16),
            pltpu.VMEM((LANES, 2 * n), BF16),
            pltpu.VMEM((2 * n // rb, LANES, rb), F32),
            pltpu.VMEM((ATTN_BUFS, 1, rb), F32),
        ] + [pltpu.VMEM((t_all, rb), F32)] * ATTN_BUFS + [pltpu.VMEM((t_all, rb), BF16)] * ATTN_BUFS,
        compiler_params=pltpu.CompilerParams(
            dimension_semantics=("arbitrary", "arbitrary"), vmem_limit_bytes=VMEM_LIMIT),
        name="attn_lat",
    )(q, k, v, cache_k, cache_v)


CTX_BATCH = 4


def _attn_ctx_kernel(q_ref, k_ref, v_ref, o_ref, *, n):
    low = lax.broadcasted_iota(jnp.int32, (n, LANES), 1) < HEAD_DIM
    ones_lo = jnp.where(low, 1.0, 0.0)
    zk = jnp.zeros((HEAD_DIM, n), F32)
    gw = 2 * LANES
    for b in range(CTX_BATCH):
        kt_all = k_ref[b, 0].T
        v = v_ref[b, 0]
        vr = pltpu.roll(v, HEAD_DIM, 1)
        for g in range(N_KV_HEADS):
            kg = kt_all[g * HEAD_DIM:(g + 1) * HEAD_DIM, :]
            kts = (jnp.concatenate([kg, zk], axis=0).astype(BF16),
                   jnp.concatenate([zk, kg], axis=0).astype(BF16))
            v_lo, v_hi = (v, vr) if g == 0 else (vr, v)
            vs = (jnp.concatenate([jnp.where(low, v_lo, 0.0), ones_lo], axis=1).astype(BF16),
                  jnp.concatenate([jnp.where(low, 0.0, v_hi), 1.0 - ones_lo], axis=1).astype(BF16))
            q2 = jnp.concatenate([q_ref[b, :, g * gw:g * gw + LANES],
                                  q_ref[b, :, g * gw + LANES:(g + 1) * gw]], axis=0)
            acc = None
            for kt_s, v_s in zip(kts, vs):
                s = _dot(q2, kt_s)
                m = jnp.max(s, axis=-1, keepdims=True)
                pv = _dot(jnp.exp(s - m).astype(BF16), v_s)
                acc = pv if acc is None else acc + pv
            o = acc[:, 0:LANES] / acc[:, LANES:]
            o_ref[b, :, g * gw:g * gw + LANES] = o[0:n]
            o_ref[b, :, g * gw + LANES:(g + 1) * gw] = o[n:]


def _attention_ctx(q, k, v, layer):
    bsz, n, _ = q.shape
    blk = lambda w: pl.BlockSpec((CTX_BATCH, n, w), lambda b: (b, 0, 0))
    kv_blk = pl.BlockSpec((CTX_BATCH, 1, n, KV_W), lambda b: (b, layer, 0, 0))
    return pl.pallas_call(
        functools.partial(_attn_ctx_kernel, n=n),
        grid=(bsz // CTX_BATCH,),
        in_specs=[blk(ATTN_W), kv_blk, kv_blk],
        out_specs=blk(ATTN_W),
        out_shape=jax.ShapeDtypeStruct((bsz, n, ATTN_W), F32),
        compiler_params=pltpu.CompilerParams(
            dimension_semantics=("arbitrary",), vmem_limit_bytes=VMEM_LIMIT),
        name="attn_ctx",
    )(q, k, v)


def _mix_kernel(attn_ref, x_ref, xp_ref, xn_ref, mod_ref, g_ref, w_ref, wout_ref,
                convw_ref, convb_ref, wpool_ref, pscale_ref, fg_ref,
                o_ref, xe, yext, uext, *, tm, tps, seq, final):
    j = pl.program_id(0) % tps
    pm = jnp.where(j == 0, 0.0, 1.0)
    nm = jnp.where(j == tps - 1, 0.0, 1.0)
    c = CONV_W
    g = g_ref[...]
    x = x_ref[...]
    xe[0:tm, :] = _mod_norm(x, g, mod_ref).astype(BF16)
    if tps > 1:
        halo = jnp.concatenate([xn_ref[...], xp_ref[...]], axis=0)
        xe[tm:, :] = _mod_norm(halo, g, mod_ref).astype(BF16)

    def proj(rows, off, width):
        return _dot(xe[0:rows, :], w_ref[0, :, off:off + width])

    nseq = max(tm // seq, 1)
    whole = tps == 1
    ln = tm // nseq
    hrows = tm if whole else tm + 2 * HALO
    h_all = proj(hrows, COL_CONV_H, c)
    c_all = proj(hrows, COL_CONV_C, c)
    u_all = proj(hrows, COL_POOL_U, POOL_W)
    z_a = proj(tm, COL_ATTN_GATE, ATTN_W)
    a = ATTN_W
    attn_o = (attn_ref[...] * _silu(z_a)).astype(BF16)
    out_a = _dot(attn_o, wout_ref[0, 0:a, :])
    b_c = proj(tm, COL_CONV_B, c)
    z_c = proj(tm, COL_CONV_GATE, c)
    z_p = proj(tm, COL_POOL_GATE, POOL_W)

    def extend(ext, p, val):
        zeros = jnp.zeros((HALO, val.shape[1]), F32)
        ext[p, 0:HALO, :] = zeros if whole else val[tm + HALO:, :] * pm
        ext[p, HALO:HALO + ln, :] = val[p * ln:(p + 1) * ln, :]
        ext[p, HALO + ln:, :] = zeros if whole else val[tm:tm + HALO, :] * nm

    y = h_all * c_all
    convs = []
    for p in range(nseq):
        extend(yext, p, y)
        convs.append(yext[p, HALO - 1:HALO - 1 + ln, :] * convw_ref[0:1, :]
                     + yext[p, HALO:HALO + ln, :] * convw_ref[1:2, :]
                     + yext[p, HALO + 1:HALO + 1 + ln, :] * convw_ref[2:3, :] + convb_ref[...])
    conv = convs[0] if nseq == 1 else jnp.concatenate(convs, axis=0)
    conv_o = (b_c * conv * _silu(z_c)).astype(BF16)

    lane = lax.broadcasted_iota(jnp.int32, (ln, LANES), 1)
    pos = lax.broadcasted_iota(jnp.int32, (ln, LANES), 0) + (0 if whole else j * tm)
    low = lane < POOL_GROUP
    d_rows = []
    for p in range(nseq):
        extend(uext, p, u_all)

        def win(k0, k1, sl):
            tot = None
            for k in range(k0, k1):
                t = uext[p, HALO + k:HALO + k + ln, sl]
                tot = t if tot is None else tot + t
            return tot

        ds = []
        for ti, (wa, wb) in enumerate(((2, 4), (8, 16))):
            sl = slice(ti * LANES, (ti + 1) * LANES)
            s_a = win(-(wa // 2), wa // 2, sl)
            s_b = s_a + win(-(wb // 2), -(wa // 2), sl) + win(wa // 2, wb // 2, sl)
            half = jnp.where(low, wa // 2, wb // 2)
            cnt = jnp.minimum(pos + half - 1, seq - 1) - jnp.maximum(pos - half, 0) + 1
            tot = jnp.where(low, s_a, s_b)
            ds.append(tot / cnt.astype(F32) - u_all[p * ln:(p + 1) * ln, sl])
        d_rows.append(jnp.concatenate(ds, axis=-1))
    d = (d_rows[0] if nseq == 1 else jnp.concatenate(d_rows, axis=0)).astype(BF16)
    pool_o = (_dot(d, wpool_ref[...]) * pscale_ref[...] * _silu(z_p)).astype(BF16)

    out = out_a + _dot(conv_o, wout_ref[0, a:a + c, :]) + _dot(pool_o, wout_ref[0, a + c:, :])
    res = x + mod_ref[0, 0, 2:3, :] * out
    if final:
        ms = jnp.mean(res * res, axis=-1, keepdims=True)
        res = res * lax.rsqrt(ms + EPS) * fg_ref[...]
    o_ref[...] = res


def _mix(attn2, x2, mod, mod_row, norm_g, w_in_bf, wout_bf, layer, conv_w, conv_b, wpool_bd, pool_scale,
         final_g, *, seq, tm, final):
    tokens = x2.shape[0]
    tps = max(seq // tm, 1)
    nseq = max(tm // seq, 1)
    hb = tm // HALO
    nhb = tokens // HALO
    row = lambda i: (i, 0)
    const = lambda i: (0, 0)
    in_specs = [
        pl.BlockSpec((tm, ATTN_W), row),
        pl.BlockSpec((tm, D_MODEL), row),
        pl.BlockSpec((HALO, D_MODEL), lambda i: (jnp.maximum(i * hb - 1, 0), 0)),
        pl.BlockSpec((HALO, D_MODEL), lambda i: (jnp.minimum((i + 1) * hb, nhb - 1), 0)),
        _mod_spec(layer, mod_row, tps),
        pl.BlockSpec((1, D_MODEL), const),
        pl.BlockSpec((1, D_MODEL, IN_W), lambda i: (layer, 0, 0)),
        pl.BlockSpec((1, D_MODEL, D_MODEL), lambda i: (layer, 0, 0)),
        pl.BlockSpec((3, CONV_W), const),
        pl.BlockSpec((1, CONV_W), const),
        pl.BlockSpec((POOL_W, POOL_W), const),
        pl.BlockSpec((1, POOL_W), const),
        pl.BlockSpec((1, D_MODEL), const),
    ]
    return pl.pallas_call(
        functools.partial(_mix_kernel, tm=tm, tps=tps, seq=seq, final=final),
        grid=(tokens // tm,),
        in_specs=in_specs,
        out_specs=pl.BlockSpec((tm, D_MODEL), row),
        out_shape=jax.ShapeDtypeStruct((tokens, D_MODEL), F32),
        scratch_shapes=[
            pltpu.VMEM((tm + 2 * HALO, D_MODEL), BF16),
            pltpu.VMEM((nseq, tm // nseq + 2 * HALO, CONV_W), F32),
            pltpu.VMEM((nseq, tm // nseq + 2 * HALO, POOL_W), F32),
        ],
        compiler_params=pltpu.CompilerParams(
            dimension_semantics=("arbitrary",), vmem_limit_bytes=VMEM_LIMIT),
        name="mix_final" if final else "mix",
    )(attn2, x2, x2, x2, mod, norm_g.reshape(1, D_MODEL), w_in_bf, wout_bf,
      conv_w, conv_b.reshape(1, CONV_W), wpool_bd, pool_scale.reshape(1, POOL_W),
      final_g.reshape(1, D_MODEL))


def _rope_tables(n):
    rows = n // GRID_W
    row = jnp.repeat(jnp.arange(rows), GRID_W).astype(F32)
    col = jnp.tile(jnp.arange(GRID_W), rows).astype(F32)
    half = HEAD_DIM // 2
    inv = 1.0 / (ROPE_THETA ** (jnp.arange(0, half, 2, dtype=F32) / half))
    ar = row[:, None] * inv
    ac = col[:, None] * inv
    ang = jnp.tile(jnp.concatenate([ar, ar, ac, ac], axis=-1), (1, LANES // HEAD_DIM))
    cos = jnp.cos(ang)
    sin = jnp.sin(ang)
    second = (jnp.arange(LANES) % half) >= (half // 2)
    return cos, jnp.where(second, sin, -sin)


def _block_diag(blocks):
    n = len(blocks)
    rows = []
    for i, b in enumerate(blocks):
        rows.append(jnp.concatenate(
            [b if j == i else jnp.zeros((b.shape[0], blocks[j].shape[1]), b.dtype) for j in range(n)],
            axis=1))
    return jnp.concatenate(rows, axis=0)


def kernel(x_prompt, x_sample, cache_k, cache_v, c, c_ctx, norm_g, w_ada, b_ada, w_in,
           q_norm_g, k_norm_g, conv_w, conv_b, pool_w, pool_scale, w_out, final_g):
    depth = w_in.shape[0]
    bp, sp, _ = x_prompt.shape
    bs, ss, _ = x_sample.shape
    n_ctx = cache_k.shape[2]

    cvec = jnp.concatenate(
        [c, c_ctx[None, :], jnp.zeros((ADA_ROWS - bs - 1, D_MODEL), F32)], axis=0)
    mod = _ada(cvec, w_ada, b_ada)
    mod = mod.reshape(depth, ADA_ROWS, 3, D_MODEL)

    w_in_bf = w_in.astype(BF16)
    w_out_bf = w_out.astype(BF16)
    tables = _rope_tables(ss)
    ones = jnp.full((HEAD_DIM, HEAD_DIM), 1.0 / HEAD_DIM, BF16)
    bd = _block_diag([ones, ones])
    ck = cache_k.reshape(bs, depth, n_ctx, KV_W)
    cv = cache_v.reshape(bs, depth, n_ctx, KV_W)

    h = x_prompt.reshape(bp * sp, D_MODEL)
    z = x_sample.reshape(bs * ss, D_MODEL)
    kv_ctx = (jnp.zeros((bp, depth, sp, KV_W), F32), jnp.zeros((bp, depth, sp, KV_W), F32))
    gains = jnp.concatenate(
        [jnp.tile(q_norm_g * (HEAD_DIM ** -0.5), (1, N_HEADS)), jnp.tile(k_norm_g, (1, N_KV_HEADS))],
        axis=-1).reshape(depth, 1, QK_W)
    for l in range(depth):
        wpool_bd = _block_diag([pool_w[l, g] for g in range(pool_w.shape[1])]).astype(BF16)
        final = l == depth - 1
        mix = functools.partial(
            _mix, norm_g=norm_g[l], w_in_bf=w_in_bf, wout_bf=w_out_bf, layer=l,
            conv_w=conv_w[l], conv_b=conv_b[l], wpool_bd=wpool_bd, pool_scale=pool_scale[l],
            final_g=final_g, final=final)

        q, k, v = _qkv(h, mod, bs, norm_g[l], w_in_bf, l, gains, bd, None, kv_ctx, seq=sp, tm=1024)
        kv_ctx = (k, v)
        attn = _attention_ctx(q.reshape(bp, sp, ATTN_W), k, v, l)
        h = mix(attn.reshape(bp * sp, ATTN_W), h, mod, bs, seq=sp, tm=512)

        q, k, v = _qkv(z, mod, None, norm_g[l], w_in_bf, l, gains, bd, tables, None, seq=ss, tm=1024)
        attn = _attention(q.reshape(bs, ss, ATTN_W), k.reshape(bs, ss, KV_W), v.reshape(bs, ss, KV_W),
                          ck, cv, l, rb=256)
        z = mix(attn.reshape(bs * ss, ATTN_W), z, mod, None, seq=ss, tm=512)

    y_prompt = h.reshape(bp, sp, D_MODEL)
    y_sample = z.reshape(bs, ss, D_MODEL)
    new_k, new_v = (a.reshape(bp, depth, sp, N_KV_HEADS, HEAD_DIM) for a in kv_ctx)
    return (y_prompt, y_sample, new_k, new_v)
```

```python
import functools

import jax
import jax.numpy as jnp
from jax import lax
from jax.experimental import pallas as pl
from jax.experimental.pallas import tpu as pltpu

D_MODEL = 1024
GRID_W = 64
N_HEADS = 8
N_KV_HEADS = 2
HEAD_DIM = 64
ATTN_W = N_HEADS * HEAD_DIM
KV_W = N_KV_HEADS * HEAD_DIM
QK_W = ATTN_W + KV_W
QKV_W = QK_W + KV_W
CONV_W = 256
POOL_W = 256
POOL_GROUP = 64
IN_W = 2816
COL_ATTN_GATE = QKV_W
COL_CONV_H = COL_ATTN_GATE + ATTN_W
COL_CONV_B = COL_CONV_H + CONV_W
COL_CONV_C = COL_CONV_B + CONV_W
COL_CONV_GATE = COL_CONV_C + CONV_W
COL_POOL_U = COL_CONV_GATE + CONV_W
COL_POOL_GATE = COL_POOL_U + POOL_W
ROPE_THETA = 10000.0
EPS = 1e-6
LANES = 128
HALO = 8
ADA_ROWS = 8
VMEM_LIMIT = 48 * 1024 * 1024
ATTN_VMEM_LIMIT = 56 * 1024 * 1024

ADA_COLS = 1024
QKV_TILE = 1024
QKV_SUB = 256
MIX_TILE = 512
ATTN_ROWS = 256
ATTN_REGION = 4
ATTN_BUFS = 2 * ATTN_REGION
ATTN_KEY_CHUNK = 256
CTX_BATCH = 4

F32 = jnp.float32
BF16 = jnp.bfloat16


def _silu(z):
    return z * (1.0 / (1.0 + jnp.exp(-z)))


def _dot(a, b):
    return jnp.dot(a, b, preferred_element_type=F32)


def _mod_norm(x, g, mod_ref):
    ms = jnp.mean(x * x, axis=-1, keepdims=True)
    gs = g * (1.0 + mod_ref[0, 0, 1:2, :])
    return x * lax.rsqrt(ms + EPS) * gs + mod_ref[0, 0, 0:1, :]


def _ada_kernel(c_ref, w_ref, b_ref, o_ref):
    s = _silu(c_ref[...]).astype(BF16)
    o_ref[0] = _dot(s, w_ref[0].astype(BF16)) + b_ref[0]


def _ada(cvec, w_ada, b_ada):
    depth, _, n3 = w_ada.shape
    tn = ADA_COLS
    return pl.pallas_call(
        _ada_kernel,
        grid=(depth, n3 // tn),
        in_specs=[
            pl.BlockSpec((ADA_ROWS, D_MODEL), lambda l, j: (0, 0)),
            pl.BlockSpec((1, D_MODEL, tn), lambda l, j: (l, 0, j)),
            pl.BlockSpec((1, 1, tn), lambda l, j: (l, 0, j)),
        ],
        out_specs=pl.BlockSpec((1, ADA_ROWS, tn), lambda l, j: (l, 0, j)),
        out_shape=jax.ShapeDtypeStruct((depth, ADA_ROWS, n3), F32),
        name="ada_mod",
    )(cvec, w_ada, b_ada.reshape(depth, 1, n3))


def _qkv_kernel(*refs, rope, nsub, first_layer):
    if rope:
        (x_ref, mod_ref, g_ref, w_ref, gains_ref, bd_ref, cos_ref, sin_ref,
         q_ref, k_ref, v_ref) = refs
    else:
        x_ref, mod_ref, g_ref, w_ref, gains_ref, bd_ref = refs[:6]
        q_ref, k_ref, v_ref = refs[-3:]
    tm = x_ref.shape[0]
    sub = tm // nsub
    bd = bd_ref[...]
    g = g_ref[...]
    if rope:
        lane = lax.broadcasted_iota(jnp.int32, (sub, LANES), 1)
        second = (lane % (HEAD_DIM // 2)) >= (HEAD_DIM // 4)
    k_parts, v_parts = [None] * nsub, [None] * nsub

    def finish(s, c, part):
        rows = slice(s * sub, (s + 1) * sub)
        for h in range(2):
            j = 2 * c + h
            t = part[:, h * LANES:(h + 1) * LANES]
            if j * LANES >= QK_W:
                v_parts[s] = t
                continue
            sl = slice(j * LANES, (j + 1) * LANES)
            sq = t * t
            hi = sq.astype(BF16)
            lo = (sq - hi.astype(F32)).astype(BF16)
            msq = _dot(hi, bd) + _dot(lo, bd)
            t = t * lax.rsqrt(msq + EPS) * gains_ref[0, :, sl]
            if rope:
                partner = jnp.where(second, pltpu.roll(t, HEAD_DIM // 4, 1),
                                    pltpu.roll(t, LANES - HEAD_DIM // 4, 1))
                t = t * cos_ref[rows, :] + partner * sin_ref[rows, :]
            if j * LANES < ATTN_W:
                q_ref[rows, sl] = t.astype(BF16)
            else:
                k_parts[s] = t

    ncol = QKV_W // (2 * LANES)
    prev = None
    for s in range(nsub + 1):
        if s < nsub:
            xn = _mod_norm(x_ref[s * sub:(s + 1) * sub, :], g, mod_ref).astype(BF16)
        cur = []
        for c in range(ncol):
            if s < nsub:
                cur.append(_dot(xn, w_ref[0, :, c * 2 * LANES:(c + 1) * 2 * LANES]))
            if prev is not None:
                finish(s - 1, c, prev[c])
        prev = cur
    for ref, parts in ((k_ref, k_parts), (v_ref, v_parts)):
        val = jnp.concatenate(parts, axis=0)
        if first_layer is None:
            ref[...] = val.reshape(ref.shape)
        else:
            for l2 in range(ref.shape[1]):
                piece = val.reshape(ref.shape[0], *ref.shape[2:])
                ref[:, l2] = piece if l2 == first_layer else jnp.zeros_like(piece)


def _mod_spec(layer, mod_row, tps):
    if mod_row is None:
        return pl.BlockSpec((1, 1, 3, D_MODEL), lambda i: (layer, i // tps, 0, 0))
    return pl.BlockSpec((1, 1, 3, D_MODEL), lambda i: (layer, mod_row, 0, 0))


def _qkv(x2, mod, mod_row, norm_g, w_in_bf, layer, gains, bd, tables, kv_prev, *, seq, tm):
    tokens = x2.shape[0]
    depth = w_in_bf.shape[0]
    tps = max(seq // tm, 1)
    rope = tables is not None
    row = lambda i: (i, 0)
    const = lambda i: (0, 0)
    in_specs = [
        pl.BlockSpec((tm, D_MODEL), row),
        _mod_spec(layer, mod_row, tps),
        pl.BlockSpec((1, D_MODEL), const),
        pl.BlockSpec((1, D_MODEL, QKV_W), lambda i: (layer, 0, 0)),
        pl.BlockSpec((1, 1, QK_W), lambda i: (layer, 0, 0)),
        pl.BlockSpec((LANES, LANES), const),
    ]
    args = [x2, mod, norm_g.reshape(1, D_MODEL), w_in_bf, gains, bd]
    out_shapes = [jax.ShapeDtypeStruct((tokens, ATTN_W), BF16)]
    out_specs = [pl.BlockSpec((tm, ATTN_W), row)]
    aliases = {}
    first_layer = None
    if rope:
        in_specs += [pl.BlockSpec((tm, LANES), lambda i: (i % tps, 0))] * 2
        args += list(tables)
        out_shapes += [jax.ShapeDtypeStruct((tokens, KV_W), F32)] * 2
        out_specs += [pl.BlockSpec((tm, KV_W), row)] * 2
    else:
        nseq = tm // seq
        out_shapes += [jax.ShapeDtypeStruct((tokens // seq, depth, seq, KV_W), F32)] * 2
        if kv_prev is None:
            first_layer = layer
            out_specs += [pl.BlockSpec((nseq, depth, seq, KV_W), lambda i: (i, 0, 0, 0))] * 2
        else:
            out_specs += [pl.BlockSpec((nseq, 1, seq, KV_W), lambda i: (i, layer, 0, 0))] * 2
            in_specs += [pl.BlockSpec(memory_space=pl.ANY)] * 2
            aliases = {len(args): 1, len(args) + 1: 2}
            args += list(kv_prev)
    return pl.pallas_call(
        functools.partial(_qkv_kernel, rope=rope, nsub=tm // QKV_SUB, first_layer=first_layer),
        grid=(tokens // tm,),
        in_specs=in_specs,
        out_specs=out_specs,
        out_shape=out_shapes,
        input_output_aliases=aliases,
        compiler_params=pltpu.CompilerParams(
            dimension_semantics=("arbitrary",), vmem_limit_bytes=VMEM_LIMIT),
        name="qkv_rope" if rope else "qkv",
    )(*args)


VT_ROWS = HEAD_DIM + 16


def _attn_kernel(q_ref, k_ref, v_ref, ck_ref, cv_ref, o_ref, kab, vt, q2, acct, mbuf, *bufs, n, n_ctx, rb):
    kv = pl.program_id(1)
    nr = 2 * n // rb
    t_all = n + n_ctx
    sbuf, pbuf = bufs[:ATTN_BUFS], bufs[ATTN_BUFS:]

    def fill(k_src, v_src, r0, m):
        lane = lax.broadcasted_iota(jnp.int32, (m, LANES), 1)
        k_own = jnp.where((lane // HEAD_DIM) == kv, k_src, 0.0)
        k_oth = pltpu.roll(k_own, HEAD_DIM, 1)
        is0 = kv == 0
        kab[0, r0:r0 + m, :] = jnp.where(is0, k_own, k_oth).astype(BF16)
        kab[1, r0:r0 + m, :] = jnp.where(is0, k_oth, k_own).astype(BF16)
        v_t = v_src.T
        vt[0:HEAD_DIM, r0:r0 + m] = jnp.where(is0, v_t[0:HEAD_DIM, :], v_t[HEAD_DIM:, :]).astype(BF16)

    piece = 256
    for r0 in range(0, n, piece):
        fill(k_ref[0, r0:r0 + piece, :], v_ref[0, r0:r0 + piece, :], r0, piece)
    fill(ck_ref[0, 0], cv_ref[0, 0], n, n_ctx)
    vt[HEAD_DIM:, :] = jnp.ones((VT_ROWS - HEAD_DIM, t_all), BF16)
    for c0 in range(0, n, piece):
        q2[:, c0:c0 + piece] = q_ref[0, c0:c0 + piece, 0:LANES].astype(F32).T.astype(BF16)
        q2[:, n + c0:n + c0 + piece] = q_ref[0, c0:c0 + piece, LANES:].astype(F32).T.astype(BF16)

    span = ATTN_REGION
    n_regions = 2 * nr // span
    chunks = [slice(k0, k0 + ATTN_KEY_CHUNK) for k0 in range(0, t_all, ATTN_KEY_CHUNK)]

    def score_chunk(slot, ks, qr, buf, run):
        sc = _dot(kab[slot, ks, :], qr)
        sbuf[buf][ks, :] = sc
        while sc.shape[0] > 8:
            half = sc.shape[0] // 2
            sc = jnp.maximum(sc[0:half, :], sc[half:, :])
        return sc if run is None else jnp.maximum(run, sc)

    def q_rows(r):
        return q2[:, pl.ds(pl.multiple_of(r * rb, rb), rb)]

    per_tile = n // rb

    def flush(r):
        r0 = (r % per_tile) * rb
        if not isinstance(r, int):
            r0 = pl.multiple_of(r0, rb)
        o_ref[0, r // per_tile, pl.ds(r0, rb), :] = acct[r].T

    def region(g, parity, do_scores, do_numer, do_weigh, do_flush):
        cur, oth = parity * span, (1 - parity) * span
        for j in range(span):
            slot = j % 2
            if do_flush and slot == 0:
                flush((g - 2) * (span // 2) + j // 2)
            if do_numer:
                m = mbuf[cur + j]
            if do_scores:
                qr = q_rows((g + 1) * (span // 2) + j // 2)
            run, res = None, None
            for ks in chunks:
                if do_scores:
                    run = score_chunk(slot, ks, qr, oth + j, run)
                if do_numer:
                    pbuf[cur + j][ks, :] = jnp.exp(sbuf[cur + j][ks, :] - m).astype(BF16)
                if do_weigh:
                    part = _dot(vt[:, ks], pbuf[oth + j][ks, :])
                    res = part if res is None else res + part
            if do_scores:
                mbuf[oth + j] = jnp.max(run, axis=0, keepdims=True)
            if do_weigh:
                acct[(g - 1) * (span // 2) + j // 2, slot * HEAD_DIM:(slot + 1) * HEAD_DIM, :] = (
                    res[0:HEAD_DIM, :] * (1.0 / res[HEAD_DIM:HEAD_DIM + 1, :]))

    region(-1, 1, True, False, False, False)
    region(0, 0, True, True, False, False)

    def body(g, carry):
        for parity in range(2):
            pl.when(g % 2 == parity)(lambda: region(g, parity, True, True, True, False))
        return carry

    lax.fori_loop(1, n_regions - 1, body, 0)
    region(n_regions - 1, (n_regions - 1) % 2, False, True, True, False)
    region(n_regions, n_regions % 2, False, False, True, False)
    for r in range(nr):
        flush(r)


def _attention(q, k, v, cache_k, cache_v, layer):
    rb = ATTN_ROWS
    bsz, n, _ = q.shape
    n_ctx = cache_k.shape[2]
    t_all = n + n_ctx
    gw = 2 * LANES
    qmap = lambda b, g: (b, 0, g)
    kmap = lambda b, g: (b, 0, 0)
    cmap = lambda b, g: (b, layer, 0, 0)
    return pl.pallas_call(
        functools.partial(_attn_kernel, n=n, n_ctx=n_ctx, rb=rb),
        grid=(bsz, N_KV_HEADS),
        in_specs=[
            pl.BlockSpec((1, n, gw), qmap),
            pl.BlockSpec((1, n, KV_W), kmap),
            pl.BlockSpec((1, n, KV_W), kmap),
            pl.BlockSpec((1, 1, n_ctx, KV_W), cmap),
            pl.BlockSpec((1, 1, n_ctx, KV_W), cmap),
        ],
        out_specs=pl.BlockSpec((1, 2, n, LANES), lambda b, g: (b, g, 0, 0)),
        out_shape=jax.ShapeDtypeStruct((bsz, ATTN_W // LANES, n, LANES), F32),
        scratch_shapes=[
            pltpu.VMEM((2, t_all, LANES), BF16),
            pltpu.VMEM((VT_ROWS, t_all), BF16),
            pltpu.VMEM((LANES, 2 * n), BF16),
            pltpu.VMEM((2 * n // rb, LANES, rb), F32),
            pltpu.VMEM((ATTN_BUFS, 1, rb), F32),
        ] + [pltpu.VMEM((t_all, rb), F32)] * ATTN_BUFS + [pltpu.VMEM((t_all, rb), BF16)] * ATTN_BUFS,
        compiler_params=pltpu.CompilerParams(
            dimension_semantics=("arbitrary", "arbitrary"), vmem_limit_bytes=ATTN_VMEM_LIMIT),
        name="attn_lat",
    )(q, k, v, cache_k, cache_v)


def _attn_ctx_kernel(q_ref, k_ref, v_ref, o_ref, *, n):
    low = lax.broadcasted_iota(jnp.int32, (n, LANES), 1) < HEAD_DIM
    ones_lo = jnp.where(low, 1.0, 0.0)
    zk = jnp.zeros((HEAD_DIM, n), F32)
    gw = 2 * LANES
    for b in range(CTX_BATCH):
        kt_all = k_ref[b, 0].T
        v = v_ref[b, 0]
        vr = pltpu.roll(v, HEAD_DIM, 1)
        for g in range(N_KV_HEADS):
            kg = kt_all[g * HEAD_DIM:(g + 1) * HEAD_DIM, :]
            kts = (jnp.concatenate([kg, zk], axis=0).astype(BF16),
                   jnp.concatenate([zk, kg], axis=0).astype(BF16))
            v_lo, v_hi = (v, vr) if g == 0 else (vr, v)
            vs = (jnp.concatenate([jnp.where(low, v_lo, 0.0), ones_lo], axis=1).astype(BF16),
                  jnp.concatenate([jnp.where(low, 0.0, v_hi), 1.0 - ones_lo], axis=1).astype(BF16))
            q2 = jnp.concatenate([q_ref[b, :, g * gw:g * gw + LANES],
                                  q_ref[b, :, g * gw + LANES:(g + 1) * gw]], axis=0)
            acc = None
            for kt_s, v_s in zip(kts, vs):
                s = _dot(q2, kt_s)
                m = jnp.max(s, axis=-1, keepdims=True)
                pv = _dot(jnp.exp(s - m).astype(BF16), v_s)
                acc = pv if acc is None else acc + pv
            o = acc[:, 0:LANES] / acc[:, LANES:]
            o_ref[b, 2 * g] = o[0:n]
            o_ref[b, 2 * g + 1] = o[n:]


def _attention_ctx(q, k, v, layer):
    bsz, n, _ = q.shape
    blk = lambda w: pl.BlockSpec((CTX_BATCH, n, w), lambda b: (b, 0, 0))
    kv_blk = pl.BlockSpec((CTX_BATCH, 1, n, KV_W), lambda b: (b, layer, 0, 0))
    return pl.pallas_call(
        functools.partial(_attn_ctx_kernel, n=n),
        grid=(bsz // CTX_BATCH,),
        in_specs=[blk(ATTN_W), kv_blk, kv_blk],
        out_specs=pl.BlockSpec((CTX_BATCH, ATTN_W // LANES, n, LANES), lambda b: (b, 0, 0, 0)),
        out_shape=jax.ShapeDtypeStruct((bsz, ATTN_W // LANES, n, LANES), F32),
        compiler_params=pltpu.CompilerParams(
            dimension_semantics=("arbitrary",), vmem_limit_bytes=VMEM_LIMIT),
        name="attn_ctx",
    )(q, k, v)


def _mix_kernel(attn_ref, x_ref, xp_ref, xn_ref, mod_ref, g_ref, w_ref, wout_ref,
                convw_ref, convb_ref, wpool_ref, pscale_ref, fg_ref,
                o_ref, xe, yext, uext, *, tm, tps, seq, final):
    j = pl.program_id(0) % tps
    pm = jnp.where(j == 0, 0.0, 1.0)
    nm = jnp.where(j == tps - 1, 0.0, 1.0)
    c = CONV_W
    g = g_ref[...]
    x = x_ref[...]
    xe[0:tm, :] = _mod_norm(x, g, mod_ref).astype(BF16)
    if tps > 1:
        halo = jnp.concatenate([xn_ref[...], xp_ref[...]], axis=0)
        xe[tm:, :] = _mod_norm(halo, g, mod_ref).astype(BF16)

    def proj(rows, off, width):
        return _dot(xe[0:rows, :], w_ref[0, :, off:off + width])

    nseq = max(tm // seq, 1)
    whole = tps == 1
    ln = tm // nseq
    hrows = tm if whole else tm + 2 * HALO
    h_all = proj(hrows, COL_CONV_H, c)
    c_all = proj(hrows, COL_CONV_C, c)
    u_all = proj(hrows, COL_POOL_U, POOL_W)
    z_a = proj(tm, COL_ATTN_GATE, ATTN_W)
    a = ATTN_W
    attn = jnp.concatenate(
        [jnp.concatenate([attn_ref[p, t] for t in range(attn_ref.shape[1])], axis=1)
         for p in range(attn_ref.shape[0])], axis=0)
    attn_o = (attn * _silu(z_a)).astype(BF16)
    out_a = _dot(attn_o, wout_ref[0, 0:a, :])
    b_c = proj(tm, COL_CONV_B, c)
    z_c = proj(tm, COL_CONV_GATE, c)
    z_p = proj(tm, COL_POOL_GATE, POOL_W)

    def extend(ext, p, val):
        zeros = jnp.zeros((HALO, val.shape[1]), F32)
        ext[p, 0:HALO, :] = zeros if whole else val[tm + HALO:, :] * pm
        ext[p, HALO:HALO + ln, :] = val[p * ln:(p + 1) * ln, :]
        ext[p, HALO + ln:, :] = zeros if whole else val[tm:tm + HALO, :] * nm

    y = h_all * c_all
    convs = []
    for p in range(nseq):
        extend(yext, p, y)
        convs.append(yext[p, HALO - 1:HALO - 1 + ln, :] * convw_ref[0:1, :]
                     + yext[p, HALO:HALO + ln, :] * convw_ref[1:2, :]
                     + yext[p, HALO + 1:HALO + 1 + ln, :] * convw_ref[2:3, :] + convb_ref[...])
    conv = convs[0] if nseq == 1 else jnp.concatenate(convs, axis=0)
    conv_o = (b_c * conv * _silu(z_c)).astype(BF16)

    lane = lax.broadcasted_iota(jnp.int32, (ln, LANES), 1)
    pos = lax.broadcasted_iota(jnp.int32, (ln, LANES), 0) + (0 if whole else j * tm)
    low = lane < POOL_GROUP
    d_rows = []
    for p in range(nseq):
        extend(uext, p, u_all)

        def win(k0, k1, sl):
            tot = None
            for k in range(k0, k1):
                t = uext[p, HALO + k:HALO + k + ln, sl]
                tot = t if tot is None else tot + t
            return tot

        ds = []
        for ti, (wa, wb) in enumerate(((2, 4), (8, 16))):
            sl = slice(ti * LANES, (ti + 1) * LANES)
            s_a = win(-(wa // 2), wa // 2, sl)
            s_b = s_a + win(-(wb // 2), -(wa // 2), sl) + win(wa // 2, wb // 2, sl)
            half = jnp.where(low, wa // 2, wb // 2)
            cnt = jnp.minimum(pos + half - 1, seq - 1) - jnp.maximum(pos - half, 0) + 1
            tot = jnp.where(low, s_a, s_b)
            ds.append(tot / cnt.astype(F32) - u_all[p * ln:(p + 1) * ln, sl])
        d_rows.append(jnp.concatenate(ds, axis=-1))
    d = (d_rows[0] if nseq == 1 else jnp.concatenate(d_rows, axis=0)).astype(BF16)
    pool_o = (_dot(d, wpool_ref[...]) * pscale_ref[...] * _silu(z_p)).astype(BF16)

    out = out_a + _dot(conv_o, wout_ref[0, a:a + c, :]) + _dot(pool_o, wout_ref[0, a + c:, :])
    res = x + mod_ref[0, 0, 2:3, :] * out
    if final:
        ms = jnp.mean(res * res, axis=-1, keepdims=True)
        res = res * lax.rsqrt(ms + EPS) * fg_ref[...]
    o_ref[...] = res


def _mix(attn2, x2, mod, mod_row, norm_g, w_in_bf, wout_bf, layer, conv_w, conv_b, wpool_bd, pool_scale,
         final_g, *, seq, tm, final):
    tokens = x2.shape[0]
    tps = max(seq // tm, 1)
    nseq = max(tm // seq, 1)
    hb = tm // HALO
    nhb = tokens // HALO
    row = lambda i: (i, 0)
    const = lambda i: (0, 0)
    in_specs = [
        (pl.BlockSpec((nseq, ATTN_W // LANES, tm // nseq, LANES), lambda i: (i, 0, 0, 0)) if tps == 1 else
         pl.BlockSpec((1, ATTN_W // LANES, tm, LANES), lambda i: (i // tps, 0, i % tps, 0))),
        pl.BlockSpec((tm, D_MODEL), row),
        pl.BlockSpec((HALO, D_MODEL), lambda i: (jnp.maximum(i * hb - 1, 0), 0)),
        pl.BlockSpec((HALO, D_MODEL), lambda i: (jnp.minimum((i + 1) * hb, nhb - 1), 0)),
        _mod_spec(layer, mod_row, tps),
        pl.BlockSpec((1, D_MODEL), const),
        pl.BlockSpec((1, D_MODEL, IN_W), lambda i: (layer, 0, 0)),
        pl.BlockSpec((1, D_MODEL, D_MODEL), lambda i: (layer, 0, 0)),
        pl.BlockSpec((3, CONV_W), const),
        pl.BlockSpec((1, CONV_W), const),
        pl.BlockSpec((POOL_W, POOL_W), const),
        pl.BlockSpec((1, POOL_W), const),
        pl.BlockSpec((1, D_MODEL), const),
    ]
    return pl.pallas_call(
        functools.partial(_mix_kernel, tm=tm, tps=tps, seq=seq, final=final),
        grid=(tokens // tm,),
        in_specs=in_specs,
        out_specs=pl.BlockSpec((tm, D_MODEL), row),
        out_shape=jax.ShapeDtypeStruct((tokens, D_MODEL), F32),
        scratch_shapes=[
            pltpu.VMEM((tm + 2 * HALO, D_MODEL), BF16),
            pltpu.VMEM((nseq, tm // nseq + 2 * HALO, CONV_W), F32),
            pltpu.VMEM((nseq, tm // nseq + 2 * HALO, POOL_W), F32),
        ],
        compiler_params=pltpu.CompilerParams(
            dimension_semantics=("arbitrary",), vmem_limit_bytes=VMEM_LIMIT),
        name="mix_final" if final else "mix",
    )(attn2, x2, x2, x2, mod, norm_g.reshape(1, D_MODEL), w_in_bf, wout_bf,
      conv_w, conv_b.reshape(1, CONV_W), wpool_bd, pool_scale.reshape(1, POOL_W),
      final_g.reshape(1, D_MODEL))


def _rope_tables(n):
    rows = n // GRID_W
    row = jnp.repeat(jnp.arange(rows), GRID_W).astype(F32)
    col = jnp.tile(jnp.arange(GRID_W), rows).astype(F32)
    half = HEAD_DIM // 2
    inv = 1.0 / (ROPE_THETA ** (jnp.arange(0, half, 2, dtype=F32) / half))
    ar = row[:, None] * inv
    ac = col[:, None] * inv
    ang = jnp.tile(jnp.concatenate([ar, ar, ac, ac], axis=-1), (1, LANES // HEAD_DIM))
    cos = jnp.cos(ang)
    sin = jnp.sin(ang)
    second = (jnp.arange(LANES) % half) >= (half // 2)
    return cos, jnp.where(second, sin, -sin)


def _block_diag(blocks):
    n = len(blocks)
    rows = []
    for i, b in enumerate(blocks):
        rows.append(jnp.concatenate(
            [b if j == i else jnp.zeros((b.shape[0], blocks[j].shape[1]), b.dtype) for j in range(n)],
            axis=1))
    return jnp.concatenate(rows, axis=0)


def kernel(x_prompt, x_sample, cache_k, cache_v, c, c_ctx, norm_g, w_ada, b_ada, w_in,
           q_norm_g, k_norm_g, conv_w, conv_b, pool_w, pool_scale, w_out, final_g):
    depth = w_in.shape[0]
    bp, sp, _ = x_prompt.shape
    bs, ss, _ = x_sample.shape
    n_ctx = cache_k.shape[2]

    cvec = jnp.concatenate(
        [c, c_ctx[None, :], jnp.zeros((ADA_ROWS - bs - 1, D_MODEL), F32)], axis=0)
    mod = _ada(cvec, w_ada, b_ada)
    mod = mod.reshape(depth, ADA_ROWS, 3, D_MODEL)

    w_in_bf = w_in.astype(BF16)
    w_out_bf = w_out.astype(BF16)
    tables = _rope_tables(ss)
    ones = jnp.full((HEAD_DIM, HEAD_DIM), 1.0 / HEAD_DIM, BF16)
    bd = _block_diag([ones, ones])
    ck = cache_k.reshape(bs, depth, n_ctx, KV_W)
    cv = cache_v.reshape(bs, depth, n_ctx, KV_W)

    h = x_prompt.reshape(bp * sp, D_MODEL)
    z = x_sample.reshape(bs * ss, D_MODEL)
    kv_ctx = None
    gains = jnp.concatenate(
        [jnp.tile(q_norm_g * (HEAD_DIM ** -0.5), (1, N_HEADS)), jnp.tile(k_norm_g, (1, N_KV_HEADS))],
        axis=-1).reshape(depth, 1, QK_W)
    for l in range(depth):
        wpool_bd = _block_diag([pool_w[l, g] for g in range(pool_w.shape[1])]).astype(BF16)
        final = l == depth - 1
        mix = functools.partial(
            _mix, norm_g=norm_g[l], w_in_bf=w_in_bf, wout_bf=w_out_bf, layer=l,
            conv_w=conv_w[l], conv_b=conv_b[l], wpool_bd=wpool_bd, pool_scale=pool_scale[l],
            final_g=final_g, final=final)

        q, k, v = _qkv(h, mod, bs, norm_g[l], w_in_bf, l, gains, bd, None, kv_ctx, seq=sp, tm=QKV_TILE)
        kv_ctx = (k, v)
        attn = _attention_ctx(q.reshape(bp, sp, ATTN_W), k, v, l)
        h = mix(attn, h, mod, bs, seq=sp, tm=MIX_TILE)

        q, k, v = _qkv(z, mod, None, norm_g[l], w_in_bf, l, gains, bd, tables, None, seq=ss, tm=QKV_TILE)
        attn = _attention(q.reshape(bs, ss, ATTN_W), k.reshape(bs, ss, KV_W), v.reshape(bs, ss, KV_W),
                          ck, cv, l)
        z = mix(attn, z, mod, None, seq=ss, tm=MIX_TILE)

    y_prompt = h.reshape(bp, sp, D_MODEL)
    y_sample = z.reshape(bs, ss, D_MODEL)
    new_k, new_v = (a.reshape(bp, depth, sp, N_KV_HEADS, HEAD_DIM) for a in kv_ctx)
    return (y_prompt, y_sample, new_k, new_v)
```

```python
import functools

import jax
import jax.numpy as jnp
from jax import lax
from jax.experimental import pallas as pl
from jax.experimental.pallas import tpu as pltpu

D_MODEL = 1024
GRID_W = 64
N_HEADS = 8
N_KV_HEADS = 2
HEAD_DIM = 64
ATTN_W = N_HEADS * HEAD_DIM
KV_W = N_KV_HEADS * HEAD_DIM
QK_W = ATTN_W + KV_W
QKV_W = QK_W + KV_W
CONV_W = 256
POOL_W = 256
POOL_GROUP = 64
IN_W = 2816
COL_ATTN_GATE = QKV_W
COL_CONV_H = COL_ATTN_GATE + ATTN_W
COL_CONV_B = COL_CONV_H + CONV_W
COL_CONV_C = COL_CONV_B + CONV_W
COL_CONV_GATE = COL_CONV_C + CONV_W
COL_POOL_U = COL_CONV_GATE + CONV_W
COL_POOL_GATE = COL_POOL_U + POOL_W
ROPE_THETA = 10000.0
EPS = 1e-6
LANES = 128
HALO = 8
ADA_ROWS = 8
VMEM_LIMIT = 48 * 1024 * 1024
ATTN_VMEM_LIMIT = 56 * 1024 * 1024

ADA_COLS = 1024
QKV_TILE = 1024
QKV_SUB = 256
MIX_TILE = 512
ATTN_ROWS = 256
ATTN_REGION = 4
ATTN_BUFS = 2 * ATTN_REGION
ATTN_KEY_CHUNK = 256
CTX_BATCH = 4

F32 = jnp.float32
BF16 = jnp.bfloat16


def _silu(z):
    return z * (1.0 / (1.0 + jnp.exp(-z)))


def _dot(a, b):
    return jnp.dot(a, b, preferred_element_type=F32)


def _mod_norm(x, g, mod_ref):
    ms = jnp.mean(x * x, axis=-1, keepdims=True)
    gs = g * (1.0 + mod_ref[0, 0, 1:2, :])
    return x * lax.rsqrt(ms + EPS) * gs + mod_ref[0, 0, 0:1, :]


def _ada_kernel(c_ref, w_ref, b_ref, o_ref):
    s = _silu(c_ref[...]).astype(BF16)
    o_ref[0] = _dot(s, w_ref[0].astype(BF16)) + b_ref[0]


def _ada(cvec, w_ada, b_ada):
    depth, _, n3 = w_ada.shape
    tn = ADA_COLS
    return pl.pallas_call(
        _ada_kernel,
        grid=(depth, n3 // tn),
        in_specs=[
            pl.BlockSpec((ADA_ROWS, D_MODEL), lambda l, j: (0, 0)),
            pl.BlockSpec((1, D_MODEL, tn), lambda l, j: (l, 0, j)),
            pl.BlockSpec((1, 1, tn), lambda l, j: (l, 0, j)),
        ],
        out_specs=pl.BlockSpec((1, ADA_ROWS, tn), lambda l, j: (l, 0, j)),
        out_shape=jax.ShapeDtypeStruct((depth, ADA_ROWS, n3), F32),
        name="ada_mod",
    )(cvec, w_ada, b_ada.reshape(depth, 1, n3))


def _qkv_kernel(*refs, rope, nsub, first_layer):
    if rope:
        (x_ref, mod_ref, g_ref, w_ref, gains_ref, bd_ref, cos_ref, sin_ref,
         q_ref, k_ref, v_ref) = refs
    else:
        x_ref, mod_ref, g_ref, w_ref, gains_ref, bd_ref = refs[:6]
        q_ref, k_ref, v_ref = refs[-3:]
    tm = x_ref.shape[0]
    sub = tm // nsub
    bd = bd_ref[...]
    g = g_ref[...]
    if rope:
        lane = lax.broadcasted_iota(jnp.int32, (sub, LANES), 1)
        second = (lane % (HEAD_DIM // 2)) >= (HEAD_DIM // 4)
    k_parts, v_parts = [None] * nsub, [None] * nsub

    def finish(s, c, part):
        rows = slice(s * sub, (s + 1) * sub)
        for h in range(2):
            j = 2 * c + h
            t = part[:, h * LANES:(h + 1) * LANES]
            if j * LANES >= QK_W:
                v_parts[s] = t
                continue
            sl = slice(j * LANES, (j + 1) * LANES)
            sq = t * t
            hi = sq.astype(BF16)
            lo = (sq - hi.astype(F32)).astype(BF16)
            msq = _dot(hi, bd) + _dot(lo, bd)
            t = t * lax.rsqrt(msq + EPS) * gains_ref[0, :, sl]
            if rope:
                partner = jnp.where(second, pltpu.roll(t, HEAD_DIM // 4, 1),
                                    pltpu.roll(t, LANES - HEAD_DIM // 4, 1))
                t = t * cos_ref[rows, :] + partner * sin_ref[rows, :]
            if j * LANES < ATTN_W:
                q_ref[rows, sl] = t.astype(BF16)
            else:
                k_parts[s] = t

    ncol = QKV_W // (2 * LANES)
    prev = None
    for s in range(nsub + 1):
        if s < nsub:
            xn = _mod_norm(x_ref[s * sub:(s + 1) * sub, :], g, mod_ref).astype(BF16)
        cur = []
        for c in range(ncol):
            if s < nsub:
                cur.append(_dot(xn, w_ref[0, :, c * 2 * LANES:(c + 1) * 2 * LANES]))
            if prev is not None:
                finish(s - 1, c, prev[c])
        prev = cur
    for ref, parts in ((k_ref, k_parts), (v_ref, v_parts)):
        val = jnp.concatenate(parts, axis=0)
        if rope:
            ref[...] = val
            continue
        seq = ref.shape[3]
        for p in range(ref.shape[0]):
            piece = val[p * seq:(p + 1) * seq, :].T
            if first_layer is None:
                ref[p, 0] = piece
            else:
                for l2 in range(ref.shape[1]):
                    ref[p, l2] = piece if l2 == first_layer else jnp.zeros_like(piece)


def _mod_spec(layer, mod_row, tps):
    if mod_row is None:
        return pl.BlockSpec((1, 1, 3, D_MODEL), lambda i: (layer, i // tps, 0, 0))
    return pl.BlockSpec((1, 1, 3, D_MODEL), lambda i: (layer, mod_row, 0, 0))


def _qkv(x2, mod, mod_row, norm_g, w_in_bf, layer, gains, bd, tables, kv_prev, *, seq, tm):
    tokens = x2.shape[0]
    depth = w_in_bf.shape[0]
    tps = max(seq // tm, 1)
    rope = tables is not None
    row = lambda i: (i, 0)
    const = lambda i: (0, 0)
    in_specs = [
        pl.BlockSpec((tm, D_MODEL), row),
        _mod_spec(layer, mod_row, tps),
        pl.BlockSpec((1, D_MODEL), const),
        pl.BlockSpec((1, D_MODEL, QKV_W), lambda i: (layer, 0, 0)),
        pl.BlockSpec((1, 1, QK_W), lambda i: (layer, 0, 0)),
        pl.BlockSpec((LANES, LANES), const),
    ]
    args = [x2, mod, norm_g.reshape(1, D_MODEL), w_in_bf, gains, bd]
    out_shapes = [jax.ShapeDtypeStruct((tokens, ATTN_W), BF16)]
    out_specs = [pl.BlockSpec((tm, ATTN_W), row)]
    aliases = {}
    first_layer = None
    if rope:
        in_specs += [pl.BlockSpec((tm, LANES), lambda i: (i % tps, 0))] * 2
        args += list(tables)
        out_shapes += [jax.ShapeDtypeStruct((tokens, KV_W), F32)] * 2
        out_specs += [pl.BlockSpec((tm, KV_W), row)] * 2
    else:
        nseq = tm // seq
        out_shapes += [jax.ShapeDtypeStruct((tokens // seq, depth, KV_W, seq), F32)] * 2
        if kv_prev is None:
            first_layer = layer
            out_specs += [pl.BlockSpec((nseq, depth, KV_W, seq), lambda i: (i, 0, 0, 0))] * 2
        else:
            out_specs += [pl.BlockSpec((nseq, 1, KV_W, seq), lambda i: (i, layer, 0, 0))] * 2
            in_specs += [pl.BlockSpec(memory_space=pl.ANY)] * 2
            aliases = {len(args): 1, len(args) + 1: 2}
            args += list(kv_prev)
    return pl.pallas_call(
        functools.partial(_qkv_kernel, rope=rope, nsub=tm // QKV_SUB, first_layer=first_layer),
        grid=(tokens // tm,),
        in_specs=in_specs,
        out_specs=out_specs,
        out_shape=out_shapes,
        input_output_aliases=aliases,
        compiler_params=pltpu.CompilerParams(
            dimension_semantics=("arbitrary",), vmem_limit_bytes=VMEM_LIMIT),
        name="qkv_rope" if rope else "qkv",
    )(*args)


VT_ROWS = HEAD_DIM + 16


def _attn_kernel(q_ref, k_ref, v_ref, ck_ref, cv_ref, o_ref, kab, vt, q2, acct, mbuf, *bufs, n, n_ctx, rb):
    kv = pl.program_id(1)
    nr = 2 * n // rb
    t_all = n + n_ctx
    sbuf, pbuf = bufs[:ATTN_BUFS], bufs[ATTN_BUFS:]

    def fill(k_src, v_src, r0, m):
        lane = lax.broadcasted_iota(jnp.int32, (m, LANES), 1)
        k_own = jnp.where((lane // HEAD_DIM) == kv, k_src, 0.0)
        k_oth = pltpu.roll(k_own, HEAD_DIM, 1)
        is0 = kv == 0
        kab[0, r0:r0 + m, :] = jnp.where(is0, k_own, k_oth).astype(BF16)
        kab[1, r0:r0 + m, :] = jnp.where(is0, k_oth, k_own).astype(BF16)
        v_t = v_src.T
        vt[0:HEAD_DIM, r0:r0 + m] = jnp.where(is0, v_t[0:HEAD_DIM, :], v_t[HEAD_DIM:, :]).astype(BF16)

    piece = 256
    for r0 in range(0, n, piece):
        fill(k_ref[0, r0:r0 + piece, :], v_ref[0, r0:r0 + piece, :], r0, piece)
    fill(ck_ref[0, 0], cv_ref[0, 0], n, n_ctx)
    vt[HEAD_DIM:, :] = jnp.ones((VT_ROWS - HEAD_DIM, t_all), BF16)
    for c0 in range(0, n, piece):
        q2[:, c0:c0 + piece] = q_ref[0, c0:c0 + piece, 0:LANES].astype(F32).T.astype(BF16)
        q2[:, n + c0:n + c0 + piece] = q_ref[0, c0:c0 + piece, LANES:].astype(F32).T.astype(BF16)

    span = ATTN_REGION
    n_regions = 2 * nr // span
    chunks = [slice(k0, k0 + ATTN_KEY_CHUNK) for k0 in range(0, t_all, ATTN_KEY_CHUNK)]

    def score_chunk(slot, ks, qr, buf, run):
        sc = _dot(kab[slot, ks, :], qr)
        sbuf[buf][ks, :] = sc
        while sc.shape[0] > 8:
            half = sc.shape[0] // 2
            sc = jnp.maximum(sc[0:half, :], sc[half:, :])
        return sc if run is None else jnp.maximum(run, sc)

    def q_rows(r):
        return q2[:, pl.ds(pl.multiple_of(r * rb, rb), rb)]

    per_tile = n // rb

    def flush(r):
        r0 = (r % per_tile) * rb
        if not isinstance(r, int):
            r0 = pl.multiple_of(r0, rb)
        o_ref[0, r // per_tile, pl.ds(r0, rb), :] = acct[r].T

    def region(g, parity, do_scores, do_numer, do_weigh, do_flush):
        cur, oth = parity * span, (1 - parity) * span
        for j in range(span):
            slot = j % 2
            if do_flush and slot == 0:
                flush((g - 2) * (span // 2) + j // 2)
            if do_numer:
                m = mbuf[cur + j]
            if do_scores:
                qr = q_rows((g + 1) * (span // 2) + j // 2)
            run, res = None, None
            for ks in chunks:
                if do_scores:
                    run = score_chunk(slot, ks, qr, oth + j, run)
                if do_numer:
                    pbuf[cur + j][ks, :] = jnp.exp(sbuf[cur + j][ks, :] - m).astype(BF16)
                if do_weigh:
                    part = _dot(vt[:, ks], pbuf[oth + j][ks, :])
                    res = part if res is None else res + part
            if do_scores:
                mbuf[oth + j] = jnp.max(run, axis=0, keepdims=True)
            if do_weigh:
                acct[(g - 1) * (span // 2) + j // 2, slot * HEAD_DIM:(slot + 1) * HEAD_DIM, :] = (
                    res[0:HEAD_DIM, :] * (1.0 / res[HEAD_DIM:HEAD_DIM + 1, :]))

    region(-1, 1, True, False, False, False)
    region(0, 0, True, True, False, False)

    def body(g, carry):
        for parity in range(2):
            pl.when(g % 2 == parity)(lambda: region(g, parity, True, True, True, False))
        return carry

    lax.fori_loop(1, n_regions - 1, body, 0)
    region(n_regions - 1, (n_regions - 1) % 2, False, True, True, False)
    region(n_regions, n_regions % 2, False, False, True, False)
    for r in range(nr):
        flush(r)


def _attention(q, k, v, cache_k, cache_v, layer):
    rb = ATTN_ROWS
    bsz, n, _ = q.shape
    n_ctx = cache_k.shape[2]
    t_all = n + n_ctx
    gw = 2 * LANES
    qmap = lambda b, g: (b, 0, g)
    kmap = lambda b, g: (b, 0, 0)
    cmap = lambda b, g: (b, layer, 0, 0)
    return pl.pallas_call(
        functools.partial(_attn_kernel, n=n, n_ctx=n_ctx, rb=rb),
        grid=(bsz, N_KV_HEADS),
        in_specs=[
            pl.BlockSpec((1, n, gw), qmap),
            pl.BlockSpec((1, n, KV_W), kmap),
            pl.BlockSpec((1, n, KV_W), kmap),
            pl.BlockSpec((1, 1, n_ctx, KV_W), cmap),
            pl.BlockSpec((1, 1, n_ctx, KV_W), cmap),
        ],
        out_specs=pl.BlockSpec((1, 2, n, LANES), lambda b, g: (b, g, 0, 0)),
        out_shape=jax.ShapeDtypeStruct((bsz, ATTN_W // LANES, n, LANES), F32),
        scratch_shapes=[
            pltpu.VMEM((2, t_all, LANES), BF16),
            pltpu.VMEM((VT_ROWS, t_all), BF16),
            pltpu.VMEM((LANES, 2 * n), BF16),
            pltpu.VMEM((2 * n // rb, LANES, rb), F32),
            pltpu.VMEM((ATTN_BUFS, 1, rb), F32),
        ] + [pltpu.VMEM((t_all, rb), F32)] * ATTN_BUFS + [pltpu.VMEM((t_all, rb), BF16)] * ATTN_BUFS,
        compiler_params=pltpu.CompilerParams(
            dimension_semantics=("arbitrary", "arbitrary"), vmem_limit_bytes=ATTN_VMEM_LIMIT),
        name="attn_lat",
    )(q, k, v, cache_k, cache_v)


def _attn_ctx_kernel(q_ref, k_ref, v_ref, o_ref, *, n):
    low = lax.broadcasted_iota(jnp.int32, (n, LANES), 1) < HEAD_DIM
    ones_lo = jnp.where(low, 1.0, 0.0)
    zk = jnp.zeros((HEAD_DIM, n), F32)
    gw = 2 * LANES
    for b in range(CTX_BATCH):
        kt_all = k_ref[b, 0]
        v = v_ref[b, 0].T
        vr = pltpu.roll(v, HEAD_DIM, 1)
        for g in range(N_KV_HEADS):
            kg = kt_all[g * HEAD_DIM:(g + 1) * HEAD_DIM, :]
            kts = (jnp.concatenate([kg, zk], axis=0).astype(BF16),
                   jnp.concatenate([zk, kg], axis=0).astype(BF16))
            v_lo, v_hi = (v, vr) if g == 0 else (vr, v)
            vs = (jnp.concatenate([jnp.where(low, v_lo, 0.0), ones_lo], axis=1).astype(BF16),
                  jnp.concatenate([jnp.where(low, 0.0, v_hi), 1.0 - ones_lo], axis=1).astype(BF16))
            q2 = jnp.concatenate([q_ref[b, :, g * gw:g * gw + LANES],
                                  q_ref[b, :, g * gw + LANES:(g + 1) * gw]], axis=0)
            acc = None
            for kt_s, v_s in zip(kts, vs):
                s = _dot(q2, kt_s)
                m = jnp.max(s, axis=-1, keepdims=True)
                pv = _dot(jnp.exp(s - m).astype(BF16), v_s)
                acc = pv if acc is None else acc + pv
            o = acc[:, 0:LANES] / acc[:, LANES:]
            o_ref[b, 2 * g] = o[0:n]
            o_ref[b, 2 * g + 1] = o[n:]


def _attention_ctx(q, k, v, layer):
    bsz, n, _ = q.shape
    blk = lambda w: pl.BlockSpec((CTX_BATCH, n, w), lambda b: (b, 0, 0))
    kv_blk = pl.BlockSpec((CTX_BATCH, 1, KV_W, n), lambda b: (b, layer, 0, 0))
    return pl.pallas_call(
        functools.partial(_attn_ctx_kernel, n=n),
        grid=(bsz // CTX_BATCH,),
        in_specs=[blk(ATTN_W), kv_blk, kv_blk],
        out_specs=pl.BlockSpec((CTX_BATCH, ATTN_W // LANES, n, LANES), lambda b: (b, 0, 0, 0)),
        out_shape=jax.ShapeDtypeStruct((bsz, ATTN_W // LANES, n, LANES), F32),
        compiler_params=pltpu.CompilerParams(
            dimension_semantics=("arbitrary",), vmem_limit_bytes=VMEM_LIMIT),
        name="attn_ctx",
    )(q, k, v)


def _mix_kernel(attn_ref, x_ref, xp_ref, xn_ref, mod_ref, g_ref, w_ref, wout_ref,
                convw_ref, convb_ref, wpool_ref, pscale_ref, fg_ref,
                o_ref, xe, yext, uext, *, tm, tps, seq, final):
    j = pl.program_id(0) % tps
    pm = jnp.where(j == 0, 0.0, 1.0)
    nm = jnp.where(j == tps - 1, 0.0, 1.0)
    c = CONV_W
    g = g_ref[...]
    x = x_ref[...]
    xe[0:tm, :] = _mod_norm(x, g, mod_ref).astype(BF16)
    if tps > 1:
        halo = jnp.concatenate([xn_ref[...], xp_ref[...]], axis=0)
        xe[tm:, :] = _mod_norm(halo, g, mod_ref).astype(BF16)

    def proj(rows, off, width):
        return _dot(xe[0:rows, :], w_ref[0, :, off:off + width])

    nseq = max(tm // seq, 1)
    whole = tps == 1
    ln = tm // nseq
    hrows = tm if whole else tm + 2 * HALO
    h_all = proj(hrows, COL_CONV_H, c)
    c_all = proj(hrows, COL_CONV_C, c)
    u_all = proj(hrows, COL_POOL_U, POOL_W)
    z_a = proj(tm, COL_ATTN_GATE, ATTN_W)
    a = ATTN_W
    attn = jnp.concatenate(
        [jnp.concatenate([attn_ref[p, t] for t in range(attn_ref.shape[1])], axis=1)
         for p in range(attn_ref.shape[0])], axis=0)
    attn_o = (attn * _silu(z_a)).astype(BF16)
    out_a = _dot(attn_o, wout_ref[0, 0:a, :])
    b_c = proj(tm, COL_CONV_B, c)
    z_c = proj(tm, COL_CONV_GATE, c)
    z_p = proj(tm, COL_POOL_GATE, POOL_W)

    def extend(ext, p, val):
        zeros = jnp.zeros((HALO, val.shape[1]), F32)
        ext[p, 0:HALO, :] = zeros if whole else val[tm + HALO:, :] * pm
        ext[p, HALO:HALO + ln, :] = val[p * ln:(p + 1) * ln, :]
        ext[p, HALO + ln:, :] = zeros if whole else val[tm:tm + HALO, :] * nm

    y = h_all * c_all
    convs = []
    for p in range(nseq):
        extend(yext, p, y)
        convs.append(yext[p, HALO - 1:HALO - 1 + ln, :] * convw_ref[0:1, :]
                     + yext[p, HALO:HALO + ln, :] * convw_ref[1:2, :]
                     + yext[p, HALO + 1:HALO + 1 + ln, :] * convw_ref[2:3, :] + convb_ref[...])
    conv = convs[0] if nseq == 1 else jnp.concatenate(convs, axis=0)
    conv_o = (b_c * conv * _silu(z_c)).astype(BF16)

    lane = lax.broadcasted_iota(jnp.int32, (ln, LANES), 1)
    pos = lax.broadcasted_iota(jnp.int32, (ln, LANES), 0) + (0 if whole else j * tm)
    low = lane < POOL_GROUP
    d_rows = []
    for p in range(nseq):
        extend(uext, p, u_all)

        def win(k0, k1, sl):
            tot = None
            for k in range(k0, k1):
                t = uext[p, HALO + k:HALO + k + ln, sl]
                tot = t if tot is None else tot + t
            return tot

        ds = []
        for ti, (wa, wb) in enumerate(((2, 4), (8, 16))):
            sl = slice(ti * LANES, (ti + 1) * LANES)
            s_a = win(-(wa // 2), wa // 2, sl)
            s_b = s_a + win(-(wb // 2), -(wa // 2), sl) + win(wa // 2, wb // 2, sl)
            half = jnp.where(low, wa // 2, wb // 2)
            cnt = jnp.minimum(pos + half - 1, seq - 1) - jnp.maximum(pos - half, 0) + 1
            tot = jnp.where(low, s_a, s_b)
            ds.append(tot / cnt.astype(F32) - u_all[p * ln:(p + 1) * ln, sl])
        d_rows.append(jnp.concatenate(ds, axis=-1))
    d = (d_rows[0] if nseq == 1 else jnp.concatenate(d_rows, axis=0)).astype(BF16)
    pool_o = (_dot(d, wpool_ref[...]) * pscale_ref[...] * _silu(z_p)).astype(BF16)

    out = out_a + _dot(conv_o, wout_ref[0, a:a + c, :]) + _dot(pool_o, wout_ref[0, a + c:, :])
    res = x + mod_ref[0, 0, 2:3, :] * out
    if final:
        ms = jnp.mean(res * res, axis=-1, keepdims=True)
        res = res * lax.rsqrt(ms + EPS) * fg_ref[...]
    o_ref[...] = res


def _mix(attn2, x2, mod, mod_row, norm_g, w_in_bf, wout_bf, layer, conv_w, conv_b, wpool_bd, pool_scale,
         final_g, *, seq, tm, final):
    tokens = x2.shape[0]
    tps = max(seq // tm, 1)
    nseq = max(tm // seq, 1)
    hb = tm // HALO
    nhb = tokens // HALO
    row = lambda i: (i, 0)
    const = lambda i: (0, 0)
    in_specs = [
        (pl.BlockSpec((nseq, ATTN_W // LANES, tm // nseq, LANES), lambda i: (i, 0, 0, 0)) if tps == 1 else
         pl.BlockSpec((1, ATTN_W // LANES, tm, LANES), lambda i: (i // tps, 0, i % tps, 0))),
        pl.BlockSpec((tm, D_MODEL), row),
        pl.BlockSpec((HALO, D_MODEL), lambda i: (jnp.maximum(i * hb - 1, 0), 0)),
        pl.BlockSpec((HALO, D_MODEL), lambda i: (jnp.minimum((i + 1) * hb, nhb - 1), 0)),
        _mod_spec(layer, mod_row, tps),
        pl.BlockSpec((1, D_MODEL), const),
        pl.BlockSpec((1, D_MODEL, IN_W), lambda i: (layer, 0, 0)),
        pl.BlockSpec((1, D_MODEL, D_MODEL), lambda i: (layer, 0, 0)),
        pl.BlockSpec((3, CONV_W), const),
        pl.BlockSpec((1, CONV_W), const),
        pl.BlockSpec((POOL_W, POOL_W), const),
        pl.BlockSpec((1, POOL_W), const),
        pl.BlockSpec((1, D_MODEL), const),
    ]
    return pl.pallas_call(
        functools.partial(_mix_kernel, tm=tm, tps=tps, seq=seq, final=final),
        grid=(tokens // tm,),
        in_specs=in_specs,
        out_specs=pl.BlockSpec((tm, D_MODEL), row),
        out_shape=jax.ShapeDtypeStruct((tokens, D_MODEL), F32),
        scratch_shapes=[
            pltpu.VMEM((tm + 2 * HALO, D_MODEL), BF16),
            pltpu.VMEM((nseq, tm // nseq + 2 * HALO, CONV_W), F32),
            pltpu.VMEM((nseq, tm // nseq + 2 * HALO, POOL_W), F32),
        ],
        compiler_params=pltpu.CompilerParams(
            dimension_semantics=("arbitrary",), vmem_limit_bytes=VMEM_LIMIT),
        name="mix_final" if final else "mix",
    )(attn2, x2, x2, x2, mod, norm_g.reshape(1, D_MODEL), w_in_bf, wout_bf,
      conv_w, conv_b.reshape(1, CONV_W), wpool_bd, pool_scale.reshape(1, POOL_W),
      final_g.reshape(1, D_MODEL))


def _rope_tables(n):
    rows = n // GRID_W
    row = jnp.repeat(jnp.arange(rows), GRID_W).astype(F32)
    col = jnp.tile(jnp.arange(GRID_W), rows).astype(F32)
    half = HEAD_DIM // 2
    inv = 1.0 / (ROPE_THETA ** (jnp.arange(0, half, 2, dtype=F32) / half))
    ar = row[:, None] * inv
    ac = col[:, None] * inv
    ang = jnp.tile(jnp.concatenate([ar, ar, ac, ac], axis=-1), (1, LANES // HEAD_DIM))
    cos = jnp.cos(ang)
    sin = jnp.sin(ang)
    second = (jnp.arange(LANES) % half) >= (half // 2)
    return cos, jnp.where(second, sin, -sin)


def _block_diag(blocks):
    n = len(blocks)
    rows = []
    for i, b in enumerate(blocks):
        rows.append(jnp.concatenate(
            [b if j == i else jnp.zeros((b.shape[0], blocks[j].shape[1]), b.dtype) for j in range(n)],
            axis=1))
    return jnp.concatenate(rows, axis=0)


def kernel(x_prompt, x_sample, cache_k, cache_v, c, c_ctx, norm_g, w_ada, b_ada, w_in,
           q_norm_g, k_norm_g, conv_w, conv_b, pool_w, pool_scale, w_out, final_g):
    depth = w_in.shape[0]
    bp, sp, _ = x_prompt.shape
    bs, ss, _ = x_sample.shape
    n_ctx = cache_k.shape[2]

    cvec = jnp.concatenate(
        [c, c_ctx[None, :], jnp.zeros((ADA_ROWS - bs - 1, D_MODEL), F32)], axis=0)
    mod = _ada(cvec, w_ada, b_ada)
    mod = mod.reshape(depth, ADA_ROWS, 3, D_MODEL)

    w_in_bf = w_in.astype(BF16)
    w_out_bf = w_out.astype(BF16)
    tables = _rope_tables(ss)
    ones = jnp.full((HEAD_DIM, HEAD_DIM), 1.0 / HEAD_DIM, BF16)
    bd = _block_diag([ones, ones])
    ck = cache_k.reshape(bs, depth, n_ctx, KV_W)
    cv = cache_v.reshape(bs, depth, n_ctx, KV_W)

    h = x_prompt.reshape(bp * sp, D_MODEL)
    z = x_sample.reshape(bs * ss, D_MODEL)
    kv_ctx = None
    gains = jnp.concatenate(
        [jnp.tile(q_norm_g * (HEAD_DIM ** -0.5), (1, N_HEADS)), jnp.tile(k_norm_g, (1, N_KV_HEADS))],
        axis=-1).reshape(depth, 1, QK_W)
    for l in range(depth):
        wpool_bd = _block_diag([pool_w[l, g] for g in range(pool_w.shape[1])]).astype(BF16)
        final = l == depth - 1
        mix = functools.partial(
            _mix, norm_g=norm_g[l], w_in_bf=w_in_bf, wout_bf=w_out_bf, layer=l,
            conv_w=conv_w[l], conv_b=conv_b[l], wpool_bd=wpool_bd, pool_scale=pool_scale[l],
            final_g=final_g, final=final)

        q, k, v = _qkv(h, mod, bs, norm_g[l], w_in_bf, l, gains, bd, None, kv_ctx, seq=sp, tm=QKV_TILE)
        kv_ctx = (k, v)
        attn = _attention_ctx(q.reshape(bp, sp, ATTN_W), k, v, l)
        h = mix(attn, h, mod, bs, seq=sp, tm=MIX_TILE)

        q, k, v = _qkv(z, mod, None, norm_g[l], w_in_bf, l, gains, bd, tables, None, seq=ss, tm=QKV_TILE)
        attn = _attention(q.reshape(bs, ss, ATTN_W), k.reshape(bs, ss, KV_W), v.reshape(bs, ss, KV_W),
                          ck, cv, l)
        z = mix(attn, z, mod, None, seq=ss, tm=MIX_TILE)

    y_prompt = h.reshape(bp, sp, D_MODEL)
    y_sample = z.reshape(bs, ss, D_MODEL)
    new_k, new_v = (a.reshape(bp, depth, N_KV_HEADS, HEAD_DIM, sp).transpose(0, 1, 4, 2, 3) for a in kv_ctx)
    return (y_prompt, y_sample, new_k, new_v)
```

```python
import functools

import jax
import jax.numpy as jnp
from jax import lax
from jax.experimental import pallas as pl
from jax.experimental.pallas import tpu as pltpu

D_MODEL = 1024
GRID_W = 64
N_HEADS = 8
N_KV_HEADS = 2
HEAD_DIM = 64
ATTN_W = N_HEADS * HEAD_DIM
KV_W = N_KV_HEADS * HEAD_DIM
QK_W = ATTN_W + KV_W
QKV_W = QK_W + KV_W
CONV_W = 256
POOL_W = 256
POOL_GROUP = 64
IN_W = 2816
COL_ATTN_GATE = QKV_W
COL_CONV_H = COL_ATTN_GATE + ATTN_W
COL_CONV_B = COL_CONV_H + CONV_W
COL_CONV_C = COL_CONV_B + CONV_W
COL_CONV_GATE = COL_CONV_C + CONV_W
COL_POOL_U = COL_CONV_GATE + CONV_W
COL_POOL_GATE = COL_POOL_U + POOL_W
ROPE_THETA = 10000.0
EPS = 1e-6
LANES = 128
HALO = 8
ADA_ROWS = 8
VMEM_LIMIT = 48 * 1024 * 1024
ATTN_VMEM_LIMIT = 56 * 1024 * 1024

ADA_COLS = 1024
QKV_TILE = 1024
QKV_SUB = 256
MIX_TILE = 512
ATTN_ROWS = 256
ATTN_REGION = 4
ATTN_BUFS = 2 * ATTN_REGION
ATTN_KEY_CHUNK = 256
CTX_BATCH = 4

F32 = jnp.float32
BF16 = jnp.bfloat16


def _silu(z):
    return z * (1.0 / (1.0 + jnp.exp(-z)))


def _dot(a, b):
    return jnp.dot(a, b, preferred_element_type=F32)


def _mod_norm(x, g, mod_ref):
    ms = jnp.mean(x * x, axis=-1, keepdims=True)
    gs = g * (1.0 + mod_ref[0, 0, 1:2, :])
    return x * lax.rsqrt(ms + EPS) * gs + mod_ref[0, 0, 0:1, :]


def _ada_kernel(c_ref, w_ref, b_ref, o_ref):
    s = _silu(c_ref[...]).astype(BF16)
    o_ref[0] = _dot(s, w_ref[0].astype(BF16)) + b_ref[0]


def _ada(cvec, w_ada, b_ada):
    depth, _, n3 = w_ada.shape
    tn = ADA_COLS
    return pl.pallas_call(
        _ada_kernel,
        grid=(depth, n3 // tn),
        in_specs=[
            pl.BlockSpec((ADA_ROWS, D_MODEL), lambda l, j: (0, 0)),
            pl.BlockSpec((1, D_MODEL, tn), lambda l, j: (l, 0, j)),
            pl.BlockSpec((1, 1, tn), lambda l, j: (l, 0, j)),
        ],
        out_specs=pl.BlockSpec((1, ADA_ROWS, tn), lambda l, j: (l, 0, j)),
        out_shape=jax.ShapeDtypeStruct((depth, ADA_ROWS, n3), F32),
        name="ada_mod",
    )(cvec, w_ada, b_ada.reshape(depth, 1, n3))


def _qkv_kernel(*refs, rope, nsub, first_layer):
    if rope:
        (x_ref, mod_ref, g_ref, w_ref, gains_ref, bd_ref, cos_ref, sin_ref,
         q_ref, k_ref, v_ref) = refs
    else:
        x_ref, mod_ref, g_ref, w_ref, gains_ref, bd_ref = refs[:6]
        q_ref, k_ref, v_ref = refs[-3:]
    tm = x_ref.shape[0]
    sub = tm // nsub
    bd = bd_ref[...]
    g = g_ref[...]
    if rope:
        lane = lax.broadcasted_iota(jnp.int32, (sub, LANES), 1)
        second = (lane % (HEAD_DIM // 2)) >= (HEAD_DIM // 4)
    k_parts, v_parts = [None] * nsub, [None] * nsub

    def finish(s, c, part):
        rows = slice(s * sub, (s + 1) * sub)
        for h in range(2):
            j = 2 * c + h
            t = part[:, h * LANES:(h + 1) * LANES]
            if j * LANES >= QK_W:
                v_parts[s] = t
                continue
            sl = slice(j * LANES, (j + 1) * LANES)
            sq = t * t
            hi = sq.astype(BF16)
            lo = (sq - hi.astype(F32)).astype(BF16)
            msq = _dot(hi, bd) + _dot(lo, bd)
            t = t * lax.rsqrt(msq + EPS) * gains_ref[0, :, sl]
            if rope:
                partner = jnp.where(second, pltpu.roll(t, HEAD_DIM // 4, 1),
                                    pltpu.roll(t, LANES - HEAD_DIM // 4, 1))
                t = t * cos_ref[rows, :] + partner * sin_ref[rows, :]
            if j * LANES < ATTN_W:
                q_ref[rows, sl] = t.astype(BF16)
            else:
                k_parts[s] = t

    ncol = QKV_W // (2 * LANES)
    prev = None
    for s in range(nsub + 1):
        if s < nsub:
            xn = _mod_norm(x_ref[s * sub:(s + 1) * sub, :], g, mod_ref).astype(BF16)
        cur = []
        for c in range(ncol):
            if s < nsub:
                cur.append(_dot(xn, w_ref[0, :, c * 2 * LANES:(c + 1) * 2 * LANES]))
            if prev is not None:
                finish(s - 1, c, prev[c])
        prev = cur
    for ref, parts in ((k_ref, k_parts), (v_ref, v_parts)):
        val = jnp.concatenate(parts, axis=0)
        if rope:
            ref[...] = val
            continue
        seq = ref.shape[3]
        for p in range(ref.shape[0]):
            piece = val[p * seq:(p + 1) * seq, :].T
            if first_layer is None:
                ref[p, 0] = piece
            else:
                for l2 in range(ref.shape[1]):
                    ref[p, l2] = piece if l2 == first_layer else jnp.zeros_like(piece)


def _mod_spec(layer, mod_row, tps):
    if mod_row is None:
        return pl.BlockSpec((1, 1, 3, D_MODEL), lambda i: (layer, i // tps, 0, 0))
    return pl.BlockSpec((1, 1, 3, D_MODEL), lambda i: (layer, mod_row, 0, 0))


def _qkv(x2, mod, mod_row, norm_g, w_in_bf, layer, gains, bd, tables, kv_prev, *, seq, tm):
    tokens = x2.shape[0]
    depth = w_in_bf.shape[0]
    tps = max(seq // tm, 1)
    rope = tables is not None
    row = lambda i: (i, 0)
    const = lambda i: (0, 0)
    in_specs = [
        pl.BlockSpec((tm, D_MODEL), row),
        _mod_spec(layer, mod_row, tps),
        pl.BlockSpec((1, D_MODEL), const),
        pl.BlockSpec((1, D_MODEL, QKV_W), lambda i: (layer, 0, 0)),
        pl.BlockSpec((1, 1, QK_W), lambda i: (layer, 0, 0)),
        pl.BlockSpec((LANES, LANES), const),
    ]
    args = [x2, mod, norm_g.reshape(1, D_MODEL), w_in_bf, gains, bd]
    out_shapes = [jax.ShapeDtypeStruct((tokens, ATTN_W), BF16)]
    out_specs = [pl.BlockSpec((tm, ATTN_W), row)]
    aliases = {}
    first_layer = None
    if rope:
        in_specs += [pl.BlockSpec((tm, LANES), lambda i: (i % tps, 0))] * 2
        args += list(tables)
        out_shapes += [jax.ShapeDtypeStruct((tokens, KV_W), F32)] * 2
        out_specs += [pl.BlockSpec((tm, KV_W), row)] * 2
    else:
        nseq = tm // seq
        out_shapes += [jax.ShapeDtypeStruct((tokens // seq, depth, KV_W, seq), F32)] * 2
        if kv_prev is None:
            first_layer = layer
            out_specs += [pl.BlockSpec((nseq, depth, KV_W, seq), lambda i: (i, 0, 0, 0))] * 2
        else:
            out_specs += [pl.BlockSpec((nseq, 1, KV_W, seq), lambda i: (i, layer, 0, 0))] * 2
            in_specs += [pl.BlockSpec(memory_space=pl.ANY)] * 2
            aliases = {len(args): 1, len(args) + 1: 2}
            args += list(kv_prev)
    return pl.pallas_call(
        functools.partial(_qkv_kernel, rope=rope, nsub=tm // QKV_SUB, first_layer=first_layer),
        grid=(tokens // tm,),
        in_specs=in_specs,
        out_specs=out_specs,
        out_shape=out_shapes,
        input_output_aliases=aliases,
        compiler_params=pltpu.CompilerParams(
            dimension_semantics=("arbitrary",), vmem_limit_bytes=VMEM_LIMIT),
        name="qkv_rope" if rope else "qkv",
    )(*args)


VT_ROWS = HEAD_DIM + 16


def _attn_kernel(q_ref, k_ref, v_ref, ck_ref, cv_ref, o_ref, kab, vt, q2, acct, mbuf, *bufs, n, n_ctx, rb):
    kv = pl.program_id(1)
    nr = 2 * n // rb
    t_all = n + n_ctx
    sbuf, pbuf = bufs[:ATTN_BUFS], bufs[ATTN_BUFS:]

    def fill(k_src, v_t, r0, m):
        lane = lax.broadcasted_iota(jnp.int32, (m, LANES), 1)
        k_own = jnp.where((lane // HEAD_DIM) == kv, k_src, 0.0)
        k_oth = pltpu.roll(k_own, HEAD_DIM, 1)
        is0 = kv == 0
        kab[0, r0:r0 + m, :] = jnp.where(is0, k_own, k_oth).astype(BF16)
        kab[1, r0:r0 + m, :] = jnp.where(is0, k_oth, k_own).astype(BF16)
        vt[0:HEAD_DIM, r0:r0 + m] = jnp.where(is0, v_t[0:HEAD_DIM, :], v_t[HEAD_DIM:, :]).astype(BF16)

    piece = 256
    for r0 in range(0, n, piece):
        fill(k_ref[0, r0:r0 + piece, :], v_ref[0, r0:r0 + piece, :].T, r0, piece)
    fill(ck_ref[0, 0].T, cv_ref[0, 0], n, n_ctx)
    vt[HEAD_DIM:, :] = jnp.ones((VT_ROWS - HEAD_DIM, t_all), BF16)
    for c0 in range(0, n, piece):
        q2[:, c0:c0 + piece] = q_ref[0, c0:c0 + piece, 0:LANES].astype(F32).T.astype(BF16)
        q2[:, n + c0:n + c0 + piece] = q_ref[0, c0:c0 + piece, LANES:].astype(F32).T.astype(BF16)

    span = ATTN_REGION
    n_regions = 2 * nr // span
    chunks = [slice(k0, k0 + ATTN_KEY_CHUNK) for k0 in range(0, t_all, ATTN_KEY_CHUNK)]

    def score_chunk(slot, ks, qr, buf, run):
        sc = _dot(kab[slot, ks, :], qr)
        sbuf[buf][ks, :] = sc
        while sc.shape[0] > 8:
            half = sc.shape[0] // 2
            sc = jnp.maximum(sc[0:half, :], sc[half:, :])
        return sc if run is None else jnp.maximum(run, sc)

    def q_rows(r):
        return q2[:, pl.ds(pl.multiple_of(r * rb, rb), rb)]

    per_tile = n // rb

    def flush(r):
        r0 = (r % per_tile) * rb
        if not isinstance(r, int):
            r0 = pl.multiple_of(r0, rb)
        o_ref[0, r // per_tile, pl.ds(r0, rb), :] = acct[r].T

    def region(g, parity, do_scores, do_numer, do_weigh, do_flush):
        cur, oth = parity * span, (1 - parity) * span
        for j in range(span):
            slot = j % 2
            if do_flush and slot == 0:
                flush((g - 2) * (span // 2) + j // 2)
            if do_numer:
                m = mbuf[cur + j]
            if do_scores:
                qr = q_rows((g + 1) * (span // 2) + j // 2)
            run, res = None, None
            for ks in chunks:
                if do_scores:
                    run = score_chunk(slot, ks, qr, oth + j, run)
                if do_numer:
                    pbuf[cur + j][ks, :] = jnp.exp(sbuf[cur + j][ks, :] - m).astype(BF16)
                if do_weigh:
                    part = _dot(vt[:, ks], pbuf[oth + j][ks, :])
                    res = part if res is None else res + part
            if do_scores:
                mbuf[oth + j] = jnp.max(run, axis=0, keepdims=True)
            if do_weigh:
                acct[(g - 1) * (span // 2) + j // 2, slot * HEAD_DIM:(slot + 1) * HEAD_DIM, :] = (
                    res[0:HEAD_DIM, :] * (1.0 / res[HEAD_DIM:HEAD_DIM + 1, :]))

    region(-1, 1, True, False, False, False)
    region(0, 0, True, True, False, False)

    def body(g, carry):
        for parity in range(2):
            pl.when(g % 2 == parity)(lambda: region(g, parity, True, True, True, False))
        return carry

    lax.fori_loop(1, n_regions - 1, body, 0)
    region(n_regions - 1, (n_regions - 1) % 2, False, True, True, False)
    region(n_regions, n_regions % 2, False, False, True, False)
    for r in range(nr):
        flush(r)


def _attention(q, k, v, cache_k, cache_v, layer):
    rb = ATTN_ROWS
    bsz, n, _ = q.shape
    n_ctx = cache_k.shape[3]
    t_all = n + n_ctx
    gw = 2 * LANES
    qmap = lambda b, g: (b, 0, g)
    kmap = lambda b, g: (b, 0, 0)
    cmap = lambda b, g: (b, layer, 0, 0)
    return pl.pallas_call(
        functools.partial(_attn_kernel, n=n, n_ctx=n_ctx, rb=rb),
        grid=(bsz, N_KV_HEADS),
        in_specs=[
            pl.BlockSpec((1, n, gw), qmap),
            pl.BlockSpec((1, n, KV_W), kmap),
            pl.BlockSpec((1, n, KV_W), kmap),
            pl.BlockSpec((1, 1, KV_W, n_ctx), cmap),
            pl.BlockSpec((1, 1, KV_W, n_ctx), cmap),
        ],
        out_specs=pl.BlockSpec((1, 2, n, LANES), lambda b, g: (b, g, 0, 0)),
        out_shape=jax.ShapeDtypeStruct((bsz, ATTN_W // LANES, n, LANES), F32),
        scratch_shapes=[
            pltpu.VMEM((2, t_all, LANES), BF16),
            pltpu.VMEM((VT_ROWS, t_all), BF16),
            pltpu.VMEM((LANES, 2 * n), BF16),
            pltpu.VMEM((2 * n // rb, LANES, rb), F32),
            pltpu.VMEM((ATTN_BUFS, 1, rb), F32),
        ] + [pltpu.VMEM((t_all, rb), F32)] * ATTN_BUFS + [pltpu.VMEM((t_all, rb), BF16)] * ATTN_BUFS,
        compiler_params=pltpu.CompilerParams(
            dimension_semantics=("arbitrary", "arbitrary"), vmem_limit_bytes=ATTN_VMEM_LIMIT),
        name="attn_lat",
    )(q, k, v, cache_k, cache_v)


def _attn_ctx_kernel(q_ref, k_ref, v_ref, o_ref, *, n):
    low = lax.broadcasted_iota(jnp.int32, (n, LANES), 1) < HEAD_DIM
    ones_lo = jnp.where(low, 1.0, 0.0)
    zk = jnp.zeros((HEAD_DIM, n), F32)
    gw = 2 * LANES
    for b in range(CTX_BATCH):
        kt_all = k_ref[b, 0]
        v = v_ref[b, 0].T
        vr = pltpu.roll(v, HEAD_DIM, 1)
        for g in range(N_KV_HEADS):
            kg = kt_all[g * HEAD_DIM:(g + 1) * HEAD_DIM, :]
            kts = (jnp.concatenate([kg, zk], axis=0).astype(BF16),
                   jnp.concatenate([zk, kg], axis=0).astype(BF16))
            v_lo, v_hi = (v, vr) if g == 0 else (vr, v)
            vs = (jnp.concatenate([jnp.where(low, v_lo, 0.0), ones_lo], axis=1).astype(BF16),
                  jnp.concatenate([jnp.where(low, 0.0, v_hi), 1.0 - ones_lo], axis=1).astype(BF16))
            q2 = jnp.concatenate([q_ref[b, :, g * gw:g * gw + LANES],
                                  q_ref[b, :, g * gw + LANES:(g + 1) * gw]], axis=0)
            acc = None
            for kt_s, v_s in zip(kts, vs):
                s = _dot(q2, kt_s)
                m = jnp.max(s, axis=-1, keepdims=True)
                pv = _dot(jnp.exp(s - m).astype(BF16), v_s)
                acc = pv if acc is None else acc + pv
            o = acc[:, 0:LANES] / acc[:, LANES:]
            o_ref[b, 2 * g] = o[0:n]
            o_ref[b, 2 * g + 1] = o[n:]


def _attention_ctx(q, k, v, layer):
    bsz, n, _ = q.shape
    blk = lambda w: pl.BlockSpec((CTX_BATCH, n, w), lambda b: (b, 0, 0))
    kv_blk = pl.BlockSpec((CTX_BATCH, 1, KV_W, n), lambda b: (b, layer, 0, 0))
    return pl.pallas_call(
        functools.partial(_attn_ctx_kernel, n=n),
        grid=(bsz // CTX_BATCH,),
        in_specs=[blk(ATTN_W), kv_blk, kv_blk],
        out_specs=pl.BlockSpec((CTX_BATCH, ATTN_W // LANES, n, LANES), lambda b: (b, 0, 0, 0)),
        out_shape=jax.ShapeDtypeStruct((bsz, ATTN_W // LANES, n, LANES), F32),
        compiler_params=pltpu.CompilerParams(
            dimension_semantics=("arbitrary",), vmem_limit_bytes=VMEM_LIMIT),
        name="attn_ctx",
    )(q, k, v)


def _mix_kernel(attn_ref, x_ref, xp_ref, xn_ref, mod_ref, g_ref, w_ref, wout_ref,
                convw_ref, convb_ref, wpool_ref, pscale_ref, fg_ref,
                o_ref, xe, yext, uext, *, tm, tps, seq, final):
    j = pl.program_id(0) % tps
    pm = jnp.where(j == 0, 0.0, 1.0)
    nm = jnp.where(j == tps - 1, 0.0, 1.0)
    c = CONV_W
    g = g_ref[...]
    x = x_ref[...]
    xe[0:tm, :] = _mod_norm(x, g, mod_ref).astype(BF16)
    if tps > 1:
        halo = jnp.concatenate([xn_ref[...], xp_ref[...]], axis=0)
        xe[tm:, :] = _mod_norm(halo, g, mod_ref).astype(BF16)

    def proj(rows, off, width):
        return _dot(xe[0:rows, :], w_ref[0, :, off:off + width])

    nseq = max(tm // seq, 1)
    whole = tps == 1
    ln = tm // nseq
    hrows = tm if whole else tm + 2 * HALO
    h_all = proj(hrows, COL_CONV_H, c)
    c_all = proj(hrows, COL_CONV_C, c)
    u_all = proj(hrows, COL_POOL_U, POOL_W)
    z_a = proj(tm, COL_ATTN_GATE, ATTN_W)
    a = ATTN_W
    attn = jnp.concatenate(
        [jnp.concatenate([attn_ref[p, t] for t in range(attn_ref.shape[1])], axis=1)
         for p in range(attn_ref.shape[0])], axis=0)
    attn_o = (attn * _silu(z_a)).astype(BF16)
    out_a = _dot(attn_o, wout_ref[0, 0:a, :])
    b_c = proj(tm, COL_CONV_B, c)
    z_c = proj(tm, COL_CONV_GATE, c)
    z_p = proj(tm, COL_POOL_GATE, POOL_W)

    def extend(ext, p, val):
        zeros = jnp.zeros((HALO, val.shape[1]), F32)
        ext[p, 0:HALO, :] = zeros if whole else val[tm + HALO:, :] * pm
        ext[p, HALO:HALO + ln, :] = val[p * ln:(p + 1) * ln, :]
        ext[p, HALO + ln:, :] = zeros if whole else val[tm:tm + HALO, :] * nm

    y = h_all * c_all
    convs = []
    for p in range(nseq):
        extend(yext, p, y)
        convs.append(yext[p, HALO - 1:HALO - 1 + ln, :] * convw_ref[0:1, :]
                     + yext[p, HALO:HALO + ln, :] * convw_ref[1:2, :]
                     + yext[p, HALO + 1:HALO + 1 + ln, :] * convw_ref[2:3, :] + convb_ref[...])
    conv = convs[0] if nseq == 1 else jnp.concatenate(convs, axis=0)
    conv_o = (b_c * conv * _silu(z_c)).astype(BF16)

    lane = lax.broadcasted_iota(jnp.int32, (ln, LANES), 1)
    pos = lax.broadcasted_iota(jnp.int32, (ln, LANES), 0) + (0 if whole else j * tm)
    low = lane < POOL_GROUP
    d_rows = []
    for p in range(nseq):
        extend(uext, p, u_all)

        def win(k0, k1, sl):
            tot = None
            for k in range(k0, k1):
                t = uext[p, HALO + k:HALO + k + ln, sl]
                tot = t if tot is None else tot + t
            return tot

        ds = []
        for ti, (wa, wb) in enumerate(((2, 4), (8, 16))):
            sl = slice(ti * LANES, (ti + 1) * LANES)
            s_a = win(-(wa // 2), wa // 2, sl)
            s_b = s_a + win(-(wb // 2), -(wa // 2), sl) + win(wa // 2, wb // 2, sl)
            half = jnp.where(low, wa // 2, wb // 2)
            cnt = jnp.minimum(pos + half - 1, seq - 1) - jnp.maximum(pos - half, 0) + 1
            tot = jnp.where(low, s_a, s_b)
            ds.append(tot / cnt.astype(F32) - u_all[p * ln:(p + 1) * ln, sl])
        d_rows.append(jnp.concatenate(ds, axis=-1))
    d = (d_rows[0] if nseq == 1 else jnp.concatenate(d_rows, axis=0)).astype(BF16)
    pool_o = (_dot(d, wpool_ref[...]) * pscale_ref[...] * _silu(z_p)).astype(BF16)

    out = out_a + _dot(conv_o, wout_ref[0, a:a + c, :]) + _dot(pool_o, wout_ref[0, a + c:, :])
    res = x + mod_ref[0, 0, 2:3, :] * out
    if final:
        ms = jnp.mean(res * res, axis=-1, keepdims=True)
        res = res * lax.rsqrt(ms + EPS) * fg_ref[...]
    o_ref[...] = res


def _mix(attn2, x2, mod, mod_row, norm_g, w_in_bf, wout_bf, layer, conv_w, conv_b, wpool_bd, pool_scale,
         final_g, *, seq, tm, final):
    tokens = x2.shape[0]
    tps = max(seq // tm, 1)
    nseq = max(tm // seq, 1)
    hb = tm // HALO
    nhb = tokens // HALO
    row = lambda i: (i, 0)
    const = lambda i: (0, 0)
    in_specs = [
        (pl.BlockSpec((nseq, ATTN_W // LANES, tm // nseq, LANES), lambda i: (i, 0, 0, 0)) if tps == 1 else
         pl.BlockSpec((1, ATTN_W // LANES, tm, LANES), lambda i: (i // tps, 0, i % tps, 0))),
        pl.BlockSpec((tm, D_MODEL), row),
        pl.BlockSpec((HALO, D_MODEL), lambda i: (jnp.maximum(i * hb - 1, 0), 0)),
        pl.BlockSpec((HALO, D_MODEL), lambda i: (jnp.minimum((i + 1) * hb, nhb - 1), 0)),
        _mod_spec(layer, mod_row, tps),
        pl.BlockSpec((1, D_MODEL), const),
        pl.BlockSpec((1, D_MODEL, IN_W), lambda i: (layer, 0, 0)),
        pl.BlockSpec((1, D_MODEL, D_MODEL), lambda i: (layer, 0, 0)),
        pl.BlockSpec((3, CONV_W), const),
        pl.BlockSpec((1, CONV_W), const),
        pl.BlockSpec((POOL_W, POOL_W), const),
        pl.BlockSpec((1, POOL_W), const),
        pl.BlockSpec((1, D_MODEL), const),
    ]
    return pl.pallas_call(
        functools.partial(_mix_kernel, tm=tm, tps=tps, seq=seq, final=final),
        grid=(tokens // tm,),
        in_specs=in_specs,
        out_specs=pl.BlockSpec((tm, D_MODEL), row),
        out_shape=jax.ShapeDtypeStruct((tokens, D_MODEL), F32),
        scratch_shapes=[
            pltpu.VMEM((tm + 2 * HALO, D_MODEL), BF16),
            pltpu.VMEM((nseq, tm // nseq + 2 * HALO, CONV_W), F32),
            pltpu.VMEM((nseq, tm // nseq + 2 * HALO, POOL_W), F32),
        ],
        compiler_params=pltpu.CompilerParams(
            dimension_semantics=("arbitrary",), vmem_limit_bytes=VMEM_LIMIT),
        name="mix_final" if final else "mix",
    )(attn2, x2, x2, x2, mod, norm_g.reshape(1, D_MODEL), w_in_bf, wout_bf,
      conv_w, conv_b.reshape(1, CONV_W), wpool_bd, pool_scale.reshape(1, POOL_W),
      final_g.reshape(1, D_MODEL))


def _rope_tables(n):
    rows = n // GRID_W
    row = jnp.repeat(jnp.arange(rows), GRID_W).astype(F32)
    col = jnp.tile(jnp.arange(GRID_W), rows).astype(F32)
    half = HEAD_DIM // 2
    inv = 1.0 / (ROPE_THETA ** (jnp.arange(0, half, 2, dtype=F32) / half))
    ar = row[:, None] * inv
    ac = col[:, None] * inv
    ang = jnp.tile(jnp.concatenate([ar, ar, ac, ac], axis=-1), (1, LANES // HEAD_DIM))
    cos = jnp.cos(ang)
    sin = jnp.sin(ang)
    second = (jnp.arange(LANES) % half) >= (half // 2)
    return cos, jnp.where(second, sin, -sin)


def _block_diag(blocks):
    n = len(blocks)
    rows = []
    for i, b in enumerate(blocks):
        rows.append(jnp.concatenate(
            [b if j == i else jnp.zeros((b.shape[0], blocks[j].shape[1]), b.dtype) for j in range(n)],
            axis=1))
    return jnp.concatenate(rows, axis=0)


def kernel(x_prompt, x_sample, cache_k, cache_v, c, c_ctx, norm_g, w_ada, b_ada, w_in,
           q_norm_g, k_norm_g, conv_w, conv_b, pool_w, pool_scale, w_out, final_g):
    depth = w_in.shape[0]
    bp, sp, _ = x_prompt.shape
    bs, ss, _ = x_sample.shape
    n_ctx = cache_k.shape[2]

    cvec = jnp.concatenate(
        [c, c_ctx[None, :], jnp.zeros((ADA_ROWS - bs - 1, D_MODEL), F32)], axis=0)
    mod = _ada(cvec, w_ada, b_ada)
    mod = mod.reshape(depth, ADA_ROWS, 3, D_MODEL)

    w_in_bf = w_in.astype(BF16)
    w_out_bf = w_out.astype(BF16)
    tables = _rope_tables(ss)
    ones = jnp.full((HEAD_DIM, HEAD_DIM), 1.0 / HEAD_DIM, BF16)
    bd = _block_diag([ones, ones])
    ck = cache_k.transpose(0, 1, 3, 4, 2).reshape(bs, depth, KV_W, n_ctx)
    cv = cache_v.transpose(0, 1, 3, 4, 2).reshape(bs, depth, KV_W, n_ctx)

    h = x_prompt.reshape(bp * sp, D_MODEL)
    z = x_sample.reshape(bs * ss, D_MODEL)
    kv_ctx = None
    gains = jnp.concatenate(
        [jnp.tile(q_norm_g * (HEAD_DIM ** -0.5), (1, N_HEADS)), jnp.tile(k_norm_g, (1, N_KV_HEADS))],
        axis=-1).reshape(depth, 1, QK_W)
    for l in range(depth):
        wpool_bd = _block_diag([pool_w[l, g] for g in range(pool_w.shape[1])]).astype(BF16)
        final = l == depth - 1
        mix = functools.partial(
            _mix, norm_g=norm_g[l], w_in_bf=w_in_bf, wout_bf=w_out_bf, layer=l,
            conv_w=conv_w[l], conv_b=conv_b[l], wpool_bd=wpool_bd, pool_scale=pool_scale[l],
            final_g=final_g, final=final)

        q, k, v = _qkv(h, mod, bs, norm_g[l], w_in_bf, l, gains, bd, None, kv_ctx, seq=sp, tm=QKV_TILE)
        kv_ctx = (k, v)
        attn = _attention_ctx(q.reshape(bp, sp, ATTN_W), k, v, l)
        h = mix(attn, h, mod, bs, seq=sp, tm=MIX_TILE)

        q, k, v = _qkv(z, mod, None, norm_g[l], w_in_bf, l, gains, bd, tables, None, seq=ss, tm=QKV_TILE)
        attn = _attention(q.reshape(bs, ss, ATTN_W), k.reshape(bs, ss, KV_W), v.reshape(bs, ss, KV_W),
                          ck, cv, l)
        z = mix(attn, z, mod, None, seq=ss, tm=MIX_TILE)

    y_prompt = h.reshape(bp, sp, D_MODEL)
    y_sample = z.reshape(bs, ss, D_MODEL)
    new_k, new_v = (a.reshape(bp, depth, N_KV_HEADS, HEAD_DIM, sp).transpose(0, 1, 4, 2, 3) for a in kv_ctx)
    return (y_prompt, y_sample, new_k, new_v)
```

```python
import functools

import jax
import jax.numpy as jnp
from jax import lax
from jax.experimental import pallas as pl
from jax.experimental.pallas import tpu as pltpu

D_MODEL = 1024
GRID_W = 64
N_HEADS = 8
N_KV_HEADS = 2
HEAD_DIM = 64
ATTN_W = N_HEADS * HEAD_DIM
KV_W = N_KV_HEADS * HEAD_DIM
QK_W = ATTN_W + KV_W
QKV_W = QK_W + KV_W
CONV_W = 256
POOL_W = 256
POOL_GROUP = 64
IN_W = 2816
COL_ATTN_GATE = QKV_W
COL_CONV_H = COL_ATTN_GATE + ATTN_W
COL_CONV_B = COL_CONV_H + CONV_W
COL_CONV_C = COL_CONV_B + CONV_W
COL_CONV_GATE = COL_CONV_C + CONV_W
COL_POOL_U = COL_CONV_GATE + CONV_W
COL_POOL_GATE = COL_POOL_U + POOL_W
ROPE_THETA = 10000.0
EPS = 1e-6
LANES = 128
HALO = 8
ADA_ROWS = 8
VMEM_LIMIT = 48 * 1024 * 1024
ATTN_VMEM_LIMIT = 56 * 1024 * 1024

ADA_K = 256
QKV_TILE = 1024
QKV_SUB = 256
MIX_TILE = 512
ATTN_ROWS = 256
ATTN_REGION = 4
ATTN_BUFS = 2 * ATTN_REGION
ATTN_KEY_CHUNK = 256
CTX_BATCH = 4

F32 = jnp.float32
BF16 = jnp.bfloat16


def _silu(z):
    return z * (1.0 / (1.0 + jnp.exp(-z)))


def _dot(a, b):
    return jnp.dot(a, b, preferred_element_type=F32)


def _mod_norm(x, g, mod_ref):
    ms = jnp.mean(x * x, axis=-1, keepdims=True)
    gs = g * (1.0 + mod_ref[0, 0, 1:2, :])
    return x * lax.rsqrt(ms + EPS) * gs + mod_ref[0, 0, 0:1, :]


def _ada_kernel(c_ref, w_ref, b_ref, o_ref):
    part = _dot(_silu(c_ref[...]).astype(BF16), w_ref[0].astype(BF16))

    @pl.when(pl.program_id(1) == 0)
    def _first():
        o_ref[0] = part + b_ref[pl.ds(pl.program_id(0), 1), :]

    @pl.when(pl.program_id(1) > 0)
    def _rest():
        o_ref[0] += part


def _ada(cvec, w_ada, b_ada):
    depth, _, n3 = w_ada.shape
    tk = ADA_K
    return pl.pallas_call(
        _ada_kernel,
        grid=(depth, D_MODEL // tk),
        in_specs=[
            pl.BlockSpec((ADA_ROWS, tk), lambda l, k: (0, k)),
            pl.BlockSpec((1, tk, n3), lambda l, k: (l, k, 0)),
            pl.BlockSpec((depth, n3), lambda l, k: (0, 0)),
        ],
        out_specs=pl.BlockSpec((1, ADA_ROWS, n3), lambda l, k: (l, 0, 0)),
        out_shape=jax.ShapeDtypeStruct((depth, ADA_ROWS, n3), F32),
        compiler_params=pltpu.CompilerParams(dimension_semantics=("arbitrary", "arbitrary")),
        name="ada_mod",
    )(cvec, w_ada, b_ada)


def _qkv_kernel(*refs, rope, nsub, first_layer):
    if rope:
        (x_ref, mod_ref, g_ref, w_ref, gains_ref, bd_ref, cos_ref, sin_ref,
         q_ref, k_ref, v_ref) = refs
    else:
        x_ref, mod_ref, g_ref, w_ref, gains_ref, bd_ref = refs[:6]
        q_ref, k_ref, v_ref = refs[-3:]
    tm = x_ref.shape[0]
    sub = tm // nsub
    bd = bd_ref[...]
    g = g_ref[...]
    if rope:
        lane = lax.broadcasted_iota(jnp.int32, (sub, LANES), 1)
        second = (lane % (HEAD_DIM // 2)) >= (HEAD_DIM // 4)
    k_parts, v_parts = [None] * nsub, [None] * nsub

    def finish(s, c, part):
        rows = slice(s * sub, (s + 1) * sub)
        for h in range(2):
            j = 2 * c + h
            t = part[:, h * LANES:(h + 1) * LANES]
            if j * LANES >= QK_W:
                v_parts[s] = t
                continue
            sl = slice(j * LANES, (j + 1) * LANES)
            sq = t * t
            hi = sq.astype(BF16)
            lo = (sq - hi.astype(F32)).astype(BF16)
            msq = _dot(hi, bd) + _dot(lo, bd)
            t = t * lax.rsqrt(msq + EPS) * gains_ref[0, :, sl]
            if rope:
                partner = jnp.where(second, pltpu.roll(t, HEAD_DIM // 4, 1),
                                    pltpu.roll(t, LANES - HEAD_DIM // 4, 1))
                t = t * cos_ref[rows, :] + partner * sin_ref[rows, :]
            if j * LANES < ATTN_W:
                q_ref[rows, sl] = t.astype(BF16)
            else:
                k_parts[s] = t

    ncol = QKV_W // (2 * LANES)
    prev = None
    for s in range(nsub + 1):
        if s < nsub:
            xn = _mod_norm(x_ref[s * sub:(s + 1) * sub, :], g, mod_ref).astype(BF16)
        cur = []
        for c in range(ncol):
            if s < nsub:
                cur.append(_dot(xn, w_ref[0, :, c * 2 * LANES:(c + 1) * 2 * LANES]))
            if prev is not None:
                finish(s - 1, c, prev[c])
        prev = cur
    for ref, parts in ((k_ref, k_parts), (v_ref, v_parts)):
        val = jnp.concatenate(parts, axis=0)
        if rope:
            ref[...] = val
            continue
        seq = ref.shape[3]
        for p in range(ref.shape[0]):
            piece = val[p * seq:(p + 1) * seq, :].T
            if first_layer is None:
                ref[p, 0] = piece
            else:
                for l2 in range(ref.shape[1]):
                    ref[p, l2] = piece if l2 == first_layer else jnp.zeros_like(piece)


def _mod_spec(layer, mod_row, tps):
    if mod_row is None:
        return pl.BlockSpec((1, 1, 3, D_MODEL), lambda i: (layer, i // tps, 0, 0))
    return pl.BlockSpec((1, 1, 3, D_MODEL), lambda i: (layer, mod_row, 0, 0))


def _qkv(x2, mod, mod_row, norm_g, w_in_bf, layer, gains, bd, tables, kv_prev, *, seq, tm):
    tokens = x2.shape[0]
    depth = w_in_bf.shape[0]
    tps = max(seq // tm, 1)
    rope = tables is not None
    row = lambda i: (i, 0)
    const = lambda i: (0, 0)
    in_specs = [
        pl.BlockSpec((tm, D_MODEL), row),
        _mod_spec(layer, mod_row, tps),
        pl.BlockSpec((1, D_MODEL), const),
        pl.BlockSpec((1, D_MODEL, QKV_W), lambda i: (layer, 0, 0)),
        pl.BlockSpec((1, 1, QK_W), lambda i: (layer, 0, 0)),
        pl.BlockSpec((LANES, LANES), const),
    ]
    args = [x2, mod, norm_g.reshape(1, D_MODEL), w_in_bf, gains, bd]
    out_shapes = [jax.ShapeDtypeStruct((tokens, ATTN_W), BF16)]
    out_specs = [pl.BlockSpec((tm, ATTN_W), row)]
    aliases = {}
    first_layer = None
    if rope:
        in_specs += [pl.BlockSpec((tm, LANES), lambda i: (i % tps, 0))] * 2
        args += list(tables)
        out_shapes += [jax.ShapeDtypeStruct((tokens, KV_W), F32)] * 2
        out_specs += [pl.BlockSpec((tm, KV_W), row)] * 2
    else:
        nseq = tm // seq
        out_shapes += [jax.ShapeDtypeStruct((tokens // seq, depth, KV_W, seq), F32)] * 2
        if kv_prev is None:
            first_layer = layer
            out_specs += [pl.BlockSpec((nseq, depth, KV_W, seq), lambda i: (i, 0, 0, 0))] * 2
        else:
            out_specs += [pl.BlockSpec((nseq, 1, KV_W, seq), lambda i: (i, layer, 0, 0))] * 2
            in_specs += [pl.BlockSpec(memory_space=pl.ANY)] * 2
            aliases = {len(args): 1, len(args) + 1: 2}
            args += list(kv_prev)
    return pl.pallas_call(
        functools.partial(_qkv_kernel, rope=rope, nsub=tm // QKV_SUB, first_layer=first_layer),
        grid=(tokens // tm,),
        in_specs=in_specs,
        out_specs=out_specs,
        out_shape=out_shapes,
        input_output_aliases=aliases,
        compiler_params=pltpu.CompilerParams(
            dimension_semantics=("arbitrary",), vmem_limit_bytes=VMEM_LIMIT),
        name="qkv_rope" if rope else "qkv",
    )(*args)


VT_ROWS = HEAD_DIM + 16


def _attn_kernel(q_ref, k_ref, v_ref, ck_ref, cv_ref, o_ref, kab, vt, q2, acct, mbuf, *bufs, n, n_ctx, rb):
    kv = pl.program_id(1)
    nr = 2 * n // rb
    t_all = n + n_ctx
    sbuf, pbuf = bufs[:ATTN_BUFS], bufs[ATTN_BUFS:]

    def fill(k_src, v_t, r0, m):
        lane = lax.broadcasted_iota(jnp.int32, (m, LANES), 1)
        k_own = jnp.where((lane // HEAD_DIM) == kv, k_src, 0.0)
        k_oth = pltpu.roll(k_own, HEAD_DIM, 1)
        is0 = kv == 0
        kab[0, r0:r0 + m, :] = jnp.where(is0, k_own, k_oth).astype(BF16)
        kab[1, r0:r0 + m, :] = jnp.where(is0, k_oth, k_own).astype(BF16)
        vt[0:HEAD_DIM, r0:r0 + m] = jnp.where(is0, v_t[0:HEAD_DIM, :], v_t[HEAD_DIM:, :]).astype(BF16)

    piece = 256
    for r0 in range(0, n, piece):
        fill(k_ref[0, r0:r0 + piece, :], v_ref[0, r0:r0 + piece, :].T, r0, piece)
    fill(ck_ref[0, 0].T, cv_ref[0, 0], n, n_ctx)
    vt[HEAD_DIM:, :] = jnp.ones((VT_ROWS - HEAD_DIM, t_all), BF16)
    for c0 in range(0, n, piece):
        q2[:, c0:c0 + piece] = q_ref[0, c0:c0 + piece, 0:LANES].astype(F32).T.astype(BF16)
        q2[:, n + c0:n + c0 + piece] = q_ref[0, c0:c0 + piece, LANES:].astype(F32).T.astype(BF16)

    span = ATTN_REGION
    n_regions = 2 * nr // span
    chunks = [slice(k0, k0 + ATTN_KEY_CHUNK) for k0 in range(0, t_all, ATTN_KEY_CHUNK)]

    def score_chunk(slot, ks, qr, buf, run):
        sc = _dot(kab[slot, ks, :], qr)
        sbuf[buf][ks, :] = sc
        while sc.shape[0] > 8:
            half = sc.shape[0] // 2
            sc = jnp.maximum(sc[0:half, :], sc[half:, :])
        return sc if run is None else jnp.maximum(run, sc)

    def q_rows(r):
        return q2[:, pl.ds(pl.multiple_of(r * rb, rb), rb)]

    per_tile = n // rb

    def flush(r):
        r0 = (r % per_tile) * rb
        if not isinstance(r, int):
            r0 = pl.multiple_of(r0, rb)
        o_ref[0, r // per_tile, pl.ds(r0, rb), :] = acct[r].T

    def region(g, parity, do_scores, do_numer, do_weigh, do_flush):
        cur, oth = parity * span, (1 - parity) * span
        for j in range(span):
            slot = j % 2
            if do_flush and slot == 0:
                flush((g - 2) * (span // 2) + j // 2)
            if do_numer:
                m = mbuf[cur + j]
            if do_scores:
                qr = q_rows((g + 1) * (span // 2) + j // 2)
            run, res = None, None
            for ks in chunks:
                if do_scores:
                    run = score_chunk(slot, ks, qr, oth + j, run)
                if do_numer:
                    pbuf[cur + j][ks, :] = jnp.exp(sbuf[cur + j][ks, :] - m).astype(BF16)
                if do_weigh:
                    part = _dot(vt[:, ks], pbuf[oth + j][ks, :])
                    res = part if res is None else res + part
            if do_scores:
                mbuf[oth + j] = jnp.max(run, axis=0, keepdims=True)
            if do_weigh:
                acct[(g - 1) * (span // 2) + j // 2, slot * HEAD_DIM:(slot + 1) * HEAD_DIM, :] = (
                    res[0:HEAD_DIM, :] * (1.0 / res[HEAD_DIM:HEAD_DIM + 1, :]))

    region(-1, 1, True, False, False, False)
    region(0, 0, True, True, False, False)

    def body(g, carry):
        for parity in range(2):
            pl.when(g % 2 == parity)(lambda: region(g, parity, True, True, True, False))
        return carry

    lax.fori_loop(1, n_regions - 1, body, 0)
    region(n_regions - 1, (n_regions - 1) % 2, False, True, True, False)
    region(n_regions, n_regions % 2, False, False, True, False)
    for r in range(nr):
        flush(r)


def _attention(q, k, v, cache_k, cache_v, layer):
    rb = ATTN_ROWS
    bsz, n, _ = q.shape
    n_ctx = cache_k.shape[3]
    t_all = n + n_ctx
    gw = 2 * LANES
    qmap = lambda b, g: (b, 0, g)
    kmap = lambda b, g: (b, 0, 0)
    cmap = lambda b, g: (b, layer, 0, 0)
    return pl.pallas_call(
        functools.partial(_attn_kernel, n=n, n_ctx=n_ctx, rb=rb),
        grid=(bsz, N_KV_HEADS),
        in_specs=[
            pl.BlockSpec((1, n, gw), qmap),
            pl.BlockSpec((1, n, KV_W), kmap),
            pl.BlockSpec((1, n, KV_W), kmap),
            pl.BlockSpec((1, 1, KV_W, n_ctx), cmap),
            pl.BlockSpec((1, 1, KV_W, n_ctx), cmap),
        ],
        out_specs=pl.BlockSpec((1, 2, n, LANES), lambda b, g: (b, g, 0, 0)),
        out_shape=jax.ShapeDtypeStruct((bsz, ATTN_W // LANES, n, LANES), F32),
        scratch_shapes=[
            pltpu.VMEM((2, t_all, LANES), BF16),
            pltpu.VMEM((VT_ROWS, t_all), BF16),
            pltpu.VMEM((LANES, 2 * n), BF16),
            pltpu.VMEM((2 * n // rb, LANES, rb), F32),
            pltpu.VMEM((ATTN_BUFS, 1, rb), F32),
        ] + [pltpu.VMEM((t_all, rb), F32)] * ATTN_BUFS + [pltpu.VMEM((t_all, rb), BF16)] * ATTN_BUFS,
        compiler_params=pltpu.CompilerParams(
            dimension_semantics=("arbitrary", "arbitrary"), vmem_limit_bytes=ATTN_VMEM_LIMIT),
        name="attn_lat",
    )(q, k, v, cache_k, cache_v)


def _attn_ctx_kernel(q_ref, k_ref, v_ref, o_ref, *, n):
    low = lax.broadcasted_iota(jnp.int32, (n, LANES), 1) < HEAD_DIM
    ones_lo = jnp.where(low, 1.0, 0.0)
    zk = jnp.zeros((HEAD_DIM, n), F32)
    gw = 2 * LANES
    for b in range(CTX_BATCH):
        kt_all = k_ref[b, 0]
        v = v_ref[b, 0].T
        vr = pltpu.roll(v, HEAD_DIM, 1)
        for g in range(N_KV_HEADS):
            kg = kt_all[g * HEAD_DIM:(g + 1) * HEAD_DIM, :]
            kts = (jnp.concatenate([kg, zk], axis=0).astype(BF16),
                   jnp.concatenate([zk, kg], axis=0).astype(BF16))
            v_lo, v_hi = (v, vr) if g == 0 else (vr, v)
            vs = (jnp.concatenate([jnp.where(low, v_lo, 0.0), ones_lo], axis=1).astype(BF16),
                  jnp.concatenate([jnp.where(low, 0.0, v_hi), 1.0 - ones_lo], axis=1).astype(BF16))
            q2 = jnp.concatenate([q_ref[b, :, g * gw:g * gw + LANES],
                                  q_ref[b, :, g * gw + LANES:(g + 1) * gw]], axis=0)
            acc = None
            for kt_s, v_s in zip(kts, vs):
                s = _dot(q2, kt_s)
                m = jnp.max(s, axis=-1, keepdims=True)
                pv = _dot(jnp.exp(s - m).astype(BF16), v_s)
                acc = pv if acc is None else acc + pv
            o = acc[:, 0:LANES] / acc[:, LANES:]
            o_ref[b, 2 * g] = o[0:n]
            o_ref[b, 2 * g + 1] = o[n:]


def _attention_ctx(q, k, v, layer):
    bsz, n, _ = q.shape
    blk = lambda w: pl.BlockSpec((CTX_BATCH, n, w), lambda b: (b, 0, 0))
    kv_blk = pl.BlockSpec((CTX_BATCH, 1, KV_W, n), lambda b: (b, layer, 0, 0))
    return pl.pallas_call(
        functools.partial(_attn_ctx_kernel, n=n),
        grid=(bsz // CTX_BATCH,),
        in_specs=[blk(ATTN_W), kv_blk, kv_blk],
        out_specs=pl.BlockSpec((CTX_BATCH, ATTN_W // LANES, n, LANES), lambda b: (b, 0, 0, 0)),
        out_shape=jax.ShapeDtypeStruct((bsz, ATTN_W // LANES, n, LANES), F32),
        compiler_params=pltpu.CompilerParams(
            dimension_semantics=("arbitrary",), vmem_limit_bytes=VMEM_LIMIT),
        name="attn_ctx",
    )(q, k, v)


def _mix_kernel(attn_ref, x_ref, xp_ref, xn_ref, mod_ref, g_ref, w_ref, wout_ref,
                convw_ref, convb_ref, wpool_ref, pscale_ref, fg_ref,
                o_ref, xe, yext, uext, *, tm, tps, seq, final):
    j = pl.program_id(0) % tps
    pm = jnp.where(j == 0, 0.0, 1.0)
    nm = jnp.where(j == tps - 1, 0.0, 1.0)
    c = CONV_W
    g = g_ref[...]
    x = x_ref[...]
    xe[0:tm, :] = _mod_norm(x, g, mod_ref).astype(BF16)
    if tps > 1:
        halo = jnp.concatenate([xn_ref[...], xp_ref[...]], axis=0)
        xe[tm:, :] = _mod_norm(halo, g, mod_ref).astype(BF16)

    def proj(rows, off, width):
        return _dot(xe[0:rows, :], w_ref[0, :, off:off + width])

    nseq = max(tm // seq, 1)
    whole = tps == 1
    ln = tm // nseq
    hrows = tm if whole else tm + 2 * HALO
    h_all = proj(hrows, COL_CONV_H, c)
    c_all = proj(hrows, COL_CONV_C, c)
    u_all = proj(hrows, COL_POOL_U, POOL_W)
    z_a = proj(tm, COL_ATTN_GATE, ATTN_W)
    a = ATTN_W
    attn = jnp.concatenate(
        [jnp.concatenate([attn_ref[p, t] for t in range(attn_ref.shape[1])], axis=1)
         for p in range(attn_ref.shape[0])], axis=0)
    attn_o = (attn * _silu(z_a)).astype(BF16)
    out_a = _dot(attn_o, wout_ref[0, 0:a, :])
    b_c = proj(tm, COL_CONV_B, c)
    z_c = proj(tm, COL_CONV_GATE, c)
    z_p = proj(tm, COL_POOL_GATE, POOL_W)

    def extend(ext, p, val):
        zeros = jnp.zeros((HALO, val.shape[1]), F32)
        ext[p, 0:HALO, :] = zeros if whole else val[tm + HALO:, :] * pm
        ext[p, HALO:HALO + ln, :] = val[p * ln:(p + 1) * ln, :]
        ext[p, HALO + ln:, :] = zeros if whole else val[tm:tm + HALO, :] * nm

    y = h_all * c_all
    convs = []
    for p in range(nseq):
        extend(yext, p, y)
        convs.append(yext[p, HALO - 1:HALO - 1 + ln, :] * convw_ref[0:1, :]
                     + yext[p, HALO:HALO + ln, :] * convw_ref[1:2, :]
                     + yext[p, HALO + 1:HALO + 1 + ln, :] * convw_ref[2:3, :] + convb_ref[...])
    conv = convs[0] if nseq == 1 else jnp.concatenate(convs, axis=0)
    conv_o = (b_c * conv * _silu(z_c)).astype(BF16)

    lane = lax.broadcasted_iota(jnp.int32, (ln, LANES), 1)
    pos = lax.broadcasted_iota(jnp.int32, (ln, LANES), 0) + (0 if whole else j * tm)
    low = lane < POOL_GROUP
    d_rows = []
    for p in range(nseq):
        extend(uext, p, u_all)

        def win(k0, k1, sl):
            tot = None
            for k in range(k0, k1):
                t = uext[p, HALO + k:HALO + k + ln, sl]
                tot = t if tot is None else tot + t
            return tot

        ds = []
        for ti, (wa, wb) in enumerate(((2, 4), (8, 16))):
            sl = slice(ti * LANES, (ti + 1) * LANES)
            s_a = win(-(wa // 2), wa // 2, sl)
            s_b = s_a + win(-(wb // 2), -(wa // 2), sl) + win(wa // 2, wb // 2, sl)
            half = jnp.where(low, wa // 2, wb // 2)
            cnt = jnp.minimum(pos + half - 1, seq - 1) - jnp.maximum(pos - half, 0) + 1
            tot = jnp.where(low, s_a, s_b)
            ds.append(tot / cnt.astype(F32) - u_all[p * ln:(p + 1) * ln, sl])
        d_rows.append(jnp.concatenate(ds, axis=-1))
    d = (d_rows[0] if nseq == 1 else jnp.concatenate(d_rows, axis=0)).astype(BF16)
    pool_o = (_dot(d, wpool_ref[...]) * pscale_ref[...] * _silu(z_p)).astype(BF16)

    out = out_a + _dot(conv_o, wout_ref[0, a:a + c, :]) + _dot(pool_o, wout_ref[0, a + c:, :])
    res = x + mod_ref[0, 0, 2:3, :] * out
    if final:
        ms = jnp.mean(res * res, axis=-1, keepdims=True)
        res = res * lax.rsqrt(ms + EPS) * fg_ref[...]
    o_ref[...] = res


def _mix(attn2, x2, mod, mod_row, norm_g, w_in_bf, wout_bf, layer, conv_w, conv_b, wpool_bd, pool_scale,
         final_g, *, seq, tm, final):
    tokens = x2.shape[0]
    tps = max(seq // tm, 1)
    nseq = max(tm // seq, 1)
    hb = tm // HALO
    nhb = tokens // HALO
    row = lambda i: (i, 0)
    const = lambda i: (0, 0)
    in_specs = [
        (pl.BlockSpec((nseq, ATTN_W // LANES, tm // nseq, LANES), lambda i: (i, 0, 0, 0)) if tps == 1 else
         pl.BlockSpec((1, ATTN_W // LANES, tm, LANES), lambda i: (i // tps, 0, i % tps, 0))),
        pl.BlockSpec((tm, D_MODEL), row),
        pl.BlockSpec((HALO, D_MODEL), lambda i: (jnp.maximum(i * hb - 1, 0), 0)),
        pl.BlockSpec((HALO, D_MODEL), lambda i: (jnp.minimum((i + 1) * hb, nhb - 1), 0)),
        _mod_spec(layer, mod_row, tps),
        pl.BlockSpec((1, D_MODEL), const),
        pl.BlockSpec((1, D_MODEL, IN_W), lambda i: (layer, 0, 0)),
        pl.BlockSpec((1, D_MODEL, D_MODEL), lambda i: (layer, 0, 0)),
        pl.BlockSpec((3, CONV_W), const),
        pl.BlockSpec((1, CONV_W), const),
        pl.BlockSpec((POOL_W, POOL_W), const),
        pl.BlockSpec((1, POOL_W), const),
        pl.BlockSpec((1, D_MODEL), const),
    ]
    return pl.pallas_call(
        functools.partial(_mix_kernel, tm=tm, tps=tps, seq=seq, final=final),
        grid=(tokens // tm,),
        in_specs=in_specs,
        out_specs=pl.BlockSpec((tm, D_MODEL), row),
        out_shape=jax.ShapeDtypeStruct((tokens, D_MODEL), F32),
        scratch_shapes=[
            pltpu.VMEM((tm + 2 * HALO, D_MODEL), BF16),
            pltpu.VMEM((nseq, tm // nseq + 2 * HALO, CONV_W), F32),
            pltpu.VMEM((nseq, tm // nseq + 2 * HALO, POOL_W), F32),
        ],
        compiler_params=pltpu.CompilerParams(
            dimension_semantics=("arbitrary",), vmem_limit_bytes=VMEM_LIMIT),
        name="mix_final" if final else "mix",
    )(attn2, x2, x2, x2, mod, norm_g.reshape(1, D_MODEL), w_in_bf, wout_bf,
      conv_w, conv_b.reshape(1, CONV_W), wpool_bd, pool_scale.reshape(1, POOL_W),
      final_g.reshape(1, D_MODEL))


def _rope_tables(n):
    rows = n // GRID_W
    row = jnp.repeat(jnp.arange(rows), GRID_W).astype(F32)
    col = jnp.tile(jnp.arange(GRID_W), rows).astype(F32)
    half = HEAD_DIM // 2
    inv = 1.0 / (ROPE_THETA ** (jnp.arange(0, half, 2, dtype=F32) / half))
    ar = row[:, None] * inv
    ac = col[:, None] * inv
    ang = jnp.tile(jnp.concatenate([ar, ar, ac, ac], axis=-1), (1, LANES // HEAD_DIM))
    cos = jnp.cos(ang)
    sin = jnp.sin(ang)
    second = (jnp.arange(LANES) % half) >= (half // 2)
    return cos, jnp.where(second, sin, -sin)


def _block_diag(blocks):
    n = len(blocks)
    rows = []
    for i, b in enumerate(blocks):
        rows.append(jnp.concatenate(
            [b if j == i else jnp.zeros((b.shape[0], blocks[j].shape[1]), b.dtype) for j in range(n)],
            axis=1))
    return jnp.concatenate(rows, axis=0)


def kernel(x_prompt, x_sample, cache_k, cache_v, c, c_ctx, norm_g, w_ada, b_ada, w_in,
           q_norm_g, k_norm_g, conv_w, conv_b, pool_w, pool_scale, w_out, final_g):
    depth = w_in.shape[0]
    bp, sp, _ = x_prompt.shape
    bs, ss, _ = x_sample.shape
    n_ctx = cache_k.shape[2]

    cvec = jnp.concatenate(
        [c, c_ctx[None, :], jnp.zeros((ADA_ROWS - bs - 1, D_MODEL), F32)], axis=0)
    mod = _ada(cvec, w_ada, b_ada)
    mod = mod.reshape(depth, ADA_ROWS, 3, D_MODEL)

    w_in_bf = w_in.astype(BF16)
    w_out_bf = w_out.astype(BF16)
    tables = _rope_tables(ss)
    ones = jnp.full((HEAD_DIM, HEAD_DIM), 1.0 / HEAD_DIM, BF16)
    bd = _block_diag([ones, ones])
    ck = cache_k.transpose(0, 1, 3, 4, 2).reshape(bs, depth, KV_W, n_ctx)
    cv = cache_v.transpose(0, 1, 3, 4, 2).reshape(bs, depth, KV_W, n_ctx)

    h = x_prompt.reshape(bp * sp, D_MODEL)
    z = x_sample.reshape(bs * ss, D_MODEL)
    kv_ctx = None
    gains = jnp.concatenate(
        [jnp.tile(q_norm_g * (HEAD_DIM ** -0.5), (1, N_HEADS)), jnp.tile(k_norm_g, (1, N_KV_HEADS))],
        axis=-1).reshape(depth, 1, QK_W)
    for l in range(depth):
        wpool_bd = _block_diag([pool_w[l, g] for g in range(pool_w.shape[1])]).astype(BF16)
        final = l == depth - 1
        mix = functools.partial(
            _mix, norm_g=norm_g[l], w_in_bf=w_in_bf, wout_bf=w_out_bf, layer=l,
            conv_w=conv_w[l], conv_b=conv_b[l], wpool_bd=wpool_bd, pool_scale=pool_scale[l],
            final_g=final_g, final=final)

        q, k, v = _qkv(h, mod, bs, norm_g[l], w_in_bf, l, gains, bd, None, kv_ctx, seq=sp, tm=QKV_TILE)
        kv_ctx = (k, v)
        attn = _attention_ctx(q.reshape(bp, sp, ATTN_W), k, v, l)
        h = mix(attn, h, mod, bs, seq=sp, tm=MIX_TILE)

        q, k, v = _qkv(z, mod, None, norm_g[l], w_in_bf, l, gains, bd, tables, None, seq=ss, tm=QKV_TILE)
        attn = _attention(q.reshape(bs, ss, ATTN_W), k.reshape(bs, ss, KV_W), v.reshape(bs, ss, KV_W),
                          ck, cv, l)
        z = mix(attn, z, mod, None, seq=ss, tm=MIX_TILE)

    y_prompt = h.reshape(bp, sp, D_MODEL)
    y_sample = z.reshape(bs, ss, D_MODEL)
    new_k, new_v = (a.reshape(bp, depth, N_KV_HEADS, HEAD_DIM, sp).transpose(0, 1, 4, 2, 3) for a in kv_ctx)
    return (y_prompt, y_sample, new_k, new_v)
```

```python
import functools

import jax
import jax.numpy as jnp
import numpy as np
from jax import lax
from jax.experimental import pallas as pl
from jax.experimental.pallas import tpu as pltpu

D_MODEL = 1024
GRID_W = 64
N_HEADS = 8
N_KV_HEADS = 2
HEAD_DIM = 64
ATTN_W = N_HEADS * HEAD_DIM
KV_W = N_KV_HEADS * HEAD_DIM
QK_W = ATTN_W + KV_W
QKV_W = QK_W + KV_W
CONV_W = 256
POOL_W = 256
POOL_GROUP = 64
IN_W = 2816
COL_ATTN_GATE = QKV_W
COL_CONV_H = COL_ATTN_GATE + ATTN_W
COL_CONV_B = COL_CONV_H + CONV_W
COL_CONV_C = COL_CONV_B + CONV_W
COL_CONV_GATE = COL_CONV_C + CONV_W
COL_POOL_U = COL_CONV_GATE + CONV_W
COL_POOL_GATE = COL_POOL_U + POOL_W
ROPE_THETA = 10000.0
EPS = 1e-6
LANES = 128
HALO = 8
ADA_ROWS = 8
VMEM_LIMIT = 48 * 1024 * 1024
ATTN_VMEM_LIMIT = 56 * 1024 * 1024

ADA_K = 512
QKV_TILE = 1024
QKV_SUB = 256
MIX_TILE = 512
ATTN_ROWS = 256
ATTN_REGION = 4
ATTN_BUFS = 2 * ATTN_REGION
ATTN_KEY_CHUNK = 256
CTX_BATCH = 4

F32 = jnp.float32
BF16 = jnp.bfloat16


def _silu(z):
    return z * (1.0 / (1.0 + jnp.exp(-z)))


def _dot(a, b):
    return jnp.dot(a, b, preferred_element_type=F32)


def _mod_row(mod_ref, mod_row, tps):
    row = mod_row if mod_row is not None else pl.program_id(0) // tps
    return mod_ref[0, pl.ds(row, 1), :]


def _mod_norm(x, g, mod):
    ms = jnp.mean(x * x, axis=-1, keepdims=True)
    gs = g * (1.0 + mod[:, D_MODEL:2 * D_MODEL])
    return x * lax.rsqrt(ms + EPS) * gs + mod[:, 0:D_MODEL]


def _ada_kernel(c_ref, w_ref, b_ref, o_ref):
    part = _dot(_silu(c_ref[...]).astype(BF16), w_ref[0].astype(BF16))

    @pl.when(pl.program_id(1) == 0)
    def _first():
        o_ref[0] = part + b_ref[pl.ds(pl.program_id(0), 1), :]

    @pl.when(pl.program_id(1) > 0)
    def _rest():
        o_ref[0] += part


def _ada(cvec, w_ada, b_ada):
    depth, _, n3 = w_ada.shape
    tk = ADA_K
    return pl.pallas_call(
        _ada_kernel,
        grid=(depth, D_MODEL // tk),
        in_specs=[
            pl.BlockSpec((ADA_ROWS, tk), lambda l, k: (0, k)),
            pl.BlockSpec((1, tk, n3), lambda l, k: (l, k, 0)),
            pl.BlockSpec((depth, n3), lambda l, k: (0, 0)),
        ],
        out_specs=pl.BlockSpec((1, ADA_ROWS, n3), lambda l, k: (l, 0, 0)),
        out_shape=jax.ShapeDtypeStruct((depth, ADA_ROWS, n3), F32),
        compiler_params=pltpu.CompilerParams(dimension_semantics=("arbitrary", "arbitrary")),
        name="ada_mod",
    )(cvec, w_ada, b_ada)


def _qkv_kernel(*refs, rope, nsub, first_layer, mod_row, tps):
    if rope:
        (x_ref, mod_ref, g_ref, w_ref, gains_ref, bd_ref, cos_ref, sin_ref,
         q_ref, k_ref, v_ref) = refs
    else:
        x_ref, mod_ref, g_ref, w_ref, gains_ref, bd_ref = refs[:6]
        q_ref, k_ref, v_ref = refs[-3:]
    tm = x_ref.shape[0]
    sub = tm // nsub
    bd = bd_ref[...]
    g = g_ref[...]
    mod = _mod_row(mod_ref, mod_row, tps)
    if rope:
        lane = lax.broadcasted_iota(jnp.int32, (sub, LANES), 1)
        second = (lane % (HEAD_DIM // 2)) >= (HEAD_DIM // 4)
    k_parts, v_parts = [None] * nsub, [None] * nsub

    def finish(s, c, part):
        rows = slice(s * sub, (s + 1) * sub)
        for h in range(2):
            j = 2 * c + h
            t = part[:, h * LANES:(h + 1) * LANES]
            if j * LANES >= QK_W:
                v_parts[s] = t
                continue
            sl = slice(j * LANES, (j + 1) * LANES)
            sq = t * t
            hi = sq.astype(BF16)
            lo = (sq - hi.astype(F32)).astype(BF16)
            msq = _dot(hi, bd) + _dot(lo, bd)
            t = t * lax.rsqrt(msq + EPS) * gains_ref[0, :, sl]
            if rope:
                partner = jnp.where(second, pltpu.roll(t, HEAD_DIM // 4, 1),
                                    pltpu.roll(t, LANES - HEAD_DIM // 4, 1))
                t = t * cos_ref[rows, :] + partner * sin_ref[rows, :]
            if j * LANES < ATTN_W:
                q_ref[rows, sl] = t.astype(BF16)
            else:
                k_parts[s] = t

    ncol = QKV_W // (2 * LANES)
    prev = None
    for s in range(nsub + 1):
        if s < nsub:
            xn = _mod_norm(x_ref[s * sub:(s + 1) * sub, :], g, mod).astype(BF16)
        cur = []
        for c in range(ncol):
            if s < nsub:
                cur.append(_dot(xn, w_ref[0, :, c * 2 * LANES:(c + 1) * 2 * LANES]))
            if prev is not None:
                finish(s - 1, c, prev[c])
        prev = cur
    for ref, parts in ((k_ref, k_parts), (v_ref, v_parts)):
        val = jnp.concatenate(parts, axis=0)
        if rope:
            ref[...] = val
            continue
        seq = ref.shape[3]
        for p in range(ref.shape[0]):
            piece = val[p * seq:(p + 1) * seq, :].T
            if first_layer is None:
                ref[p, 0] = piece
            else:
                for l2 in range(ref.shape[1]):
                    ref[p, l2] = piece if l2 == first_layer else jnp.zeros_like(piece)


def _mod_spec(layer):
    return pl.BlockSpec((1, ADA_ROWS, 3 * D_MODEL), lambda i: (layer, 0, 0))


def _qkv(x2, mod, mod_row, norm_g, w_in_bf, layer, gains, bd, tables, kv_prev, *, seq, tm):
    tokens = x2.shape[0]
    depth = w_in_bf.shape[0]
    tps = max(seq // tm, 1)
    rope = tables is not None
    row = lambda i: (i, 0)
    const = lambda i: (0, 0)
    in_specs = [
        pl.BlockSpec((tm, D_MODEL), row),
        _mod_spec(layer),
        pl.BlockSpec((1, D_MODEL), const),
        pl.BlockSpec((1, D_MODEL, QKV_W), lambda i: (layer, 0, 0)),
        pl.BlockSpec((1, 1, QK_W), lambda i: (layer, 0, 0)),
        pl.BlockSpec((LANES, LANES), const),
    ]
    args = [x2, mod, norm_g.reshape(1, D_MODEL), w_in_bf, gains, bd]
    out_shapes = [jax.ShapeDtypeStruct((tokens, ATTN_W), BF16)]
    out_specs = [pl.BlockSpec((tm, ATTN_W), row)]
    aliases = {}
    first_layer = None
    if rope:
        in_specs += [pl.BlockSpec((tm, LANES), lambda i: (i % tps, 0))] * 2
        args += list(tables)
        out_shapes += [jax.ShapeDtypeStruct((tokens, KV_W), F32)] * 2
        out_specs += [pl.BlockSpec((tm, KV_W), row)] * 2
    else:
        nseq = tm // seq
        out_shapes += [jax.ShapeDtypeStruct((tokens // seq, depth, KV_W, seq), F32)] * 2
        if kv_prev is None:
            first_layer = layer
            out_specs += [pl.BlockSpec((nseq, depth, KV_W, seq), lambda i: (i, 0, 0, 0))] * 2
        else:
            out_specs += [pl.BlockSpec((nseq, 1, KV_W, seq), lambda i: (i, layer, 0, 0))] * 2
            in_specs += [pl.BlockSpec(memory_space=pl.ANY)] * 2
            aliases = {len(args): 1, len(args) + 1: 2}
            args += list(kv_prev)
    return pl.pallas_call(
        functools.partial(_qkv_kernel, rope=rope, nsub=tm // QKV_SUB, first_layer=first_layer,
                          mod_row=mod_row, tps=tps),
        grid=(tokens // tm,),
        in_specs=in_specs,
        out_specs=out_specs,
        out_shape=out_shapes,
        input_output_aliases=aliases,
        compiler_params=pltpu.CompilerParams(
            dimension_semantics=("arbitrary",), vmem_limit_bytes=VMEM_LIMIT),
        name="qkv_rope" if rope else "qkv",
    )(*args)


VT_ROWS = HEAD_DIM + 16


def _attn_kernel(q_ref, k_ref, v_ref, ck_ref, cv_ref, o_ref, kab, vt, q2, acct, mbuf, *bufs, n, n_ctx, rb):
    kv = pl.program_id(1)
    nr = 2 * n // rb
    t_all = n + n_ctx
    sbuf, pbuf = bufs[:ATTN_BUFS], bufs[ATTN_BUFS:]

    def fill(k_src, v_t, r0, m):
        lane = lax.broadcasted_iota(jnp.int32, (m, LANES), 1)
        k_own = jnp.where((lane // HEAD_DIM) == kv, k_src, 0.0)
        k_oth = pltpu.roll(k_own, HEAD_DIM, 1)
        is0 = kv == 0
        kab[0, r0:r0 + m, :] = jnp.where(is0, k_own, k_oth).astype(BF16)
        kab[1, r0:r0 + m, :] = jnp.where(is0, k_oth, k_own).astype(BF16)
        vt[0:HEAD_DIM, r0:r0 + m] = jnp.where(is0, v_t[0:HEAD_DIM, :], v_t[HEAD_DIM:, :]).astype(BF16)

    piece = 256
    for r0 in range(0, n, piece):
        fill(k_ref[0, r0:r0 + piece, :], v_ref[0, r0:r0 + piece, :].T, r0, piece)
    fill(ck_ref[0, 0].T, cv_ref[0, 0], n, n_ctx)
    vt[HEAD_DIM:, :] = jnp.ones((VT_ROWS - HEAD_DIM, t_all), BF16)
    for c0 in range(0, n, piece):
        q2[:, c0:c0 + piece] = q_ref[0, c0:c0 + piece, 0:LANES].astype(F32).T.astype(BF16)
        q2[:, n + c0:n + c0 + piece] = q_ref[0, c0:c0 + piece, LANES:].astype(F32).T.astype(BF16)

    span = ATTN_REGION
    n_regions = 2 * nr // span
    chunks = [slice(k0, k0 + ATTN_KEY_CHUNK) for k0 in range(0, t_all, ATTN_KEY_CHUNK)]

    def score_chunk(slot, ks, qr, buf, run):
        sc = _dot(kab[slot, ks, :], qr)
        sbuf[buf][ks, :] = sc
        while sc.shape[0] > 8:
            half = sc.shape[0] // 2
            sc = jnp.maximum(sc[0:half, :], sc[half:, :])
        return sc if run is None else jnp.maximum(run, sc)

    def q_rows(r):
        return q2[:, pl.ds(pl.multiple_of(r * rb, rb), rb)]

    per_tile = n // rb

    def flush(r):
        r0 = (r % per_tile) * rb
        if not isinstance(r, int):
            r0 = pl.multiple_of(r0, rb)
        o_ref[0, r // per_tile, pl.ds(r0, rb), :] = acct[r].T

    def region(g, parity, do_scores, do_numer, do_weigh, do_flush):
        cur, oth = parity * span, (1 - parity) * span
        for j in range(span):
            slot = j % 2
            if do_flush and slot == 0:
                flush((g - 2) * (span // 2) + j // 2)
            if do_numer:
                m = mbuf[cur + j]
            if do_scores:
                qr = q_rows((g + 1) * (span // 2) + j // 2)
            run, res = None, None
            for ks in chunks:
                if do_scores:
                    run = score_chunk(slot, ks, qr, oth + j, run)
                if do_numer:
                    pbuf[cur + j][ks, :] = jnp.exp(sbuf[cur + j][ks, :] - m).astype(BF16)
                if do_weigh:
                    part = _dot(vt[:, ks], pbuf[oth + j][ks, :])
                    res = part if res is None else res + part
            if do_scores:
                mbuf[oth + j] = jnp.max(run, axis=0, keepdims=True)
            if do_weigh:
                acct[(g - 1) * (span // 2) + j // 2, slot * HEAD_DIM:(slot + 1) * HEAD_DIM, :] = (
                    res[0:HEAD_DIM, :] * (1.0 / res[HEAD_DIM:HEAD_DIM + 1, :]))

    region(-1, 1, True, False, False, False)
    region(0, 0, True, True, False, False)

    def body(g, carry):
        for parity in range(2):
            pl.when(g % 2 == parity)(lambda: region(g, parity, True, True, True, False))
        return carry

    lax.fori_loop(1, n_regions - 1, body, 0)
    region(n_regions - 1, (n_regions - 1) % 2, False, True, True, False)
    region(n_regions, n_regions % 2, False, False, True, False)
    for r in range(nr):
        flush(r)


def _attention(q, k, v, cache_k, cache_v, layer):
    rb = ATTN_ROWS
    bsz, n, _ = q.shape
    n_ctx = cache_k.shape[3]
    t_all = n + n_ctx
    gw = 2 * LANES
    qmap = lambda b, g: (b, 0, g)
    kmap = lambda b, g: (b, 0, 0)
    cmap = lambda b, g: (b, layer, 0, 0)
    return pl.pallas_call(
        functools.partial(_attn_kernel, n=n, n_ctx=n_ctx, rb=rb),
        grid=(bsz, N_KV_HEADS),
        in_specs=[
            pl.BlockSpec((1, n, gw), qmap),
            pl.BlockSpec((1, n, KV_W), kmap),
            pl.BlockSpec((1, n, KV_W), kmap),
            pl.BlockSpec((1, 1, KV_W, n_ctx), cmap),
            pl.BlockSpec((1, 1, KV_W, n_ctx), cmap),
        ],
        out_specs=pl.BlockSpec((1, 2, n, LANES), lambda b, g: (b, g, 0, 0)),
        out_shape=jax.ShapeDtypeStruct((bsz, ATTN_W // LANES, n, LANES), F32),
        scratch_shapes=[
            pltpu.VMEM((2, t_all, LANES), BF16),
            pltpu.VMEM((VT_ROWS, t_all), BF16),
            pltpu.VMEM((LANES, 2 * n), BF16),
            pltpu.VMEM((2 * n // rb, LANES, rb), F32),
            pltpu.VMEM((ATTN_BUFS, 1, rb), F32),
        ] + [pltpu.VMEM((t_all, rb), F32)] * ATTN_BUFS + [pltpu.VMEM((t_all, rb), BF16)] * ATTN_BUFS,
        compiler_params=pltpu.CompilerParams(
            dimension_semantics=("arbitrary", "arbitrary"), vmem_limit_bytes=ATTN_VMEM_LIMIT),
        name="attn_lat",
    )(q, k, v, cache_k, cache_v)


def _attn_ctx_kernel(q_ref, k_ref, v_ref, o_ref, *, n):
    low = lax.broadcasted_iota(jnp.int32, (n, LANES), 1) < HEAD_DIM
    ones_lo = jnp.where(low, 1.0, 0.0)
    zk = jnp.zeros((HEAD_DIM, n), F32)
    gw = 2 * LANES
    for b in range(CTX_BATCH):
        kt_all = k_ref[b, 0]
        v = v_ref[b, 0].T
        vr = pltpu.roll(v, HEAD_DIM, 1)
        for g in range(N_KV_HEADS):
            kg = kt_all[g * HEAD_DIM:(g + 1) * HEAD_DIM, :]
            kts = (jnp.concatenate([kg, zk], axis=0).astype(BF16),
                   jnp.concatenate([zk, kg], axis=0).astype(BF16))
            v_lo, v_hi = (v, vr) if g == 0 else (vr, v)
            vs = (jnp.concatenate([jnp.where(low, v_lo, 0.0), ones_lo], axis=1).astype(BF16),
                  jnp.concatenate([jnp.where(low, 0.0, v_hi), 1.0 - ones_lo], axis=1).astype(BF16))
            q2 = jnp.concatenate([q_ref[b, :, g * gw:g * gw + LANES],
                                  q_ref[b, :, g * gw + LANES:(g + 1) * gw]], axis=0)
            acc = None
            for kt_s, v_s in zip(kts, vs):
                s = _dot(q2, kt_s)
                m = jnp.max(s, axis=-1, keepdims=True)
                pv = _dot(jnp.exp(s - m).astype(BF16), v_s)
                acc = pv if acc is None else acc + pv
            o = acc[:, 0:LANES] / acc[:, LANES:]
            o_ref[b, 2 * g] = o[0:n]
            o_ref[b, 2 * g + 1] = o[n:]


def _attention_ctx(q, k, v, layer):
    bsz, n, _ = q.shape
    blk = lambda w: pl.BlockSpec((CTX_BATCH, n, w), lambda b: (b, 0, 0))
    kv_blk = pl.BlockSpec((CTX_BATCH, 1, KV_W, n), lambda b: (b, layer, 0, 0))
    return pl.pallas_call(
        functools.partial(_attn_ctx_kernel, n=n),
        grid=(bsz // CTX_BATCH,),
        in_specs=[blk(ATTN_W), kv_blk, kv_blk],
        out_specs=pl.BlockSpec((CTX_BATCH, ATTN_W // LANES, n, LANES), lambda b: (b, 0, 0, 0)),
        out_shape=jax.ShapeDtypeStruct((bsz, ATTN_W // LANES, n, LANES), F32),
        compiler_params=pltpu.CompilerParams(
            dimension_semantics=("arbitrary",), vmem_limit_bytes=VMEM_LIMIT),
        name="attn_ctx",
    )(q, k, v)


def _mix_kernel(attn_ref, x_ref, xp_ref, xn_ref, mod_ref, g_ref, w_ref, wout_ref,
                convw_ref, convb_ref, wpool_ref, pscale_ref, fg_ref,
                o_ref, xe, yext, uext, *, tm, tps, seq, final, mod_row):
    mod = _mod_row(mod_ref, mod_row, tps)
    j = pl.program_id(0) % tps
    pm = jnp.where(j == 0, 0.0, 1.0)
    nm = jnp.where(j == tps - 1, 0.0, 1.0)
    c = CONV_W
    g = g_ref[...]
    x = x_ref[...]
    xe[0:tm, :] = _mod_norm(x, g, mod).astype(BF16)
    if tps > 1:
        halo = jnp.concatenate([xn_ref[...], xp_ref[...]], axis=0)
        xe[tm:, :] = _mod_norm(halo, g, mod).astype(BF16)

    def proj(rows, off, width):
        return _dot(xe[0:rows, :], w_ref[0, :, off:off + width])

    nseq = max(tm // seq, 1)
    whole = tps == 1
    ln = tm // nseq
    hrows = tm if whole else tm + 2 * HALO
    h_all = proj(hrows, COL_CONV_H, c)
    c_all = proj(hrows, COL_CONV_C, c)
    u_all = proj(hrows, COL_POOL_U, POOL_W)
    z_a = proj(tm, COL_ATTN_GATE, ATTN_W)
    a = ATTN_W
    attn = jnp.concatenate(
        [jnp.concatenate([attn_ref[p, t] for t in range(attn_ref.shape[1])], axis=1)
         for p in range(attn_ref.shape[0])], axis=0)
    attn_o = (attn * _silu(z_a)).astype(BF16)
    out_a = _dot(attn_o, wout_ref[0, 0:a, :])
    b_c = proj(tm, COL_CONV_B, c)
    z_c = proj(tm, COL_CONV_GATE, c)
    z_p = proj(tm, COL_POOL_GATE, POOL_W)

    def extend(ext, p, val):
        zeros = jnp.zeros((HALO, val.shape[1]), F32)
        ext[p, 0:HALO, :] = zeros if whole else val[tm + HALO:, :] * pm
        ext[p, HALO:HALO + ln, :] = val[p * ln:(p + 1) * ln, :]
        ext[p, HALO + ln:, :] = zeros if whole else val[tm:tm + HALO, :] * nm

    y = h_all * c_all
    convs = []
    for p in range(nseq):
        extend(yext, p, y)
        convs.append(yext[p, HALO - 1:HALO - 1 + ln, :] * convw_ref[0:1, :]
                     + yext[p, HALO:HALO + ln, :] * convw_ref[1:2, :]
                     + yext[p, HALO + 1:HALO + 1 + ln, :] * convw_ref[2:3, :] + convb_ref[...])
    conv = convs[0] if nseq == 1 else jnp.concatenate(convs, axis=0)
    conv_o = (b_c * conv * _silu(z_c)).astype(BF16)

    lane = lax.broadcasted_iota(jnp.int32, (ln, LANES), 1)
    pos = lax.broadcasted_iota(jnp.int32, (ln, LANES), 0) + (0 if whole else j * tm)
    low = lane < POOL_GROUP
    d_rows = []
    for p in range(nseq):
        extend(uext, p, u_all)

        def win(k0, k1, sl):
            tot = None
            for k in range(k0, k1):
                t = uext[p, HALO + k:HALO + k + ln, sl]
                tot = t if tot is None else tot + t
            return tot

        ds = []
        for ti, (wa, wb) in enumerate(((2, 4), (8, 16))):
            sl = slice(ti * LANES, (ti + 1) * LANES)
            s_a = win(-(wa // 2), wa // 2, sl)
            s_b = s_a + win(-(wb // 2), -(wa // 2), sl) + win(wa // 2, wb // 2, sl)
            half = jnp.where(low, wa // 2, wb // 2)
            cnt = jnp.minimum(pos + half - 1, seq - 1) - jnp.maximum(pos - half, 0) + 1
            tot = jnp.where(low, s_a, s_b)
            ds.append(tot / cnt.astype(F32) - u_all[p * ln:(p + 1) * ln, sl])
        d_rows.append(jnp.concatenate(ds, axis=-1))
    d = (d_rows[0] if nseq == 1 else jnp.concatenate(d_rows, axis=0)).astype(BF16)
    pool_o = (_dot(d, wpool_ref[...]) * pscale_ref[...] * _silu(z_p)).astype(BF16)

    out = out_a + _dot(conv_o, wout_ref[0, a:a + c, :]) + _dot(pool_o, wout_ref[0, a + c:, :])
    res = x + mod[:, 2 * D_MODEL:] * out
    if final:
        ms = jnp.mean(res * res, axis=-1, keepdims=True)
        res = res * lax.rsqrt(ms + EPS) * fg_ref[...]
    o_ref[...] = res


def _mix(attn2, x2, mod, mod_row, norm_g, w_in_bf, wout_bf, layer, conv_w, conv_b, wpool_bd, pool_scale,
         final_g, *, seq, tm, final):
    tokens = x2.shape[0]
    tps = max(seq // tm, 1)
    nseq = max(tm // seq, 1)
    hb = tm // HALO
    nhb = tokens // HALO
    row = lambda i: (i, 0)
    const = lambda i: (0, 0)
    in_specs = [
        (pl.BlockSpec((nseq, ATTN_W // LANES, tm // nseq, LANES), lambda i: (i, 0, 0, 0)) if tps == 1 else
         pl.BlockSpec((1, ATTN_W // LANES, tm, LANES), lambda i: (i // tps, 0, i % tps, 0))),
        pl.BlockSpec((tm, D_MODEL), row),
        pl.BlockSpec((HALO, D_MODEL), lambda i: (jnp.maximum(i * hb - 1, 0), 0)),
        pl.BlockSpec((HALO, D_MODEL), lambda i: (jnp.minimum((i + 1) * hb, nhb - 1), 0)),
        _mod_spec(layer),
        pl.BlockSpec((1, D_MODEL), const),
        pl.BlockSpec((1, D_MODEL, IN_W), lambda i: (layer, 0, 0)),
        pl.BlockSpec((1, D_MODEL, D_MODEL), lambda i: (layer, 0, 0)),
        pl.BlockSpec((3, CONV_W), const),
        pl.BlockSpec((1, CONV_W), const),
        pl.BlockSpec((POOL_W, POOL_W), const),
        pl.BlockSpec((1, POOL_W), const),
        pl.BlockSpec((1, D_MODEL), const),
    ]
    return pl.pallas_call(
        functools.partial(_mix_kernel, tm=tm, tps=tps, seq=seq, final=final, mod_row=mod_row),
        grid=(tokens // tm,),
        in_specs=in_specs,
        out_specs=pl.BlockSpec((tm, D_MODEL), row),
        out_shape=jax.ShapeDtypeStruct((tokens, D_MODEL), F32),
        scratch_shapes=[
            pltpu.VMEM((tm + 2 * HALO, D_MODEL), BF16),
            pltpu.VMEM((nseq, tm // nseq + 2 * HALO, CONV_W), F32),
            pltpu.VMEM((nseq, tm // nseq + 2 * HALO, POOL_W), F32),
        ],
        compiler_params=pltpu.CompilerParams(
            dimension_semantics=("arbitrary",), vmem_limit_bytes=VMEM_LIMIT),
        name="mix_final" if final else "mix",
    )(attn2, x2, x2, x2, mod, norm_g.reshape(1, D_MODEL), w_in_bf, wout_bf,
      conv_w, conv_b.reshape(1, CONV_W), wpool_bd, pool_scale.reshape(1, POOL_W),
      final_g.reshape(1, D_MODEL))


def _rope_tables(n):
    rows = n // GRID_W
    row = np.repeat(np.arange(rows), GRID_W).astype(np.float64)
    col = np.tile(np.arange(GRID_W), rows).astype(np.float64)
    half = HEAD_DIM // 2
    inv = 1.0 / (ROPE_THETA ** (np.arange(0, half, 2, dtype=np.float64) / half))
    ar = row[:, None] * inv
    ac = col[:, None] * inv
    ang = np.tile(np.concatenate([ar, ar, ac, ac], axis=-1), (1, LANES // HEAD_DIM))
    second = (np.arange(LANES) % half) >= (half // 2)
    return jnp.asarray(np.cos(ang), F32), jnp.asarray(np.where(second, np.sin(ang), -np.sin(ang)), F32)


def _block_diag(blocks):
    n = len(blocks)
    rows = []
    for i, b in enumerate(blocks):
        rows.append(jnp.concatenate(
            [b if j == i else jnp.zeros((b.shape[0], blocks[j].shape[1]), b.dtype) for j in range(n)],
            axis=1))
    return jnp.concatenate(rows, axis=0)


def kernel(x_prompt, x_sample, cache_k, cache_v, c, c_ctx, norm_g, w_ada, b_ada, w_in,
           q_norm_g, k_norm_g, conv_w, conv_b, pool_w, pool_scale, w_out, final_g):
    depth = w_in.shape[0]
    bp, sp, _ = x_prompt.shape
    bs, ss, _ = x_sample.shape
    n_ctx = cache_k.shape[2]

    cvec = jnp.concatenate(
        [c, c_ctx[None, :], jnp.zeros((ADA_ROWS - bs - 1, D_MODEL), F32)], axis=0)
    mod = _ada(cvec, w_ada, b_ada)

    w_in_bf = w_in.astype(BF16)
    w_out_bf = w_out.astype(BF16)
    tables = _rope_tables(ss)
    ones = jnp.full((HEAD_DIM, HEAD_DIM), 1.0 / HEAD_DIM, BF16)
    bd = _block_diag([ones, ones])
    ck = cache_k.transpose(0, 1, 3, 4, 2).reshape(bs, depth, KV_W, n_ctx)
    cv = cache_v.transpose(0, 1, 3, 4, 2).reshape(bs, depth, KV_W, n_ctx)

    h = x_prompt.reshape(bp * sp, D_MODEL)
    z = x_sample.reshape(bs * ss, D_MODEL)
    kv_ctx = None
    gains = jnp.concatenate(
        [jnp.tile(q_norm_g * (HEAD_DIM ** -0.5), (1, N_HEADS)), jnp.tile(k_norm_g, (1, N_KV_HEADS))],
        axis=-1).reshape(depth, 1, QK_W)
    for l in range(depth):
        wpool_bd = _block_diag([pool_w[l, g] for g in range(pool_w.shape[1])]).astype(BF16)
        final = l == depth - 1
        mix = functools.partial(
            _mix, norm_g=norm_g[l], w_in_bf=w_in_bf, wout_bf=w_out_bf, layer=l,
            conv_w=conv_w[l], conv_b=conv_b[l], wpool_bd=wpool_bd, pool_scale=pool_scale[l],
            final_g=final_g, final=final)

        q, k, v = _qkv(h, mod, bs, norm_g[l], w_in_bf, l, gains, bd, None, kv_ctx, seq=sp, tm=QKV_TILE)
        kv_ctx = (k, v)
        attn = _attention_ctx(q.reshape(bp, sp, ATTN_W), k, v, l)
        h = mix(attn, h, mod, bs, seq=sp, tm=MIX_TILE)

        q, k, v = _qkv(z, mod, None, norm_g[l], w_in_bf, l, gains, bd, tables, None, seq=ss, tm=QKV_TILE)
        attn = _attention(q.reshape(bs, ss, ATTN_W), k.reshape(bs, ss, KV_W), v.reshape(bs, ss, KV_W),
                          ck, cv, l)
        z = mix(attn, z, mod, None, seq=ss, tm=MIX_TILE)

    y_prompt = h.reshape(bp, sp, D_MODEL)
    y_sample = z.reshape(bs, ss, D_MODEL)
    new_k, new_v = (a.reshape(bp, depth, N_KV_HEADS, HEAD_DIM, sp).transpose(0, 1, 4, 2, 3) for a in kv_ctx)
    return (y_prompt, y_sample, new_k, new_v)
```

```python
import functools

import jax
import jax.numpy as jnp
import numpy as np
from jax import lax
from jax.experimental import pallas as pl
from jax.experimental.pallas import tpu as pltpu

D_MODEL = 1024
GRID_W = 64
N_HEADS = 8
N_KV_HEADS = 2
HEAD_DIM = 64
ATTN_W = N_HEADS * HEAD_DIM
KV_W = N_KV_HEADS * HEAD_DIM
QK_W = ATTN_W + KV_W
QKV_W = QK_W + KV_W
CONV_W = 256
POOL_W = 256
POOL_GROUP = 64
IN_W = 2816
COL_ATTN_GATE = QKV_W
COL_CONV_H = COL_ATTN_GATE + ATTN_W
COL_CONV_B = COL_CONV_H + CONV_W
COL_CONV_C = COL_CONV_B + CONV_W
COL_CONV_GATE = COL_CONV_C + CONV_W
COL_POOL_U = COL_CONV_GATE + CONV_W
COL_POOL_GATE = COL_POOL_U + POOL_W
ROPE_THETA = 10000.0
EPS = 1e-6
LANES = 128
HALO = 8
ADA_ROWS = 8
VMEM_LIMIT = 48 * 1024 * 1024
ATTN_VMEM_LIMIT = 56 * 1024 * 1024

ADA_K = 512
QKV_TILE = 1024
QKV_SUB = 256
MIX_TILE = 512
ATTN_ROWS = 256
ATTN_REGION = 4
ATTN_BUFS = 2 * ATTN_REGION
ATTN_KEY_CHUNK = 256
CTX_BATCH = 4

F32 = jnp.float32
BF16 = jnp.bfloat16


def _silu(z):
    return z * (1.0 / (1.0 + jnp.exp(-z)))


def _dot(a, b):
    return jnp.dot(a, b, preferred_element_type=F32)


def _mod_row(mod_ref, mod_row, tps):
    row = mod_row if mod_row is not None else pl.program_id(0) // tps
    return mod_ref[0, pl.ds(row, 1), :]


def _mod_norm(x, g, mod):
    ms = jnp.mean(x * x, axis=-1, keepdims=True)
    gs = g * (1.0 + mod[:, D_MODEL:2 * D_MODEL])
    return x * lax.rsqrt(ms + EPS) * gs + mod[:, 0:D_MODEL]


def _ada_kernel(c_ref, w_ref, b_ref, o_ref):
    part = _dot(_silu(c_ref[...]).astype(BF16), w_ref[0].astype(BF16))

    @pl.when(pl.program_id(1) == 0)
    def _first():
        o_ref[0] = part + b_ref[pl.ds(pl.program_id(0), 1), :]

    @pl.when(pl.program_id(1) > 0)
    def _rest():
        o_ref[0] += part


def _ada(cvec, w_ada, b_ada):
    depth, _, n3 = w_ada.shape
    tk = ADA_K
    return pl.pallas_call(
        _ada_kernel,
        grid=(depth, D_MODEL // tk),
        in_specs=[
            pl.BlockSpec((ADA_ROWS, tk), lambda l, k: (0, k)),
            pl.BlockSpec((1, tk, n3), lambda l, k: (l, k, 0)),
            pl.BlockSpec((depth, n3), lambda l, k: (0, 0)),
        ],
        out_specs=pl.BlockSpec((1, ADA_ROWS, n3), lambda l, k: (l, 0, 0)),
        out_shape=jax.ShapeDtypeStruct((depth, ADA_ROWS, n3), F32),
        compiler_params=pltpu.CompilerParams(dimension_semantics=("arbitrary", "arbitrary")),
        name="ada_mod",
    )(cvec, w_ada, b_ada)


def _qkv_kernel(*refs, rope, nsub, first_layer, mod_row, tps):
    if rope:
        (x_ref, mod_ref, g_ref, w_ref, gains_ref, bd_ref, cos_ref, sin_ref,
         q_ref, k_ref, v_ref) = refs
    else:
        x_ref, mod_ref, g_ref, w_ref, gains_ref, bd_ref = refs[:6]
        q_ref, k_ref, v_ref = refs[-3:]
    tm = x_ref.shape[0]
    sub = tm // nsub
    bd = bd_ref[...]
    g = g_ref[...]
    mod = _mod_row(mod_ref, mod_row, tps)
    if rope:
        lane = lax.broadcasted_iota(jnp.int32, (sub, LANES), 1)
        second = (lane % (HEAD_DIM // 2)) >= (HEAD_DIM // 4)
    k_parts, v_parts = [None] * nsub, [None] * nsub

    def finish(s, c, part):
        rows = slice(s * sub, (s + 1) * sub)
        for h in range(2):
            j = 2 * c + h
            t = part[:, h * LANES:(h + 1) * LANES]
            if j * LANES >= QK_W:
                v_parts[s] = t
                continue
            sl = slice(j * LANES, (j + 1) * LANES)
            msq = _dot((t * t).astype(BF16), bd)
            t = t * lax.rsqrt(msq + EPS) * gains_ref[0, :, sl]
            if rope:
                partner = jnp.where(second, pltpu.roll(t, HEAD_DIM // 4, 1),
                                    pltpu.roll(t, LANES - HEAD_DIM // 4, 1))
                t = t * cos_ref[rows, :] + partner * sin_ref[rows, :]
            if j * LANES < ATTN_W:
                q_ref[rows, sl] = t.astype(BF16)
            else:
                k_parts[s] = t

    ncol = QKV_W // (2 * LANES)
    prev = None
    for s in range(nsub + 1):
        if s < nsub:
            xn = _mod_norm(x_ref[s * sub:(s + 1) * sub, :], g, mod).astype(BF16)
        cur = []
        for c in range(ncol):
            if s < nsub:
                cur.append(_dot(xn, w_ref[0, :, c * 2 * LANES:(c + 1) * 2 * LANES]))
            if prev is not None:
                finish(s - 1, c, prev[c])
        prev = cur
    for ref, parts in ((k_ref, k_parts), (v_ref, v_parts)):
        val = jnp.concatenate(parts, axis=0)
        if rope:
            ref[...] = val
            continue
        seq = ref.shape[3]
        for p in range(ref.shape[0]):
            piece = val[p * seq:(p + 1) * seq, :].T
            if first_layer is None:
                ref[p, 0] = piece
            else:
                for l2 in range(ref.shape[1]):
                    ref[p, l2] = piece if l2 == first_layer else jnp.zeros_like(piece)


def _mod_spec(layer):
    return pl.BlockSpec((1, ADA_ROWS, 3 * D_MODEL), lambda i: (layer, 0, 0))


def _qkv(x2, mod, mod_row, norm_g, w_in_bf, layer, gains, bd, tables, kv_prev, *, seq, tm):
    tokens = x2.shape[0]
    depth = w_in_bf.shape[0]
    tps = max(seq // tm, 1)
    rope = tables is not None
    row = lambda i: (i, 0)
    const = lambda i: (0, 0)
    in_specs = [
        pl.BlockSpec((tm, D_MODEL), row),
        _mod_spec(layer),
        pl.BlockSpec((1, D_MODEL), const),
        pl.BlockSpec((1, D_MODEL, QKV_W), lambda i: (layer, 0, 0)),
        pl.BlockSpec((1, 1, QK_W), lambda i: (layer, 0, 0)),
        pl.BlockSpec((LANES, LANES), const),
    ]
    args = [x2, mod, norm_g.reshape(1, D_MODEL), w_in_bf, gains, bd]
    out_shapes = [jax.ShapeDtypeStruct((tokens, ATTN_W), BF16)]
    out_specs = [pl.BlockSpec((tm, ATTN_W), row)]
    aliases = {}
    first_layer = None
    if rope:
        in_specs += [pl.BlockSpec((tm, LANES), lambda i: (i % tps, 0))] * 2
        args += list(tables)
        out_shapes += [jax.ShapeDtypeStruct((tokens, KV_W), F32)] * 2
        out_specs += [pl.BlockSpec((tm, KV_W), row)] * 2
    else:
        nseq = tm // seq
        out_shapes += [jax.ShapeDtypeStruct((tokens // seq, depth, KV_W, seq), F32)] * 2
        if kv_prev is None:
            first_layer = layer
            out_specs += [pl.BlockSpec((nseq, depth, KV_W, seq), lambda i: (i, 0, 0, 0))] * 2
        else:
            out_specs += [pl.BlockSpec((nseq, 1, KV_W, seq), lambda i: (i, layer, 0, 0))] * 2
            in_specs += [pl.BlockSpec(memory_space=pl.ANY)] * 2
            aliases = {len(args): 1, len(args) + 1: 2}
            args += list(kv_prev)
    return pl.pallas_call(
        functools.partial(_qkv_kernel, rope=rope, nsub=tm // QKV_SUB, first_layer=first_layer,
                          mod_row=mod_row, tps=tps),
        grid=(tokens // tm,),
        in_specs=in_specs,
        out_specs=out_specs,
        out_shape=out_shapes,
        input_output_aliases=aliases,
        compiler_params=pltpu.CompilerParams(
            dimension_semantics=("arbitrary",), vmem_limit_bytes=VMEM_LIMIT),
        name="qkv_rope" if rope else "qkv",
    )(*args)


VT_ROWS = HEAD_DIM + 16


def _attn_kernel(q_ref, k_ref, v_ref, ck_ref, cv_ref, o_ref, kab, vt, q2, acct, mbuf, *bufs, n, n_ctx, rb):
    kv = pl.program_id(1)
    nr = 2 * n // rb
    t_all = n + n_ctx
    sbuf, pbuf = bufs[:ATTN_BUFS], bufs[ATTN_BUFS:]

    def fill(k_src, v_t, r0, m):
        lane = lax.broadcasted_iota(jnp.int32, (m, LANES), 1)
        k_own = jnp.where((lane // HEAD_DIM) == kv, k_src, 0.0)
        k_oth = pltpu.roll(k_own, HEAD_DIM, 1)
        is0 = kv == 0
        kab[0, r0:r0 + m, :] = jnp.where(is0, k_own, k_oth).astype(BF16)
        kab[1, r0:r0 + m, :] = jnp.where(is0, k_oth, k_own).astype(BF16)
        vt[0:HEAD_DIM, r0:r0 + m] = jnp.where(is0, v_t[0:HEAD_DIM, :], v_t[HEAD_DIM:, :]).astype(BF16)

    piece = 256
    for r0 in range(0, n, piece):
        fill(k_ref[0, r0:r0 + piece, :], v_ref[0, r0:r0 + piece, :].T, r0, piece)
    fill(ck_ref[0, 0].T, cv_ref[0, 0], n, n_ctx)
    vt[HEAD_DIM:, :] = jnp.ones((VT_ROWS - HEAD_DIM, t_all), BF16)
    for c0 in range(0, n, piece):
        q2[:, c0:c0 + piece] = q_ref[0, c0:c0 + piece, 0:LANES].astype(F32).T.astype(BF16)
        q2[:, n + c0:n + c0 + piece] = q_ref[0, c0:c0 + piece, LANES:].astype(F32).T.astype(BF16)

    span = ATTN_REGION
    n_regions = 2 * nr // span
    chunks = [slice(k0, k0 + ATTN_KEY_CHUNK) for k0 in range(0, t_all, ATTN_KEY_CHUNK)]

    def score_chunk(slot, ks, qr, buf, run):
        sc = _dot(kab[slot, ks, :], qr)
        sbuf[buf][ks, :] = sc
        while sc.shape[0] > 8:
            half = sc.shape[0] // 2
            sc = jnp.maximum(sc[0:half, :], sc[half:, :])
        return sc if run is None else jnp.maximum(run, sc)

    def q_rows(r):
        return q2[:, pl.ds(pl.multiple_of(r * rb, rb), rb)]

    per_tile = n // rb

    def flush(r):
        r0 = (r % per_tile) * rb
        if not isinstance(r, int):
            r0 = pl.multiple_of(r0, rb)
        o_ref[0, r // per_tile, pl.ds(r0, rb), :] = acct[r].T

    def region(g, parity, do_scores, do_numer, do_weigh, do_flush):
        cur, oth = parity * span, (1 - parity) * span
        for j in range(span):
            slot = j % 2
            if do_flush and slot == 0:
                flush((g - 2) * (span // 2) + j // 2)
            if do_numer:
                m = mbuf[cur + j]
            if do_scores:
                qr = q_rows((g + 1) * (span // 2) + j // 2)
            run, res = None, None
            for ks in chunks:
                if do_scores:
                    run = score_chunk(slot, ks, qr, oth + j, run)
                if do_numer:
                    pbuf[cur + j][ks, :] = jnp.exp(sbuf[cur + j][ks, :] - m).astype(BF16)
                if do_weigh:
                    part = _dot(vt[:, ks], pbuf[oth + j][ks, :])
                    res = part if res is None else res + part
            if do_scores:
                mbuf[oth + j] = jnp.max(run, axis=0, keepdims=True)
            if do_weigh:
                acct[(g - 1) * (span // 2) + j // 2, slot * HEAD_DIM:(slot + 1) * HEAD_DIM, :] = (
                    res[0:HEAD_DIM, :] * (1.0 / res[HEAD_DIM:HEAD_DIM + 1, :]))

    region(-1, 1, True, False, False, False)
    region(0, 0, True, True, False, False)

    def body(g, carry):
        for parity in range(2):
            pl.when(g % 2 == parity)(lambda: region(g, parity, True, True, True, False))
        return carry

    lax.fori_loop(1, n_regions - 1, body, 0)
    region(n_regions - 1, (n_regions - 1) % 2, False, True, True, False)
    region(n_regions, n_regions % 2, False, False, True, False)
    for r in range(nr):
        flush(r)


def _attention(q, k, v, cache_k, cache_v, layer):
    rb = ATTN_ROWS
    bsz, n, _ = q.shape
    n_ctx = cache_k.shape[3]
    t_all = n + n_ctx
    gw = 2 * LANES
    qmap = lambda b, g: (b, 0, g)
    kmap = lambda b, g: (b, 0, 0)
    cmap = lambda b, g: (b, layer, 0, 0)
    return pl.pallas_call(
        functools.partial(_attn_kernel, n=n, n_ctx=n_ctx, rb=rb),
        grid=(bsz, N_KV_HEADS),
        in_specs=[
            pl.BlockSpec((1, n, gw), qmap),
            pl.BlockSpec((1, n, KV_W), kmap),
            pl.BlockSpec((1, n, KV_W), kmap),
            pl.BlockSpec((1, 1, KV_W, n_ctx), cmap),
            pl.BlockSpec((1, 1, KV_W, n_ctx), cmap),
        ],
        out_specs=pl.BlockSpec((1, 2, n, LANES), lambda b, g: (b, g, 0, 0)),
        out_shape=jax.ShapeDtypeStruct((bsz, ATTN_W // LANES, n, LANES), F32),
        scratch_shapes=[
            pltpu.VMEM((2, t_all, LANES), BF16),
            pltpu.VMEM((VT_ROWS, t_all), BF16),
            pltpu.VMEM((LANES, 2 * n), BF16),
            pltpu.VMEM((2 * n // rb, LANES, rb), F32),
            pltpu.VMEM((ATTN_BUFS, 1, rb), F32),
        ] + [pltpu.VMEM((t_all, rb), F32)] * ATTN_BUFS + [pltpu.VMEM((t_all, rb), BF16)] * ATTN_BUFS,
        compiler_params=pltpu.CompilerParams(
            dimension_semantics=("arbitrary", "arbitrary"), vmem_limit_bytes=ATTN_VMEM_LIMIT),
        name="attn_lat",
    )(q, k, v, cache_k, cache_v)


def _attn_ctx_kernel(q_ref, k_ref, v_ref, o_ref, *, n):
    low = lax.broadcasted_iota(jnp.int32, (n, LANES), 1) < HEAD_DIM
    ones_lo = jnp.where(low, 1.0, 0.0)
    zk = jnp.zeros((HEAD_DIM, n), F32)
    gw = 2 * LANES
    for b in range(CTX_BATCH):
        kt_all = k_ref[b, 0]
        v = v_ref[b, 0].T
        vr = pltpu.roll(v, HEAD_DIM, 1)
        for g in range(N_KV_HEADS):
            kg = kt_all[g * HEAD_DIM:(g + 1) * HEAD_DIM, :]
            kts = (jnp.concatenate([kg, zk], axis=0).astype(BF16),
                   jnp.concatenate([zk, kg], axis=0).astype(BF16))
            v_lo, v_hi = (v, vr) if g == 0 else (vr, v)
            vs = (jnp.concatenate([jnp.where(low, v_lo, 0.0), ones_lo], axis=1).astype(BF16),
                  jnp.concatenate([jnp.where(low, 0.0, v_hi), 1.0 - ones_lo], axis=1).astype(BF16))
            q2 = jnp.concatenate([q_ref[b, :, g * gw:g * gw + LANES],
                                  q_ref[b, :, g * gw + LANES:(g + 1) * gw]], axis=0)
            acc = None
            for kt_s, v_s in zip(kts, vs):
                s = _dot(q2, kt_s)
                m = jnp.max(s, axis=-1, keepdims=True)
                pv = _dot(jnp.exp(s - m).astype(BF16), v_s)
                acc = pv if acc is None else acc + pv
            o = acc[:, 0:LANES] / acc[:, LANES:]
            o_ref[b, 2 * g] = o[0:n]
            o_ref[b, 2 * g + 1] = o[n:]


def _attention_ctx(q, k, v, layer):
    bsz, n, _ = q.shape
    blk = lambda w: pl.BlockSpec((CTX_BATCH, n, w), lambda b: (b, 0, 0))
    kv_blk = pl.BlockSpec((CTX_BATCH, 1, KV_W, n), lambda b: (b, layer, 0, 0))
    return pl.pallas_call(
        functools.partial(_attn_ctx_kernel, n=n),
        grid=(bsz // CTX_BATCH,),
        in_specs=[blk(ATTN_W), kv_blk, kv_blk],
        out_specs=pl.BlockSpec((CTX_BATCH, ATTN_W // LANES, n, LANES), lambda b: (b, 0, 0, 0)),
        out_shape=jax.ShapeDtypeStruct((bsz, ATTN_W // LANES, n, LANES), F32),
        compiler_params=pltpu.CompilerParams(
            dimension_semantics=("arbitrary",), vmem_limit_bytes=VMEM_LIMIT),
        name="attn_ctx",
    )(q, k, v)


def _mix_kernel(attn_ref, x_ref, xp_ref, xn_ref, mod_ref, g_ref, w_ref, wout_ref,
                convw_ref, convb_ref, wpool_ref, pscale_ref, fg_ref,
                o_ref, xe, yext, uext, *, tm, tps, seq, final, mod_row):
    mod = _mod_row(mod_ref, mod_row, tps)
    j = pl.program_id(0) % tps
    pm = jnp.where(j == 0, 0.0, 1.0)
    nm = jnp.where(j == tps - 1, 0.0, 1.0)
    c = CONV_W
    g = g_ref[...]
    x = x_ref[...]
    xe[0:tm, :] = _mod_norm(x, g, mod).astype(BF16)
    if tps > 1:
        halo = jnp.concatenate([xn_ref[...], xp_ref[...]], axis=0)
        xe[tm:, :] = _mod_norm(halo, g, mod).astype(BF16)

    def proj(rows, off, width):
        return _dot(xe[0:rows, :], w_ref[0, :, off:off + width])

    nseq = max(tm // seq, 1)
    whole = tps == 1
    ln = tm // nseq
    hrows = tm if whole else tm + 2 * HALO
    h_all = proj(hrows, COL_CONV_H, c)
    c_all = proj(hrows, COL_CONV_C, c)
    u_all = proj(hrows, COL_POOL_U, POOL_W)
    z_a = proj(tm, COL_ATTN_GATE, ATTN_W)
    a = ATTN_W
    attn = jnp.concatenate(
        [jnp.concatenate([attn_ref[p, t] for t in range(attn_ref.shape[1])], axis=1)
         for p in range(attn_ref.shape[0])], axis=0)
    attn_o = (attn * _silu(z_a)).astype(BF16)
    out_a = _dot(attn_o, wout_ref[0, 0:a, :])
    b_c = proj(tm, COL_CONV_B, c)
    z_c = proj(tm, COL_CONV_GATE, c)
    z_p = proj(tm, COL_POOL_GATE, POOL_W)

    def extend(ext, p, val):
        zeros = jnp.zeros((HALO, val.shape[1]), F32)
        ext[p, 0:HALO, :] = zeros if whole else val[tm + HALO:, :] * pm
        ext[p, HALO:HALO + ln, :] = val[p * ln:(p + 1) * ln, :]
        ext[p, HALO + ln:, :] = zeros if whole else val[tm:tm + HALO, :] * nm

    y = h_all * c_all
    convs = []
    for p in range(nseq):
        extend(yext, p, y)
        convs.append(yext[p, HALO - 1:HALO - 1 + ln, :] * convw_ref[0:1, :]
                     + yext[p, HALO:HALO + ln, :] * convw_ref[1:2, :]
                     + yext[p, HALO + 1:HALO + 1 + ln, :] * convw_ref[2:3, :] + convb_ref[...])
    conv = convs[0] if nseq == 1 else jnp.concatenate(convs, axis=0)
    conv_o = (b_c * conv * _silu(z_c)).astype(BF16)

    lane = lax.broadcasted_iota(jnp.int32, (ln, LANES), 1)
    pos = lax.broadcasted_iota(jnp.int32, (ln, LANES), 0) + (0 if whole else j * tm)
    low = lane < POOL_GROUP
    d_rows = []
    for p in range(nseq):
        extend(uext, p, u_all)

        def win(k0, k1, sl):
            tot = None
            for k in range(k0, k1):
                t = uext[p, HALO + k:HALO + k + ln, sl]
                tot = t if tot is None else tot + t
            return tot

        ds = []
        for ti, (wa, wb) in enumerate(((2, 4), (8, 16))):
            sl = slice(ti * LANES, (ti + 1) * LANES)
            s_a = win(-(wa // 2), wa // 2, sl)
            s_b = s_a + win(-(wb // 2), -(wa // 2), sl) + win(wa // 2, wb // 2, sl)
            half = jnp.where(low, wa // 2, wb // 2)
            cnt = jnp.minimum(pos + half - 1, seq - 1) - jnp.maximum(pos - half, 0) + 1
            tot = jnp.where(low, s_a, s_b)
            ds.append(tot / cnt.astype(F32) - u_all[p * ln:(p + 1) * ln, sl])
        d_rows.append(jnp.concatenate(ds, axis=-1))
    d = (d_rows[0] if nseq == 1 else jnp.concatenate(d_rows, axis=0)).astype(BF16)
    pool_o = (_dot(d, wpool_ref[...]) * pscale_ref[...] * _silu(z_p)).astype(BF16)

    out = out_a + _dot(conv_o, wout_ref[0, a:a + c, :]) + _dot(pool_o, wout_ref[0, a + c:, :])
    res = x + mod[:, 2 * D_MODEL:] * out
    if final:
        ms = jnp.mean(res * res, axis=-1, keepdims=True)
        res = res * lax.rsqrt(ms + EPS) * fg_ref[...]
    o_ref[...] = res


def _mix(attn2, x2, mod, mod_row, norm_g, w_in_bf, wout_bf, layer, conv_w, conv_b, wpool_bd, pool_scale,
         final_g, *, seq, tm, final):
    tokens = x2.shape[0]
    tps = max(seq // tm, 1)
    nseq = max(tm // seq, 1)
    hb = tm // HALO
    nhb = tokens // HALO
    row = lambda i: (i, 0)
    const = lambda i: (0, 0)
    in_specs = [
        (pl.BlockSpec((nseq, ATTN_W // LANES, tm // nseq, LANES), lambda i: (i, 0, 0, 0)) if tps == 1 else
         pl.BlockSpec((1, ATTN_W // LANES, tm, LANES), lambda i: (i // tps, 0, i % tps, 0))),
        pl.BlockSpec((tm, D_MODEL), row),
        pl.BlockSpec((HALO, D_MODEL), lambda i: (jnp.maximum(i * hb - 1, 0), 0)),
        pl.BlockSpec((HALO, D_MODEL), lambda i: (jnp.minimum((i + 1) * hb, nhb - 1), 0)),
        _mod_spec(layer),
        pl.BlockSpec((1, D_MODEL), const),
        pl.BlockSpec((1, D_MODEL, IN_W), lambda i: (layer, 0, 0)),
        pl.BlockSpec((1, D_MODEL, D_MODEL), lambda i: (layer, 0, 0)),
        pl.BlockSpec((3, CONV_W), const),
        pl.BlockSpec((1, CONV_W), const),
        pl.BlockSpec((POOL_W, POOL_W), const),
        pl.BlockSpec((1, POOL_W), const),
        pl.BlockSpec((1, D_MODEL), const),
    ]
    return pl.pallas_call(
        functools.partial(_mix_kernel, tm=tm, tps=tps, seq=seq, final=final, mod_row=mod_row),
        grid=(tokens // tm,),
        in_specs=in_specs,
        out_specs=pl.BlockSpec((tm, D_MODEL), row),
        out_shape=jax.ShapeDtypeStruct((tokens, D_MODEL), F32),
        scratch_shapes=[
            pltpu.VMEM((tm + 2 * HALO, D_MODEL), BF16),
            pltpu.VMEM((nseq, tm // nseq + 2 * HALO, CONV_W), F32),
            pltpu.VMEM((nseq, tm // nseq + 2 * HALO, POOL_W), F32),
        ],
        compiler_params=pltpu.CompilerParams(
            dimension_semantics=("arbitrary",), vmem_limit_bytes=VMEM_LIMIT),
        name="mix_final" if final else "mix",
    )(attn2, x2, x2, x2, mod, norm_g.reshape(1, D_MODEL), w_in_bf, wout_bf,
      conv_w, conv_b.reshape(1, CONV_W), wpool_bd, pool_scale.reshape(1, POOL_W),
      final_g.reshape(1, D_MODEL))


def _rope_tables(n):
    rows = n // GRID_W
    row = np.repeat(np.arange(rows), GRID_W).astype(np.float64)
    col = np.tile(np.arange(GRID_W), rows).astype(np.float64)
    half = HEAD_DIM // 2
    inv = 1.0 / (ROPE_THETA ** (np.arange(0, half, 2, dtype=np.float64) / half))
    ar = row[:, None] * inv
    ac = col[:, None] * inv
    ang = np.tile(np.concatenate([ar, ar, ac, ac], axis=-1), (1, LANES // HEAD_DIM))
    second = (np.arange(LANES) % half) >= (half // 2)
    return jnp.asarray(np.cos(ang), F32), jnp.asarray(np.where(second, np.sin(ang), -np.sin(ang)), F32)


def _block_diag(blocks):
    n = len(blocks)
    rows = []
    for i, b in enumerate(blocks):
        rows.append(jnp.concatenate(
            [b if j == i else jnp.zeros((b.shape[0], blocks[j].shape[1]), b.dtype) for j in range(n)],
            axis=1))
    return jnp.concatenate(rows, axis=0)


def kernel(x_prompt, x_sample, cache_k, cache_v, c, c_ctx, norm_g, w_ada, b_ada, w_in,
           q_norm_g, k_norm_g, conv_w, conv_b, pool_w, pool_scale, w_out, final_g):
    depth = w_in.shape[0]
    bp, sp, _ = x_prompt.shape
    bs, ss, _ = x_sample.shape
    n_ctx = cache_k.shape[2]

    cvec = jnp.concatenate(
        [c, c_ctx[None, :], jnp.zeros((ADA_ROWS - bs - 1, D_MODEL), F32)], axis=0)
    mod = _ada(cvec, w_ada, b_ada)

    w_in_bf = w_in.astype(BF16)
    w_out_bf = w_out.astype(BF16)
    tables = _rope_tables(ss)
    ones = jnp.full((HEAD_DIM, HEAD_DIM), 1.0 / HEAD_DIM, BF16)
    bd = _block_diag([ones, ones])
    ck = cache_k.transpose(0, 1, 3, 4, 2).reshape(bs, depth, KV_W, n_ctx)
    cv = cache_v.transpose(0, 1, 3, 4, 2).reshape(bs, depth, KV_W, n_ctx)

    h = x_prompt.reshape(bp * sp, D_MODEL)
    z = x_sample.reshape(bs * ss, D_MODEL)
    kv_ctx = None
    gains = jnp.concatenate(
        [jnp.tile(q_norm_g * (HEAD_DIM ** -0.5), (1, N_HEADS)), jnp.tile(k_norm_g, (1, N_KV_HEADS))],
        axis=-1).reshape(depth, 1, QK_W)
    for l in range(depth):
        wpool_bd = _block_diag([pool_w[l, g] for g in range(pool_w.shape[1])]).astype(BF16)
        final = l == depth - 1
        mix = functools.partial(
            _mix, norm_g=norm_g[l], w_in_bf=w_in_bf, wout_bf=w_out_bf, layer=l,
            conv_w=conv_w[l], conv_b=conv_b[l], wpool_bd=wpool_bd, pool_scale=pool_scale[l],
            final_g=final_g, final=final)

        q, k, v = _qkv(h, mod, bs, norm_g[l], w_in_bf, l, gains, bd, None, kv_ctx, seq=sp, tm=QKV_TILE)
        kv_ctx = (k, v)
        attn = _attention_ctx(q.reshape(bp, sp, ATTN_W), k, v, l)
        h = mix(attn, h, mod, bs, seq=sp, tm=MIX_TILE)

        q, k, v = _qkv(z, mod, None, norm_g[l], w_in_bf, l, gains, bd, tables, None, seq=ss, tm=QKV_TILE)
        attn = _attention(q.reshape(bs, ss, ATTN_W), k.reshape(bs, ss, KV_W), v.reshape(bs, ss, KV_W),
                          ck, cv, l)
        z = mix(attn, z, mod, None, seq=ss, tm=MIX_TILE)

    y_prompt = h.reshape(bp, sp, D_MODEL)
    y_sample = z.reshape(bs, ss, D_MODEL)
    new_k, new_v = (a.reshape(bp, depth, N_KV_HEADS, HEAD_DIM, sp).transpose(0, 1, 4, 2, 3) for a in kv_ctx)
    return (y_prompt, y_sample, new_k, new_v)
```

```python
import functools

import jax
import jax.numpy as jnp
import numpy as np
from jax import lax
from jax.experimental import pallas as pl
from jax.experimental.pallas import tpu as pltpu

D_MODEL = 1024
GRID_W = 64
N_HEADS = 8
N_KV_HEADS = 2
HEAD_DIM = 64
ATTN_W = N_HEADS * HEAD_DIM
KV_W = N_KV_HEADS * HEAD_DIM
QK_W = ATTN_W + KV_W
QKV_W = QK_W + KV_W
CONV_W = 256
POOL_W = 256
POOL_GROUP = 64
IN_W = 2816
COL_ATTN_GATE = QKV_W
COL_CONV_H = COL_ATTN_GATE + ATTN_W
COL_CONV_B = COL_CONV_H + CONV_W
COL_CONV_C = COL_CONV_B + CONV_W
COL_CONV_GATE = COL_CONV_C + CONV_W
COL_POOL_U = COL_CONV_GATE + CONV_W
COL_POOL_GATE = COL_POOL_U + POOL_W
ROPE_THETA = 10000.0
EPS = 1e-6
LANES = 128
HALO = 8
ADA_ROWS = 8
VMEM_LIMIT = 48 * 1024 * 1024
ATTN_VMEM_LIMIT = 56 * 1024 * 1024

ADA_K = 512
QKV_TILE = 1024
QKV_SUB = 256
MIX_TILE = 512
ATTN_ROWS = 256
ATTN_REGION = 4
ATTN_BUFS = 2 * ATTN_REGION
ATTN_KEY_CHUNK = 256
CTX_BATCH = 4

F32 = jnp.float32
BF16 = jnp.bfloat16


def _silu(z):
    return z * (1.0 / (1.0 + jnp.exp(-z)))


def _dot(a, b):
    return jnp.dot(a, b, preferred_element_type=F32)


def _mod_row(mod_ref, mod_row, tps):
    row = mod_row if mod_row is not None else pl.program_id(0) // tps
    return mod_ref[0, pl.ds(row, 1), :]


def _mod_norm(x, g, mod):
    ms = jnp.mean(x * x, axis=-1, keepdims=True)
    gs = g * (1.0 + mod[:, D_MODEL:2 * D_MODEL])
    return x * lax.rsqrt(ms + EPS) * gs + mod[:, 0:D_MODEL]


def _ada_kernel(c_ref, w_ref, b_ref, o_ref):
    part = _dot(_silu(c_ref[...]).astype(BF16), w_ref[0].astype(BF16))

    @pl.when(pl.program_id(1) == 0)
    def _first():
        o_ref[0] = part + b_ref[pl.ds(pl.program_id(0), 1), :]

    @pl.when(pl.program_id(1) > 0)
    def _rest():
        o_ref[0] += part


def _ada(cvec, w_ada, b_ada):
    depth, _, n3 = w_ada.shape
    tk = ADA_K
    return pl.pallas_call(
        _ada_kernel,
        grid=(depth, D_MODEL // tk),
        in_specs=[
            pl.BlockSpec((ADA_ROWS, tk), lambda l, k: (0, k)),
            pl.BlockSpec((1, tk, n3), lambda l, k: (l, k, 0)),
            pl.BlockSpec((depth, n3), lambda l, k: (0, 0)),
        ],
        out_specs=pl.BlockSpec((1, ADA_ROWS, n3), lambda l, k: (l, 0, 0)),
        out_shape=jax.ShapeDtypeStruct((depth, ADA_ROWS, n3), F32),
        compiler_params=pltpu.CompilerParams(dimension_semantics=("arbitrary", "arbitrary")),
        name="ada_mod",
    )(cvec, w_ada, b_ada)


def _qkv_kernel(*refs, rope, nsub, first_layer, mod_row, tps):
    if rope:
        (x_ref, mod_ref, g_ref, w_ref, gains_ref, bd_ref, cos_ref, sin_ref,
         q_ref, k_ref, v_ref) = refs
    else:
        x_ref, mod_ref, g_ref, w_ref, gains_ref, bd_ref = refs[:6]
        q_ref, k_ref, v_ref = refs[-3:]
    tm = x_ref.shape[0]
    sub = tm // nsub
    bd = bd_ref[...]
    g = g_ref[...]
    mod = _mod_row(mod_ref, mod_row, tps)
    if rope:
        lane = lax.broadcasted_iota(jnp.int32, (sub, LANES), 1)
        second = (lane % (HEAD_DIM // 2)) >= (HEAD_DIM // 4)
    k_parts, v_parts = [None] * nsub, [None] * nsub

    def finish(s, c, part):
        rows = slice(s * sub, (s + 1) * sub)
        for h in range(2):
            j = 2 * c + h
            t = part[:, h * LANES:(h + 1) * LANES]
            if j * LANES >= QK_W:
                v_parts[s] = t
                continue
            sl = slice(j * LANES, (j + 1) * LANES)
            msq = _dot((t * t).astype(BF16), bd)
            t = t * lax.rsqrt(msq + EPS) * gains_ref[0, :, sl]
            if rope:
                partner = jnp.where(second, pltpu.roll(t, HEAD_DIM // 4, 1),
                                    pltpu.roll(t, LANES - HEAD_DIM // 4, 1))
                t = t * cos_ref[rows, :] + partner * sin_ref[rows, :]
            if j * LANES < ATTN_W:
                q_ref[rows, sl] = t.astype(BF16)
            else:
                k_parts[s] = t

    ncol = QKV_W // (2 * LANES)
    prev = None
    for s in range(nsub + 1):
        if s < nsub:
            xn = _mod_norm(x_ref[s * sub:(s + 1) * sub, :], g, mod).astype(BF16)
        cur = []
        for c in range(ncol):
            if s < nsub:
                cur.append(_dot(xn, w_ref[0, :, c * 2 * LANES:(c + 1) * 2 * LANES]))
            if prev is not None:
                finish(s - 1, c, prev[c])
        prev = cur
    for ref, parts in ((k_ref, k_parts), (v_ref, v_parts)):
        val = jnp.concatenate(parts, axis=0)
        if rope:
            ref[...] = val
            continue
        seq = ref.shape[3]
        for p in range(ref.shape[0]):
            piece = val[p * seq:(p + 1) * seq, :].T
            if first_layer is None:
                ref[p, 0] = piece
            else:
                for l2 in range(ref.shape[1]):
                    ref[p, l2] = piece if l2 == first_layer else jnp.zeros_like(piece)


def _mod_spec(layer):
    return pl.BlockSpec((1, ADA_ROWS, 3 * D_MODEL), lambda i: (layer, 0, 0))


def _qkv(x2, mod, mod_row, norm_g, w_in_bf, layer, gains, bd, tables, kv_prev, *, seq, tm):
    tokens = x2.shape[0]
    depth = w_in_bf.shape[0]
    tps = max(seq // tm, 1)
    rope = tables is not None
    row = lambda i: (i, 0)
    const = lambda i: (0, 0)
    in_specs = [
        pl.BlockSpec((tm, D_MODEL), row),
        _mod_spec(layer),
        pl.BlockSpec((1, D_MODEL), const),
        pl.BlockSpec((1, D_MODEL, QKV_W), lambda i: (layer, 0, 0)),
        pl.BlockSpec((1, 1, QK_W), lambda i: (layer, 0, 0)),
        pl.BlockSpec((LANES, LANES), const),
    ]
    args = [x2, mod, norm_g.reshape(1, D_MODEL), w_in_bf, gains, bd]
    out_shapes = [jax.ShapeDtypeStruct((tokens, ATTN_W), BF16)]
    out_specs = [pl.BlockSpec((tm, ATTN_W), row)]
    aliases = {}
    first_layer = None
    if rope:
        in_specs += [pl.BlockSpec((tm, LANES), lambda i: (i % tps, 0))] * 2
        args += list(tables)
        out_shapes += [jax.ShapeDtypeStruct((tokens, KV_W), F32)] * 2
        out_specs += [pl.BlockSpec((tm, KV_W), row)] * 2
    else:
        nseq = tm // seq
        out_shapes += [jax.ShapeDtypeStruct((tokens // seq, depth, KV_W, seq), F32)] * 2
        if kv_prev is None:
            first_layer = layer
            out_specs += [pl.BlockSpec((nseq, depth, KV_W, seq), lambda i: (i, 0, 0, 0))] * 2
        else:
            out_specs += [pl.BlockSpec((nseq, 1, KV_W, seq), lambda i: (i, layer, 0, 0))] * 2
            in_specs += [pl.BlockSpec(memory_space=pl.ANY)] * 2
            aliases = {len(args): 1, len(args) + 1: 2}
            args += list(kv_prev)
    return pl.pallas_call(
        functools.partial(_qkv_kernel, rope=rope, nsub=tm // QKV_SUB, first_layer=first_layer,
                          mod_row=mod_row, tps=tps),
        grid=(tokens // tm,),
        in_specs=in_specs,
        out_specs=out_specs,
        out_shape=out_shapes,
        input_output_aliases=aliases,
        compiler_params=pltpu.CompilerParams(
            dimension_semantics=("arbitrary",), vmem_limit_bytes=VMEM_LIMIT),
        name="qkv_rope" if rope else "qkv",
    )(*args)


VT_ROWS = HEAD_DIM + 16


def _attn_kernel(q_ref, k_ref, v_ref, ck_ref, cv_ref, o_ref, kab, vt, q2, acct, mbuf, *bufs, n, n_ctx, rb):
    kv = pl.program_id(1)
    nr = 2 * n // rb
    t_all = n + n_ctx
    sbuf, pbuf = bufs[:ATTN_BUFS], bufs[ATTN_BUFS:]

    def fill(k_src, v_t, r0, m):
        lane = lax.broadcasted_iota(jnp.int32, (m, LANES), 1)
        k_own = jnp.where((lane // HEAD_DIM) == kv, k_src, 0.0)
        k_oth = pltpu.roll(k_own, HEAD_DIM, 1)
        is0 = kv == 0
        kab[0, r0:r0 + m, :] = jnp.where(is0, k_own, k_oth).astype(BF16)
        kab[1, r0:r0 + m, :] = jnp.where(is0, k_oth, k_own).astype(BF16)
        vt[0:HEAD_DIM, r0:r0 + m] = jnp.where(is0, v_t[0:HEAD_DIM, :], v_t[HEAD_DIM:, :]).astype(BF16)

    piece = 256
    for r0 in range(0, n, piece):
        fill(k_ref[0, r0:r0 + piece, :], v_ref[0, r0:r0 + piece, :].T, r0, piece)
    fill(ck_ref[0, 0].T, cv_ref[0, 0], n, n_ctx)
    vt[HEAD_DIM:, :] = jnp.ones((VT_ROWS - HEAD_DIM, t_all), BF16)
    for c0 in range(0, n, piece):
        q2[:, c0:c0 + piece] = q_ref[0, c0:c0 + piece, 0:LANES].astype(F32).T.astype(BF16)
        q2[:, n + c0:n + c0 + piece] = q_ref[0, c0:c0 + piece, LANES:].astype(F32).T.astype(BF16)

    span = ATTN_REGION
    n_regions = 2 * nr // span
    chunks = [slice(k0, k0 + ATTN_KEY_CHUNK) for k0 in range(0, t_all, ATTN_KEY_CHUNK)]

    def score_chunk(slot, ks, qr, buf, run):
        sc = _dot(kab[slot, ks, :], qr)
        sbuf[buf][ks, :] = sc
        while sc.shape[0] > 8:
            half = sc.shape[0] // 2
            sc = jnp.maximum(sc[0:half, :], sc[half:, :])
        return sc if run is None else jnp.maximum(run, sc)

    def q_rows(r):
        return q2[:, pl.ds(pl.multiple_of(r * rb, rb), rb)]

    per_tile = n // rb

    def flush(r):
        r0 = (r % per_tile) * rb
        if not isinstance(r, int):
            r0 = pl.multiple_of(r0, rb)
        o_ref[0, r // per_tile, pl.ds(r0, rb), :] = acct[r].T

    def region(g, parity, do_scores, do_numer, do_weigh, do_flush):
        cur, oth = parity * span, (1 - parity) * span
        for j in range(span):
            slot = j % 2
            if do_flush and slot == 0:
                flush((g - 2) * (span // 2) + j // 2)
            if do_numer:
                m = mbuf[cur + j]
            if do_scores:
                qr = q_rows((g + 1) * (span // 2) + j // 2)
            run, res = None, None
            for ks in chunks:
                if do_scores:
                    run = score_chunk(slot, ks, qr, oth + j, run)
                mid = (ks.start + ks.stop) // 2
                ka, kb = slice(ks.start, mid), slice(mid, ks.stop)
                if do_numer:
                    pbuf[cur + j][ka, :] = jnp.exp(sbuf[cur + j][ka, :] - m).astype(BF16)
                if do_weigh:
                    part = _dot(vt[:, ks], pbuf[oth + j][ks, :])
                    res = part if res is None else res + part
                if do_numer:
                    pbuf[cur + j][kb, :] = jnp.exp(sbuf[cur + j][kb, :] - m).astype(BF16)
            if do_scores:
                mbuf[oth + j] = jnp.max(run, axis=0, keepdims=True)
            if do_weigh:
                acct[(g - 1) * (span // 2) + j // 2, slot * HEAD_DIM:(slot + 1) * HEAD_DIM, :] = (
                    res[0:HEAD_DIM, :] * (1.0 / res[HEAD_DIM:HEAD_DIM + 1, :]))

    region(-1, 1, True, False, False, False)
    region(0, 0, True, True, False, False)

    def body(g, carry):
        for parity in range(2):
            pl.when(g % 2 == parity)(lambda: region(g, parity, True, True, True, False))
        return carry

    lax.fori_loop(1, n_regions - 1, body, 0)
    region(n_regions - 1, (n_regions - 1) % 2, False, True, True, False)
    region(n_regions, n_regions % 2, False, False, True, False)
    for r in range(nr):
        flush(r)


def _attention(q, k, v, cache_k, cache_v, layer):
    rb = ATTN_ROWS
    bsz, n, _ = q.shape
    n_ctx = cache_k.shape[3]
    t_all = n + n_ctx
    gw = 2 * LANES
    qmap = lambda b, g: (b, 0, g)
    kmap = lambda b, g: (b, 0, 0)
    cmap = lambda b, g: (b, layer, 0, 0)
    return pl.pallas_call(
        functools.partial(_attn_kernel, n=n, n_ctx=n_ctx, rb=rb),
        grid=(bsz, N_KV_HEADS),
        in_specs=[
            pl.BlockSpec((1, n, gw), qmap),
            pl.BlockSpec((1, n, KV_W), kmap),
            pl.BlockSpec((1, n, KV_W), kmap),
            pl.BlockSpec((1, 1, KV_W, n_ctx), cmap),
            pl.BlockSpec((1, 1, KV_W, n_ctx), cmap),
        ],
        out_specs=pl.BlockSpec((1, 2, n, LANES), lambda b, g: (b, g, 0, 0)),
        out_shape=jax.ShapeDtypeStruct((bsz, ATTN_W // LANES, n, LANES), F32),
        scratch_shapes=[
            pltpu.VMEM((2, t_all, LANES), BF16),
            pltpu.VMEM((VT_ROWS, t_all), BF16),
            pltpu.VMEM((LANES, 2 * n), BF16),
            pltpu.VMEM((2 * n // rb, LANES, rb), F32),
            pltpu.VMEM((ATTN_BUFS, 1, rb), F32),
        ] + [pltpu.VMEM((t_all, rb), F32)] * ATTN_BUFS + [pltpu.VMEM((t_all, rb), BF16)] * ATTN_BUFS,
        compiler_params=pltpu.CompilerParams(
            dimension_semantics=("arbitrary", "arbitrary"), vmem_limit_bytes=ATTN_VMEM_LIMIT),
        name="attn_lat",
    )(q, k, v, cache_k, cache_v)


def _attn_ctx_kernel(q_ref, k_ref, v_ref, o_ref, *, n):
    low = lax.broadcasted_iota(jnp.int32, (n, LANES), 1) < HEAD_DIM
    ones_lo = jnp.where(low, 1.0, 0.0)
    zk = jnp.zeros((HEAD_DIM, n), F32)
    gw = 2 * LANES
    for b in range(CTX_BATCH):
        kt_all = k_ref[b, 0]
        v = v_ref[b, 0].T
        vr = pltpu.roll(v, HEAD_DIM, 1)
        for g in range(N_KV_HEADS):
            kg = kt_all[g * HEAD_DIM:(g + 1) * HEAD_DIM, :]
            kts = (jnp.concatenate([kg, zk], axis=0).astype(BF16),
                   jnp.concatenate([zk, kg], axis=0).astype(BF16))
            v_lo, v_hi = (v, vr) if g == 0 else (vr, v)
            vs = (jnp.concatenate([jnp.where(low, v_lo, 0.0), ones_lo], axis=1).astype(BF16),
                  jnp.concatenate([jnp.where(low, 0.0, v_hi), 1.0 - ones_lo], axis=1).astype(BF16))
            q2 = jnp.concatenate([q_ref[b, :, g * gw:g * gw + LANES],
                                  q_ref[b, :, g * gw + LANES:(g + 1) * gw]], axis=0)
            acc = None
            for kt_s, v_s in zip(kts, vs):
                s = _dot(q2, kt_s)
                m = jnp.max(s, axis=-1, keepdims=True)
                pv = _dot(jnp.exp(s - m).astype(BF16), v_s)
                acc = pv if acc is None else acc + pv
            o = acc[:, 0:LANES] / acc[:, LANES:]
            o_ref[b, 2 * g] = o[0:n]
            o_ref[b, 2 * g + 1] = o[n:]


def _attention_ctx(q, k, v, layer):
    bsz, n, _ = q.shape
    blk = lambda w: pl.BlockSpec((CTX_BATCH, n, w), lambda b: (b, 0, 0))
    kv_blk = pl.BlockSpec((CTX_BATCH, 1, KV_W, n), lambda b: (b, layer, 0, 0))
    return pl.pallas_call(
        functools.partial(_attn_ctx_kernel, n=n),
        grid=(bsz // CTX_BATCH,),
        in_specs=[blk(ATTN_W), kv_blk, kv_blk],
        out_specs=pl.BlockSpec((CTX_BATCH, ATTN_W // LANES, n, LANES), lambda b: (b, 0, 0, 0)),
        out_shape=jax.ShapeDtypeStruct((bsz, ATTN_W // LANES, n, LANES), F32),
        compiler_params=pltpu.CompilerParams(
            dimension_semantics=("arbitrary",), vmem_limit_bytes=VMEM_LIMIT),
        name="attn_ctx",
    )(q, k, v)


def _mix_kernel(attn_ref, x_ref, xp_ref, xn_ref, mod_ref, g_ref, w_ref, wout_ref,
                convw_ref, convb_ref, wpool_ref, pscale_ref, fg_ref,
                o_ref, xe, yext, uext, *, tm, tps, seq, final, mod_row):
    mod = _mod_row(mod_ref, mod_row, tps)
    j = pl.program_id(0) % tps
    pm = jnp.where(j == 0, 0.0, 1.0)
    nm = jnp.where(j == tps - 1, 0.0, 1.0)
    c = CONV_W
    g = g_ref[...]
    x = x_ref[...]
    xe[0:tm, :] = _mod_norm(x, g, mod).astype(BF16)
    if tps > 1:
        halo = jnp.concatenate([xn_ref[...], xp_ref[...]], axis=0)
        xe[tm:, :] = _mod_norm(halo, g, mod).astype(BF16)

    def proj(rows, off, width):
        return _dot(xe[0:rows, :], w_ref[0, :, off:off + width])

    nseq = max(tm // seq, 1)
    whole = tps == 1
    ln = tm // nseq
    hrows = tm if whole else tm + 2 * HALO
    h_all = proj(hrows, COL_CONV_H, c)
    c_all = proj(hrows, COL_CONV_C, c)
    u_all = proj(hrows, COL_POOL_U, POOL_W)
    z_a = proj(tm, COL_ATTN_GATE, ATTN_W)
    a = ATTN_W
    attn = jnp.concatenate(
        [jnp.concatenate([attn_ref[p, t] for t in range(attn_ref.shape[1])], axis=1)
         for p in range(attn_ref.shape[0])], axis=0)
    attn_o = (attn * _silu(z_a)).astype(BF16)
    out_a = _dot(attn_o, wout_ref[0, 0:a, :])
    b_c = proj(tm, COL_CONV_B, c)
    z_c = proj(tm, COL_CONV_GATE, c)
    z_p = proj(tm, COL_POOL_GATE, POOL_W)

    def extend(ext, p, val):
        zeros = jnp.zeros((HALO, val.shape[1]), F32)
        ext[p, 0:HALO, :] = zeros if whole else val[tm + HALO:, :] * pm
        ext[p, HALO:HALO + ln, :] = val[p * ln:(p + 1) * ln, :]
        ext[p, HALO + ln:, :] = zeros if whole else val[tm:tm + HALO, :] * nm

    y = h_all * c_all
    convs = []
    for p in range(nseq):
        extend(yext, p, y)
        convs.append(yext[p, HALO - 1:HALO - 1 + ln, :] * convw_ref[0:1, :]
                     + yext[p, HALO:HALO + ln, :] * convw_ref[1:2, :]
                     + yext[p, HALO + 1:HALO + 1 + ln, :] * convw_ref[2:3, :] + convb_ref[...])
    conv = convs[0] if nseq == 1 else jnp.concatenate(convs, axis=0)
    conv_o = (b_c * conv * _silu(z_c)).astype(BF16)

    lane = lax.broadcasted_iota(jnp.int32, (ln, LANES), 1)
    pos = lax.broadcasted_iota(jnp.int32, (ln, LANES), 0) + (0 if whole else j * tm)
    low = lane < POOL_GROUP
    d_rows = []
    for p in range(nseq):
        extend(uext, p, u_all)

        def win(k0, k1, sl):
            tot = None
            for k in range(k0, k1):
                t = uext[p, HALO + k:HALO + k + ln, sl]
                tot = t if tot is None else tot + t
            return tot

        ds = []
        for ti, (wa, wb) in enumerate(((2, 4), (8, 16))):
            sl = slice(ti * LANES, (ti + 1) * LANES)
            s_a = win(-(wa // 2), wa // 2, sl)
            s_b = s_a + win(-(wb // 2), -(wa // 2), sl) + win(wa // 2, wb // 2, sl)
            half = jnp.where(low, wa // 2, wb // 2)
            cnt = jnp.minimum(pos + half - 1, seq - 1) - jnp.maximum(pos - half, 0) + 1
            tot = jnp.where(low, s_a, s_b)
            ds.append(tot / cnt.astype(F32) - u_all[p * ln:(p + 1) * ln, sl])
        d_rows.append(jnp.concatenate(ds, axis=-1))
    d = (d_rows[0] if nseq == 1 else jnp.concatenate(d_rows, axis=0)).astype(BF16)
    pool_o = (_dot(d, wpool_ref[...]) * pscale_ref[...] * _silu(z_p)).astype(BF16)

    out = out_a + _dot(conv_o, wout_ref[0, a:a + c, :]) + _dot(pool_o, wout_ref[0, a + c:, :])
    res = x + mod[:, 2 * D_MODEL:] * out
    if final:
        ms = jnp.mean(res * res, axis=-1, keepdims=True)
        res = res * lax.rsqrt(ms + EPS) * fg_ref[...]
    o_ref[...] = res


def _mix(attn2, x2, mod, mod_row, norm_g, w_in_bf, wout_bf, layer, conv_w, conv_b, wpool_bd, pool_scale,
         final_g, *, seq, tm, final):
    tokens = x2.shape[0]
    tps = max(seq // tm, 1)
    nseq = max(tm // seq, 1)
    hb = tm // HALO
    nhb = tokens // HALO
    row = lambda i: (i, 0)
    const = lambda i: (0, 0)
    in_specs = [
        (pl.BlockSpec((nseq, ATTN_W // LANES, tm // nseq, LANES), lambda i: (i, 0, 0, 0)) if tps == 1 else
         pl.BlockSpec((1, ATTN_W // LANES, tm, LANES), lambda i: (i // tps, 0, i % tps, 0))),
        pl.BlockSpec((tm, D_MODEL), row),
        pl.BlockSpec((HALO, D_MODEL), lambda i: (jnp.maximum(i * hb - 1, 0), 0)),
        pl.BlockSpec((HALO, D_MODEL), lambda i: (jnp.minimum((i + 1) * hb, nhb - 1), 0)),
        _mod_spec(layer),
        pl.BlockSpec((1, D_MODEL), const),
        pl.BlockSpec((1, D_MODEL, IN_W), lambda i: (layer, 0, 0)),
        pl.BlockSpec((1, D_MODEL, D_MODEL), lambda i: (layer, 0, 0)),
        pl.BlockSpec((3, CONV_W), const),
        pl.BlockSpec((1, CONV_W), const),
        pl.BlockSpec((POOL_W, POOL_W), const),
        pl.BlockSpec((1, POOL_W), const),
        pl.BlockSpec((1, D_MODEL), const),
    ]
    return pl.pallas_call(
        functools.partial(_mix_kernel, tm=tm, tps=tps, seq=seq, final=final, mod_row=mod_row),
        grid=(tokens // tm,),
        in_specs=in_specs,
        out_specs=pl.BlockSpec((tm, D_MODEL), row),
        out_shape=jax.ShapeDtypeStruct((tokens, D_MODEL), F32),
        scratch_shapes=[
            pltpu.VMEM((tm + 2 * HALO, D_MODEL), BF16),
            pltpu.VMEM((nseq, tm // nseq + 2 * HALO, CONV_W), F32),
            pltpu.VMEM((nseq, tm // nseq + 2 * HALO, POOL_W), F32),
        ],
        compiler_params=pltpu.CompilerParams(
            dimension_semantics=("arbitrary",), vmem_limit_bytes=VMEM_LIMIT),
        name="mix_final" if final else "mix",
    )(attn2, x2, x2, x2, mod, norm_g.reshape(1, D_MODEL), w_in_bf, wout_bf,
      conv_w, conv_b.reshape(1, CONV_W), wpool_bd, pool_scale.reshape(1, POOL_W),
      final_g.reshape(1, D_MODEL))


def _rope_tables(n):
    rows = n // GRID_W
    row = np.repeat(np.arange(rows), GRID_W).astype(np.float64)
    col = np.tile(np.arange(GRID_W), rows).astype(np.float64)
    half = HEAD_DIM // 2
    inv = 1.0 / (ROPE_THETA ** (np.arange(0, half, 2, dtype=np.float64) / half))
    ar = row[:, None] * inv
    ac = col[:, None] * inv
    ang = np.tile(np.concatenate([ar, ar, ac, ac], axis=-1), (1, LANES // HEAD_DIM))
    second = (np.arange(LANES) % half) >= (half // 2)
    return jnp.asarray(np.cos(ang), F32), jnp.asarray(np.where(second, np.sin(ang), -np.sin(ang)), F32)


def _block_diag(blocks):
    n = len(blocks)
    rows = []
    for i, b in enumerate(blocks):
        rows.append(jnp.concatenate(
            [b if j == i else jnp.zeros((b.shape[0], blocks[j].shape[1]), b.dtype) for j in range(n)],
            axis=1))
    return jnp.concatenate(rows, axis=0)


def kernel(x_prompt, x_sample, cache_k, cache_v, c, c_ctx, norm_g, w_ada, b_ada, w_in,
           q_norm_g, k_norm_g, conv_w, conv_b, pool_w, pool_scale, w_out, final_g):
    depth = w_in.shape[0]
    bp, sp, _ = x_prompt.shape
    bs, ss, _ = x_sample.shape
    n_ctx = cache_k.shape[2]

    cvec = jnp.concatenate(
        [c, c_ctx[None, :], jnp.zeros((ADA_ROWS - bs - 1, D_MODEL), F32)], axis=0)
    mod = _ada(cvec, w_ada, b_ada)

    w_in_bf = w_in.astype(BF16)
    w_out_bf = w_out.astype(BF16)
    tables = _rope_tables(ss)
    ones = jnp.full((HEAD_DIM, HEAD_DIM), 1.0 / HEAD_DIM, BF16)
    bd = _block_diag([ones, ones])
    ck = cache_k.transpose(0, 1, 3, 4, 2).reshape(bs, depth, KV_W, n_ctx)
    cv = cache_v.transpose(0, 1, 3, 4, 2).reshape(bs, depth, KV_W, n_ctx)

    h = x_prompt.reshape(bp * sp, D_MODEL)
    z = x_sample.reshape(bs * ss, D_MODEL)
    kv_ctx = None
    gains = jnp.concatenate(
        [jnp.tile(q_norm_g * (HEAD_DIM ** -0.5), (1, N_HEADS)), jnp.tile(k_norm_g, (1, N_KV_HEADS))],
        axis=-1).reshape(depth, 1, QK_W)
    for l in range(depth):
        wpool_bd = _block_diag([pool_w[l, g] for g in range(pool_w.shape[1])]).astype(BF16)
        final = l == depth - 1
        mix = functools.partial(
            _mix, norm_g=norm_g[l], w_in_bf=w_in_bf, wout_bf=w_out_bf, layer=l,
            conv_w=conv_w[l], conv_b=conv_b[l], wpool_bd=wpool_bd, pool_scale=pool_scale[l],
            final_g=final_g, final=final)

        q, k, v = _qkv(h, mod, bs, norm_g[l], w_in_bf, l, gains, bd, None, kv_ctx, seq=sp, tm=QKV_TILE)
        kv_ctx = (k, v)
        attn = _attention_ctx(q.reshape(bp, sp, ATTN_W), k, v, l)
        h = mix(attn, h, mod, bs, seq=sp, tm=MIX_TILE)

        q, k, v = _qkv(z, mod, None, norm_g[l], w_in_bf, l, gains, bd, tables, None, seq=ss, tm=QKV_TILE)
        attn = _attention(q.reshape(bs, ss, ATTN_W), k.reshape(bs, ss, KV_W), v.reshape(bs, ss, KV_W),
                          ck, cv, l)
        z = mix(attn, z, mod, None, seq=ss, tm=MIX_TILE)

    y_prompt = h.reshape(bp, sp, D_MODEL)
    y_sample = z.reshape(bs, ss, D_MODEL)
    new_k, new_v = (a.reshape(bp, depth, N_KV_HEADS, HEAD_DIM, sp).transpose(0, 1, 4, 2, 3) for a in kv_ctx)
    return (y_prompt, y_sample, new_k, new_v)
```

```python
import functools

import jax
import jax.numpy as jnp
import numpy as np
from jax import lax
from jax.experimental import pallas as pl
from jax.experimental.pallas import tpu as pltpu

D_MODEL = 1024
GRID_W = 64
N_HEADS = 8
N_KV_HEADS = 2
HEAD_DIM = 64
ATTN_W = N_HEADS * HEAD_DIM
KV_W = N_KV_HEADS * HEAD_DIM
QK_W = ATTN_W + KV_W
QKV_W = QK_W + KV_W
CONV_W = 256
POOL_W = 256
POOL_GROUP = 64
IN_W = 2816
COL_ATTN_GATE = QKV_W
COL_CONV_H = COL_ATTN_GATE + ATTN_W
COL_CONV_B = COL_CONV_H + CONV_W
COL_CONV_C = COL_CONV_B + CONV_W
COL_CONV_GATE = COL_CONV_C + CONV_W
COL_POOL_U = COL_CONV_GATE + CONV_W
COL_POOL_GATE = COL_POOL_U + POOL_W
ROPE_THETA = 10000.0
EPS = 1e-6
LANES = 128
HALO = 8
ADA_ROWS = 8
VMEM_LIMIT = 48 * 1024 * 1024
ATTN_VMEM_LIMIT = 56 * 1024 * 1024

ADA_K = 512
QKV_TILE = 1024
QKV_SUB = 256
MIX_TILE = 512
ATTN_ROWS = 256
ATTN_REGION = 4
ATTN_BUFS = 2 * ATTN_REGION
ATTN_KEY_CHUNK = 256
CTX_BATCH = 4

F32 = jnp.float32
BF16 = jnp.bfloat16


def _silu(z):
    return z * (1.0 / (1.0 + jnp.exp(-z)))


def _dot(a, b):
    return jnp.dot(a, b, preferred_element_type=F32)


def _mod_row(mod_ref, mod_row, tps):
    row = mod_row if mod_row is not None else pl.program_id(0) // tps
    return mod_ref[0, pl.ds(row, 1), :]


def _mod_norm(x, g, mod):
    ms = jnp.mean(x * x, axis=-1, keepdims=True)
    gs = g * (1.0 + mod[:, D_MODEL:2 * D_MODEL])
    return x * lax.rsqrt(ms + EPS) * gs + mod[:, 0:D_MODEL]


def _ada_kernel(c_ref, w_ref, b_ref, o_ref):
    part = _dot(_silu(c_ref[...]).astype(BF16), w_ref[0].astype(BF16))

    @pl.when(pl.program_id(1) == 0)
    def _first():
        o_ref[0] = part + b_ref[pl.ds(pl.program_id(0), 1), :]

    @pl.when(pl.program_id(1) > 0)
    def _rest():
        o_ref[0] += part


def _ada(cvec, w_ada, b_ada):
    depth, _, n3 = w_ada.shape
    tk = ADA_K
    return pl.pallas_call(
        _ada_kernel,
        grid=(depth, D_MODEL // tk),
        in_specs=[
            pl.BlockSpec((ADA_ROWS, tk), lambda l, k: (0, k)),
            pl.BlockSpec((1, tk, n3), lambda l, k: (l, k, 0)),
            pl.BlockSpec((depth, n3), lambda l, k: (0, 0)),
        ],
        out_specs=pl.BlockSpec((1, ADA_ROWS, n3), lambda l, k: (l, 0, 0)),
        out_shape=jax.ShapeDtypeStruct((depth, ADA_ROWS, n3), F32),
        compiler_params=pltpu.CompilerParams(dimension_semantics=("arbitrary", "arbitrary")),
        name="ada_mod",
    )(cvec, w_ada, b_ada)


def _qkv_kernel(*refs, rope, nsub, first_layer, mod_row, tps):
    if rope:
        (x_ref, mod_ref, g_ref, w_ref, gains_ref, bd_ref, cos_ref, sin_ref,
         q_ref, k_ref, v_ref) = refs
    else:
        x_ref, mod_ref, g_ref, w_ref, gains_ref, bd_ref = refs[:6]
        q_ref, k_ref, v_ref = refs[-3:]
    tm = x_ref.shape[0]
    sub = tm // nsub
    bd = bd_ref[...]
    g = g_ref[...]
    mod = _mod_row(mod_ref, mod_row, tps)
    if rope:
        lane = lax.broadcasted_iota(jnp.int32, (sub, LANES), 1)
        second = (lane % (HEAD_DIM // 2)) >= (HEAD_DIM // 4)
    k_parts, v_parts = [None] * nsub, [None] * nsub

    def finish(s, c, part):
        rows = slice(s * sub, (s + 1) * sub)
        for h in range(2):
            j = 2 * c + h
            t = part[:, h * LANES:(h + 1) * LANES]
            if j * LANES >= QK_W:
                v_parts[s] = t
                continue
            sl = slice(j * LANES, (j + 1) * LANES)
            msq = _dot((t * t).astype(BF16), bd)
            t = t * lax.rsqrt(msq + EPS) * gains_ref[0, :, sl]
            if rope:
                partner = jnp.where(second, pltpu.roll(t, HEAD_DIM // 4, 1),
                                    pltpu.roll(t, LANES - HEAD_DIM // 4, 1))
                t = t * cos_ref[rows, :] + partner * sin_ref[rows, :]
            if j * LANES < ATTN_W:
                q_ref[rows, sl] = t.astype(BF16)
            else:
                k_parts[s] = t

    ncol = QKV_W // (2 * LANES)
    prev = None
    for s in range(nsub + 1):
        if s < nsub:
            xn = _mod_norm(x_ref[s * sub:(s + 1) * sub, :], g, mod).astype(BF16)
        cur = []
        for c in range(ncol):
            if s < nsub:
                cur.append(_dot(xn, w_ref[0, :, c * 2 * LANES:(c + 1) * 2 * LANES]))
            if prev is not None:
                finish(s - 1, c, prev[c])
        prev = cur
    for ref, parts in ((k_ref, k_parts), (v_ref, v_parts)):
        val = jnp.concatenate(parts, axis=0)
        if rope:
            ref[...] = val
            continue
        seq = ref.shape[3]
        for p in range(ref.shape[0]):
            piece = val[p * seq:(p + 1) * seq, :].T
            if first_layer is None:
                ref[p, 0] = piece
            else:
                for l2 in range(ref.shape[1]):
                    ref[p, l2] = piece if l2 == first_layer else jnp.zeros_like(piece)


def _mod_spec(layer):
    return pl.BlockSpec((1, ADA_ROWS, 3 * D_MODEL), lambda i: (layer, 0, 0))


def _qkv(x2, mod, mod_row, norm_g, w_in_bf, layer, gains, bd, tables, kv_prev, *, seq, tm):
    tokens = x2.shape[0]
    depth = w_in_bf.shape[0]
    tps = max(seq // tm, 1)
    rope = tables is not None
    row = lambda i: (i, 0)
    const = lambda i: (0, 0)
    in_specs = [
        pl.BlockSpec((tm, D_MODEL), row),
        _mod_spec(layer),
        pl.BlockSpec((1, D_MODEL), const),
        pl.BlockSpec((1, D_MODEL, QKV_W), lambda i: (layer, 0, 0)),
        pl.BlockSpec((1, 1, QK_W), lambda i: (layer, 0, 0)),
        pl.BlockSpec((LANES, LANES), const),
    ]
    args = [x2, mod, norm_g.reshape(1, D_MODEL), w_in_bf, gains, bd]
    out_shapes = [jax.ShapeDtypeStruct((tokens, ATTN_W), BF16)]
    out_specs = [pl.BlockSpec((tm, ATTN_W), row)]
    aliases = {}
    first_layer = None
    if rope:
        in_specs += [pl.BlockSpec((tm, LANES), lambda i: (i % tps, 0))] * 2
        args += list(tables)
        out_shapes += [jax.ShapeDtypeStruct((tokens, KV_W), F32)] * 2
        out_specs += [pl.BlockSpec((tm, KV_W), row)] * 2
    else:
        nseq = tm // seq
        out_shapes += [jax.ShapeDtypeStruct((tokens // seq, depth, KV_W, seq), F32)] * 2
        if kv_prev is None:
            first_layer = layer
            out_specs += [pl.BlockSpec((nseq, depth, KV_W, seq), lambda i: (i, 0, 0, 0))] * 2
        else:
            out_specs += [pl.BlockSpec((nseq, 1, KV_W, seq), lambda i: (i, layer, 0, 0))] * 2
            in_specs += [pl.BlockSpec(memory_space=pl.ANY)] * 2
            aliases = {len(args): 1, len(args) + 1: 2}
            args += list(kv_prev)
    return pl.pallas_call(
        functools.partial(_qkv_kernel, rope=rope, nsub=tm // QKV_SUB, first_layer=first_layer,
                          mod_row=mod_row, tps=tps),
        grid=(tokens // tm,),
        in_specs=in_specs,
        out_specs=out_specs,
        out_shape=out_shapes,
        input_output_aliases=aliases,
        compiler_params=pltpu.CompilerParams(
            dimension_semantics=("arbitrary",), vmem_limit_bytes=VMEM_LIMIT),
        name="qkv_rope" if rope else "qkv",
    )(*args)


VT_ROWS = HEAD_DIM + 16


def _attn_kernel(q_ref, k_ref, v_ref, ck_ref, cv_ref, o_ref, kab, vt, q2, acct, mbuf, *bufs, n, n_ctx, rb):
    kv = pl.program_id(1)
    nr = 2 * n // rb
    t_all = n + n_ctx
    sbuf, pbuf = bufs[:ATTN_BUFS], bufs[ATTN_BUFS:]

    def fill(k_src, v_t, r0, m):
        lane = lax.broadcasted_iota(jnp.int32, (m, LANES), 1)
        k_own = jnp.where((lane // HEAD_DIM) == kv, k_src, 0.0)
        k_oth = pltpu.roll(k_own, HEAD_DIM, 1)
        is0 = kv == 0
        kab[0, r0:r0 + m, :] = jnp.where(is0, k_own, k_oth).astype(BF16)
        kab[1, r0:r0 + m, :] = jnp.where(is0, k_oth, k_own).astype(BF16)
        vt[0:HEAD_DIM, r0:r0 + m] = jnp.where(is0, v_t[0:HEAD_DIM, :], v_t[HEAD_DIM:, :]).astype(BF16)

    piece = 256
    for r0 in range(0, n, piece):
        fill(k_ref[0, r0:r0 + piece, :], v_ref[0, r0:r0 + piece, :].T, r0, piece)
    fill(ck_ref[0, 0].T, cv_ref[0, 0], n, n_ctx)
    vt[HEAD_DIM:, :] = jnp.ones((VT_ROWS - HEAD_DIM, t_all), BF16)
    for c0 in range(0, n, piece):
        q2[:, c0:c0 + piece] = q_ref[0, c0:c0 + piece, 0:LANES].astype(F32).T.astype(BF16)
        q2[:, n + c0:n + c0 + piece] = q_ref[0, c0:c0 + piece, LANES:].astype(F32).T.astype(BF16)

    span = ATTN_REGION
    n_regions = 2 * nr // span
    chunks = [slice(k0, k0 + ATTN_KEY_CHUNK) for k0 in range(0, t_all, ATTN_KEY_CHUNK)]

    def score_chunk(slot, ks, qr, buf, run):
        sc = _dot(kab[slot, ks, :], qr)
        sbuf[buf][ks, :] = sc
        while sc.shape[0] > 8:
            half = sc.shape[0] // 2
            sc = jnp.maximum(sc[0:half, :], sc[half:, :])
        return sc if run is None else jnp.maximum(run, sc)

    def q_rows(r):
        return q2[:, pl.ds(pl.multiple_of(r * rb, rb), rb)]

    per_tile = n // rb

    def flush(r):
        r0 = (r % per_tile) * rb
        if not isinstance(r, int):
            r0 = pl.multiple_of(r0, rb)
        o_ref[0, r // per_tile, pl.ds(r0, rb), :] = acct[r].T

    def region(g, parity, do_scores, do_numer, do_weigh, do_flush):
        cur, oth = parity * span, (1 - parity) * span
        for j in range(span):
            slot = j % 2
            if do_flush and slot == 0:
                flush((g - 2) * (span // 2) + j // 2)
            if do_numer:
                m = mbuf[cur + j]
            if do_scores:
                qr = q_rows((g + 1) * (span // 2) + j // 2)
            run, res = None, None
            for ks in chunks:
                mid = (ks.start + ks.stop) // 2
                ka, kb = slice(ks.start, mid), slice(mid, ks.stop)
                if do_scores:
                    run = score_chunk(slot, ka, qr, oth + j, run)
                if do_numer:
                    pbuf[cur + j][ka, :] = jnp.exp(sbuf[cur + j][ka, :] - m).astype(BF16)
                if do_scores:
                    run = score_chunk(slot, kb, qr, oth + j, run)
                if do_weigh:
                    part = _dot(vt[:, ks], pbuf[oth + j][ks, :])
                    res = part if res is None else res + part
                if do_numer:
                    pbuf[cur + j][kb, :] = jnp.exp(sbuf[cur + j][kb, :] - m).astype(BF16)
            if do_scores:
                mbuf[oth + j] = jnp.max(run, axis=0, keepdims=True)
            if do_weigh:
                acct[(g - 1) * (span // 2) + j // 2, slot * HEAD_DIM:(slot + 1) * HEAD_DIM, :] = (
                    res[0:HEAD_DIM, :] * (1.0 / res[HEAD_DIM:HEAD_DIM + 1, :]))

    region(-1, 1, True, False, False, False)
    region(0, 0, True, True, False, False)

    def body(g, carry):
        for parity in range(2):
            pl.when(g % 2 == parity)(lambda: region(g, parity, True, True, True, False))
        return carry

    lax.fori_loop(1, n_regions - 1, body, 0)
    region(n_regions - 1, (n_regions - 1) % 2, False, True, True, False)
    region(n_regions, n_regions % 2, False, False, True, False)
    for r in range(nr):
        flush(r)


def _attention(q, k, v, cache_k, cache_v, layer):
    rb = ATTN_ROWS
    bsz, n, _ = q.shape
    n_ctx = cache_k.shape[3]
    t_all = n + n_ctx
    gw = 2 * LANES
    qmap = lambda b, g: (b, 0, g)
    kmap = lambda b, g: (b, 0, 0)
    cmap = lambda b, g: (b, layer, 0, 0)
    return pl.pallas_call(
        functools.partial(_attn_kernel, n=n, n_ctx=n_ctx, rb=rb),
        grid=(bsz, N_KV_HEADS),
        in_specs=[
            pl.BlockSpec((1, n, gw), qmap),
            pl.BlockSpec((1, n, KV_W), kmap),
            pl.BlockSpec((1, n, KV_W), kmap),
            pl.BlockSpec((1, 1, KV_W, n_ctx), cmap),
            pl.BlockSpec((1, 1, KV_W, n_ctx), cmap),
        ],
        out_specs=pl.BlockSpec((1, 2, n, LANES), lambda b, g: (b, g, 0, 0)),
        out_shape=jax.ShapeDtypeStruct((bsz, ATTN_W // LANES, n, LANES), F32),
        scratch_shapes=[
            pltpu.VMEM((2, t_all, LANES), BF16),
            pltpu.VMEM((VT_ROWS, t_all), BF16),
            pltpu.VMEM((LANES, 2 * n), BF16),
            pltpu.VMEM((2 * n // rb, LANES, rb), F32),
            pltpu.VMEM((ATTN_BUFS, 1, rb), F32),
        ] + [pltpu.VMEM((t_all, rb), F32)] * ATTN_BUFS + [pltpu.VMEM((t_all, rb), BF16)] * ATTN_BUFS,
        compiler_params=pltpu.CompilerParams(
            dimension_semantics=("arbitrary", "arbitrary"), vmem_limit_bytes=ATTN_VMEM_LIMIT),
        name="attn_lat",
    )(q, k, v, cache_k, cache_v)


def _attn_ctx_kernel(q_ref, k_ref, v_ref, o_ref, *, n):
    low = lax.broadcasted_iota(jnp.int32, (n, LANES), 1) < HEAD_DIM
    ones_lo = jnp.where(low, 1.0, 0.0)
    zk = jnp.zeros((HEAD_DIM, n), F32)
    gw = 2 * LANES
    for b in range(CTX_BATCH):
        kt_all = k_ref[b, 0]
        v = v_ref[b, 0].T
        vr = pltpu.roll(v, HEAD_DIM, 1)
        for g in range(N_KV_HEADS):
            kg = kt_all[g * HEAD_DIM:(g + 1) * HEAD_DIM, :]
            kts = (jnp.concatenate([kg, zk], axis=0).astype(BF16),
                   jnp.concatenate([zk, kg], axis=0).astype(BF16))
            v_lo, v_hi = (v, vr) if g == 0 else (vr, v)
            vs = (jnp.concatenate([jnp.where(low, v_lo, 0.0), ones_lo], axis=1).astype(BF16),
                  jnp.concatenate([jnp.where(low, 0.0, v_hi), 1.0 - ones_lo], axis=1).astype(BF16))
            q2 = jnp.concatenate([q_ref[b, :, g * gw:g * gw + LANES],
                                  q_ref[b, :, g * gw + LANES:(g + 1) * gw]], axis=0)
            acc = None
            for kt_s, v_s in zip(kts, vs):
                s = _dot(q2, kt_s)
                m = jnp.max(s, axis=-1, keepdims=True)
                pv = _dot(jnp.exp(s - m).astype(BF16), v_s)
                acc = pv if acc is None else acc + pv
            o = acc[:, 0:LANES] / acc[:, LANES:]
            o_ref[b, 2 * g] = o[0:n]
            o_ref[b, 2 * g + 1] = o[n:]


def _attention_ctx(q, k, v, layer):
    bsz, n, _ = q.shape
    blk = lambda w: pl.BlockSpec((CTX_BATCH, n, w), lambda b: (b, 0, 0))
    kv_blk = pl.BlockSpec((CTX_BATCH, 1, KV_W, n), lambda b: (b, layer, 0, 0))
    return pl.pallas_call(
        functools.partial(_attn_ctx_kernel, n=n),
        grid=(bsz // CTX_BATCH,),
        in_specs=[blk(ATTN_W), kv_blk, kv_blk],
        out_specs=pl.BlockSpec((CTX_BATCH, ATTN_W // LANES, n, LANES), lambda b: (b, 0, 0, 0)),
        out_shape=jax.ShapeDtypeStruct((bsz, ATTN_W // LANES, n, LANES), F32),
        compiler_params=pltpu.CompilerParams(
            dimension_semantics=("arbitrary",), vmem_limit_bytes=VMEM_LIMIT),
        name="attn_ctx",
    )(q, k, v)


def _mix_kernel(attn_ref, x_ref, xp_ref, xn_ref, mod_ref, g_ref, w_ref, wout_ref,
                convw_ref, convb_ref, wpool_ref, pscale_ref, fg_ref,
                o_ref, xe, yext, uext, *, tm, tps, seq, final, mod_row):
    mod = _mod_row(mod_ref, mod_row, tps)
    j = pl.program_id(0) % tps
    pm = jnp.where(j == 0, 0.0, 1.0)
    nm = jnp.where(j == tps - 1, 0.0, 1.0)
    c = CONV_W
    g = g_ref[...]
    x = x_ref[...]
    xe[0:tm, :] = _mod_norm(x, g, mod).astype(BF16)
    if tps > 1:
        halo = jnp.concatenate([xn_ref[...], xp_ref[...]], axis=0)
        xe[tm:, :] = _mod_norm(halo, g, mod).astype(BF16)

    def proj(rows, off, width):
        return _dot(xe[0:rows, :], w_ref[0, :, off:off + width])

    nseq = max(tm // seq, 1)
    whole = tps == 1
    ln = tm // nseq
    hrows = tm if whole else tm + 2 * HALO
    h_all = proj(hrows, COL_CONV_H, c)
    c_all = proj(hrows, COL_CONV_C, c)
    u_all = proj(hrows, COL_POOL_U, POOL_W)
    z_a = proj(tm, COL_ATTN_GATE, ATTN_W)
    a = ATTN_W
    attn = jnp.concatenate(
        [jnp.concatenate([attn_ref[p, t] for t in range(attn_ref.shape[1])], axis=1)
         for p in range(attn_ref.shape[0])], axis=0)
    attn_o = (attn * _silu(z_a)).astype(BF16)
    out_a = _dot(attn_o, wout_ref[0, 0:a, :])
    b_c = proj(tm, COL_CONV_B, c)
    z_c = proj(tm, COL_CONV_GATE, c)
    z_p = proj(tm, COL_POOL_GATE, POOL_W)

    def extend(ext, p, val):
        zeros = jnp.zeros((HALO, val.shape[1]), F32)
        ext[p, 0:HALO, :] = zeros if whole else val[tm + HALO:, :] * pm
        ext[p, HALO:HALO + ln, :] = val[p * ln:(p + 1) * ln, :]
        ext[p, HALO + ln:, :] = zeros if whole else val[tm:tm + HALO, :] * nm

    y = h_all * c_all
    convs = []
    for p in range(nseq):
        extend(yext, p, y)
        convs.append(yext[p, HALO - 1:HALO - 1 + ln, :] * convw_ref[0:1, :]
                     + yext[p, HALO:HALO + ln, :] * convw_ref[1:2, :]
                     + yext[p, HALO + 1:HALO + 1 + ln, :] * convw_ref[2:3, :] + convb_ref[...])
    conv = convs[0] if nseq == 1 else jnp.concatenate(convs, axis=0)
    conv_o = (b_c * conv * _silu(z_c)).astype(BF16)

    lane = lax.broadcasted_iota(jnp.int32, (ln, LANES), 1)
    pos = lax.broadcasted_iota(jnp.int32, (ln, LANES), 0) + (0 if whole else j * tm)
    low = lane < POOL_GROUP
    d_rows = []
    for p in range(nseq):
        extend(uext, p, u_all)

        def win(k0, k1, sl):
            tot = None
            for k in range(k0, k1):
                t = uext[p, HALO + k:HALO + k + ln, sl]
                tot = t if tot is None else tot + t
            return tot

        ds = []
        for ti, (wa, wb) in enumerate(((2, 4), (8, 16))):
            sl = slice(ti * LANES, (ti + 1) * LANES)
            s_a = win(-(wa // 2), wa // 2, sl)
            s_b = s_a + win(-(wb // 2), -(wa // 2), sl) + win(wa // 2, wb // 2, sl)
            half = jnp.where(low, wa // 2, wb // 2)
            cnt = jnp.minimum(pos + half - 1, seq - 1) - jnp.maximum(pos - half, 0) + 1
            tot = jnp.where(low, s_a, s_b)
            ds.append(tot / cnt.astype(F32) - u_all[p * ln:(p + 1) * ln, sl])
        d_rows.append(jnp.concatenate(ds, axis=-1))
    d = (d_rows[0] if nseq == 1 else jnp.concatenate(d_rows, axis=0)).astype(BF16)
    pool_o = (_dot(d, wpool_ref[...]) * pscale_ref[...] * _silu(z_p)).astype(BF16)

    out = out_a + _dot(conv_o, wout_ref[0, a:a + c, :]) + _dot(pool_o, wout_ref[0, a + c:, :])
    res = x + mod[:, 2 * D_MODEL:] * out
    if final:
        ms = jnp.mean(res * res, axis=-1, keepdims=True)
        res = res * lax.rsqrt(ms + EPS) * fg_ref[...]
    o_ref[...] = res


def _mix(attn2, x2, mod, mod_row, norm_g, w_in_bf, wout_bf, layer, conv_w, conv_b, wpool_bd, pool_scale,
         final_g, *, seq, tm, final):
    tokens = x2.shape[0]
    tps = max(seq // tm, 1)
    nseq = max(tm // seq, 1)
    hb = tm // HALO
    nhb = tokens // HALO
    row = lambda i: (i, 0)
    const = lambda i: (0, 0)
    in_specs = [
        (pl.BlockSpec((nseq, ATTN_W // LANES, tm // nseq, LANES), lambda i: (i, 0, 0, 0)) if tps == 1 else
         pl.BlockSpec((1, ATTN_W // LANES, tm, LANES), lambda i: (i // tps, 0, i % tps, 0))),
        pl.BlockSpec((tm, D_MODEL), row),
        pl.BlockSpec((HALO, D_MODEL), lambda i: (jnp.maximum(i * hb - 1, 0), 0)),
        pl.BlockSpec((HALO, D_MODEL), lambda i: (jnp.minimum((i + 1) * hb, nhb - 1), 0)),
        _mod_spec(layer),
        pl.BlockSpec((1, D_MODEL), const),
        pl.BlockSpec((1, D_MODEL, IN_W), lambda i: (layer, 0, 0)),
        pl.BlockSpec((1, D_MODEL, D_MODEL), lambda i: (layer, 0, 0)),
        pl.BlockSpec((3, CONV_W), const),
        pl.BlockSpec((1, CONV_W), const),
        pl.BlockSpec((POOL_W, POOL_W), const),
        pl.BlockSpec((1, POOL_W), const),
        pl.BlockSpec((1, D_MODEL), const),
    ]
    return pl.pallas_call(
        functools.partial(_mix_kernel, tm=tm, tps=tps, seq=seq, final=final, mod_row=mod_row),
        grid=(tokens // tm,),
        in_specs=in_specs,
        out_specs=pl.BlockSpec((tm, D_MODEL), row),
        out_shape=jax.ShapeDtypeStruct((tokens, D_MODEL), F32),
        scratch_shapes=[
            pltpu.VMEM((tm + 2 * HALO, D_MODEL), BF16),
            pltpu.VMEM((nseq, tm // nseq + 2 * HALO, CONV_W), F32),
            pltpu.VMEM((nseq, tm // nseq + 2 * HALO, POOL_W), F32),
        ],
        compiler_params=pltpu.CompilerParams(
            dimension_semantics=("arbitrary",), vmem_limit_bytes=VMEM_LIMIT),
        name="mix_final" if final else "mix",
    )(attn2, x2, x2, x2, mod, norm_g.reshape(1, D_MODEL), w_in_bf, wout_bf,
      conv_w, conv_b.reshape(1, CONV_W), wpool_bd, pool_scale.reshape(1, POOL_W),
      final_g.reshape(1, D_MODEL))


def _rope_tables(n):
    rows = n // GRID_W
    row = np.repeat(np.arange(rows), GRID_W).astype(np.float64)
    col = np.tile(np.arange(GRID_W), rows).astype(np.float64)
    half = HEAD_DIM // 2
    inv = 1.0 / (ROPE_THETA ** (np.arange(0, half, 2, dtype=np.float64) / half))
    ar = row[:, None] * inv
    ac = col[:, None] * inv
    ang = np.tile(np.concatenate([ar, ar, ac, ac], axis=-1), (1, LANES // HEAD_DIM))
    second = (np.arange(LANES) % half) >= (half // 2)
    return jnp.asarray(np.cos(ang), F32), jnp.asarray(np.where(second, np.sin(ang), -np.sin(ang)), F32)


def _block_diag(blocks):
    n = len(blocks)
    rows = []
    for i, b in enumerate(blocks):
        rows.append(jnp.concatenate(
            [b if j == i else jnp.zeros((b.shape[0], blocks[j].shape[1]), b.dtype) for j in range(n)],
            axis=1))
    return jnp.concatenate(rows, axis=0)


def kernel(x_prompt, x_sample, cache_k, cache_v, c, c_ctx, norm_g, w_ada, b_ada, w_in,
           q_norm_g, k_norm_g, conv_w, conv_b, pool_w, pool_scale, w_out, final_g):
    depth = w_in.shape[0]
    bp, sp, _ = x_prompt.shape
    bs, ss, _ = x_sample.shape
    n_ctx = cache_k.shape[2]

    cvec = jnp.concatenate(
        [c, c_ctx[None, :], jnp.zeros((ADA_ROWS - bs - 1, D_MODEL), F32)], axis=0)
    mod = _ada(cvec, w_ada, b_ada)

    w_in_bf = w_in.astype(BF16)
    w_out_bf = w_out.astype(BF16)
    tables = _rope_tables(ss)
    ones = jnp.full((HEAD_DIM, HEAD_DIM), 1.0 / HEAD_DIM, BF16)
    bd = _block_diag([ones, ones])
    ck = cache_k.transpose(0, 1, 3, 4, 2).reshape(bs, depth, KV_W, n_ctx)
    cv = cache_v.transpose(0, 1, 3, 4, 2).reshape(bs, depth, KV_W, n_ctx)

    h = x_prompt.reshape(bp * sp, D_MODEL)
    z = x_sample.reshape(bs * ss, D_MODEL)
    kv_ctx = None
    gains = jnp.concatenate(
        [jnp.tile(q_norm_g * (HEAD_DIM ** -0.5), (1, N_HEADS)), jnp.tile(k_norm_g, (1, N_KV_HEADS))],
        axis=-1).reshape(depth, 1, QK_W)
    for l in range(depth):
        wpool_bd = _block_diag([pool_w[l, g] for g in range(pool_w.shape[1])]).astype(BF16)
        final = l == depth - 1
        mix = functools.partial(
            _mix, norm_g=norm_g[l], w_in_bf=w_in_bf, wout_bf=w_out_bf, layer=l,
            conv_w=conv_w[l], conv_b=conv_b[l], wpool_bd=wpool_bd, pool_scale=pool_scale[l],
            final_g=final_g, final=final)

        q, k, v = _qkv(h, mod, bs, norm_g[l], w_in_bf, l, gains, bd, None, kv_ctx, seq=sp, tm=QKV_TILE)
        kv_ctx = (k, v)
        attn = _attention_ctx(q.reshape(bp, sp, ATTN_W), k, v, l)
        h = mix(attn, h, mod, bs, seq=sp, tm=MIX_TILE)

        q, k, v = _qkv(z, mod, None, norm_g[l], w_in_bf, l, gains, bd, tables, None, seq=ss, tm=QKV_TILE)
        attn = _attention(q.reshape(bs, ss, ATTN_W), k.reshape(bs, ss, KV_W), v.reshape(bs, ss, KV_W),
                          ck, cv, l)
        z = mix(attn, z, mod, None, seq=ss, tm=MIX_TILE)

    y_prompt = h.reshape(bp, sp, D_MODEL)
    y_sample = z.reshape(bs, ss, D_MODEL)
    new_k, new_v = (a.reshape(bp, depth, N_KV_HEADS, HEAD_DIM, sp).transpose(0, 1, 4, 2, 3) for a in kv_ctx)
    return (y_prompt, y_sample, new_k, new_v)
```

```python
import functools

import jax
import jax.numpy as jnp
import numpy as np
from jax import lax
from jax.experimental import pallas as pl
from jax.experimental.pallas import tpu as pltpu

D_MODEL = 1024
GRID_W = 64
N_HEADS = 8
N_KV_HEADS = 2
HEAD_DIM = 64
ATTN_W = N_HEADS * HEAD_DIM
KV_W = N_KV_HEADS * HEAD_DIM
QK_W = ATTN_W + KV_W
QKV_W = QK_W + KV_W
CONV_W = 256
POOL_W = 256
POOL_GROUP = 64
IN_W = 2816
COL_ATTN_GATE = QKV_W
COL_CONV_H = COL_ATTN_GATE + ATTN_W
COL_CONV_B = COL_CONV_H + CONV_W
COL_CONV_C = COL_CONV_B + CONV_W
COL_CONV_GATE = COL_CONV_C + CONV_W
COL_POOL_U = COL_CONV_GATE + CONV_W
COL_POOL_GATE = COL_POOL_U + POOL_W
ROPE_THETA = 10000.0
EPS = 1e-6
LANES = 128
HALO = 8
ADA_ROWS = 8
VMEM_LIMIT = 48 * 1024 * 1024
ATTN_VMEM_LIMIT = 56 * 1024 * 1024

ADA_K = 512
QKV_TILE = 1024
QKV_SUB = 256
MIX_TILE = 512
MIX_PIECE = 256
ATTN_ROWS = 256
ATTN_REGION = 4
ATTN_BUFS = 2 * ATTN_REGION
ATTN_KEY_CHUNK = 256
CTX_BATCH = 4

F32 = jnp.float32
BF16 = jnp.bfloat16


def _silu(z):
    return z * (1.0 / (1.0 + jnp.exp(-z)))


def _dot(a, b):
    return jnp.dot(a, b, preferred_element_type=F32)


def _mod_row(mod_ref, mod_row, tps):
    row = mod_row if mod_row is not None else pl.program_id(0) // tps
    return mod_ref[0, pl.ds(row, 1), :]


def _mod_norm(x, g, mod):
    ms = jnp.mean(x * x, axis=-1, keepdims=True)
    gs = g * (1.0 + mod[:, D_MODEL:2 * D_MODEL])
    return x * lax.rsqrt(ms + EPS) * gs + mod[:, 0:D_MODEL]


def _ada_kernel(c_ref, w_ref, b_ref, o_ref):
    part = _dot(_silu(c_ref[...]).astype(BF16), w_ref[0].astype(BF16))

    @pl.when(pl.program_id(1) == 0)
    def _first():
        o_ref[0] = part + b_ref[pl.ds(pl.program_id(0), 1), :]

    @pl.when(pl.program_id(1) > 0)
    def _rest():
        o_ref[0] += part


def _ada(cvec, w_ada, b_ada):
    depth, _, n3 = w_ada.shape
    tk = ADA_K
    return pl.pallas_call(
        _ada_kernel,
        grid=(depth, D_MODEL // tk),
        in_specs=[
            pl.BlockSpec((ADA_ROWS, tk), lambda l, k: (0, k)),
            pl.BlockSpec((1, tk, n3), lambda l, k: (l, k, 0)),
            pl.BlockSpec((depth, n3), lambda l, k: (0, 0)),
        ],
        out_specs=pl.BlockSpec((1, ADA_ROWS, n3), lambda l, k: (l, 0, 0)),
        out_shape=jax.ShapeDtypeStruct((depth, ADA_ROWS, n3), F32),
        compiler_params=pltpu.CompilerParams(dimension_semantics=("arbitrary", "arbitrary")),
        name="ada_mod",
    )(cvec, w_ada, b_ada)


def _qkv_kernel(*refs, rope, nsub, first_layer, mod_row, tps):
    if rope:
        (x_ref, mod_ref, g_ref, w_ref, gains_ref, bd_ref, cos_ref, sin_ref,
         q_ref, k_ref, v_ref) = refs
    else:
        x_ref, mod_ref, g_ref, w_ref, gains_ref, bd_ref = refs[:6]
        q_ref, k_ref, v_ref = refs[-3:]
    tm = x_ref.shape[0]
    sub = tm // nsub
    bd = bd_ref[...]
    g = g_ref[...]
    mod = _mod_row(mod_ref, mod_row, tps)
    if rope:
        lane = lax.broadcasted_iota(jnp.int32, (sub, LANES), 1)
        second = (lane % (HEAD_DIM // 2)) >= (HEAD_DIM // 4)
    k_parts, v_parts = [None] * nsub, [None] * nsub

    def finish(s, c, part):
        rows = slice(s * sub, (s + 1) * sub)
        for h in range(2):
            j = 2 * c + h
            t = part[:, h * LANES:(h + 1) * LANES]
            if j * LANES >= QK_W:
                v_parts[s] = t
                continue
            sl = slice(j * LANES, (j + 1) * LANES)
            msq = _dot((t * t).astype(BF16), bd)
            t = t * lax.rsqrt(msq + EPS) * gains_ref[0, :, sl]
            if rope:
                partner = jnp.where(second, pltpu.roll(t, HEAD_DIM // 4, 1),
                                    pltpu.roll(t, LANES - HEAD_DIM // 4, 1))
                t = t * cos_ref[rows, :] + partner * sin_ref[rows, :]
            if j * LANES < ATTN_W:
                q_ref[rows, sl] = t.astype(BF16)
            else:
                k_parts[s] = t

    ncol = QKV_W // (2 * LANES)
    prev = None
    for s in range(nsub + 1):
        if s < nsub:
            xn = _mod_norm(x_ref[s * sub:(s + 1) * sub, :], g, mod).astype(BF16)
        cur = []
        for c in range(ncol):
            if s < nsub:
                cur.append(_dot(xn, w_ref[0, :, c * 2 * LANES:(c + 1) * 2 * LANES]))
            if prev is not None:
                finish(s - 1, c, prev[c])
        prev = cur
    for ref, parts in ((k_ref, k_parts), (v_ref, v_parts)):
        val = jnp.concatenate(parts, axis=0)
        if rope:
            ref[...] = val
            continue
        seq = ref.shape[3]
        for p in range(ref.shape[0]):
            piece = val[p * seq:(p + 1) * seq, :].T
            if first_layer is None:
                ref[p, 0] = piece
            else:
                for l2 in range(ref.shape[1]):
                    ref[p, l2] = piece if l2 == first_layer else jnp.zeros_like(piece)


def _mod_spec(layer):
    return pl.BlockSpec((1, ADA_ROWS, 3 * D_MODEL), lambda i: (layer, 0, 0))


def _qkv(x2, mod, mod_row, norm_g, w_in_bf, layer, gains, bd, tables, kv_prev, *, seq, tm):
    tokens = x2.shape[0]
    depth = w_in_bf.shape[0]
    tps = max(seq // tm, 1)
    rope = tables is not None
    row = lambda i: (i, 0)
    const = lambda i: (0, 0)
    in_specs = [
        pl.BlockSpec((tm, D_MODEL), row),
        _mod_spec(layer),
        pl.BlockSpec((1, D_MODEL), const),
        pl.BlockSpec((1, D_MODEL, QKV_W), lambda i: (layer, 0, 0)),
        pl.BlockSpec((1, 1, QK_W), lambda i: (layer, 0, 0)),
        pl.BlockSpec((LANES, LANES), const),
    ]
    args = [x2, mod, norm_g.reshape(1, D_MODEL), w_in_bf, gains, bd]
    out_shapes = [jax.ShapeDtypeStruct((tokens, ATTN_W), BF16)]
    out_specs = [pl.BlockSpec((tm, ATTN_W), row)]
    aliases = {}
    first_layer = None
    if rope:
        in_specs += [pl.BlockSpec((tm, LANES), lambda i: (i % tps, 0))] * 2
        args += list(tables)
        out_shapes += [jax.ShapeDtypeStruct((tokens, KV_W), F32)] * 2
        out_specs += [pl.BlockSpec((tm, KV_W), row)] * 2
    else:
        nseq = tm // seq
        out_shapes += [jax.ShapeDtypeStruct((tokens // seq, depth, KV_W, seq), F32)] * 2
        if kv_prev is None:
            first_layer = layer
            out_specs += [pl.BlockSpec((nseq, depth, KV_W, seq), lambda i: (i, 0, 0, 0))] * 2
        else:
            out_specs += [pl.BlockSpec((nseq, 1, KV_W, seq), lambda i: (i, layer, 0, 0))] * 2
            in_specs += [pl.BlockSpec(memory_space=pl.ANY)] * 2
            aliases = {len(args): 1, len(args) + 1: 2}
            args += list(kv_prev)
    return pl.pallas_call(
        functools.partial(_qkv_kernel, rope=rope, nsub=tm // QKV_SUB, first_layer=first_layer,
                          mod_row=mod_row, tps=tps),
        grid=(tokens // tm,),
        in_specs=in_specs,
        out_specs=out_specs,
        out_shape=out_shapes,
        input_output_aliases=aliases,
        compiler_params=pltpu.CompilerParams(
            dimension_semantics=("arbitrary",), vmem_limit_bytes=VMEM_LIMIT),
        name="qkv_rope" if rope else "qkv",
    )(*args)


VT_ROWS = HEAD_DIM + 16


def _attn_kernel(q_ref, k_ref, v_ref, ck_ref, cv_ref, o_ref, kab, vt, q2, acct, mbuf, *bufs, n, n_ctx, rb):
    kv = pl.program_id(1)
    nr = 2 * n // rb
    t_all = n + n_ctx
    sbuf, pbuf = bufs[:ATTN_BUFS], bufs[ATTN_BUFS:]

    def fill(k_src, v_t, r0, m):
        lane = lax.broadcasted_iota(jnp.int32, (m, LANES), 1)
        k_own = jnp.where((lane // HEAD_DIM) == kv, k_src, 0.0)
        k_oth = pltpu.roll(k_own, HEAD_DIM, 1)
        is0 = kv == 0
        kab[0, r0:r0 + m, :] = jnp.where(is0, k_own, k_oth).astype(BF16)
        kab[1, r0:r0 + m, :] = jnp.where(is0, k_oth, k_own).astype(BF16)
        vt[0:HEAD_DIM, r0:r0 + m] = jnp.where(is0, v_t[0:HEAD_DIM, :], v_t[HEAD_DIM:, :]).astype(BF16)

    piece = 256
    for r0 in range(0, n, piece):
        fill(k_ref[0, r0:r0 + piece, :], v_ref[0, r0:r0 + piece, :].T, r0, piece)
    fill(ck_ref[0, 0].T, cv_ref[0, 0], n, n_ctx)
    vt[HEAD_DIM:, :] = jnp.ones((VT_ROWS - HEAD_DIM, t_all), BF16)
    for c0 in range(0, n, piece):
        q2[:, c0:c0 + piece] = q_ref[0, c0:c0 + piece, 0:LANES].astype(F32).T.astype(BF16)
        q2[:, n + c0:n + c0 + piece] = q_ref[0, c0:c0 + piece, LANES:].astype(F32).T.astype(BF16)

    span = ATTN_REGION
    n_regions = 2 * nr // span
    chunks = [slice(k0, k0 + ATTN_KEY_CHUNK) for k0 in range(0, t_all, ATTN_KEY_CHUNK)]

    def score_chunk(slot, ks, qr, buf, run):
        sc = _dot(kab[slot, ks, :], qr)
        sbuf[buf][ks, :] = sc
        while sc.shape[0] > 8:
            half = sc.shape[0] // 2
            sc = jnp.maximum(sc[0:half, :], sc[half:, :])
        return sc if run is None else jnp.maximum(run, sc)

    def q_rows(r):
        return q2[:, pl.ds(pl.multiple_of(r * rb, rb), rb)]

    per_tile = n // rb

    def flush(r):
        r0 = (r % per_tile) * rb
        if not isinstance(r, int):
            r0 = pl.multiple_of(r0, rb)
        o_ref[0, r // per_tile, pl.ds(r0, rb), :] = acct[r].T

    def region(g, parity, do_scores, do_numer, do_weigh, do_flush):
        cur, oth = parity * span, (1 - parity) * span
        for j in range(span):
            slot = j % 2
            if do_flush and slot == 0:
                flush((g - 2) * (span // 2) + j // 2)
            if do_numer:
                m = mbuf[cur + j]
            if do_scores:
                qr = q_rows((g + 1) * (span // 2) + j // 2)
            run, res = None, None
            for ks in chunks:
                mid = (ks.start + ks.stop) // 2
                ka, kb = slice(ks.start, mid), slice(mid, ks.stop)
                if do_scores:
                    run = score_chunk(slot, ka, qr, oth + j, run)
                if do_numer:
                    pbuf[cur + j][ka, :] = jnp.exp(sbuf[cur + j][ka, :] - m).astype(BF16)
                if do_scores:
                    run = score_chunk(slot, kb, qr, oth + j, run)
                if do_weigh:
                    part = _dot(vt[:, ks], pbuf[oth + j][ks, :])
                    res = part if res is None else res + part
                if do_numer:
                    pbuf[cur + j][kb, :] = jnp.exp(sbuf[cur + j][kb, :] - m).astype(BF16)
            if do_scores:
                mbuf[oth + j] = jnp.max(run, axis=0, keepdims=True)
            if do_weigh:
                acct[(g - 1) * (span // 2) + j // 2, slot * HEAD_DIM:(slot + 1) * HEAD_DIM, :] = (
                    res[0:HEAD_DIM, :] * (1.0 / res[HEAD_DIM:HEAD_DIM + 1, :]))

    region(-1, 1, True, False, False, False)
    region(0, 0, True, True, False, False)

    def body(g, carry):
        for parity in range(2):
            pl.when(g % 2 == parity)(lambda: region(g, parity, True, True, True, False))
        return carry

    lax.fori_loop(1, n_regions - 1, body, 0)
    region(n_regions - 1, (n_regions - 1) % 2, False, True, True, False)
    region(n_regions, n_regions % 2, False, False, True, False)
    for r in range(nr):
        flush(r)


def _attention(q, k, v, cache_k, cache_v, layer):
    rb = ATTN_ROWS
    bsz, n, _ = q.shape
    n_ctx = cache_k.shape[3]
    t_all = n + n_ctx
    gw = 2 * LANES
    qmap = lambda b, g: (b, 0, g)
    kmap = lambda b, g: (b, 0, 0)
    cmap = lambda b, g: (b, layer, 0, 0)
    return pl.pallas_call(
        functools.partial(_attn_kernel, n=n, n_ctx=n_ctx, rb=rb),
        grid=(bsz, N_KV_HEADS),
        in_specs=[
            pl.BlockSpec((1, n, gw), qmap),
            pl.BlockSpec((1, n, KV_W), kmap),
            pl.BlockSpec((1, n, KV_W), kmap),
            pl.BlockSpec((1, 1, KV_W, n_ctx), cmap),
            pl.BlockSpec((1, 1, KV_W, n_ctx), cmap),
        ],
        out_specs=pl.BlockSpec((1, 2, n, LANES), lambda b, g: (b, g, 0, 0)),
        out_shape=jax.ShapeDtypeStruct((bsz, ATTN_W // LANES, n, LANES), F32),
        scratch_shapes=[
            pltpu.VMEM((2, t_all, LANES), BF16),
            pltpu.VMEM((VT_ROWS, t_all), BF16),
            pltpu.VMEM((LANES, 2 * n), BF16),
            pltpu.VMEM((2 * n // rb, LANES, rb), F32),
            pltpu.VMEM((ATTN_BUFS, 1, rb), F32),
        ] + [pltpu.VMEM((t_all, rb), F32)] * ATTN_BUFS + [pltpu.VMEM((t_all, rb), BF16)] * ATTN_BUFS,
        compiler_params=pltpu.CompilerParams(
            dimension_semantics=("arbitrary", "arbitrary"), vmem_limit_bytes=ATTN_VMEM_LIMIT),
        name="attn_lat",
    )(q, k, v, cache_k, cache_v)


def _attn_ctx_kernel(q_ref, k_ref, v_ref, o_ref, *, n):
    low = lax.broadcasted_iota(jnp.int32, (n, LANES), 1) < HEAD_DIM
    ones_lo = jnp.where(low, 1.0, 0.0)
    zk = jnp.zeros((HEAD_DIM, n), F32)
    gw = 2 * LANES
    for b in range(CTX_BATCH):
        kt_all = k_ref[b, 0]
        v = v_ref[b, 0].T
        vr = pltpu.roll(v, HEAD_DIM, 1)
        for g in range(N_KV_HEADS):
            kg = kt_all[g * HEAD_DIM:(g + 1) * HEAD_DIM, :]
            kts = (jnp.concatenate([kg, zk], axis=0).astype(BF16),
                   jnp.concatenate([zk, kg], axis=0).astype(BF16))
            v_lo, v_hi = (v, vr) if g == 0 else (vr, v)
            vs = (jnp.concatenate([jnp.where(low, v_lo, 0.0), ones_lo], axis=1).astype(BF16),
                  jnp.concatenate([jnp.where(low, 0.0, v_hi), 1.0 - ones_lo], axis=1).astype(BF16))
            q2 = jnp.concatenate([q_ref[b, :, g * gw:g * gw + LANES],
                                  q_ref[b, :, g * gw + LANES:(g + 1) * gw]], axis=0)
            acc = None
            for kt_s, v_s in zip(kts, vs):
                s = _dot(q2, kt_s)
                m = jnp.max(s, axis=-1, keepdims=True)
                pv = _dot(jnp.exp(s - m).astype(BF16), v_s)
                acc = pv if acc is None else acc + pv
            o = acc[:, 0:LANES] / acc[:, LANES:]
            o_ref[b, 2 * g] = o[0:n]
            o_ref[b, 2 * g + 1] = o[n:]


def _attention_ctx(q, k, v, layer):
    bsz, n, _ = q.shape
    blk = lambda w: pl.BlockSpec((CTX_BATCH, n, w), lambda b: (b, 0, 0))
    kv_blk = pl.BlockSpec((CTX_BATCH, 1, KV_W, n), lambda b: (b, layer, 0, 0))
    return pl.pallas_call(
        functools.partial(_attn_ctx_kernel, n=n),
        grid=(bsz // CTX_BATCH,),
        in_specs=[blk(ATTN_W), kv_blk, kv_blk],
        out_specs=pl.BlockSpec((CTX_BATCH, ATTN_W // LANES, n, LANES), lambda b: (b, 0, 0, 0)),
        out_shape=jax.ShapeDtypeStruct((bsz, ATTN_W // LANES, n, LANES), F32),
        compiler_params=pltpu.CompilerParams(
            dimension_semantics=("arbitrary",), vmem_limit_bytes=VMEM_LIMIT),
        name="attn_ctx",
    )(q, k, v)


def _mix_kernel(attn_ref, x_ref, xp_ref, xn_ref, mod_ref, g_ref, w_ref, wout_ref,
                convw_ref, convb_ref, wpool_ref, pscale_ref, fg_ref,
                o_ref, xe, yext, uext, *, tm, tps, seq, final, mod_row):
    mod = _mod_row(mod_ref, mod_row, tps)
    j = pl.program_id(0) % tps
    pm = jnp.where(j == 0, 0.0, 1.0)
    nm = jnp.where(j == tps - 1, 0.0, 1.0)
    c = CONV_W
    g = g_ref[...]
    x = x_ref[...]
    xe[0:tm, :] = _mod_norm(x, g, mod).astype(BF16)
    if tps > 1:
        halo = jnp.concatenate([xn_ref[...], xp_ref[...]], axis=0)
        xe[tm:, :] = _mod_norm(halo, g, mod).astype(BF16)

    def proj(rows, off, width):
        cuts = list(range(0, tm, MIX_PIECE)) + [rows]
        return jnp.concatenate([_dot(xe[r0:r1, :], w_ref[0, :, off:off + width])
                                for r0, r1 in zip(cuts[:-1], cuts[1:])], axis=0)

    nseq = max(tm // seq, 1)
    whole = tps == 1
    ln = tm // nseq
    hrows = tm if whole else tm + 2 * HALO
    h_all = proj(hrows, COL_CONV_H, c)
    c_all = proj(hrows, COL_CONV_C, c)
    u_all = proj(hrows, COL_POOL_U, POOL_W)
    z_a = proj(tm, COL_ATTN_GATE, ATTN_W)
    a = ATTN_W
    attn = jnp.concatenate(
        [jnp.concatenate([attn_ref[p, t] for t in range(attn_ref.shape[1])], axis=1)
         for p in range(attn_ref.shape[0])], axis=0)
    attn_o = (attn * _silu(z_a)).astype(BF16)
    out_a = _dot(attn_o, wout_ref[0, 0:a, :])
    b_c = proj(tm, COL_CONV_B, c)
    z_c = proj(tm, COL_CONV_GATE, c)
    z_p = proj(tm, COL_POOL_GATE, POOL_W)

    def extend(ext, p, val):
        zeros = jnp.zeros((HALO, val.shape[1]), F32)
        ext[p, 0:HALO, :] = zeros if whole else val[tm + HALO:, :] * pm
        ext[p, HALO:HALO + ln, :] = val[p * ln:(p + 1) * ln, :]
        ext[p, HALO + ln:, :] = zeros if whole else val[tm:tm + HALO, :] * nm

    y = h_all * c_all
    convs = []
    for p in range(nseq):
        extend(yext, p, y)
        convs.append(yext[p, HALO - 1:HALO - 1 + ln, :] * convw_ref[0:1, :]
                     + yext[p, HALO:HALO + ln, :] * convw_ref[1:2, :]
                     + yext[p, HALO + 1:HALO + 1 + ln, :] * convw_ref[2:3, :] + convb_ref[...])
    conv = convs[0] if nseq == 1 else jnp.concatenate(convs, axis=0)
    conv_o = (b_c * conv * _silu(z_c)).astype(BF16)

    lane = lax.broadcasted_iota(jnp.int32, (ln, LANES), 1)
    pos = lax.broadcasted_iota(jnp.int32, (ln, LANES), 0) + (0 if whole else j * tm)
    low = lane < POOL_GROUP
    d_rows = []
    for p in range(nseq):
        extend(uext, p, u_all)

        def win(k0, k1, sl):
            tot = None
            for k in range(k0, k1):
                t = uext[p, HALO + k:HALO + k + ln, sl]
                tot = t if tot is None else tot + t
            return tot

        ds = []
        for ti, (wa, wb) in enumerate(((2, 4), (8, 16))):
            sl = slice(ti * LANES, (ti + 1) * LANES)
            s_a = win(-(wa // 2), wa // 2, sl)
            s_b = s_a + win(-(wb // 2), -(wa // 2), sl) + win(wa // 2, wb // 2, sl)
            half = jnp.where(low, wa // 2, wb // 2)
            cnt = jnp.minimum(pos + half - 1, seq - 1) - jnp.maximum(pos - half, 0) + 1
            tot = jnp.where(low, s_a, s_b)
            ds.append(tot / cnt.astype(F32) - u_all[p * ln:(p + 1) * ln, sl])
        d_rows.append(jnp.concatenate(ds, axis=-1))
    d = (d_rows[0] if nseq == 1 else jnp.concatenate(d_rows, axis=0)).astype(BF16)
    pool_o = (_dot(d, wpool_ref[...]) * pscale_ref[...] * _silu(z_p)).astype(BF16)

    out = out_a + _dot(conv_o, wout_ref[0, a:a + c, :]) + _dot(pool_o, wout_ref[0, a + c:, :])
    res = x + mod[:, 2 * D_MODEL:] * out
    if final:
        ms = jnp.mean(res * res, axis=-1, keepdims=True)
        res = res * lax.rsqrt(ms + EPS) * fg_ref[...]
    o_ref[...] = res


def _mix(attn2, x2, mod, mod_row, norm_g, w_in_bf, wout_bf, layer, conv_w, conv_b, wpool_bd, pool_scale,
         final_g, *, seq, tm, final):
    tokens = x2.shape[0]
    tps = max(seq // tm, 1)
    nseq = max(tm // seq, 1)
    hb = tm // HALO
    nhb = tokens // HALO
    row = lambda i: (i, 0)
    const = lambda i: (0, 0)
    in_specs = [
        (pl.BlockSpec((nseq, ATTN_W // LANES, tm // nseq, LANES), lambda i: (i, 0, 0, 0)) if tps == 1 else
         pl.BlockSpec((1, ATTN_W // LANES, tm, LANES), lambda i: (i // tps, 0, i % tps, 0))),
        pl.BlockSpec((tm, D_MODEL), row),
        pl.BlockSpec((HALO, D_MODEL), lambda i: (jnp.maximum(i * hb - 1, 0), 0)),
        pl.BlockSpec((HALO, D_MODEL), lambda i: (jnp.minimum((i + 1) * hb, nhb - 1), 0)),
        _mod_spec(layer),
        pl.BlockSpec((1, D_MODEL), const),
        pl.BlockSpec((1, D_MODEL, IN_W), lambda i: (layer, 0, 0)),
        pl.BlockSpec((1, D_MODEL, D_MODEL), lambda i: (layer, 0, 0)),
        pl.BlockSpec((3, CONV_W), const),
        pl.BlockSpec((1, CONV_W), const),
        pl.BlockSpec((POOL_W, POOL_W), const),
        pl.BlockSpec((1, POOL_W), const),
        pl.BlockSpec((1, D_MODEL), const),
    ]
    return pl.pallas_call(
        functools.partial(_mix_kernel, tm=tm, tps=tps, seq=seq, final=final, mod_row=mod_row),
        grid=(tokens // tm,),
        in_specs=in_specs,
        out_specs=pl.BlockSpec((tm, D_MODEL), row),
        out_shape=jax.ShapeDtypeStruct((tokens, D_MODEL), F32),
        scratch_shapes=[
            pltpu.VMEM((tm + 2 * HALO, D_MODEL), BF16),
            pltpu.VMEM((nseq, tm // nseq + 2 * HALO, CONV_W), F32),
            pltpu.VMEM((nseq, tm // nseq + 2 * HALO, POOL_W), F32),
        ],
        compiler_params=pltpu.CompilerParams(
            dimension_semantics=("arbitrary",), vmem_limit_bytes=VMEM_LIMIT),
        name="mix_final" if final else "mix",
    )(attn2, x2, x2, x2, mod, norm_g.reshape(1, D_MODEL), w_in_bf, wout_bf,
      conv_w, conv_b.reshape(1, CONV_W), wpool_bd, pool_scale.reshape(1, POOL_W),
      final_g.reshape(1, D_MODEL))


def _rope_tables(n):
    rows = n // GRID_W
    row = np.repeat(np.arange(rows), GRID_W).astype(np.float64)
    col = np.tile(np.arange(GRID_W), rows).astype(np.float64)
    half = HEAD_DIM // 2
    inv = 1.0 / (ROPE_THETA ** (np.arange(0, half, 2, dtype=np.float64) / half))
    ar = row[:, None] * inv
    ac = col[:, None] * inv
    ang = np.tile(np.concatenate([ar, ar, ac, ac], axis=-1), (1, LANES // HEAD_DIM))
    second = (np.arange(LANES) % half) >= (half // 2)
    return jnp.asarray(np.cos(ang), F32), jnp.asarray(np.where(second, np.sin(ang), -np.sin(ang)), F32)


def _block_diag(blocks):
    n = len(blocks)
    rows = []
    for i, b in enumerate(blocks):
        rows.append(jnp.concatenate(
            [b if j == i else jnp.zeros((b.shape[0], blocks[j].shape[1]), b.dtype) for j in range(n)],
            axis=1))
    return jnp.concatenate(rows, axis=0)


def kernel(x_prompt, x_sample, cache_k, cache_v, c, c_ctx, norm_g, w_ada, b_ada, w_in,
           q_norm_g, k_norm_g, conv_w, conv_b, pool_w, pool_scale, w_out, final_g):
    depth = w_in.shape[0]
    bp, sp, _ = x_prompt.shape
    bs, ss, _ = x_sample.shape
    n_ctx = cache_k.shape[2]

    cvec = jnp.concatenate(
        [c, c_ctx[None, :], jnp.zeros((ADA_ROWS - bs - 1, D_MODEL), F32)], axis=0)
    mod = _ada(cvec, w_ada, b_ada)

    w_in_bf = w_in.astype(BF16)
    w_out_bf = w_out.astype(BF16)
    tables = _rope_tables(ss)
    ones = jnp.full((HEAD_DIM, HEAD_DIM), 1.0 / HEAD_DIM, BF16)
    bd = _block_diag([ones, ones])
    ck = cache_k.transpose(0, 1, 3, 4, 2).reshape(bs, depth, KV_W, n_ctx)
    cv = cache_v.transpose(0, 1, 3, 4, 2).reshape(bs, depth, KV_W, n_ctx)

    h = x_prompt.reshape(bp * sp, D_MODEL)
    z = x_sample.reshape(bs * ss, D_MODEL)
    kv_ctx = None
    gains = jnp.concatenate(
        [jnp.tile(q_norm_g * (HEAD_DIM ** -0.5), (1, N_HEADS)), jnp.tile(k_norm_g, (1, N_KV_HEADS))],
        axis=-1).reshape(depth, 1, QK_W)
    for l in range(depth):
        wpool_bd = _block_diag([pool_w[l, g] for g in range(pool_w.shape[1])]).astype(BF16)
        final = l == depth - 1
        mix = functools.partial(
            _mix, norm_g=norm_g[l], w_in_bf=w_in_bf, wout_bf=w_out_bf, layer=l,
            conv_w=conv_w[l], conv_b=conv_b[l], wpool_bd=wpool_bd, pool_scale=pool_scale[l],
            final_g=final_g, final=final)

        q, k, v = _qkv(h, mod, bs, norm_g[l], w_in_bf, l, gains, bd, None, kv_ctx, seq=sp, tm=QKV_TILE)
        kv_ctx = (k, v)
        attn = _attention_ctx(q.reshape(bp, sp, ATTN_W), k, v, l)
        h = mix(attn, h, mod, bs, seq=sp, tm=MIX_TILE)

        q, k, v = _qkv(z, mod, None, norm_g[l], w_in_bf, l, gains, bd, tables, None, seq=ss, tm=QKV_TILE)
        attn = _attention(q.reshape(bs, ss, ATTN_W), k.reshape(bs, ss, KV_W), v.reshape(bs, ss, KV_W),
                          ck, cv, l)
        z = mix(attn, z, mod, None, seq=ss, tm=MIX_TILE)

    y_prompt = h.reshape(bp, sp, D_MODEL)
    y_sample = z.reshape(bs, ss, D_MODEL)
    new_k, new_v = (a.reshape(bp, depth, N_KV_HEADS, HEAD_DIM, sp).transpose(0, 1, 4, 2, 3) for a in kv_ctx)
    return (y_prompt, y_sample, new_k, new_v)
```

```python
import functools

import jax
import jax.numpy as jnp
import numpy as np
from jax import lax
from jax.experimental import pallas as pl
from jax.experimental.pallas import tpu as pltpu

D_MODEL = 1024
GRID_W = 64
N_HEADS = 8
N_KV_HEADS = 2
HEAD_DIM = 64
ATTN_W = N_HEADS * HEAD_DIM
KV_W = N_KV_HEADS * HEAD_DIM
QK_W = ATTN_W + KV_W
QKV_W = QK_W + KV_W
CONV_W = 256
POOL_W = 256
POOL_GROUP = 64
IN_W = 2816
COL_ATTN_GATE = QKV_W
COL_CONV_H = COL_ATTN_GATE + ATTN_W
COL_CONV_B = COL_CONV_H + CONV_W
COL_CONV_C = COL_CONV_B + CONV_W
COL_CONV_GATE = COL_CONV_C + CONV_W
COL_POOL_U = COL_CONV_GATE + CONV_W
COL_POOL_GATE = COL_POOL_U + POOL_W
ROPE_THETA = 10000.0
EPS = 1e-6
LANES = 128
HALO = 8
ADA_ROWS = 8
VMEM_LIMIT = 48 * 1024 * 1024
ATTN_VMEM_LIMIT = 60 * 1024 * 1024

ADA_K = 512
QKV_TILE = 1024
QKV_SUB = 256
MIX_TILE = 512
ATTN_ROWS = 256
ATTN_REGION = 4
ATTN_BUFS = 2 * ATTN_REGION
ATTN_KEY_CHUNK = 256
CTX_BATCH = 4

F32 = jnp.float32
BF16 = jnp.bfloat16


def _silu(z):
    return z * (1.0 / (1.0 + jnp.exp(-z)))


def _dot(a, b):
    return jnp.dot(a, b, preferred_element_type=F32)


def _mod_row(mod_ref, mod_row, tps):
    row = mod_row if mod_row is not None else pl.program_id(0) // tps
    return mod_ref[0, pl.ds(row, 1), :]


def _mod_norm(x, g, mod):
    ms = jnp.mean(x * x, axis=-1, keepdims=True)
    gs = g * (1.0 + mod[:, D_MODEL:2 * D_MODEL])
    return x * lax.rsqrt(ms + EPS) * gs + mod[:, 0:D_MODEL]


def _ada_kernel(c_ref, w_ref, b_ref, o_ref):
    part = _dot(_silu(c_ref[...]).astype(BF16), w_ref[0].astype(BF16))

    @pl.when(pl.program_id(1) == 0)
    def _first():
        o_ref[0] = part + b_ref[pl.ds(pl.program_id(0), 1), :]

    @pl.when(pl.program_id(1) > 0)
    def _rest():
        o_ref[0] += part


def _ada(cvec, w_ada, b_ada):
    depth, _, n3 = w_ada.shape
    tk = ADA_K
    return pl.pallas_call(
        _ada_kernel,
        grid=(depth, D_MODEL // tk),
        in_specs=[
            pl.BlockSpec((ADA_ROWS, tk), lambda l, k: (0, k)),
            pl.BlockSpec((1, tk, n3), lambda l, k: (l, k, 0)),
            pl.BlockSpec((depth, n3), lambda l, k: (0, 0)),
        ],
        out_specs=pl.BlockSpec((1, ADA_ROWS, n3), lambda l, k: (l, 0, 0)),
        out_shape=jax.ShapeDtypeStruct((depth, ADA_ROWS, n3), F32),
        compiler_params=pltpu.CompilerParams(dimension_semantics=("arbitrary", "arbitrary")),
        name="ada_mod",
    )(cvec, w_ada, b_ada)


def _qkv_kernel(*refs, rope, nsub, first_layer, mod_row, tps):
    if rope:
        (x_ref, mod_ref, g_ref, w_ref, gains_ref, bd_ref, cos_ref, sin_ref,
         q_ref, k_ref, v_ref) = refs
    else:
        x_ref, mod_ref, g_ref, w_ref, gains_ref, bd_ref = refs[:6]
        q_ref, k_ref, v_ref = refs[-3:]
    tm = x_ref.shape[0]
    sub = tm // nsub
    bd = bd_ref[...]
    g = g_ref[...]
    mod = _mod_row(mod_ref, mod_row, tps)
    if rope:
        lane = lax.broadcasted_iota(jnp.int32, (sub, LANES), 1)
        second = (lane % (HEAD_DIM // 2)) >= (HEAD_DIM // 4)
    k_parts, v_parts = [None] * nsub, [None] * nsub

    def finish(s, c, part):
        rows = slice(s * sub, (s + 1) * sub)
        for h in range(2):
            j = 2 * c + h
            t = part[:, h * LANES:(h + 1) * LANES]
            if j * LANES >= QK_W:
                v_parts[s] = t
                continue
            sl = slice(j * LANES, (j + 1) * LANES)
            msq = _dot((t * t).astype(BF16), bd)
            t = t * lax.rsqrt(msq + EPS) * gains_ref[0, :, sl]
            if rope:
                partner = jnp.where(second, pltpu.roll(t, HEAD_DIM // 4, 1),
                                    pltpu.roll(t, LANES - HEAD_DIM // 4, 1))
                t = t * cos_ref[rows, :] + partner * sin_ref[rows, :]
            if j * LANES < ATTN_W:
                q_ref[rows, sl] = t.astype(BF16)
            else:
                k_parts[s] = t

    ncol = QKV_W // (2 * LANES)
    prev = None
    for s in range(nsub + 1):
        if s < nsub:
            xn = _mod_norm(x_ref[s * sub:(s + 1) * sub, :], g, mod).astype(BF16)
        cur = []
        for c in range(ncol):
            if s < nsub:
                cur.append(_dot(xn, w_ref[0, :, c * 2 * LANES:(c + 1) * 2 * LANES]))
            if prev is not None:
                finish(s - 1, c, prev[c])
        prev = cur
    for ref, parts in ((k_ref, k_parts), (v_ref, v_parts)):
        val = jnp.concatenate(parts, axis=0)
        if rope:
            ref[...] = val
            continue
        seq = ref.shape[3]
        for p in range(ref.shape[0]):
            piece = val[p * seq:(p + 1) * seq, :].T
            if first_layer is None:
                ref[p, 0] = piece
            else:
                for l2 in range(ref.shape[1]):
                    ref[p, l2] = piece if l2 == first_layer else jnp.zeros_like(piece)


def _mod_spec(layer):
    return pl.BlockSpec((1, ADA_ROWS, 3 * D_MODEL), lambda i: (layer, 0, 0))


def _qkv(x2, mod, mod_row, norm_g, w_in_bf, layer, gains, bd, tables, kv_prev, *, seq, tm):
    tokens = x2.shape[0]
    depth = w_in_bf.shape[0]
    tps = max(seq // tm, 1)
    rope = tables is not None
    row = lambda i: (i, 0)
    const = lambda i: (0, 0)
    in_specs = [
        pl.BlockSpec((tm, D_MODEL), row),
        _mod_spec(layer),
        pl.BlockSpec((1, D_MODEL), const),
        pl.BlockSpec((1, D_MODEL, QKV_W), lambda i: (layer, 0, 0)),
        pl.BlockSpec((1, 1, QK_W), lambda i: (layer, 0, 0)),
        pl.BlockSpec((LANES, LANES), const),
    ]
    args = [x2, mod, norm_g.reshape(1, D_MODEL), w_in_bf, gains, bd]
    out_shapes = [jax.ShapeDtypeStruct((tokens, ATTN_W), BF16)]
    out_specs = [pl.BlockSpec((tm, ATTN_W), row)]
    aliases = {}
    first_layer = None
    if rope:
        in_specs += [pl.BlockSpec((tm, LANES), lambda i: (i % tps, 0))] * 2
        args += list(tables)
        out_shapes += [jax.ShapeDtypeStruct((tokens, KV_W), F32)] * 2
        out_specs += [pl.BlockSpec((tm, KV_W), row)] * 2
    else:
        nseq = tm // seq
        out_shapes += [jax.ShapeDtypeStruct((tokens // seq, depth, KV_W, seq), F32)] * 2
        if kv_prev is None:
            first_layer = layer
            out_specs += [pl.BlockSpec((nseq, depth, KV_W, seq), lambda i: (i, 0, 0, 0))] * 2
        else:
            out_specs += [pl.BlockSpec((nseq, 1, KV_W, seq), lambda i: (i, layer, 0, 0))] * 2
            in_specs += [pl.BlockSpec(memory_space=pl.ANY)] * 2
            aliases = {len(args): 1, len(args) + 1: 2}
            args += list(kv_prev)
    return pl.pallas_call(
        functools.partial(_qkv_kernel, rope=rope, nsub=tm // QKV_SUB, first_layer=first_layer,
                          mod_row=mod_row, tps=tps),
        grid=(tokens // tm,),
        in_specs=in_specs,
        out_specs=out_specs,
        out_shape=out_shapes,
        input_output_aliases=aliases,
        compiler_params=pltpu.CompilerParams(
            dimension_semantics=("arbitrary",), vmem_limit_bytes=VMEM_LIMIT),
        name="qkv_rope" if rope else "qkv",
    )(*args)


VT_ROWS = HEAD_DIM + 16


def _attn_kernel(q_ref, k_ref, v_ref, ck_ref, cv_ref, o_ref, kab, vt, q2, acct, mbuf, *bufs, n, n_ctx, rb):
    nr = N_KV_HEADS * 2 * n // rb
    per_group = 2 * n // rb
    t_all = n + n_ctx
    sbuf, pbuf = bufs[:ATTN_BUFS], bufs[ATTN_BUFS:]

    def fill(k_src, v_t, r0, m):
        lane = lax.broadcasted_iota(jnp.int32, (m, LANES), 1)
        for grp in range(N_KV_HEADS):
            k_own = jnp.where((lane // HEAD_DIM) == grp, k_src, 0.0)
            k_oth = pltpu.roll(k_own, HEAD_DIM, 1)
            kab[2 * grp, r0:r0 + m, :] = (k_own if grp == 0 else k_oth).astype(BF16)
            kab[2 * grp + 1, r0:r0 + m, :] = (k_oth if grp == 0 else k_own).astype(BF16)
            vt[grp, 0:HEAD_DIM, r0:r0 + m] = v_t[grp * HEAD_DIM:(grp + 1) * HEAD_DIM, :].astype(BF16)

    piece = 256
    for r0 in range(0, n, piece):
        fill(k_ref[0, r0:r0 + piece, :], v_ref[0, r0:r0 + piece, :].T, r0, piece)
    fill(ck_ref[0, 0].T, cv_ref[0, 0], n, n_ctx)
    for grp in range(N_KV_HEADS):
        vt[grp, HEAD_DIM:, :] = jnp.ones((VT_ROWS - HEAD_DIM, t_all), BF16)
    for t in range(ATTN_W // LANES):
        for c0 in range(0, n, piece):
            q2[:, t * n + c0:t * n + c0 + piece] = (
                q_ref[0, c0:c0 + piece, t * LANES:(t + 1) * LANES].astype(F32).T.astype(BF16))

    span = ATTN_REGION
    n_regions = 2 * nr // span
    chunks = [slice(k0, k0 + ATTN_KEY_CHUNK) for k0 in range(0, t_all, ATTN_KEY_CHUNK)]

    def score_chunk(kidx, ks, qr, buf, run):
        sc = _dot(kab[kidx, ks, :], qr)
        sbuf[buf][ks, :] = sc
        while sc.shape[0] > 8:
            half = sc.shape[0] // 2
            sc = jnp.maximum(sc[0:half, :], sc[half:, :])
        return sc if run is None else jnp.maximum(run, sc)

    def q_rows(r):
        return q2[:, pl.ds(pl.multiple_of(r * rb, rb), rb)]

    per_tile = n // rb

    def flush(r):
        r0 = (r % per_tile) * rb
        if not isinstance(r, int):
            r0 = pl.multiple_of(r0, rb)
        o_ref[0, r // per_tile, pl.ds(r0, rb), :] = acct[r].T

    def region(g, parity, do_scores, do_numer, do_weigh, do_flush):
        cur, oth = parity * span, (1 - parity) * span
        for j in range(span):
            slot = j % 2
            if do_flush and slot == 0:
                flush((g - 2) * (span // 2) + j // 2)
            if do_numer:
                m = mbuf[cur + j]
            if do_scores:
                r_next = (g + 1) * (span // 2) + j // 2
                qr = q_rows(r_next)
                kidx = (r_next // per_group) * 2 + slot
            if do_weigh:
                r_prev = (g - 1) * (span // 2) + j // 2
                vt_g = vt.at[r_prev // per_group]
            run, res = None, None
            for ks in chunks:
                mid = (ks.start + ks.stop) // 2
                ka, kb = slice(ks.start, mid), slice(mid, ks.stop)
                if do_scores:
                    run = score_chunk(kidx, ka, qr, oth + j, run)
                if do_numer:
                    pbuf[cur + j][ka, :] = jnp.exp(sbuf[cur + j][ka, :] - m).astype(BF16)
                if do_scores:
                    run = score_chunk(kidx, kb, qr, oth + j, run)
                if do_weigh:
                    part = _dot(vt_g[:, ks], pbuf[oth + j][ks, :])
                    res = part if res is None else res + part
                if do_numer:
                    pbuf[cur + j][kb, :] = jnp.exp(sbuf[cur + j][kb, :] - m).astype(BF16)
            if do_scores:
                mbuf[oth + j] = jnp.max(run, axis=0, keepdims=True)
            if do_weigh:
                acct[r_prev, slot * HEAD_DIM:(slot + 1) * HEAD_DIM, :] = (
                    res[0:HEAD_DIM, :] * (1.0 / res[HEAD_DIM:HEAD_DIM + 1, :]))

    region(-1, 1, True, False, False, False)
    region(0, 0, True, True, False, False)

    def body(g, carry):
        for parity in range(2):
            pl.when(g % 2 == parity)(lambda: region(g, parity, True, True, True, False))
        return carry

    lax.fori_loop(1, n_regions - 1, body, 0)
    region(n_regions - 1, (n_regions - 1) % 2, False, True, True, False)
    region(n_regions, n_regions % 2, False, False, True, False)
    for r in range(nr):
        flush(r)


def _attention(q, k, v, cache_k, cache_v, layer):
    rb = ATTN_ROWS
    bsz, n, _ = q.shape
    n_ctx = cache_k.shape[3]
    t_all = n + n_ctx
    tiles = ATTN_W // LANES
    kmap = lambda b: (b, 0, 0)
    cmap = lambda b: (b, layer, 0, 0)
    return pl.pallas_call(
        functools.partial(_attn_kernel, n=n, n_ctx=n_ctx, rb=rb),
        grid=(bsz,),
        in_specs=[
            pl.BlockSpec((1, n, ATTN_W), kmap),
            pl.BlockSpec((1, n, KV_W), kmap),
            pl.BlockSpec((1, n, KV_W), kmap),
            pl.BlockSpec((1, 1, KV_W, n_ctx), cmap),
            pl.BlockSpec((1, 1, KV_W, n_ctx), cmap),
        ],
        out_specs=pl.BlockSpec((1, tiles, n, LANES), lambda b: (b, 0, 0, 0)),
        out_shape=jax.ShapeDtypeStruct((bsz, tiles, n, LANES), F32),
        scratch_shapes=[
            pltpu.VMEM((2 * N_KV_HEADS, t_all, LANES), BF16),
            pltpu.VMEM((N_KV_HEADS, VT_ROWS, t_all), BF16),
            pltpu.VMEM((LANES, tiles * n), BF16),
            pltpu.VMEM((tiles * n // rb, LANES, rb), F32),
            pltpu.VMEM((ATTN_BUFS, 1, rb), F32),
        ] + [pltpu.VMEM((t_all, rb), F32)] * ATTN_BUFS + [pltpu.VMEM((t_all, rb), BF16)] * ATTN_BUFS,
        compiler_params=pltpu.CompilerParams(
            dimension_semantics=("arbitrary",), vmem_limit_bytes=ATTN_VMEM_LIMIT),
        name="attn_lat",
    )(q, k, v, cache_k, cache_v)


def _attn_ctx_kernel(q_ref, k_ref, v_ref, o_ref, *, n):
    low = lax.broadcasted_iota(jnp.int32, (n, LANES), 1) < HEAD_DIM
    ones_lo = jnp.where(low, 1.0, 0.0)
    zk = jnp.zeros((HEAD_DIM, n), F32)
    gw = 2 * LANES
    for b in range(CTX_BATCH):
        kt_all = k_ref[b, 0]
        v = v_ref[b, 0].T
        vr = pltpu.roll(v, HEAD_DIM, 1)
        for g in range(N_KV_HEADS):
            kg = kt_all[g * HEAD_DIM:(g + 1) * HEAD_DIM, :]
            kts = (jnp.concatenate([kg, zk], axis=0).astype(BF16),
                   jnp.concatenate([zk, kg], axis=0).astype(BF16))
            v_lo, v_hi = (v, vr) if g == 0 else (vr, v)
            vs = (jnp.concatenate([jnp.where(low, v_lo, 0.0), ones_lo], axis=1).astype(BF16),
                  jnp.concatenate([jnp.where(low, 0.0, v_hi), 1.0 - ones_lo], axis=1).astype(BF16))
            q2 = jnp.concatenate([q_ref[b, :, g * gw:g * gw + LANES],
                                  q_ref[b, :, g * gw + LANES:(g + 1) * gw]], axis=0)
            acc = None
            for kt_s, v_s in zip(kts, vs):
                s = _dot(q2, kt_s)
                m = jnp.max(s, axis=-1, keepdims=True)
                pv = _dot(jnp.exp(s - m).astype(BF16), v_s)
                acc = pv if acc is None else acc + pv
            o = acc[:, 0:LANES] / acc[:, LANES:]
            o_ref[b, 2 * g] = o[0:n]
            o_ref[b, 2 * g + 1] = o[n:]


def _attention_ctx(q, k, v, layer):
    bsz, n, _ = q.shape
    blk = lambda w: pl.BlockSpec((CTX_BATCH, n, w), lambda b: (b, 0, 0))
    kv_blk = pl.BlockSpec((CTX_BATCH, 1, KV_W, n), lambda b: (b, layer, 0, 0))
    return pl.pallas_call(
        functools.partial(_attn_ctx_kernel, n=n),
        grid=(bsz // CTX_BATCH,),
        in_specs=[blk(ATTN_W), kv_blk, kv_blk],
        out_specs=pl.BlockSpec((CTX_BATCH, ATTN_W // LANES, n, LANES), lambda b: (b, 0, 0, 0)),
        out_shape=jax.ShapeDtypeStruct((bsz, ATTN_W // LANES, n, LANES), F32),
        compiler_params=pltpu.CompilerParams(
            dimension_semantics=("arbitrary",), vmem_limit_bytes=VMEM_LIMIT),
        name="attn_ctx",
    )(q, k, v)


def _mix_kernel(attn_ref, x_ref, xp_ref, xn_ref, mod_ref, g_ref, w_ref, wout_ref,
                convw_ref, convb_ref, wpool_ref, pscale_ref, fg_ref,
                o_ref, xe, yext, uext, *, tm, tps, seq, final, mod_row):
    mod = _mod_row(mod_ref, mod_row, tps)
    j = pl.program_id(0) % tps
    pm = jnp.where(j == 0, 0.0, 1.0)
    nm = jnp.where(j == tps - 1, 0.0, 1.0)
    c = CONV_W
    g = g_ref[...]
    x = x_ref[...]
    xe[0:tm, :] = _mod_norm(x, g, mod).astype(BF16)
    if tps > 1:
        halo = jnp.concatenate([xn_ref[...], xp_ref[...]], axis=0)
        xe[tm:, :] = _mod_norm(halo, g, mod).astype(BF16)

    def proj(rows, off, width):
        return _dot(xe[0:rows, :], w_ref[0, :, off:off + width])

    nseq = max(tm // seq, 1)
    whole = tps == 1
    ln = tm // nseq
    hrows = tm if whole else tm + 2 * HALO
    h_all = proj(hrows, COL_CONV_H, c)
    c_all = proj(hrows, COL_CONV_C, c)
    u_all = proj(hrows, COL_POOL_U, POOL_W)
    z_a = proj(tm, COL_ATTN_GATE, ATTN_W)
    a = ATTN_W
    attn = jnp.concatenate(
        [jnp.concatenate([attn_ref[p, t] for t in range(attn_ref.shape[1])], axis=1)
         for p in range(attn_ref.shape[0])], axis=0)
    attn_o = (attn * _silu(z_a)).astype(BF16)
    out_a = _dot(attn_o, wout_ref[0, 0:a, :])
    b_c = proj(tm, COL_CONV_B, c)
    z_c = proj(tm, COL_CONV_GATE, c)
    z_p = proj(tm, COL_POOL_GATE, POOL_W)

    def extend(ext, p, val):
        zeros = jnp.zeros((HALO, val.shape[1]), F32)
        ext[p, 0:HALO, :] = zeros if whole else val[tm + HALO:, :] * pm
        ext[p, HALO:HALO + ln, :] = val[p * ln:(p + 1) * ln, :]
        ext[p, HALO + ln:, :] = zeros if whole else val[tm:tm + HALO, :] * nm

    y = h_all * c_all
    convs = []
    for p in range(nseq):
        extend(yext, p, y)
        convs.append(yext[p, HALO - 1:HALO - 1 + ln, :] * convw_ref[0:1, :]
                     + yext[p, HALO:HALO + ln, :] * convw_ref[1:2, :]
                     + yext[p, HALO + 1:HALO + 1 + ln, :] * convw_ref[2:3, :] + convb_ref[...])
    conv = convs[0] if nseq == 1 else jnp.concatenate(convs, axis=0)
    conv_o = (b_c * conv * _silu(z_c)).astype(BF16)

    lane = lax.broadcasted_iota(jnp.int32, (ln, LANES), 1)
    pos = lax.broadcasted_iota(jnp.int32, (ln, LANES), 0) + (0 if whole else j * tm)
    low = lane < POOL_GROUP
    d_rows = []
    for p in range(nseq):
        extend(uext, p, u_all)

        def win(k0, k1, sl):
            tot = None
            for k in range(k0, k1):
                t = uext[p, HALO + k:HALO + k + ln, sl]
                tot = t if tot is None else tot + t
            return tot

        ds = []
        for ti, (wa, wb) in enumerate(((2, 4), (8, 16))):
            sl = slice(ti * LANES, (ti + 1) * LANES)
            s_a = win(-(wa // 2), wa // 2, sl)
            s_b = s_a + win(-(wb // 2), -(wa // 2), sl) + win(wa // 2, wb // 2, sl)
            half = jnp.where(low, wa // 2, wb // 2)
            cnt = jnp.minimum(pos + half - 1, seq - 1) - jnp.maximum(pos - half, 0) + 1
            tot = jnp.where(low, s_a, s_b)
            ds.append(tot / cnt.astype(F32) - u_all[p * ln:(p + 1) * ln, sl])
        d_rows.append(jnp.concatenate(ds, axis=-1))
    d = (d_rows[0] if nseq == 1 else jnp.concatenate(d_rows, axis=0)).astype(BF16)
    pool_o = (_dot(d, wpool_ref[...]) * pscale_ref[...] * _silu(z_p)).astype(BF16)

    out = out_a + _dot(conv_o, wout_ref[0, a:a + c, :]) + _dot(pool_o, wout_ref[0, a + c:, :])
    res = x + mod[:, 2 * D_MODEL:] * out
    if final:
        ms = jnp.mean(res * res, axis=-1, keepdims=True)
        res = res * lax.rsqrt(ms + EPS) * fg_ref[...]
    o_ref[...] = res


def _mix(attn2, x2, mod, mod_row, norm_g, w_in_bf, wout_bf, layer, conv_w, conv_b, wpool_bd, pool_scale,
         final_g, *, seq, tm, final):
    tokens = x2.shape[0]
    tps = max(seq // tm, 1)
    nseq = max(tm // seq, 1)
    hb = tm // HALO
    nhb = tokens // HALO
    row = lambda i: (i, 0)
    const = lambda i: (0, 0)
    in_specs = [
        (pl.BlockSpec((nseq, ATTN_W // LANES, tm // nseq, LANES), lambda i: (i, 0, 0, 0)) if tps == 1 else
         pl.BlockSpec((1, ATTN_W // LANES, tm, LANES), lambda i: (i // tps, 0, i % tps, 0))),
        pl.BlockSpec((tm, D_MODEL), row),
        pl.BlockSpec((HALO, D_MODEL), lambda i: (jnp.maximum(i * hb - 1, 0), 0)),
        pl.BlockSpec((HALO, D_MODEL), lambda i: (jnp.minimum((i + 1) * hb, nhb - 1), 0)),
        _mod_spec(layer),
        pl.BlockSpec((1, D_MODEL), const),
        pl.BlockSpec((1, D_MODEL, IN_W), lambda i: (layer, 0, 0)),
        pl.BlockSpec((1, D_MODEL, D_MODEL), lambda i: (layer, 0, 0)),
        pl.BlockSpec((3, CONV_W), const),
        pl.BlockSpec((1, CONV_W), const),
        pl.BlockSpec((POOL_W, POOL_W), const),
        pl.BlockSpec((1, POOL_W), const),
        pl.BlockSpec((1, D_MODEL), const),
    ]
    return pl.pallas_call(
        functools.partial(_mix_kernel, tm=tm, tps=tps, seq=seq, final=final, mod_row=mod_row),
        grid=(tokens // tm,),
        in_specs=in_specs,
        out_specs=pl.BlockSpec((tm, D_MODEL), row),
        out_shape=jax.ShapeDtypeStruct((tokens, D_MODEL), F32),
        scratch_shapes=[
            pltpu.VMEM((tm + 2 * HALO, D_MODEL), BF16),
            pltpu.VMEM((nseq, tm // nseq + 2 * HALO, CONV_W), F32),
            pltpu.VMEM((nseq, tm // nseq + 2 * HALO, POOL_W), F32),
        ],
        compiler_params=pltpu.CompilerParams(
            dimension_semantics=("arbitrary",), vmem_limit_bytes=VMEM_LIMIT),
        name="mix_final" if final else "mix",
    )(attn2, x2, x2, x2, mod, norm_g.reshape(1, D_MODEL), w_in_bf, wout_bf,
      conv_w, conv_b.reshape(1, CONV_W), wpool_bd, pool_scale.reshape(1, POOL_W),
      final_g.reshape(1, D_MODEL))


def _rope_tables(n):
    rows = n // GRID_W
    row = np.repeat(np.arange(rows), GRID_W).astype(np.float64)
    col = np.tile(np.arange(GRID_W), rows).astype(np.float64)
    half = HEAD_DIM // 2
    inv = 1.0 / (ROPE_THETA ** (np.arange(0, half, 2, dtype=np.float64) / half))
    ar = row[:, None] * inv
    ac = col[:, None] * inv
    ang = np.tile(np.concatenate([ar, ar, ac, ac], axis=-1), (1, LANES // HEAD_DIM))
    second = (np.arange(LANES) % half) >= (half // 2)
    return jnp.asarray(np.cos(ang), F32), jnp.asarray(np.where(second, np.sin(ang), -np.sin(ang)), F32)


def _block_diag(blocks):
    n = len(blocks)
    rows = []
    for i, b in enumerate(blocks):
        rows.append(jnp.concatenate(
            [b if j == i else jnp.zeros((b.shape[0], blocks[j].shape[1]), b.dtype) for j in range(n)],
            axis=1))
    return jnp.concatenate(rows, axis=0)


def kernel(x_prompt, x_sample, cache_k, cache_v, c, c_ctx, norm_g, w_ada, b_ada, w_in,
           q_norm_g, k_norm_g, conv_w, conv_b, pool_w, pool_scale, w_out, final_g):
    depth = w_in.shape[0]
    bp, sp, _ = x_prompt.shape
    bs, ss, _ = x_sample.shape
    n_ctx = cache_k.shape[2]

    cvec = jnp.concatenate(
        [c, c_ctx[None, :], jnp.zeros((ADA_ROWS - bs - 1, D_MODEL), F32)], axis=0)
    mod = _ada(cvec, w_ada, b_ada)

    w_in_bf = w_in.astype(BF16)
    w_out_bf = w_out.astype(BF16)
    tables = _rope_tables(ss)
    ones = jnp.full((HEAD_DIM, HEAD_DIM), 1.0 / HEAD_DIM, BF16)
    bd = _block_diag([ones, ones])
    ck = cache_k.transpose(0, 1, 3, 4, 2).reshape(bs, depth, KV_W, n_ctx)
    cv = cache_v.transpose(0, 1, 3, 4, 2).reshape(bs, depth, KV_W, n_ctx)

    h = x_prompt.reshape(bp * sp, D_MODEL)
    z = x_sample.reshape(bs * ss, D_MODEL)
    kv_ctx = None
    gains = jnp.concatenate(
        [jnp.tile(q_norm_g * (HEAD_DIM ** -0.5), (1, N_HEADS)), jnp.tile(k_norm_g, (1, N_KV_HEADS))],
        axis=-1).reshape(depth, 1, QK_W)
    for l in range(depth):
        wpool_bd = _block_diag([pool_w[l, g] for g in range(pool_w.shape[1])]).astype(BF16)
        final = l == depth - 1
        mix = functools.partial(
            _mix, norm_g=norm_g[l], w_in_bf=w_in_bf, wout_bf=w_out_bf, layer=l,
            conv_w=conv_w[l], conv_b=conv_b[l], wpool_bd=wpool_bd, pool_scale=pool_scale[l],
            final_g=final_g, final=final)

        q, k, v = _qkv(h, mod, bs, norm_g[l], w_in_bf, l, gains, bd, None, kv_ctx, seq=sp, tm=QKV_TILE)
        kv_ctx = (k, v)
        attn = _attention_ctx(q.reshape(bp, sp, ATTN_W), k, v, l)
        h = mix(attn, h, mod, bs, seq=sp, tm=MIX_TILE)

        q, k, v = _qkv(z, mod, None, norm_g[l], w_in_bf, l, gains, bd, tables, None, seq=ss, tm=QKV_TILE)
        attn = _attention(q.reshape(bs, ss, ATTN_W), k.reshape(bs, ss, KV_W), v.reshape(bs, ss, KV_W),
                          ck, cv, l)
        z = mix(attn, z, mod, None, seq=ss, tm=MIX_TILE)

    y_prompt = h.reshape(bp, sp, D_MODEL)
    y_sample = z.reshape(bs, ss, D_MODEL)
    new_k, new_v = (a.reshape(bp, depth, N_KV_HEADS, HEAD_DIM, sp).transpose(0, 1, 4, 2, 3) for a in kv_ctx)
    return (y_prompt, y_sample, new_k, new_v)
```

```python
import functools

import jax
import jax.numpy as jnp
import numpy as np
from jax import lax
from jax.experimental import pallas as pl
from jax.experimental.pallas import tpu as pltpu

D_MODEL = 1024
GRID_W = 64
N_HEADS = 8
N_KV_HEADS = 2
HEAD_DIM = 64
ATTN_W = N_HEADS * HEAD_DIM
KV_W = N_KV_HEADS * HEAD_DIM
QK_W = ATTN_W + KV_W
QKV_W = QK_W + KV_W
CONV_W = 256
POOL_W = 256
POOL_GROUP = 64
IN_W = 2816
COL_ATTN_GATE = QKV_W
COL_CONV_H = COL_ATTN_GATE + ATTN_W
COL_CONV_B = COL_CONV_H + CONV_W
COL_CONV_C = COL_CONV_B + CONV_W
COL_CONV_GATE = COL_CONV_C + CONV_W
COL_POOL_U = COL_CONV_GATE + CONV_W
COL_POOL_GATE = COL_POOL_U + POOL_W
ROPE_THETA = 10000.0
EPS = 1e-6
LANES = 128
HALO = 8
ADA_ROWS = 8
VMEM_LIMIT = 48 * 1024 * 1024
ATTN_VMEM_LIMIT = 55 * 1024 * 1024

ADA_K = 512
QKV_TILE = 1024
QKV_SUB = 256
MIX_TILE = 512
ATTN_ROWS = 256
ATTN_REGION = 4
ATTN_BUFS = 2 * ATTN_REGION
ATTN_KEY_CHUNK = 256
CTX_BATCH = 4

F32 = jnp.float32
BF16 = jnp.bfloat16


def _silu(z):
    return z * (1.0 / (1.0 + jnp.exp(-z)))


def _dot(a, b):
    return jnp.dot(a, b, preferred_element_type=F32)


def _mod_row(mod_ref, mod_row, tps):
    row = mod_row if mod_row is not None else pl.program_id(0) // tps
    return mod_ref[0, pl.ds(row, 1), :]


def _mod_norm(x, g, mod):
    ms = jnp.mean(x * x, axis=-1, keepdims=True)
    gs = g * (1.0 + mod[:, D_MODEL:2 * D_MODEL])
    return x * lax.rsqrt(ms + EPS) * gs + mod[:, 0:D_MODEL]


def _ada_kernel(c_ref, w_ref, b_ref, o_ref):
    part = _dot(_silu(c_ref[...]).astype(BF16), w_ref[0].astype(BF16))

    @pl.when(pl.program_id(1) == 0)
    def _first():
        o_ref[0] = part + b_ref[pl.ds(pl.program_id(0), 1), :]

    @pl.when(pl.program_id(1) > 0)
    def _rest():
        o_ref[0] += part


def _ada(cvec, w_ada, b_ada):
    depth, _, n3 = w_ada.shape
    tk = ADA_K
    return pl.pallas_call(
        _ada_kernel,
        grid=(depth, D_MODEL // tk),
        in_specs=[
            pl.BlockSpec((ADA_ROWS, tk), lambda l, k: (0, k)),
            pl.BlockSpec((1, tk, n3), lambda l, k: (l, k, 0)),
            pl.BlockSpec((depth, n3), lambda l, k: (0, 0)),
        ],
        out_specs=pl.BlockSpec((1, ADA_ROWS, n3), lambda l, k: (l, 0, 0)),
        out_shape=jax.ShapeDtypeStruct((depth, ADA_ROWS, n3), F32),
        compiler_params=pltpu.CompilerParams(dimension_semantics=("arbitrary", "arbitrary")),
        name="ada_mod",
    )(cvec, w_ada, b_ada)


def _qkv_kernel(*refs, rope, nsub, first_layer, mod_row, tps):
    if rope:
        (x_ref, mod_ref, g_ref, w_ref, gains_ref, bd_ref, cos_ref, sin_ref,
         q_ref, k_ref, v_ref) = refs
    else:
        x_ref, mod_ref, g_ref, w_ref, gains_ref, bd_ref = refs[:6]
        q_ref, k_ref, v_ref = refs[-3:]
    tm = x_ref.shape[0]
    sub = tm // nsub
    bd = bd_ref[...]
    g = g_ref[...]
    mod = _mod_row(mod_ref, mod_row, tps)
    if rope:
        lane = lax.broadcasted_iota(jnp.int32, (sub, LANES), 1)
        second = (lane % (HEAD_DIM // 2)) >= (HEAD_DIM // 4)
    k_parts, v_parts = [None] * nsub, [None] * nsub

    def finish(s, c, part):
        rows = slice(s * sub, (s + 1) * sub)
        for h in range(2):
            j = 2 * c + h
            t = part[:, h * LANES:(h + 1) * LANES]
            if j * LANES >= QK_W:
                v_parts[s] = t
                continue
            sl = slice(j * LANES, (j + 1) * LANES)
            msq = _dot((t * t).astype(BF16), bd)
            t = t * lax.rsqrt(msq + EPS) * gains_ref[0, :, sl]
            if rope:
                partner = jnp.where(second, pltpu.roll(t, HEAD_DIM // 4, 1),
                                    pltpu.roll(t, LANES - HEAD_DIM // 4, 1))
                t = t * cos_ref[rows, :] + partner * sin_ref[rows, :]
            if j * LANES < ATTN_W:
                q_ref[rows, sl] = t.astype(BF16)
            else:
                k_parts[s] = t

    ncol = QKV_W // (2 * LANES)
    prev = None
    for s in range(nsub + 1):
        if s < nsub:
            xn = _mod_norm(x_ref[s * sub:(s + 1) * sub, :], g, mod).astype(BF16)
        cur = []
        for c in range(ncol):
            if s < nsub:
                cur.append(_dot(xn, w_ref[0, :, c * 2 * LANES:(c + 1) * 2 * LANES]))
            if prev is not None:
                finish(s - 1, c, prev[c])
        prev = cur
    for ref, parts in ((k_ref, k_parts), (v_ref, v_parts)):
        val = jnp.concatenate(parts, axis=0)
        if rope:
            ref[...] = val
            continue
        seq = ref.shape[3]
        for p in range(ref.shape[0]):
            piece = val[p * seq:(p + 1) * seq, :].T
            if first_layer is None:
                ref[p, 0] = piece
            else:
                for l2 in range(ref.shape[1]):
                    ref[p, l2] = piece if l2 == first_layer else jnp.zeros_like(piece)


def _mod_spec(layer):
    return pl.BlockSpec((1, ADA_ROWS, 3 * D_MODEL), lambda i: (layer, 0, 0))


def _qkv(x2, mod, mod_row, norm_g, w_in_bf, layer, gains, bd, tables, kv_prev, *, seq, tm):
    tokens = x2.shape[0]
    depth = w_in_bf.shape[0]
    tps = max(seq // tm, 1)
    rope = tables is not None
    row = lambda i: (i, 0)
    const = lambda i: (0, 0)
    in_specs = [
        pl.BlockSpec((tm, D_MODEL), row),
        _mod_spec(layer),
        pl.BlockSpec((1, D_MODEL), const),
        pl.BlockSpec((1, D_MODEL, QKV_W), lambda i: (layer, 0, 0)),
        pl.BlockSpec((1, 1, QK_W), lambda i: (layer, 0, 0)),
        pl.BlockSpec((LANES, LANES), const),
    ]
    args = [x2, mod, norm_g.reshape(1, D_MODEL), w_in_bf, gains, bd]
    out_shapes = [jax.ShapeDtypeStruct((tokens, ATTN_W), BF16)]
    out_specs = [pl.BlockSpec((tm, ATTN_W), row)]
    aliases = {}
    first_layer = None
    if rope:
        in_specs += [pl.BlockSpec((tm, LANES), lambda i: (i % tps, 0))] * 2
        args += list(tables)
        out_shapes += [jax.ShapeDtypeStruct((tokens, KV_W), F32)] * 2
        out_specs += [pl.BlockSpec((tm, KV_W), row)] * 2
    else:
        nseq = tm // seq
        out_shapes += [jax.ShapeDtypeStruct((tokens // seq, depth, KV_W, seq), F32)] * 2
        if kv_prev is None:
            first_layer = layer
            out_specs += [pl.BlockSpec((nseq, depth, KV_W, seq), lambda i: (i, 0, 0, 0))] * 2
        else:
            out_specs += [pl.BlockSpec((nseq, 1, KV_W, seq), lambda i: (i, layer, 0, 0))] * 2
            in_specs += [pl.BlockSpec(memory_space=pl.ANY)] * 2
            aliases = {len(args): 1, len(args) + 1: 2}
            args += list(kv_prev)
    return pl.pallas_call(
        functools.partial(_qkv_kernel, rope=rope, nsub=tm // QKV_SUB, first_layer=first_layer,
                          mod_row=mod_row, tps=tps),
        grid=(tokens // tm,),
        in_specs=in_specs,
        out_specs=out_specs,
        out_shape=out_shapes,
        input_output_aliases=aliases,
        compiler_params=pltpu.CompilerParams(
            dimension_semantics=("arbitrary",), vmem_limit_bytes=VMEM_LIMIT),
        name="qkv_rope" if rope else "qkv",
    )(*args)


VT_ROWS = HEAD_DIM + 16


def _attn_kernel(q_ref, k_ref, v_ref, ck_ref, cv_ref, o_ref, kab, vt, q2, acct, mbuf, *bufs, n, n_ctx, rb):
    nr = N_KV_HEADS * 2 * n // rb
    per_group = 2 * n // rb
    t_all = n + n_ctx
    sbuf, pbuf = bufs[:ATTN_BUFS], bufs[ATTN_BUFS:]

    def fill(k_src, v_t, r0, m):
        lane = lax.broadcasted_iota(jnp.int32, (m, LANES), 1)
        for grp in range(N_KV_HEADS):
            k_own = jnp.where((lane // HEAD_DIM) == grp, k_src, 0.0)
            k_oth = pltpu.roll(k_own, HEAD_DIM, 1)
            kab[2 * grp, r0:r0 + m, :] = (k_own if grp == 0 else k_oth).astype(BF16)
            kab[2 * grp + 1, r0:r0 + m, :] = (k_oth if grp == 0 else k_own).astype(BF16)
            vt[grp, 0:HEAD_DIM, r0:r0 + m] = v_t[grp * HEAD_DIM:(grp + 1) * HEAD_DIM, :].astype(BF16)

    piece = 256
    for r0 in range(0, n, piece):
        fill(k_ref[0, r0:r0 + piece, :], v_ref[0, r0:r0 + piece, :].T, r0, piece)
    fill(ck_ref[0, 0].T, cv_ref[0, 0], n, n_ctx)
    for grp in range(N_KV_HEADS):
        vt[grp, HEAD_DIM:, :] = jnp.ones((VT_ROWS - HEAD_DIM, t_all), BF16)
    for t in range(ATTN_W // LANES):
        for c0 in range(0, n, piece):
            q2[:, t * n + c0:t * n + c0 + piece] = (
                q_ref[0, c0:c0 + piece, t * LANES:(t + 1) * LANES].astype(F32).T.astype(BF16))

    span = ATTN_REGION
    n_regions = 2 * nr // span
    chunks = [slice(k0, k0 + ATTN_KEY_CHUNK) for k0 in range(0, t_all, ATTN_KEY_CHUNK)]

    def score_chunk(kidx, ks, qr, buf, run):
        sc = _dot(kab[kidx, ks, :], qr)
        sbuf[buf][ks, :] = sc
        while sc.shape[0] > 8:
            half = sc.shape[0] // 2
            sc = jnp.maximum(sc[0:half, :], sc[half:, :])
        return sc if run is None else jnp.maximum(run, sc)

    def q_rows(r):
        return q2[:, pl.ds(pl.multiple_of(r * rb, rb), rb)]

    per_tile = n // rb

    def flush(r):
        r0 = (r % per_tile) * rb
        if not isinstance(r, int):
            r0 = pl.multiple_of(r0, rb)
        o_ref[0, r // per_tile, pl.ds(r0, rb), :] = acct[r].T

    def region(g, parity, do_scores, do_numer, do_weigh, do_flush):
        cur, oth = parity * span, (1 - parity) * span
        for j in range(span):
            slot = j % 2
            if do_flush and slot == 0:
                flush((g - 2) * (span // 2) + j // 2)
            if do_numer:
                m = mbuf[cur + j]
            if do_scores:
                r_next = (g + 1) * (span // 2) + j // 2
                qr = q_rows(r_next)
                kidx = (r_next // per_group) * 2 + slot
            if do_weigh:
                r_prev = (g - 1) * (span // 2) + j // 2
                vt_g = vt.at[r_prev // per_group]
            run, res = None, None
            for ks in chunks:
                mid = (ks.start + ks.stop) // 2
                ka, kb = slice(ks.start, mid), slice(mid, ks.stop)
                if do_scores:
                    run = score_chunk(kidx, ka, qr, oth + j, run)
                if do_numer:
                    pbuf[cur + j][ka, :] = jnp.exp(sbuf[cur + j][ka, :] - m).astype(BF16)
                if do_scores:
                    run = score_chunk(kidx, kb, qr, oth + j, run)
                if do_weigh:
                    part = _dot(vt_g[:, ks], pbuf[oth + j][ks, :])
                    res = part if res is None else res + part
                if do_numer:
                    pbuf[cur + j][kb, :] = jnp.exp(sbuf[cur + j][kb, :] - m).astype(BF16)
            if do_scores:
                mbuf[oth + j] = jnp.max(run, axis=0, keepdims=True)
            if do_weigh:
                acct[r_prev, slot * HEAD_DIM:(slot + 1) * HEAD_DIM, :] = (
                    res[0:HEAD_DIM, :] * (1.0 / res[HEAD_DIM:HEAD_DIM + 1, :]))

    region(-1, 1, True, False, False, False)
    region(0, 0, True, True, False, False)

    def body(g, carry):
        for parity in range(2):
            pl.when(g % 2 == parity)(lambda: region(g, parity, True, True, True, False))
        return carry

    lax.fori_loop(1, n_regions - 1, body, 0)
    region(n_regions - 1, (n_regions - 1) % 2, False, True, True, False)
    region(n_regions, n_regions % 2, False, False, True, False)
    for r in range(nr):
        flush(r)


def _attention(q, k, v, cache_k, cache_v, layer):
    rb = ATTN_ROWS
    bsz, n, _ = q.shape
    n_ctx = cache_k.shape[3]
    t_all = n + n_ctx
    tiles = ATTN_W // LANES
    kmap = lambda b: (b, 0, 0)
    cmap = lambda b: (b, layer, 0, 0)
    return pl.pallas_call(
        functools.partial(_attn_kernel, n=n, n_ctx=n_ctx, rb=rb),
        grid=(bsz,),
        in_specs=[
            pl.BlockSpec((1, n, ATTN_W), kmap),
            pl.BlockSpec((1, n, KV_W), kmap),
            pl.BlockSpec((1, n, KV_W), kmap),
            pl.BlockSpec((1, 1, KV_W, n_ctx), cmap),
            pl.BlockSpec((1, 1, KV_W, n_ctx), cmap),
        ],
        out_specs=pl.BlockSpec((1, tiles, n, LANES), lambda b: (b, 0, 0, 0)),
        out_shape=jax.ShapeDtypeStruct((bsz, tiles, n, LANES), F32),
        scratch_shapes=[
            pltpu.VMEM((2 * N_KV_HEADS, t_all, LANES), BF16),
            pltpu.VMEM((N_KV_HEADS, VT_ROWS, t_all), BF16),
            pltpu.VMEM((LANES, tiles * n), BF16),
            pltpu.VMEM((tiles * n // rb, LANES, rb), F32),
            pltpu.VMEM((ATTN_BUFS, 1, rb), F32),
        ] + [pltpu.VMEM((t_all, rb), F32)] * ATTN_BUFS + [pltpu.VMEM((t_all, rb), BF16)] * ATTN_BUFS,
        compiler_params=pltpu.CompilerParams(
            dimension_semantics=("arbitrary",), vmem_limit_bytes=ATTN_VMEM_LIMIT),
        name="attn_lat",
    )(q, k, v, cache_k, cache_v)


def _attn_ctx_kernel(q_ref, k_ref, v_ref, o_ref, *, n):
    low = lax.broadcasted_iota(jnp.int32, (n, LANES), 1) < HEAD_DIM
    ones_lo = jnp.where(low, 1.0, 0.0)
    zk = jnp.zeros((HEAD_DIM, n), F32)
    gw = 2 * LANES
    for b in range(CTX_BATCH):
        kt_all = k_ref[b, 0]
        v = v_ref[b, 0].T
        vr = pltpu.roll(v, HEAD_DIM, 1)
        for g in range(N_KV_HEADS):
            kg = kt_all[g * HEAD_DIM:(g + 1) * HEAD_DIM, :]
            kts = (jnp.concatenate([kg, zk], axis=0).astype(BF16),
                   jnp.concatenate([zk, kg], axis=0).astype(BF16))
            v_lo, v_hi = (v, vr) if g == 0 else (vr, v)
            vs = (jnp.concatenate([jnp.where(low, v_lo, 0.0), ones_lo], axis=1).astype(BF16),
                  jnp.concatenate([jnp.where(low, 0.0, v_hi), 1.0 - ones_lo], axis=1).astype(BF16))
            q2 = jnp.concatenate([q_ref[b, :, g * gw:g * gw + LANES],
                                  q_ref[b, :, g * gw + LANES:(g + 1) * gw]], axis=0)
            acc = None
            for kt_s, v_s in zip(kts, vs):
                s = _dot(q2, kt_s)
                m = jnp.max(s, axis=-1, keepdims=True)
                pv = _dot(jnp.exp(s - m).astype(BF16), v_s)
                acc = pv if acc is None else acc + pv
            o = acc[:, 0:LANES] / acc[:, LANES:]
            o_ref[b, 2 * g] = o[0:n]
            o_ref[b, 2 * g + 1] = o[n:]


def _attention_ctx(q, k, v, layer):
    bsz, n, _ = q.shape
    blk = lambda w: pl.BlockSpec((CTX_BATCH, n, w), lambda b: (b, 0, 0))
    kv_blk = pl.BlockSpec((CTX_BATCH, 1, KV_W, n), lambda b: (b, layer, 0, 0))
    return pl.pallas_call(
        functools.partial(_attn_ctx_kernel, n=n),
        grid=(bsz // CTX_BATCH,),
        in_specs=[blk(ATTN_W), kv_blk, kv_blk],
        out_specs=pl.BlockSpec((CTX_BATCH, ATTN_W // LANES, n, LANES), lambda b: (b, 0, 0, 0)),
        out_shape=jax.ShapeDtypeStruct((bsz, ATTN_W // LANES, n, LANES), F32),
        compiler_params=pltpu.CompilerParams(
            dimension_semantics=("arbitrary",), vmem_limit_bytes=VMEM_LIMIT),
        name="attn_ctx",
    )(q, k, v)


def _mix_kernel(attn_ref, x_ref, xp_ref, xn_ref, mod_ref, g_ref, w_ref, wout_ref,
                convw_ref, convb_ref, wpool_ref, pscale_ref, fg_ref,
                o_ref, xe, yext, uext, *, tm, tps, seq, final, mod_row):
    mod = _mod_row(mod_ref, mod_row, tps)
    j = pl.program_id(0) % tps
    pm = jnp.where(j == 0, 0.0, 1.0)
    nm = jnp.where(j == tps - 1, 0.0, 1.0)
    c = CONV_W
    g = g_ref[...]
    x = x_ref[...]
    xe[0:tm, :] = _mod_norm(x, g, mod).astype(BF16)
    if tps > 1:
        halo = jnp.concatenate([xn_ref[...], xp_ref[...]], axis=0)
        xe[tm:, :] = _mod_norm(halo, g, mod).astype(BF16)

    def proj(rows, off, width):
        return _dot(xe[0:rows, :], w_ref[0, :, off:off + width])

    nseq = max(tm // seq, 1)
    whole = tps == 1
    ln = tm // nseq
    hrows = tm if whole else tm + 2 * HALO
    h_all = proj(hrows, COL_CONV_H, c)
    c_all = proj(hrows, COL_CONV_C, c)
    u_all = proj(hrows, COL_POOL_U, POOL_W)
    z_a = proj(tm, COL_ATTN_GATE, ATTN_W)
    a = ATTN_W
    attn = jnp.concatenate(
        [jnp.concatenate([attn_ref[p, t] for t in range(attn_ref.shape[1])], axis=1)
         for p in range(attn_ref.shape[0])], axis=0)
    attn_o = (attn * _silu(z_a)).astype(BF16)
    out_a = _dot(attn_o, wout_ref[0, 0:a, :])
    b_c = proj(tm, COL_CONV_B, c)
    z_c = proj(tm, COL_CONV_GATE, c)
    z_p = proj(tm, COL_POOL_GATE, POOL_W)

    def extend(ext, p, val):
        zeros = jnp.zeros((HALO, val.shape[1]), F32)
        ext[p, 0:HALO, :] = zeros if whole else val[tm + HALO:, :] * pm
        ext[p, HALO:HALO + ln, :] = val[p * ln:(p + 1) * ln, :]
        ext[p, HALO + ln:, :] = zeros if whole else val[tm:tm + HALO, :] * nm

    y = h_all * c_all
    convs = []
    for p in range(nseq):
        extend(yext, p, y)
        convs.append(yext[p, HALO - 1:HALO - 1 + ln, :] * convw_ref[0:1, :]
                     + yext[p, HALO:HALO + ln, :] * convw_ref[1:2, :]
                     + yext[p, HALO + 1:HALO + 1 + ln, :] * convw_ref[2:3, :] + convb_ref[...])
    conv = convs[0] if nseq == 1 else jnp.concatenate(convs, axis=0)
    conv_o = (b_c * conv * _silu(z_c)).astype(BF16)

    lane = lax.broadcasted_iota(jnp.int32, (ln, LANES), 1)
    pos = lax.broadcasted_iota(jnp.int32, (ln, LANES), 0) + (0 if whole else j * tm)
    low = lane < POOL_GROUP
    d_rows = []
    for p in range(nseq):
        extend(uext, p, u_all)

        def win(k0, k1, sl):
            tot = None
            for k in range(k0, k1):
                t = uext[p, HALO + k:HALO + k + ln, sl]
                tot = t if tot is None else tot + t
            return tot

        ds = []
        for ti, (wa, wb) in enumerate(((2, 4), (8, 16))):
            sl = slice(ti * LANES, (ti + 1) * LANES)
            s_a = win(-(wa // 2), wa // 2, sl)
            s_b = s_a + win(-(wb // 2), -(wa // 2), sl) + win(wa // 2, wb // 2, sl)
            half = jnp.where(low, wa // 2, wb // 2)
            cnt = jnp.minimum(pos + half - 1, seq - 1) - jnp.maximum(pos - half, 0) + 1
            tot = jnp.where(low, s_a, s_b)
            ds.append(tot / cnt.astype(F32) - u_all[p * ln:(p + 1) * ln, sl])
        d_rows.append(jnp.concatenate(ds, axis=-1))
    d = (d_rows[0] if nseq == 1 else jnp.concatenate(d_rows, axis=0)).astype(BF16)
    pool_o = (_dot(d, wpool_ref[...]) * pscale_ref[...] * _silu(z_p)).astype(BF16)

    out = out_a + _dot(conv_o, wout_ref[0, a:a + c, :]) + _dot(pool_o, wout_ref[0, a + c:, :])
    res = x + mod[:, 2 * D_MODEL:] * out
    if final:
        ms = jnp.mean(res * res, axis=-1, keepdims=True)
        res = res * lax.rsqrt(ms + EPS) * fg_ref[...]
    o_ref[...] = res


def _mix(attn2, x2, mod, mod_row, norm_g, w_in_bf, wout_bf, layer, conv_w, conv_b, wpool_bd, pool_scale,
         final_g, *, seq, tm, final):
    tokens = x2.shape[0]
    tps = max(seq // tm, 1)
    nseq = max(tm // seq, 1)
    hb = tm // HALO
    nhb = tokens // HALO
    row = lambda i: (i, 0)
    const = lambda i: (0, 0)
    in_specs = [
        (pl.BlockSpec((nseq, ATTN_W // LANES, tm // nseq, LANES), lambda i: (i, 0, 0, 0)) if tps == 1 else
         pl.BlockSpec((1, ATTN_W // LANES, tm, LANES), lambda i: (i // tps, 0, i % tps, 0))),
        pl.BlockSpec((tm, D_MODEL), row),
        pl.BlockSpec((HALO, D_MODEL), lambda i: (jnp.maximum(i * hb - 1, 0), 0)),
        pl.BlockSpec((HALO, D_MODEL), lambda i: (jnp.minimum((i + 1) * hb, nhb - 1), 0)),
        _mod_spec(layer),
        pl.BlockSpec((1, D_MODEL), const),
        pl.BlockSpec((1, D_MODEL, IN_W), lambda i: (layer, 0, 0)),
        pl.BlockSpec((1, D_MODEL, D_MODEL), lambda i: (layer, 0, 0)),
        pl.BlockSpec((3, CONV_W), const),
        pl.BlockSpec((1, CONV_W), const),
        pl.BlockSpec((POOL_W, POOL_W), const),
        pl.BlockSpec((1, POOL_W), const),
        pl.BlockSpec((1, D_MODEL), const),
    ]
    return pl.pallas_call(
        functools.partial(_mix_kernel, tm=tm, tps=tps, seq=seq, final=final, mod_row=mod_row),
        grid=(tokens // tm,),
        in_specs=in_specs,
        out_specs=pl.BlockSpec((tm, D_MODEL), row),
        out_shape=jax.ShapeDtypeStruct((tokens, D_MODEL), F32),
        scratch_shapes=[
            pltpu.VMEM((tm + 2 * HALO, D_MODEL), BF16),
            pltpu.VMEM((nseq, tm // nseq + 2 * HALO, CONV_W), F32),
            pltpu.VMEM((nseq, tm // nseq + 2 * HALO, POOL_W), F32),
        ],
        compiler_params=pltpu.CompilerParams(
            dimension_semantics=("arbitrary",), vmem_limit_bytes=VMEM_LIMIT),
        name="mix_final" if final else "mix",
    )(attn2, x2, x2, x2, mod, norm_g.reshape(1, D_MODEL), w_in_bf, wout_bf,
      conv_w, conv_b.reshape(1, CONV_W), wpool_bd, pool_scale.reshape(1, POOL_W),
      final_g.reshape(1, D_MODEL))


def _rope_tables(n):
    rows = n // GRID_W
    row = np.repeat(np.arange(rows), GRID_W).astype(np.float64)
    col = np.tile(np.arange(GRID_W), rows).astype(np.float64)
    half = HEAD_DIM // 2
    inv = 1.0 / (ROPE_THETA ** (np.arange(0, half, 2, dtype=np.float64) / half))
    ar = row[:, None] * inv
    ac = col[:, None] * inv
    ang = np.tile(np.concatenate([ar, ar, ac, ac], axis=-1), (1, LANES // HEAD_DIM))
    second = (np.arange(LANES) % half) >= (half // 2)
    return jnp.asarray(np.cos(ang), F32), jnp.asarray(np.where(second, np.sin(ang), -np.sin(ang)), F32)


def _block_diag(blocks):
    n = len(blocks)
    rows = []
    for i, b in enumerate(blocks):
        rows.append(jnp.concatenate(
            [b if j == i else jnp.zeros((b.shape[0], blocks[j].shape[1]), b.dtype) for j in range(n)],
            axis=1))
    return jnp.concatenate(rows, axis=0)


def kernel(x_prompt, x_sample, cache_k, cache_v, c, c_ctx, norm_g, w_ada, b_ada, w_in,
           q_norm_g, k_norm_g, conv_w, conv_b, pool_w, pool_scale, w_out, final_g):
    depth = w_in.shape[0]
    bp, sp, _ = x_prompt.shape
    bs, ss, _ = x_sample.shape
    n_ctx = cache_k.shape[2]

    cvec = jnp.concatenate(
        [c, c_ctx[None, :], jnp.zeros((ADA_ROWS - bs - 1, D_MODEL), F32)], axis=0)
    mod = _ada(cvec, w_ada, b_ada)

    w_in_bf = w_in.astype(BF16)
    w_out_bf = w_out.astype(BF16)
    tables = _rope_tables(ss)
    ones = jnp.full((HEAD_DIM, HEAD_DIM), 1.0 / HEAD_DIM, BF16)
    bd = _block_diag([ones, ones])
    ck = cache_k.transpose(0, 1, 3, 4, 2).reshape(bs, depth, KV_W, n_ctx)
    cv = cache_v.transpose(0, 1, 3, 4, 2).reshape(bs, depth, KV_W, n_ctx)

    h = x_prompt.reshape(bp * sp, D_MODEL)
    z = x_sample.reshape(bs * ss, D_MODEL)
    kv_ctx = None
    gains = jnp.concatenate(
        [jnp.tile(q_norm_g * (HEAD_DIM ** -0.5), (1, N_HEADS)), jnp.tile(k_norm_g, (1, N_KV_HEADS))],
        axis=-1).reshape(depth, 1, QK_W)
    for l in range(depth):
        wpool_bd = _block_diag([pool_w[l, g] for g in range(pool_w.shape[1])]).astype(BF16)
        final = l == depth - 1
        mix = functools.partial(
            _mix, norm_g=norm_g[l], w_in_bf=w_in_bf, wout_bf=w_out_bf, layer=l,
            conv_w=conv_w[l], conv_b=conv_b[l], wpool_bd=wpool_bd, pool_scale=pool_scale[l],
            final_g=final_g, final=final)

        q, k, v = _qkv(h, mod, bs, norm_g[l], w_in_bf, l, gains, bd, None, kv_ctx, seq=sp, tm=QKV_TILE)
        kv_ctx = (k, v)
        attn = _attention_ctx(q.reshape(bp, sp, ATTN_W), k, v, l)
        h = mix(attn, h, mod, bs, seq=sp, tm=MIX_TILE)

        q, k, v = _qkv(z, mod, None, norm_g[l], w_in_bf, l, gains, bd, tables, None, seq=ss, tm=QKV_TILE)
        attn = _attention(q.reshape(bs, ss, ATTN_W), k.reshape(bs, ss, KV_W), v.reshape(bs, ss, KV_W),
                          ck, cv, l)
        z = mix(attn, z, mod, None, seq=ss, tm=MIX_TILE)

    y_prompt = h.reshape(bp, sp, D_MODEL)
    y_sample = z.reshape(bs, ss, D_MODEL)
    new_k, new_v = (a.reshape(bp, depth, N_KV_HEADS, HEAD_DIM, sp).transpose(0, 1, 4, 2, 3) for a in kv_ctx)
    return (y_prompt, y_sample, new_k, new_v)
```
